```python
import math
import jax, jax.numpy as jnp
from jax import lax
import numpy as np

D_MODEL = 1024
BATCH = 2
SEQ = 8192
DEPTH = 2

N_HEADS_A = 8
HEAD_DIM_A = 64
WIDTH_A = N_HEADS_A * HEAD_DIM_A
MOBA_BLOCK = 256
MOBA_TOPK = 3
MOBA_Q_CHUNK = 64
N_HEADS_B = 8
QK_NOPE_DIM = 64
QK_ROPE_DIM = 32
V_HEAD_DIM = 64
Q_LORA_RANK = 384
KV_LORA_RANK = 256
WIDTH_B = N_HEADS_B * V_HEAD_DIM
ROPE_THETA = 10000.0
MLA_Q_BLOCK = 128
IN_COLS = 3 * WIDTH_A + Q_LORA_RANK + KV_LORA_RANK + QK_ROPE_DIM + 2 * D_MODEL
N_EXPERTS = 256
TOP_K = 8
N_GROUPS = 8
TOPK_GROUPS = 4
D_EXPERT = 256
D_SHARED = 256
ROUTED_SCALE = 2.5
EXPERT_BLOCK = 128
ALPHA = (2 * DEPTH) ** 0.25
BETA = (8 * DEPTH) ** -0.25
LN_EPS = 1e-5
RMS_EPS = 1e-6

kernel_name = 'hybrid_moba_mla_moe_deepnorm'


def layer_norm(x, g, b):
    xf = x.astype(jnp.float32)
    mu = xf.mean(-1, keepdims=True)
    var = jnp.square(xf - mu).mean(-1, keepdims=True)
    return ((xf - mu) * lax.rsqrt(var + LN_EPS)).astype(x.dtype) * g + b


def rms_norm(x, g):
    xf = x.astype(jnp.float32)
    return (xf * lax.rsqrt(jnp.square(xf).mean(-1, keepdims=True) + RMS_EPS)).astype(x.dtype) * g


def rotate(x, cos, sin):
    x1, x2 = jnp.split(x, 2, axis=-1)
    return jnp.concatenate([x1 * cos - x2 * sin, x2 * cos + x1 * sin], axis=-1)


def alibi_slopes(n_heads):
    return jnp.exp2(-8.0 * (jnp.arange(n_heads, dtype=jnp.float32) + 1.0) / n_heads)


def moba_attention(q, k, v):
    b, s, h, dh = q.shape
    s_pad = -(-s // MOBA_BLOCK) * MOBA_BLOCK
    pad = ((0, 0), (0, s_pad - s), (0, 0), (0, 0))
    q, k, v = (jnp.pad(t, pad).transpose(0, 2, 1, 3) for t in (q, k, v))
    n_blk = s_pad // MOBA_BLOCK
    k_blk = k.reshape(b, h, n_blk, MOBA_BLOCK, dh)
    v_blk = v.reshape(b, h, n_blk, MOBA_BLOCK, dh)
    k_mean = k_blk.mean(axis=3)
    n_sel = min(MOBA_TOPK, n_blk)
    slopes = alibi_slopes(h)[None, :, None, None]
    scale = dh ** -0.5
    offs = jnp.arange(MOBA_BLOCK, dtype=jnp.int32)
    gather = jax.vmap(jax.vmap(lambda blocks, idx: blocks[idx]))

    def chunk(c):
        q0 = c * MOBA_Q_CHUNK
        qc = lax.dynamic_slice_in_dim(q, q0, MOBA_Q_CHUNK, axis=2)
        tq = q0 + jnp.arange(MOBA_Q_CHUNK, dtype=jnp.int32)
        own = q0 // MOBA_BLOCK
        gate = jnp.einsum('bhqd,bhnd->bhqn', qc, k_mean).astype(jnp.float32)
        gate = jnp.where(jnp.arange(n_blk) < own, gate, -jnp.inf)
        _, idx = lax.top_k(gate, n_sel)
        valid = jnp.arange(n_sel) < own
        k_sel = gather(k_blk, idx)
        v_sel = gather(v_blk, idx)
        kpos_sel = idx[..., None] * MOBA_BLOCK + offs
        s_sel = jnp.einsum('bhqd,bhqrkd->bhqrk', qc, k_sel).astype(jnp.float32) * scale
        s_sel = s_sel - slopes[..., None] * (tq[:, None, None] - kpos_sel).astype(jnp.float32)
        s_sel = jnp.where(valid[:, None], s_sel, -jnp.inf)
        k_own = lax.dynamic_index_in_dim(k_blk, own, axis=2, keepdims=False)
        v_own = lax.dynamic_index_in_dim(v_blk, own, axis=2, keepdims=False)
        kpos_own = own * MOBA_BLOCK + offs
        s_own = jnp.einsum('bhqd,bhkd->bhqk', qc, k_own).astype(jnp.float32) * scale
        s_own = s_own - slopes * (tq[:, None] - kpos_own[None, :]).astype(jnp.float32)
        s_own = jnp.where(kpos_own[None, :] <= tq[:, None], s_own, -jnp.inf)
        scores = jnp.concatenate([s_sel.reshape(b, h, MOBA_Q_CHUNK, n_sel * MOBA_BLOCK), s_own], axis=-1)
        p = jax.nn.softmax(scores, axis=-1).astype(v.dtype)
        p_sel = p[..., :n_sel * MOBA_BLOCK].reshape(b, h, MOBA_Q_CHUNK, n_sel, MOBA_BLOCK)
        p_own = p[..., n_sel * MOBA_BLOCK:]
        return (jnp.einsum('bhqrk,bhqrkd->bhqd', p_sel, v_sel)
                + jnp.einsum('bhqk,bhkd->bhqd', p_own, v_own))

    out = lax.map(chunk, jnp.arange(s_pad // MOBA_Q_CHUNK, dtype=jnp.int32))
    out = out.transpose(1, 0, 3, 2, 4).reshape(b, s_pad, h, dh)[:, :s]
    return out.reshape(b, s, h * dh)


def mla_attention(q_nope, q_rope, k_nope, k_rope, v):
    b, s, h, _ = q_nope.shape
    scale = (QK_NOPE_DIM + QK_ROPE_DIM) ** -0.5
    kpos = jnp.arange(s, dtype=jnp.int32)

    def block(c):
        q0 = c * MLA_Q_BLOCK
        qn = lax.dynamic_slice_in_dim(q_nope, q0, MLA_Q_BLOCK, axis=1)
        qr = lax.dynamic_slice_in_dim(q_rope, q0, MLA_Q_BLOCK, axis=1)
        tq = q0 + jnp.arange(MLA_Q_BLOCK, dtype=jnp.int32)
        sc = (jnp.einsum('bqhd,bkhd->bhqk', qn, k_nope)
              + jnp.einsum('bqhd,bkd->bhqk', qr, k_rope)).astype(jnp.float32) * scale
        sc = jnp.where(kpos[None, :] <= tq[:, None], sc, -jnp.inf)
        p = jax.nn.softmax(sc, axis=-1).astype(v.dtype)
        return jnp.einsum('bhqk,bkhd->bqhd', p, v)

    out = lax.map(block, jnp.arange(s // MLA_Q_BLOCK, dtype=jnp.int32))
    return out.transpose(1, 0, 2, 3, 4).reshape(b, s, h * V_HEAD_DIM)


def token_mixer(x, w_in, b_gate, q_norm, w_uq, kv_norm, w_ukv, w_proj_a, w_proj_b, w_out, cos, sin):
    b, s, _ = x.shape
    proj = x @ w_in
    bounds = [WIDTH_A, 2 * WIDTH_A, 3 * WIDTH_A,
              3 * WIDTH_A + Q_LORA_RANK,
              3 * WIDTH_A + Q_LORA_RANK + KV_LORA_RANK,
              3 * WIDTH_A + Q_LORA_RANK + KV_LORA_RANK + QK_ROPE_DIM,
              3 * WIDTH_A + Q_LORA_RANK + KV_LORA_RANK + QK_ROPE_DIM + D_MODEL]
    qa, ka, va, cq, ckv, kr, ga, gb = jnp.split(proj, bounds, axis=-1)
    heads_a = lambda t: t.reshape(b, s, N_HEADS_A, HEAD_DIM_A)
    o_a = moba_attention(heads_a(qa), heads_a(ka), heads_a(va))
    q = (rms_norm(cq, q_norm) @ w_uq).reshape(b, s, N_HEADS_B, QK_NOPE_DIM + QK_ROPE_DIM)
    q_nope, q_rope = q[..., :QK_NOPE_DIM], q[..., QK_NOPE_DIM:]
    q_rope = rotate(q_rope, cos[None, :, None, :], sin[None, :, None, :])
    kv = (rms_norm(ckv, kv_norm) @ w_ukv).reshape(b, s, N_HEADS_B, QK_NOPE_DIM + V_HEAD_DIM)
    k_nope, v_b = kv[..., :QK_NOPE_DIM], kv[..., QK_NOPE_DIM:]
    k_rope = rotate(kr, cos[None], sin[None])
    o_b = mla_attention(q_nope, q_rope, k_nope, k_rope, v_b)
    merged = (jax.nn.sigmoid(ga + b_gate[0]) * (o_a @ w_proj_a)
              + jax.nn.sigmoid(gb + b_gate[1]) * (o_b @ w_proj_b))
    return merged @ w_out


def moe_ffn(x, w_router, router_bias, w_e_gate, w_e_up, w_e_down, w_s_gate, w_s_up, w_s_down):
    b, s, d = x.shape
    t = b * s
    h = x.reshape(t, d)
    shared = (jax.nn.silu(h @ w_s_gate) * (h @ w_s_up)) @ w_s_down
    scores = jax.nn.sigmoid((h @ w_router).astype(jnp.float32))
    choice = scores + router_bias.astype(jnp.float32)
    grp = choice.reshape(t, N_GROUPS, N_EXPERTS // N_GROUPS)
    grp_score = lax.top_k(grp, 2)[0].sum(-1)
    _, grp_idx = lax.top_k(grp_score, TOPK_GROUPS)
    grp_mask = (grp_idx[..., None] == jnp.arange(N_GROUPS)).any(axis=1)
    expert_mask = jnp.repeat(grp_mask, N_EXPERTS // N_GROUPS, axis=1)
    _, top_idx = lax.top_k(jnp.where(expert_mask, choice, -jnp.inf), TOP_K)
    top_w = jnp.take_along_axis(scores, top_idx, axis=1)
    top_w = top_w / top_w.sum(-1, keepdims=True) * ROUTED_SCALE
    m = t * TOP_K
    flat_e = top_idx.reshape(m).astype(jnp.int32)
    flat_tok = jnp.repeat(jnp.arange(t, dtype=jnp.int32), TOP_K)
    flat_w = top_w.reshape(m).astype(x.dtype)
    order = jnp.argsort(flat_e)
    e_sorted, tok_sorted, w_sorted = flat_e[order], flat_tok[order], flat_w[order]
    counts = jnp.bincount(flat_e, length=N_EXPERTS).astype(jnp.int32)
    starts = jnp.cumsum(counts) - counts
    padded = (counts + EXPERT_BLOCK - 1) // EXPERT_BLOCK * EXPERT_BLOCK
    padded_end = jnp.cumsum(padded)
    padded_start = padded_end - padded
    dest = padded_start[e_sorted] + jnp.arange(m, dtype=jnp.int32) - starts[e_sorted]
    n_blocks = -(-(m + N_EXPERTS * (EXPERT_BLOCK - 1)) // EXPERT_BLOCK)
    n_rows = n_blocks * EXPERT_BLOCK
    row_tok = jnp.full((n_rows,), t, jnp.int32).at[dest].set(tok_sorted)
    row_w = jnp.zeros((n_rows,), x.dtype).at[dest].set(w_sorted)
    block_e = jnp.clip(jnp.searchsorted(padded_end, jnp.arange(n_blocks, dtype=jnp.int32) * EXPERT_BLOCK,
                                        side='right'), 0, N_EXPERTS - 1)
    h_pad = jnp.concatenate([h, jnp.zeros((1, d), h.dtype)], axis=0)

    def expert_block(args):
        tok, wgt, e = args
        xb = h_pad[tok]
        a = jax.nn.silu(xb @ w_e_gate[e]) * (xb @ w_e_up[e])
        return (a @ w_e_down[e]) * wgt[:, None]

    out = lax.map(expert_block, (row_tok.reshape(n_blocks, EXPERT_BLOCK),
                                 row_w.reshape(n_blocks, EXPERT_BLOCK), block_e))
    routed = jax.ops.segment_sum(out.reshape(n_rows, d), row_tok, num_segments=t + 1)[:t]
    return (shared + routed).reshape(b, s, d)


def setup_inputs(seed: int = 0) -> dict:
    key = jax.random.key(seed)
    ks = jax.random.split(key, 24)
    L = DEPTH

    def dense(k, shape, fan_in, scale=1.0):
        return jax.random.normal(k, shape, jnp.float32) * (fan_in ** -0.5) * scale

    def gain(k, shape):
        return 1.0 + 0.05 * jax.random.normal(k, shape, jnp.float32)

    def small(k, shape, scale=0.02):
        return scale * jax.random.normal(k, shape, jnp.float32)

    return {
        'x': jax.random.normal(ks[0], (BATCH, SEQ, D_MODEL), jnp.float32),
        'w_in': dense(ks[1], (L, D_MODEL, IN_COLS), D_MODEL),
        'b_gate': small(ks[2], (L, 2, D_MODEL)),
        'q_norm': gain(ks[3], (L, Q_LORA_RANK)),
        'w_uq': dense(ks[4], (L, Q_LORA_RANK, N_HEADS_B * (QK_NOPE_DIM + QK_ROPE_DIM)), Q_LORA_RANK),
        'kv_norm': gain(ks[5], (L, KV_LORA_RANK)),
        'w_ukv': dense(ks[6], (L, KV_LORA_RANK, N_HEADS_B * (QK_NOPE_DIM + V_HEAD_DIM)), KV_LORA_RANK),
        'w_proj_a': dense(ks[7], (L, WIDTH_A, D_MODEL), WIDTH_A),
        'w_proj_b': dense(ks[8], (L, WIDTH_B, D_MODEL), WIDTH_B),
        'w_out': dense(ks[9], (L, D_MODEL, D_MODEL), D_MODEL, BETA),
        'ln1_g': gain(ks[10], (L, D_MODEL)),
        'ln1_b': small(ks[11], (L, D_MODEL)),
        'w_router': dense(ks[12], (L, D_MODEL, N_EXPERTS), D_MODEL),
        'router_bias': small(ks[13], (L, N_EXPERTS), 0.01),
        'w_e_gate': dense(ks[14], (L, N_EXPERTS, D_MODEL, D_EXPERT), D_MODEL),
        'w_e_up': dense(ks[15], (L, N_EXPERTS, D_MODEL, D_EXPERT), D_MODEL),
        'w_e_down': dense(ks[16], (L, N_EXPERTS, D_EXPERT, D_MODEL), D_EXPERT, BETA),
        'w_s_gate': dense(ks[17], (L, D_MODEL, D_SHARED), D_MODEL),
        'w_s_up': dense(ks[18], (L, D_MODEL, D_SHARED), D_MODEL),
        'w_s_down': dense(ks[19], (L, D_SHARED, D_MODEL), D_SHARED, BETA),
        'ln2_g': gain(ks[20], (L, D_MODEL)),
        'ln2_b': small(ks[21], (L, D_MODEL)),
    }


def reference(x, w_in, b_gate, q_norm, w_uq, kv_norm, w_ukv, w_proj_a, w_proj_b, w_out,
              ln1_g, ln1_b, w_router, router_bias, w_e_gate, w_e_up, w_e_down,
              w_s_gate, w_s_up, w_s_down, ln2_g, ln2_b):
    s = x.shape[1]
    pos = jnp.arange(s, dtype=jnp.float32)
    inv_freq = ROPE_THETA ** (-jnp.arange(0, QK_ROPE_DIM, 2, dtype=jnp.float32) / QK_ROPE_DIM)
    ang = pos[:, None] * inv_freq[None, :]
    cos, sin = jnp.cos(ang).astype(x.dtype), jnp.sin(ang).astype(x.dtype)
    for l in range(DEPTH):
        hmix = token_mixer(x, w_in[l], b_gate[l], q_norm[l], w_uq[l], kv_norm[l], w_ukv[l],
                           w_proj_a[l], w_proj_b[l], w_out[l], cos, sin)
        x = layer_norm(ALPHA * x + hmix, ln1_g[l], ln1_b[l])
        hffn = moe_ffn(x, w_router[l], router_bias[l], w_e_gate[l], w_e_up[l], w_e_down[l],
                       w_s_gate[l], w_s_up[l], w_s_down[l])
        x = layer_norm(ALPHA * x + hffn, ln2_g[l], ln2_b[l])
    return x
```

```python
import functools

import jax
import jax.numpy as jnp
from jax import lax
from jax.experimental import pallas as pl
from jax.experimental.pallas import tpu as pltpu

D_MODEL = 1024
N_HEADS_A = 8
HEAD_DIM_A = 64
WIDTH_A = N_HEADS_A * HEAD_DIM_A
MOBA_BLOCK = 256
MOBA_TOPK = 3
N_HEADS_B = 8
QK_NOPE_DIM = 64
QK_ROPE_DIM = 32
V_HEAD_DIM = 64
Q_LORA_RANK = 384
KV_LORA_RANK = 256
WIDTH_B = N_HEADS_B * V_HEAD_DIM
ROPE_THETA = 10000.0
N_EXPERTS = 256
TOP_K = 8
N_GROUPS = 8
TOPK_GROUPS = 4
GROUP_SIZE = N_EXPERTS // N_GROUPS
D_EXPERT = 256
D_SHARED = 256
ROUTED_SCALE = 2.5
DEPTH = 2
ALPHA = (2 * DEPTH) ** 0.25
LN_EPS = 1e-5
RMS_EPS = 1e-6

LANES = 128
NEG = -1e30
ROW_TILE = 256
ATT_TILE = 256
EXPERT_BLOCK = 256
COMBINE_TILE = 128
VMEM_LIMIT = 56 * 1024 * 1024

BF16 = jnp.bfloat16
F32 = jnp.float32


def _dot(a, b):
    return jnp.dot(a, b, preferred_element_type=F32)


def _dot_nt(a, b):
    return lax.dot_general(a, b, (((1,), (1,)), ((), ())), preferred_element_type=F32)


def _sigmoid(x):
    return 1.0 / (1.0 + jnp.exp(-x))


def _layer_norm(y, g, b):
    mu = jnp.mean(y, axis=-1, keepdims=True)
    d = y - mu
    var = jnp.mean(d * d, axis=-1, keepdims=True)
    return d * lax.rsqrt(var + LN_EPS) * g + b


def _params(*sem):
    return pltpu.CompilerParams(dimension_semantics=sem, vmem_limit_bytes=VMEM_LIMIT)


def _inproj_kernel(x_ref, wqkv_ref, wcq_ref, wckv_ref, wkr_ref, wg_ref, bg_ref, qn_ref, kvn_ref,
                   wuq_ref, wuk_ref, wuv_ref, cos_ref, s1_ref, s2_ref,
                   qa_ref, ka_ref, va_ref, kmean_ref, qm_ref, km_ref, vm_ref, sa_ref, sb_ref):
    xb = x_ref[...].astype(BF16)
    qkv = _dot(xb, wqkv_ref[...])
    qa_ref[...] = qkv[:, :WIDTH_A].astype(BF16)
    k = qkv[:, WIDTH_A:2 * WIDTH_A]
    ka_ref[...] = k.astype(BF16)
    kmean_ref[0] = jnp.mean(k, axis=0, keepdims=True)
    va_ref[...] = qkv[:, 2 * WIDTH_A:].astype(BF16)

    cq = _dot(xb, wcq_ref[...])
    cqn = cq * lax.rsqrt(jnp.mean(cq * cq, axis=-1, keepdims=True) + RMS_EPS) * qn_ref[...]
    q = _dot(cqn.astype(BF16), wuq_ref[...])
    ckv = _dot(xb, wckv_ref[...])
    ckvn = (ckv * lax.rsqrt(jnp.mean(ckv * ckv, axis=-1, keepdims=True) + RMS_EPS) * kvn_ref[...]).astype(BF16)
    kn = _dot(ckvn, wuk_ref[...])
    vm_ref[...] = _dot(ckvn, wuv_ref[...]).astype(BF16)
    kr = _dot(xb, wkr_ref[...])

    c, s1, s2 = cos_ref[...], s1_ref[...], s2_ref[...]

    def rope(t):
        return (t * c + pltpu.roll(t, LANES - QK_ROPE_DIM // 2, 1) * s1
                + pltpu.roll(t, QK_ROPE_DIM // 2, 1) * s2)

    krot = rope(kr)
    for h in range(N_HEADS_B):
        sl = slice(h * LANES, (h + 1) * LANES)
        qm_ref[:, sl] = rope(q[:, sl]).astype(BF16)
        km_ref[:, sl] = (kn[:, sl] + krot).astype(BF16)

    sig = _sigmoid(_dot(xb, wg_ref[...]) + bg_ref[...])
    sa_ref[...] = sig[:, :D_MODEL].astype(BF16)
    sb_ref[...] = sig[:, D_MODEL:].astype(BF16)


def _inproj(x, wp, tabs, seq):
    t = x.shape[0]
    tm = ROW_TILE
    nt = t // tm
    npos = seq // tm
    row = lambda w: pl.BlockSpec((tm, w), lambda i: (i, 0))
    full = lambda a: pl.BlockSpec(a.shape, lambda i: (0,) * a.ndim)
    tab = pl.BlockSpec((tm, LANES), lambda i: (i % npos, 0))
    weights = [wp['wqkv'], wp['wcq'], wp['wckv'], wp['wkr'], wp['wg'], wp['bg'], wp['qn'], wp['kvn'],
               wp['wuq'], wp['wuk'], wp['wuv']]
    out_shape = [
        jax.ShapeDtypeStruct((t, WIDTH_A), BF16), jax.ShapeDtypeStruct((t, WIDTH_A), BF16),
        jax.ShapeDtypeStruct((t, WIDTH_A), BF16), jax.ShapeDtypeStruct((nt, 1, WIDTH_A), F32),
        jax.ShapeDtypeStruct((t, N_HEADS_B * LANES), BF16), jax.ShapeDtypeStruct((t, N_HEADS_B * LANES), BF16),
        jax.ShapeDtypeStruct((t, WIDTH_B), BF16),
        jax.ShapeDtypeStruct((t, D_MODEL), BF16), jax.ShapeDtypeStruct((t, D_MODEL), BF16),
    ]
    out_specs = [row(WIDTH_A), row(WIDTH_A), row(WIDTH_A),
                 pl.BlockSpec((1, 1, WIDTH_A), lambda i: (i, 0, 0)),
                 row(N_HEADS_B * LANES), row(N_HEADS_B * LANES), row(WIDTH_B), row(D_MODEL), row(D_MODEL)]
    return pl.pallas_call(
        _inproj_kernel,
        grid=(nt,),
        in_specs=[row(D_MODEL)] + [full(w) for w in weights] + [tab, tab, tab],
        out_specs=out_specs,
        out_shape=out_shape,
        compiler_params=_params("parallel"),
        name="inproj",
    )(x, *weights, *tabs)


def _softmax_step(s, carry, v):
    m_i, l_i, acc = carry
    m_new = jnp.maximum(m_i, jnp.max(s, axis=1, keepdims=True))
    alpha = jnp.exp(m_i - m_new)
    p = jnp.exp(s - m_new)
    l_new = alpha * l_i + jnp.sum(p, axis=1, keepdims=True)
    acc_new = alpha * acc + _dot(p.astype(BF16), v)
    return m_new, l_new, acc_new


def _softmax_init(tq):
    return (jnp.full((tq, 1), -jnp.inf, F32), jnp.zeros((tq, 1), F32), jnp.zeros((tq, LANES), F32))


def _moba_kernel(slope_ref, q_ref, k_ref, v_ref, km_ref, o_ref):
    tq = ATT_TILE
    hp = pl.program_id(1)
    i = pl.program_id(2)
    nblk = km_ref.shape[1]
    q2 = q_ref[...]
    lane = lax.broadcasted_iota(jnp.int32, (tq, LANES), 1)
    kmean = jnp.concatenate([km_ref[0].astype(BF16), jnp.zeros((LANES - nblk, LANES), BF16)], axis=0)
    qi = lax.broadcasted_iota(jnp.int32, (tq, tq), 0)
    ki = lax.broadcasted_iota(jnp.int32, (tq, tq), 1)
    rel = (qi - ki).astype(F32)
    causal = ki <= qi

    outs = []
    for hh in range(2):
        head_lanes = (lane >= hh * HEAD_DIM_A) & (lane < (hh + 1) * HEAD_DIM_A)
        qh = jnp.where(head_lanes, q2, jnp.zeros_like(q2))
        gate = _dot_nt(qh, kmean)
        g = jnp.where(lane < i, gate, -jnp.inf)
        picked = lane == i
        for r in range(MOBA_TOPK):
            m = jnp.max(g, axis=1, keepdims=True)
            idx = jnp.min(jnp.where(g == m, lane, LANES), axis=1, keepdims=True)
            hit = lane == idx
            picked = picked | (hit & (r < i))
            g = jnp.where(hit, -jnp.inf, g)
        bias = jnp.where(picked, 0.0, NEG).astype(BF16)
        qaug = jnp.concatenate([qh, bias], axis=1)
        slope = slope_ref[hp * 2 + hh]

        def scores(j):
            kj = k_ref[pl.ds(pl.multiple_of(j * tq, tq), tq), :]
            ej = (lane == j).astype(BF16)
            s = _dot_nt(qaug, jnp.concatenate([kj, ej], axis=1))
            return s - slope * (rel + ((i - j) * tq).astype(F32))

        def values(j):
            return v_ref[pl.ds(pl.multiple_of(j * tq, tq), tq), :]

        carry = _softmax_step(jnp.where(causal, scores(i), NEG), _softmax_init(tq), values(i))
        carry = lax.fori_loop(0, i, lambda j, c: _softmax_step(scores(j), c, values(j)), carry)
        _, l_i, acc = carry
        outs.append(acc / l_i)
    o_ref[...] = jnp.where(lane < HEAD_DIM_A, outs[0], outs[1]).astype(BF16)


def _moba(qa, ka, va, kmean, slopes, batch, seq):
    t = qa.shape[0]
    tq = ATT_TILE
    nq = seq // tq
    nblk = seq // MOBA_BLOCK
    npair = N_HEADS_A // 2
    km = kmean.reshape(batch, nblk, WIDTH_A)
    return pl.pallas_call(
        _moba_kernel,
        grid=(batch, npair, nq),
        in_specs=[
            pl.BlockSpec(memory_space=pltpu.SMEM),
            pl.BlockSpec((tq, LANES), lambda b, h, i: (b * nq + i, h)),
            pl.BlockSpec((seq, LANES), lambda b, h, i: (b, h)),
            pl.BlockSpec((seq, LANES), lambda b, h, i: (b, h)),
            pl.BlockSpec((1, nblk, LANES), lambda b, h, i: (b, 0, h)),
        ],
        out_specs=pl.BlockSpec((tq, LANES), lambda b, h, i: (b * nq + i, h)),
        out_shape=jax.ShapeDtypeStruct((t, WIDTH_A), BF16),
        compiler_params=_params("parallel", "parallel", "arbitrary"),
        name="moba",
    )(slopes, qa, ka, va, km)


def _mla_kernel(q_ref, k_ref, v_ref, o_ref):
    tq = ATT_TILE
    i = pl.program_id(2)
    lane = lax.broadcasted_iota(jnp.int32, (tq, LANES), 1)
    qi = lax.broadcasted_iota(jnp.int32, (tq, tq), 0)
    ki = lax.broadcasted_iota(jnp.int32, (tq, tq), 1)
    causal = ki <= qi
    outs = []
    for hh in range(2):
        sl = slice(hh * LANES, (hh + 1) * LANES)
        q = q_ref[:, sl]

        def scores(j):
            return _dot_nt(q, k_ref[pl.ds(pl.multiple_of(j * tq, tq), tq), sl])

        def values(j):
            return v_ref[pl.ds(pl.multiple_of(j * tq, tq), tq), :]

        carry = _softmax_step(jnp.where(causal, scores(i), NEG), _softmax_init(tq), values(i))
        carry = lax.fori_loop(0, i, lambda j, c: _softmax_step(scores(j), c, values(j)), carry)
        _, l_i, acc = carry
        outs.append(acc / l_i)
    o_ref[...] = jnp.where(lane < V_HEAD_DIM, outs[0], outs[1]).astype(BF16)


def _mla(qm, km, vm, batch, seq):
    t = qm.shape[0]
    tq = ATT_TILE
    nq = seq // tq
    npair = N_HEADS_B // 2
    return pl.pallas_call(
        _mla_kernel,
        grid=(batch, npair, nq),
        in_specs=[
            pl.BlockSpec((tq, 2 * LANES), lambda b, h, i: (b * nq + i, h)),
            pl.BlockSpec((seq, 2 * LANES), lambda b, h, i: (b, h)),
            pl.BlockSpec((seq, LANES), lambda b, h, i: (b, h)),
        ],
        out_specs=pl.BlockSpec((tq, LANES), lambda b, h, i: (b * nq + i, h)),
        out_shape=jax.ShapeDtypeStruct((t, WIDTH_B), BF16),
        compiler_params=_params("parallel", "parallel", "arbitrary"),
        name="mla",
    )(qm, km, vm)


def _merge_kernel(x_ref, oa_ref, ob_ref, sa_ref, sb_ref, wpa_ref, wpb_ref, wo_ref, g_ref, b_ref, o_ref):
    pa = _dot(oa_ref[...], wpa_ref[...])
    pb = _dot(ob_ref[...], wpb_ref[...])
    merged = sa_ref[...].astype(F32) * pa + sb_ref[...].astype(F32) * pb
    hmix = _dot(merged.astype(BF16), wo_ref[...])
    o_ref[...] = _layer_norm(ALPHA * x_ref[...] + hmix, g_ref[...], b_ref[...])


def _merge(x, oa, ob, sa, sb, wp):
    t = x.shape[0]
    tm = ROW_TILE
    row = lambda w: pl.BlockSpec((tm, w), lambda i: (i, 0))
    full = lambda a: pl.BlockSpec(a.shape, lambda i: (0,) * a.ndim)
    weights = [wp['wpa'], wp['wpb'], wp['wo'], wp['ln1_g'], wp['ln1_b']]
    return pl.pallas_call(
        _merge_kernel,
        grid=(t // tm,),
        in_specs=[row(D_MODEL), row(WIDTH_A), row(WIDTH_B), row(D_MODEL), row(D_MODEL)] + [full(w) for w in weights],
        out_specs=row(D_MODEL),
        out_shape=jax.ShapeDtypeStruct((t, D_MODEL), F32),
        compiler_params=_params("parallel"),
        name="merge",
    )(x, oa, ob, sa, sb, *weights)


def _router_kernel(x_ref, wh_ref, wl_ref, rb_ref, idx_ref, w_ref):
    tm = x_ref.shape[0]
    x = x_ref[...]
    xh = x.astype(BF16)
    xl = (x - xh.astype(F32)).astype(BF16)
    wh, wl = wh_ref[...], wl_ref[...]
    logits = _dot_nt(wh, xh) + (_dot_nt(wh, xl) + _dot_nt(wl, xh))
    scores = _sigmoid(logits)
    choice = scores + rb_ref[...]
    row = lax.broadcasted_iota(jnp.int32, (GROUP_SIZE, tm), 0)
    groups = [choice[g * GROUP_SIZE:(g + 1) * GROUP_SIZE, :] for g in range(N_GROUPS)]
    gscore = []
    for blk in groups:
        m1 = jnp.max(blk, axis=0, keepdims=True)
        first = jnp.min(jnp.where(blk == m1, row, GROUP_SIZE), axis=0, keepdims=True)
        m2 = jnp.max(jnp.where(row == first, -jnp.inf, blk), axis=0, keepdims=True)
        gscore.append(m1 + m2)
    masked = []
    for g in range(N_GROUPS):
        ahead = jnp.zeros((1, tm), jnp.int32)
        for o in range(N_GROUPS):
            if o < g:
                ahead += (gscore[o] >= gscore[g]).astype(jnp.int32)
            elif o > g:
                ahead += (gscore[o] > gscore[g]).astype(jnp.int32)
        masked.append(jnp.where(ahead < TOPK_GROUPS, groups[g], -jnp.inf))
    cur = jnp.concatenate(masked, axis=0)
    erow = lax.broadcasted_iota(jnp.int32, (N_EXPERTS, tm), 0)
    idxs, ws = [], []
    for _ in range(TOP_K):
        m = jnp.max(cur, axis=0, keepdims=True)
        e = jnp.min(jnp.where(cur == m, erow, N_EXPERTS), axis=0, keepdims=True)
        hit = erow == e
        idxs.append(e)
        ws.append(jnp.sum(jnp.where(hit, scores, 0.0), axis=0, keepdims=True))
        cur = jnp.where(hit, -jnp.inf, cur)
    total = ws[0]
    for w in ws[1:]:
        total = total + w
    for r in range(TOP_K):
        idx_ref[r:r + 1, :] = idxs[r]
        w_ref[r:r + 1, :] = ws[r] / total * ROUTED_SCALE


def _router(x1, wp):
    t = x1.shape[0]
    tm = ROW_TILE
    full = lambda a: pl.BlockSpec(a.shape, lambda i: (0,) * a.ndim)
    weights = [wp['wr_hi'], wp['wr_lo'], wp['rbias']]
    return pl.pallas_call(
        _router_kernel,
        grid=(t // tm,),
        in_specs=[pl.BlockSpec((tm, D_MODEL), lambda i: (i, 0))] + [full(w) for w in weights],
        out_specs=[pl.BlockSpec((TOP_K, tm), lambda i: (0, i)), pl.BlockSpec((TOP_K, tm), lambda i: (0, i))],
        out_shape=[jax.ShapeDtypeStruct((TOP_K, t), jnp.int32), jax.ShapeDtypeStruct((TOP_K, t), F32)],
        compiler_params=_params("parallel"),
        name="router",
    )(x1, *weights)


def _row_copy(src_hbm, src_row, dst, dst_row, sem):
    return pltpu.make_async_copy(src_hbm.at[pl.ds(src_row, 1), :], dst.at[pl.ds(dst_row, 1), :], sem)


def _gather_kernel(nused_ref, tok_ref, x_hbm, o_hbm, sem):
    blk = pl.program_id(0)
    rows = tok_ref.shape[2]

    @pl.when(blk < nused_ref[0])
    def _():
        def issue(r, c):
            _row_copy(x_hbm, tok_ref[0, 0, r], o_hbm, blk * rows + r, sem).start()
            return c

        lax.fori_loop(0, rows, issue, 0, unroll=8)

        def drain(r, c):
            _row_copy(x_hbm, 0, o_hbm, blk * rows + r, sem).wait()
            return c

        lax.fori_loop(0, rows, drain, 0, unroll=8)

    @pl.when(blk >= nused_ref[0])
    def _():
        fill = pltpu.make_async_copy(x_hbm.at[pl.ds(0, rows), :], o_hbm.at[pl.ds(blk * rows, rows), :], sem)
        fill.start()
        fill.wait()


def _gather_rows(x1, row_tok, nused, n_blocks):
    rows = EXPERT_BLOCK
    return pl.pallas_call(
        _gather_kernel,
        grid=(n_blocks,),
        in_specs=[
            pl.BlockSpec(memory_space=pltpu.SMEM),
            pl.BlockSpec((1, 1, rows), lambda i: (i, 0, 0), memory_space=pltpu.SMEM),
            pl.BlockSpec(memory_space=pl.ANY),
        ],
        out_specs=pl.BlockSpec(memory_space=pl.ANY),
        out_shape=jax.ShapeDtypeStruct((n_blocks * rows, D_MODEL), F32),
        scratch_shapes=[pltpu.SemaphoreType.DMA(())],
        compiler_params=_params("arbitrary"),
        name="moe_gather",
    )(nused, row_tok.reshape(n_blocks, 1, rows), x1)


def _expert_kernel(be_ref, nused_ref, x_ref, w_ref, wg_ref, wu_ref, wd_ref, y_ref):
    @pl.when(pl.program_id(0) < nused_ref[0])
    def _():
        xb = x_ref[...].astype(BF16)
        g = _dot(xb, wg_ref[0].astype(BF16))
        u = _dot(xb, wu_ref[0].astype(BF16))
        a = (g * _sigmoid(g) * u).astype(BF16)
        y = _dot(a, wd_ref[0].astype(BF16))
        y_ref[...] = y * jnp.concatenate([w_ref[...]] * (D_MODEL // LANES), axis=1)

    @pl.when(pl.program_id(0) >= nused_ref[0])
    def _():
        y_ref[...] = jnp.zeros_like(y_ref)


def _experts(xs, row_w, block_e, nused, w_gate, w_up, w_down, n_blocks):
    rows = EXPERT_BLOCK
    used = lambda b, be, nu: jnp.minimum(b, nu[0] - 1)
    grid_spec = pltpu.PrefetchScalarGridSpec(
        num_scalar_prefetch=2,
        grid=(n_blocks,),
        in_specs=[
            pl.BlockSpec((rows, D_MODEL), lambda b, be, nu: (used(b, be, nu), 0)),
            pl.BlockSpec((rows, LANES), lambda b, be, nu: (used(b, be, nu), 0)),
            pl.BlockSpec((1, D_MODEL, D_EXPERT), lambda b, be, nu: (be[b], 0, 0)),
            pl.BlockSpec((1, D_MODEL, D_EXPERT), lambda b, be, nu: (be[b], 0, 0)),
            pl.BlockSpec((1, D_EXPERT, D_MODEL), lambda b, be, nu: (be[b], 0, 0)),
        ],
        out_specs=pl.BlockSpec((rows, D_MODEL), lambda b, be, nu: (b, 0)),
    )
    return pl.pallas_call(
        _expert_kernel,
        grid_spec=grid_spec,
        out_shape=jax.ShapeDtypeStruct((n_blocks * rows, D_MODEL), F32),
        compiler_params=_params("arbitrary"),
        name="moe_experts",
    )(block_e, nused, xs, row_w, w_gate, w_up, w_down)


def _combine_kernel(pos_ref, x_ref, wsg_ref, wsu_ref, wsd_ref, g_ref, b_ref, y_hbm, o_ref, buf, sem):
    tm = x_ref.shape[0]

    def issue(r, c):
        for k in range(TOP_K):
            _row_copy(y_hbm, pos_ref[0, 0, r * TOP_K + k], buf, k * tm + r, sem).start()
        return c

    lax.fori_loop(0, tm, issue, 0)

    x = x_ref[...]
    xb = x.astype(BF16)
    g = _dot(xb, wsg_ref[...])
    u = _dot(xb, wsu_ref[...])
    shared = _dot((g * _sigmoid(g) * u).astype(BF16), wsd_ref[...])

    def drain(r, c):
        _row_copy(y_hbm, 0, buf, r, sem).wait()
        return c

    lax.fori_loop(0, tm * TOP_K, drain, 0, unroll=8)

    routed = buf[0:tm, :]
    for k in range(1, TOP_K):
        routed = routed + buf[k * tm:(k + 1) * tm, :]
    o_ref[...] = _layer_norm(ALPHA * x + (shared + routed), g_ref[...], b_ref[...])


def _combine(x1, ys, pos, wp):
    t = x1.shape[0]
    tm = COMBINE_TILE
    full = lambda a: pl.BlockSpec(a.shape, lambda i: (0,) * a.ndim)
    weights = [wp['wsg'], wp['wsu'], wp['wsd'], wp['ln2_g'], wp['ln2_b']]
    return pl.pallas_call(
        _combine_kernel,
        grid=(t // tm,),
        in_specs=[pl.BlockSpec((1, 1, tm * TOP_K), lambda i: (i, 0, 0), memory_space=pltpu.SMEM),
                  pl.BlockSpec((tm, D_MODEL), lambda i: (i, 0))]
                 + [full(w) for w in weights] + [pl.BlockSpec(memory_space=pl.ANY)],
        out_specs=pl.BlockSpec((tm, D_MODEL), lambda i: (i, 0)),
        out_shape=jax.ShapeDtypeStruct((t, D_MODEL), F32),
        scratch_shapes=[pltpu.VMEM((TOP_K * tm, D_MODEL), F32), pltpu.SemaphoreType.DMA(())],
        compiler_params=_params("arbitrary"),
        name="moe_combine",
    )(pos.reshape(t // tm, 1, tm * TOP_K), x1, *weights, ys)


def _dispatch_plan(top_idx, top_w, n_blocks):
    t = top_idx.shape[1]
    m = t * TOP_K
    rows = EXPERT_BLOCK
    flat_e = top_idx.T.reshape(m)
    flat_w = top_w.T.reshape(m)
    order = jnp.argsort(flat_e, stable=True).astype(jnp.int32)
    e_sorted = flat_e[order]
    counts = jnp.zeros((N_EXPERTS,), jnp.int32).at[flat_e].add(1)
    starts = jnp.cumsum(counts) - counts
    padded = (counts + rows - 1) // rows * rows
    padded_end = jnp.cumsum(padded)
    padded_start = padded_end - padded
    nused = (padded_end[-1] // rows).astype(jnp.int32).reshape(1)
    dest = padded_start[e_sorted] + jnp.arange(m, dtype=jnp.int32) - starts[e_sorted]
    pos = jnp.zeros((m,), jnp.int32).at[order].set(dest, unique_indices=True)
    blocks = jnp.arange(n_blocks, dtype=jnp.int32)
    block_e = jnp.clip(jnp.searchsorted(padded_end, blocks * rows, side='right'), 0, N_EXPERTS - 1).astype(jnp.int32)
    block_e = jnp.where(blocks < nused[0], block_e, block_e[jnp.maximum(nused[0] - 1, 0)])
    row_e = jnp.repeat(block_e, rows)
    off = jnp.arange(n_blocks * rows, dtype=jnp.int32) - padded_start[row_e]
    valid = (off < counts[row_e]) & (jnp.repeat(blocks, rows) < nused[0])
    src = jnp.clip(starts[row_e] + off, 0, m - 1)
    row_tok = jnp.where(valid, order[src] // TOP_K, 0).astype(jnp.int32)
    row_w = jnp.where(valid, flat_w[order[src]], 0.0).astype(F32)
    row_w = jnp.broadcast_to(row_w[:, None], (n_blocks * rows, LANES))
    return row_tok, row_w, block_e, nused, pos


def _moe(x1, wp):
    t = x1.shape[0]
    m = t * TOP_K
    n_blocks = -(-(m + N_EXPERTS * (EXPERT_BLOCK - 1)) // EXPERT_BLOCK)
    top_idx, top_w = _router(x1, wp)
    row_tok, row_w, block_e, nused, pos = _dispatch_plan(top_idx, top_w, n_blocks)
    xs = _gather_rows(x1, row_tok, nused, n_blocks)
    ys = _experts(xs, row_w, block_e, nused, wp['w_e_gate'], wp['w_e_up'], wp['w_e_down'], n_blocks)
    return _combine(x1, ys, pos, wp)


def _prep_layer(l, w_in, b_gate, q_norm, w_uq, kv_norm, w_ukv, w_proj_a, w_proj_b, w_out, ln1_g, ln1_b,
                w_router, router_bias, w_e_gate, w_e_up, w_e_down, w_s_gate, w_s_up, w_s_down, ln2_g, ln2_b):
    w = w_in[l]
    o = 0
    cols = {}
    for name, width in (('qa', WIDTH_A), ('ka', WIDTH_A), ('va', WIDTH_A), ('cq', Q_LORA_RANK),
                        ('ckv', KV_LORA_RANK), ('kr', QK_ROPE_DIM), ('ga', D_MODEL), ('gb', D_MODEL)):
        cols[name] = w[:, o:o + width]
        o += width
    wqkv = jnp.concatenate([cols['qa'] * HEAD_DIM_A ** -0.5, cols['ka'], cols['va']], axis=1)
    wkr = jnp.zeros((D_MODEL, LANES), F32).at[:, QK_NOPE_DIM:QK_NOPE_DIM + QK_ROPE_DIM].set(cols['kr'])
    wq = w_uq[l].reshape(Q_LORA_RANK, N_HEADS_B, QK_NOPE_DIM + QK_ROPE_DIM) * (QK_NOPE_DIM + QK_ROPE_DIM) ** -0.5
    wq = jnp.pad(wq, ((0, 0), (0, 0), (0, LANES - QK_NOPE_DIM - QK_ROPE_DIM))).reshape(Q_LORA_RANK, N_HEADS_B * LANES)
    wkv = w_ukv[l].reshape(KV_LORA_RANK, N_HEADS_B, QK_NOPE_DIM + V_HEAD_DIM)
    wuk = jnp.pad(wkv[:, :, :QK_NOPE_DIM], ((0, 0), (0, 0), (0, LANES - QK_NOPE_DIM))).reshape(KV_LORA_RANK, N_HEADS_B * LANES)
    wuv = wkv[:, :, QK_NOPE_DIM:].reshape(KV_LORA_RANK, WIDTH_B)
    wr_t = w_router[l].T
    wr_hi = wr_t.astype(BF16)
    return dict(
        wqkv=wqkv.astype(BF16), wcq=cols['cq'].astype(BF16), wckv=cols['ckv'].astype(BF16), wkr=wkr.astype(BF16),
        wg=jnp.concatenate([cols['ga'], cols['gb']], axis=1).astype(BF16),
        bg=b_gate[l].reshape(1, 2 * D_MODEL), qn=q_norm[l].reshape(1, Q_LORA_RANK), kvn=kv_norm[l].reshape(1, KV_LORA_RANK),
        wuq=wq.astype(BF16), wuk=wuk.astype(BF16), wuv=wuv.astype(BF16),
        wpa=w_proj_a[l].astype(BF16), wpb=w_proj_b[l].astype(BF16), wo=w_out[l].astype(BF16),
        ln1_g=ln1_g[l].reshape(1, D_MODEL), ln1_b=ln1_b[l].reshape(1, D_MODEL),
        wr_hi=wr_hi, wr_lo=(wr_t - wr_hi.astype(F32)).astype(BF16), rbias=router_bias[l].reshape(N_EXPERTS, 1),
        w_e_gate=w_e_gate[l], w_e_up=w_e_up[l], w_e_down=w_e_down[l],
        wsg=w_s_gate[l].astype(BF16), wsu=w_s_up[l].astype(BF16), wsd=w_s_down[l].astype(BF16),
        ln2_g=ln2_g[l].reshape(1, D_MODEL), ln2_b=ln2_b[l].reshape(1, D_MODEL),
    )


def _rope_tables(seq):
    pos = jnp.arange(seq, dtype=F32)
    inv_freq = ROPE_THETA ** (-jnp.arange(0, QK_ROPE_DIM, 2, dtype=F32) / QK_ROPE_DIM)
    ang = pos[:, None] * inv_freq[None, :]
    cos, sin = jnp.cos(ang), jnp.sin(ang)
    half = QK_ROPE_DIM // 2
    z = lambda n: jnp.zeros((seq, n), F32)
    c = jnp.concatenate([jnp.ones((seq, QK_NOPE_DIM), F32), cos, cos, z(LANES - QK_NOPE_DIM - QK_ROPE_DIM)], axis=1)
    s1 = jnp.concatenate([z(QK_NOPE_DIM), -sin, z(LANES - QK_NOPE_DIM - half)], axis=1)
    s2 = jnp.concatenate([z(QK_NOPE_DIM + half), sin, z(LANES - QK_NOPE_DIM - QK_ROPE_DIM)], axis=1)
    return c, s1, s2


def kernel(x, w_in, b_gate, q_norm, w_uq, kv_norm, w_ukv, w_proj_a, w_proj_b, w_out, ln1_g, ln1_b, w_router, router_bias, w_e_gate, w_e_up, w_e_down, w_s_gate, w_s_up, w_s_down, ln2_g, ln2_b):
    batch, seq, d = x.shape
    assert d == D_MODEL and seq % MOBA_BLOCK == 0 and seq // MOBA_BLOCK >= MOBA_TOPK
    assert seq // MOBA_BLOCK <= LANES and (batch * seq) % ROW_TILE == 0
    tabs = _rope_tables(seq)
    slopes = jnp.exp2(-8.0 * (jnp.arange(N_HEADS_A, dtype=F32) + 1.0) / N_HEADS_A)
    h = x.reshape(batch * seq, d)
    for l in range(DEPTH):
        wp = _prep_layer(l, w_in, b_gate, q_norm, w_uq, kv_norm, w_ukv, w_proj_a, w_proj_b, w_out, ln1_g, ln1_b,
                         w_router, router_bias, w_e_gate, w_e_up, w_e_down, w_s_gate, w_s_up, w_s_down, ln2_g, ln2_b)
        qa, ka, va, kmean, qm, km, vm, sa, sb = _inproj(h, wp, tabs, seq)
        oa = _moba(qa, ka, va, kmean, slopes, batch, seq)
        ob = _mla(qm, km, vm, batch, seq)
        x1 = _merge(h, oa, ob, sa, sb, wp)
        h = _moe(x1, wp)
    return h.reshape(batch, seq, d)
```

```python
import numpy as np

import jax
import jax.numpy as jnp
from jax import lax
from jax.experimental import pallas as pl
from jax.experimental.pallas import tpu as pltpu

D_MODEL = 1024
N_HEADS_A = 8
HEAD_DIM_A = 64
WIDTH_A = N_HEADS_A * HEAD_DIM_A
MOBA_BLOCK = 256
MOBA_TOPK = 3
N_HEADS_B = 8
QK_NOPE_DIM = 64
QK_ROPE_DIM = 32
V_HEAD_DIM = 64
Q_LORA_RANK = 384
KV_LORA_RANK = 256
WIDTH_B = N_HEADS_B * V_HEAD_DIM
ROPE_THETA = 10000.0
N_EXPERTS = 256
TOP_K = 8
N_GROUPS = 8
TOPK_GROUPS = 4
GROUP_SIZE = N_EXPERTS // N_GROUPS
D_EXPERT = 256
D_SHARED = 256
ROUTED_SCALE = 2.5
DEPTH = 2
ALPHA = (2 * DEPTH) ** 0.25
LN_EPS = 1e-5
RMS_EPS = 1e-6

LANES = 128
NEG = -1e30
ROW_TILE = 256
ATT_TILE = 256
EXPERT_BLOCK = 256
COMBINE_TILE = 128
VMEM_LIMIT = 56 * 1024 * 1024
HALF = D_MODEL // 2
BIAS_HI, BIAS_LO, POS_HI, POS_LO = 0, 32, 64, 72

BF16 = jnp.bfloat16
F32 = jnp.float32
U32 = jnp.uint32


def _dot(a, b):
    return jnp.dot(a, b, preferred_element_type=F32)


def _dot_nt(a, b):
    return lax.dot_general(a, b, (((1,), (1,)), ((), ())), preferred_element_type=F32)


def _sigmoid(x):
    return 1.0 / (1.0 + jnp.exp(-x))


def _layer_norm(y, g, b):
    mu = jnp.mean(y, axis=-1, keepdims=True)
    d = y - mu
    var = jnp.mean(d * d, axis=-1, keepdims=True)
    return d * lax.rsqrt(var + LN_EPS) * g + b


def _params(*sem):
    return pltpu.CompilerParams(dimension_semantics=sem, vmem_limit_bytes=VMEM_LIMIT)


def _full(a):
    return pl.BlockSpec(a.shape, lambda *_: (0,) * a.ndim)


def _inproj_kernel(x_ref, wqt_ref, wk_ref, wvt_ref, ones_ref, wcq_ref, wckv_ref, wkr_ref, wg_ref, bg_ref,
                   qn_ref, kvn_ref, wuqt_ref, wuk_ref, wuvt_ref, cos_ref, s1_ref, s2_ref, cost_ref, s1t_ref, s2t_ref,
                   qat_ref, ka_ref, vat_ref, kmean_ref, qmt_ref, km_ref, vmt_ref, sa_ref, sb_ref):
    xb = x_ref[...].astype(BF16)
    half = QK_ROPE_DIM // 2
    qat_ref[0] = _dot_nt(wqt_ref[...], xb).astype(BF16)
    k = _dot(xb, wk_ref[...])
    ka_ref[...] = k.astype(BF16)
    kmean_ref[0] = jnp.mean(k, axis=0, keepdims=True)
    vat_ref[0] = (_dot_nt(wvt_ref[...], xb) + ones_ref[...]).astype(BF16)

    cq = _dot(xb, wcq_ref[...])
    cqn = (cq * lax.rsqrt(jnp.mean(cq * cq, axis=-1, keepdims=True) + RMS_EPS) * qn_ref[...]).astype(BF16)
    ckv = _dot(xb, wckv_ref[...])
    ckvn = (ckv * lax.rsqrt(jnp.mean(ckv * ckv, axis=-1, keepdims=True) + RMS_EPS) * kvn_ref[...]).astype(BF16)
    qt = _dot_nt(wuqt_ref[...], cqn)
    ct, s1t, s2t = cost_ref[...], s1t_ref[...], s2t_ref[...]
    for h in range(N_HEADS_B):
        t = qt[h * LANES:(h + 1) * LANES, :]
        rot = t * ct + pltpu.roll(t, LANES - half, 0) * s1t + pltpu.roll(t, half, 0) * s2t
        qmt_ref[0, h * LANES:(h + 1) * LANES, :] = rot.astype(BF16)
    kn = _dot(ckvn, wuk_ref[...])
    kr = _dot(xb, wkr_ref[...])
    c, s1, s2 = cos_ref[...], s1_ref[...], s2_ref[...]
    krot = kr * c + pltpu.roll(kr, LANES - half, 1) * s1 + pltpu.roll(kr, half, 1) * s2
    for h in range(N_HEADS_B):
        sl = slice(h * LANES, (h + 1) * LANES)
        km_ref[:, sl] = (kn[:, sl] + krot).astype(BF16)
    vmt_ref[0] = (_dot_nt(wuvt_ref[...], ckvn) + ones_ref[...]).astype(BF16)

    sig = _sigmoid(_dot(xb, wg_ref[...]) + bg_ref[...])
    sa_ref[...] = sig[:, :D_MODEL].astype(BF16)
    sb_ref[...] = sig[:, D_MODEL:].astype(BF16)


def _inproj(x, wp, tabs, seq):
    t = x.shape[0]
    tm = ROW_TILE
    nt = t // tm
    npos = seq // tm
    row = lambda w: pl.BlockSpec((tm, w), lambda i: (i, 0))
    tile = lambda r: pl.BlockSpec((1, r, tm), lambda i: (i, 0, 0))
    tab = pl.BlockSpec((tm, LANES), lambda i: (i % npos, 0))
    tabt = pl.BlockSpec((LANES, tm), lambda i: (0, i % npos))
    weights = [wp['wqt'], wp['wk'], wp['wvt'], wp['ones'], wp['wcq'], wp['wckv'], wp['wkr'], wp['wg'], wp['bg'],
               wp['qn'], wp['kvn'], wp['wuqt'], wp['wuk'], wp['wuvt']]
    hl = N_HEADS_B * LANES
    out_shape = [
        jax.ShapeDtypeStruct((nt, WIDTH_A, tm), BF16), jax.ShapeDtypeStruct((t, WIDTH_A), BF16),
        jax.ShapeDtypeStruct((nt, N_HEADS_A * LANES, tm), BF16), jax.ShapeDtypeStruct((nt, 1, WIDTH_A), F32),
        jax.ShapeDtypeStruct((nt, hl, tm), BF16), jax.ShapeDtypeStruct((t, hl), BF16),
        jax.ShapeDtypeStruct((nt, hl, tm), BF16),
        jax.ShapeDtypeStruct((t, D_MODEL), BF16), jax.ShapeDtypeStruct((t, D_MODEL), BF16),
    ]
    out_specs = [tile(WIDTH_A), row(WIDTH_A), tile(N_HEADS_A * LANES),
                 pl.BlockSpec((1, 1, WIDTH_A), lambda i: (i, 0, 0)),
                 tile(hl), row(hl), tile(hl), row(D_MODEL), row(D_MODEL)]
    return pl.pallas_call(
        _inproj_kernel,
        grid=(nt,),
        in_specs=[row(D_MODEL)] + [_full(w) for w in weights] + [tab, tab, tab, tabt, tabt, tabt],
        out_specs=out_specs,
        out_shape=out_shape,
        compiler_params=_params("parallel"),
        name="inproj",
    )(x, *weights, *tabs)


def _attend(sts, carry, vts):
    new_m, scaled, ps = [], [], []
    for st, (m_i, acc) in zip(sts, carry):
        m_new = jnp.maximum(m_i, jnp.max(st, axis=0, keepdims=True))
        new_m.append(m_new)
        scaled.append(jnp.exp(m_i - m_new) * acc)
        ps.append(jnp.exp(st - m_new).astype(BF16))
    pvs = [_dot(vt, p) for vt, p in zip(vts, ps)]
    return tuple((m, a + pv) for m, a, pv in zip(new_m, scaled, pvs))


def _attend_init(tq):
    return jnp.full((1, tq), -jnp.inf, F32), jnp.zeros((LANES, tq), F32)


def _attention_loop(i, scores, values, causal, tq):
    heads = range(2)
    carry = _attend([jnp.where(causal, scores(hh, i), NEG) for hh in heads], [_attend_init(tq) for _ in heads],
                    [values(hh, i) for hh in heads])

    def body(j, c):
        state, sts = c
        nxt = jnp.minimum(j + 1, i - 1)
        sts_next = [scores(hh, nxt) for hh in heads]
        return _attend(sts, state, [values(hh, j) for hh in heads]), sts_next

    carry, _ = lax.fori_loop(0, i, body, (carry, [scores(hh, 0) for hh in heads]))
    outs = []
    for hh in range(2):
        _, acc = carry[hh]
        outs.append(acc[:V_HEAD_DIM, :] / acc[V_HEAD_DIM:V_HEAD_DIM + 1, :])
    return jnp.concatenate(outs, axis=0).T


def _causal_t(tq):
    key = lax.broadcasted_iota(jnp.int32, (tq, tq), 0)
    qry = lax.broadcasted_iota(jnp.int32, (tq, tq), 1)
    return key <= qry


def _moba_kernel(slope_ref, qt_ref, k_ref, e_ref, vt_ref, km_ref, o_ref):
    tq = ATT_TILE
    hp = pl.program_id(1)
    i = pl.program_id(2)
    nblk = km_ref.shape[1]
    qt2 = qt_ref[0]
    row = lax.broadcasted_iota(jnp.int32, (LANES, tq), 0)
    kmean = jnp.concatenate([km_ref[0].astype(BF16), jnp.zeros((LANES - nblk, LANES), BF16)], axis=0)

    qaug = []
    for hh in range(2):
        head = hp * 2 + hh
        head_rows = (row >= hh * HEAD_DIM_A) & (row < (hh + 1) * HEAD_DIM_A)
        qh = jnp.where(head_rows, qt2, jnp.zeros_like(qt2))
        g = jnp.where(row < i, _dot(kmean, qh), -jnp.inf)
        picked = row == i
        for r in range(MOBA_TOPK):
            m = jnp.max(g, axis=0, keepdims=True)
            idx = jnp.min(jnp.where(g == m, row, LANES), axis=0, keepdims=True)
            hit = row == idx
            picked = picked | (hit & (r < i))
            g = jnp.where(hit, -jnp.inf, g)
        dist = (i - row).astype(F32) * (slope_ref[head] * MOBA_BLOCK)
        b = jnp.where(picked, -dist, NEG)
        b_hi = b.astype(BF16)
        b_lo = (b - b_hi.astype(F32)).astype(BF16)
        one_rows = (row == POS_HI + head) | (row == POS_LO + head)
        extra = jnp.where(row < BIAS_LO, b_hi,
                          jnp.where(row < POS_HI, pltpu.roll(b_lo.astype(F32), BIAS_LO, 0).astype(BF16),
                                    jnp.where(one_rows, 1.0, 0.0).astype(BF16)))
        qaug.append(jnp.concatenate([qh, extra], axis=0))

    def scores(hh, j):
        rows = pl.ds(pl.multiple_of(j * tq, tq), tq)
        return _dot(jnp.concatenate([k_ref[rows, :], e_ref[rows, :]], axis=1), qaug[hh])

    def values(hh, j):
        return vt_ref[j, hh * LANES:(hh + 1) * LANES, :]

    o_ref[...] = _attention_loop(i, scores, values, _causal_t(tq), tq).astype(BF16)


def _moba(qat, ka, vat, kmean, etab, slopes, batch, seq):
    t = ka.shape[0]
    tq = ATT_TILE
    nq = seq // tq
    nblk = seq // MOBA_BLOCK
    npair = N_HEADS_A // 2
    km = kmean.reshape(batch, nblk, WIDTH_A)
    return pl.pallas_call(
        _moba_kernel,
        grid=(batch, npair, nq),
        in_specs=[
            pl.BlockSpec(memory_space=pltpu.SMEM),
            pl.BlockSpec((1, LANES, tq), lambda b, h, i: (b * nq + i, h, 0)),
            pl.BlockSpec((seq, LANES), lambda b, h, i: (b, h)),
            pl.BlockSpec((seq, LANES), lambda b, h, i: (0, 0)),
            pl.BlockSpec((nq, 2 * LANES, tq), lambda b, h, i: (b, h, 0)),
            pl.BlockSpec((1, nblk, LANES), lambda b, h, i: (b, 0, h)),
        ],
        out_specs=pl.BlockSpec((tq, LANES), lambda b, h, i: (b * nq + i, h)),
        out_shape=jax.ShapeDtypeStruct((t, WIDTH_A), BF16),
        compiler_params=_params("parallel", "parallel", "arbitrary"),
        name="moba",
    )(slopes, qat, ka, etab, vat, km)


def _mla_kernel(qt_ref, k_ref, vt_ref, o_ref):
    tq = ATT_TILE
    i = pl.program_id(2)
    qt = [qt_ref[0, hh * LANES:(hh + 1) * LANES, :] for hh in range(2)]

    def scores(hh, j):
        return _dot(k_ref[pl.ds(pl.multiple_of(j * tq, tq), tq), hh * LANES:(hh + 1) * LANES], qt[hh])

    def values(hh, j):
        return vt_ref[j, hh * LANES:(hh + 1) * LANES, :]

    o_ref[...] = _attention_loop(i, scores, values, _causal_t(tq), tq).astype(BF16)


def _mla(qmt, km, vmt, batch, seq):
    t = km.shape[0]
    tq = ATT_TILE
    nq = seq // tq
    npair = N_HEADS_B // 2
    return pl.pallas_call(
        _mla_kernel,
        grid=(batch, npair, nq),
        in_specs=[
            pl.BlockSpec((1, 2 * LANES, tq), lambda b, h, i: (b * nq + i, h, 0)),
            pl.BlockSpec((seq, 2 * LANES), lambda b, h, i: (b, h)),
            pl.BlockSpec((nq, 2 * LANES, tq), lambda b, h, i: (b, h, 0)),
        ],
        out_specs=pl.BlockSpec((tq, LANES), lambda b, h, i: (b * nq + i, h)),
        out_shape=jax.ShapeDtypeStruct((t, WIDTH_B), BF16),
        compiler_params=_params("parallel", "parallel", "arbitrary"),
        name="mla",
    )(qmt, km, vmt)


def _pack_halves(y):
    lo = pltpu.bitcast(y[:, :HALF].astype(BF16).astype(F32), U32)
    hi = pltpu.bitcast(y[:, HALF:].astype(BF16).astype(F32), U32)
    return (hi & jnp.uint32(0xFFFF0000)) | (lo >> 16)


def _unpack_halves(w):
    lo = pltpu.bitcast(w << 16, F32).astype(BF16)
    hi = pltpu.bitcast(w & jnp.uint32(0xFFFF0000), F32).astype(BF16)
    return lo, hi


def _merge_kernel(x_ref, oa_ref, ob_ref, sa_ref, sb_ref, wpa_ref, wpb_ref, wo_ref, g_ref, b_ref, o_ref, op_ref):
    pa = _dot(oa_ref[...], wpa_ref[...])
    pb = _dot(ob_ref[...], wpb_ref[...])
    merged = sa_ref[...].astype(F32) * pa + sb_ref[...].astype(F32) * pb
    hmix = _dot(merged.astype(BF16), wo_ref[...])
    y = _layer_norm(ALPHA * x_ref[...] + hmix, g_ref[...], b_ref[...])
    o_ref[...] = y
    op_ref[...] = _pack_halves(y)


def _merge(x, oa, ob, sa, sb, wp):
    t = x.shape[0]
    tm = ROW_TILE
    row = lambda w: pl.BlockSpec((tm, w), lambda i: (i, 0))
    weights = [wp['wpa'], wp['wpb'], wp['wo'], wp['ln1_g'], wp['ln1_b']]
    return pl.pallas_call(
        _merge_kernel,
        grid=(t // tm,),
        in_specs=[row(D_MODEL), row(WIDTH_A), row(WIDTH_B), row(D_MODEL), row(D_MODEL)] + [_full(w) for w in weights],
        out_specs=[row(D_MODEL), row(HALF)],
        out_shape=[jax.ShapeDtypeStruct((t, D_MODEL), F32), jax.ShapeDtypeStruct((t, HALF), U32)],
        compiler_params=_params("parallel"),
        name="merge",
    )(x, oa, ob, sa, sb, *weights)


def _router_kernel(x_ref, wh_ref, wl_ref, rb_ref, idx_ref, rank_ref, w_ref, cnt_ref):
    tm = x_ref.shape[0]

    @pl.when(pl.program_id(0) == 0)
    def _():
        cnt_ref[...] = jnp.zeros_like(cnt_ref)

    x = x_ref[...]
    xh = x.astype(BF16)
    xl = (x - xh.astype(F32)).astype(BF16)
    wh, wl = wh_ref[...], wl_ref[...]
    logits = _dot_nt(wh, xh) + (_dot_nt(wh, xl) + _dot_nt(wl, xh))
    scores = _sigmoid(logits)
    choice = scores + rb_ref[...]
    row = lax.broadcasted_iota(jnp.int32, (GROUP_SIZE, tm), 0)
    groups = [choice[g * GROUP_SIZE:(g + 1) * GROUP_SIZE, :] for g in range(N_GROUPS)]
    gscore = []
    for blk in groups:
        m1 = jnp.max(blk, axis=0, keepdims=True)
        first = jnp.min(jnp.where(blk == m1, row, GROUP_SIZE), axis=0, keepdims=True)
        m2 = jnp.max(jnp.where(row == first, -jnp.inf, blk), axis=0, keepdims=True)
        gscore.append(m1 + m2)
    masked = []
    for g in range(N_GROUPS):
        ahead = jnp.zeros((1, tm), jnp.int32)
        for o in range(N_GROUPS):
            if o < g:
                ahead += (gscore[o] >= gscore[g]).astype(jnp.int32)
            elif o > g:
                ahead += (gscore[o] > gscore[g]).astype(jnp.int32)
        masked.append(jnp.where(ahead < TOPK_GROUPS, groups[g], -jnp.inf))
    cur = jnp.concatenate(masked, axis=0)
    erow = lax.broadcasted_iota(jnp.int32, (N_EXPERTS, tm), 0)
    hits, idxs, ws = [], [], []
    for _ in range(TOP_K):
        m = jnp.max(cur, axis=0, keepdims=True)
        e = jnp.min(jnp.where(cur == m, erow, N_EXPERTS), axis=0, keepdims=True)
        hit = erow == e
        hits.append(hit)
        idxs.append(e)
        ws.append(jnp.sum(jnp.where(hit, scores, 0.0), axis=0, keepdims=True))
        cur = jnp.where(hit, -jnp.inf, cur)
    total = ws[0]
    for w in ws[1:]:
        total = total + w
    member = hits[0]
    for hit in hits[1:]:
        member = member | hit
    member = jnp.where(member, 1.0, 0.0).astype(BF16)
    t_src = lax.broadcasted_iota(jnp.int32, (tm, tm), 0)
    t_dst = lax.broadcasted_iota(jnp.int32, (tm, tm), 1)
    before = _dot(member, jnp.where(t_src < t_dst, 1.0, 0.0).astype(BF16))
    base = cnt_ref[...]
    before = before + jnp.concatenate([base] * (tm // LANES), axis=1)
    cnt_ref[...] = base + _dot(member, jnp.ones((tm, LANES), BF16))
    for r in range(TOP_K):
        idx_ref[0, r:r + 1, :] = idxs[r]
        rank_ref[0, r:r + 1, :] = jnp.sum(jnp.where(hits[r], before, 0.0), axis=0, keepdims=True).astype(jnp.int32)
        w_ref[r:r + 1, :] = ws[r] / total * ROUTED_SCALE


def _router(x1, wp):
    t = x1.shape[0]
    tm = ROW_TILE
    nt = t // tm
    weights = [wp['wr_hi'], wp['wr_lo'], wp['rbias']]
    tile = pl.BlockSpec((1, TOP_K, tm), lambda i: (i, 0, 0))
    return pl.pallas_call(
        _router_kernel,
        grid=(nt,),
        in_specs=[pl.BlockSpec((tm, D_MODEL), lambda i: (i, 0))] + [_full(w) for w in weights],
        out_specs=[tile, tile, pl.BlockSpec((TOP_K, tm), lambda i: (0, i)),
                   pl.BlockSpec((N_EXPERTS, LANES), lambda i: (0, 0))],
        out_shape=[jax.ShapeDtypeStruct((nt, TOP_K, tm), jnp.int32), jax.ShapeDtypeStruct((nt, TOP_K, tm), jnp.int32),
                   jax.ShapeDtypeStruct((TOP_K, t), F32), jax.ShapeDtypeStruct((N_EXPERTS, LANES), F32)],
        compiler_params=_params("arbitrary"),
        name="router",
    )(x1, *weights)


def _row_copy(src, src_row, dst, dst_row, sem):
    return pltpu.make_async_copy(src.at[pl.ds(src_row, 1), :], dst.at[pl.ds(dst_row, 1), :], sem)


def _dispatch_kernel(pstart_ref, idx_ref, rank_ref, x_ref, xs_in, xs_out, sem):
    del xs_in
    tm = x_ref.shape[0]

    def issue(r, c):
        for k in range(TOP_K):
            dst = pstart_ref[idx_ref[0, k, r]] + rank_ref[0, k, r]
            _row_copy(x_ref, r, xs_out, dst, sem).start()
        return c

    lax.fori_loop(0, tm, issue, 0)

    def drain(r, c):
        _row_copy(x_ref, 0, xs_out, 0, sem).wait()
        return c

    lax.fori_loop(0, tm * TOP_K, drain, 0, unroll=8)


def _dispatch(x1p, idx3, rank3, pstart, n_rows):
    t = x1p.shape[0]
    tm = ROW_TILE
    tile = pl.BlockSpec((1, TOP_K, tm), lambda i: (i, 0, 0), memory_space=pltpu.SMEM)
    return pl.pallas_call(
        _dispatch_kernel,
        grid=(t // tm,),
        in_specs=[pl.BlockSpec(memory_space=pltpu.SMEM), tile, tile,
                  pl.BlockSpec((tm, HALF), lambda i: (i, 0)), pl.BlockSpec(memory_space=pl.ANY)],
        out_specs=pl.BlockSpec(memory_space=pl.ANY),
        out_shape=jax.ShapeDtypeStruct((n_rows, HALF), U32),
        scratch_shapes=[pltpu.SemaphoreType.DMA(())],
        input_output_aliases={4: 0},
        compiler_params=_params("arbitrary"),
        name="moe_dispatch",
    )(pstart, idx3, rank3, x1p, jnp.zeros((n_rows, HALF), U32))


def _expert_kernel(be_ref, nused_ref, x_ref, wg_ref, wu_ref, wd_ref, y_ref):
    @pl.when(pl.program_id(0) < nused_ref[0])
    def _():
        xlo, xhi = _unpack_halves(x_ref[...])
        wg = wg_ref[0, 0].astype(BF16)
        wu = wu_ref[0, 0].astype(BF16)
        g = _dot(xlo, wg[:HALF]) + _dot(xhi, wg[HALF:])
        u = _dot(xlo, wu[:HALF]) + _dot(xhi, wu[HALF:])
        a = (g * _sigmoid(g) * u).astype(BF16)
        y_ref[...] = _dot(a, wd_ref[0, 0].astype(BF16))

    @pl.when(pl.program_id(0) >= nused_ref[0])
    def _():
        y_ref[...] = jnp.zeros_like(y_ref)


def _experts(xs, block_e, nused, w_gate, w_up, w_down, layer, n_blocks):
    rows = EXPERT_BLOCK
    used = lambda b, be, nu: jnp.minimum(b, nu[0] - 1)
    grid_spec = pltpu.PrefetchScalarGridSpec(
        num_scalar_prefetch=2,
        grid=(n_blocks,),
        in_specs=[
            pl.BlockSpec((rows, HALF), lambda b, be, nu: (used(b, be, nu), 0)),
            pl.BlockSpec((1, 1, D_MODEL, D_EXPERT), lambda b, be, nu: (layer, be[b], 0, 0)),
            pl.BlockSpec((1, 1, D_MODEL, D_EXPERT), lambda b, be, nu: (layer, be[b], 0, 0)),
            pl.BlockSpec((1, 1, D_EXPERT, D_MODEL), lambda b, be, nu: (layer, be[b], 0, 0)),
        ],
        out_specs=pl.BlockSpec((rows, D_MODEL), lambda b, be, nu: (b, 0)),
    )
    return pl.pallas_call(
        _expert_kernel,
        grid_spec=grid_spec,
        out_shape=jax.ShapeDtypeStruct((n_blocks * rows, D_MODEL), F32),
        compiler_params=_params("arbitrary"),
        name="moe_experts",
    )(block_e, nused, xs, w_gate, w_up, w_down)


def _combine_kernel(pstart_ref, idx_ref, rank_ref, x_ref, w_ref, wsg_ref, wsu_ref, wsd_ref, g_ref, b_ref, y_hbm,
                    o_ref, buf, sem):
    tm = x_ref.shape[0]

    def issue(r, c):
        for k in range(TOP_K):
            src = pstart_ref[idx_ref[0, k, r]] + rank_ref[0, k, r]
            _row_copy(y_hbm, src, buf, k * tm + r, sem).start()
        return c

    lax.fori_loop(0, tm, issue, 0)

    x = x_ref[...]
    xb = x.astype(BF16)
    g = _dot(xb, wsg_ref[...])
    u = _dot(xb, wsu_ref[...])
    shared = _dot((g * _sigmoid(g) * u).astype(BF16), wsd_ref[...])

    def drain(r, c):
        _row_copy(y_hbm, 0, buf, r, sem).wait()
        return c

    lax.fori_loop(0, tm * TOP_K, drain, 0, unroll=8)

    w = w_ref[...]
    routed = buf[0:tm, :] * w[:, 0:1]
    for k in range(1, TOP_K):
        routed = routed + buf[k * tm:(k + 1) * tm, :] * w[:, k:k + 1]
    o_ref[...] = _layer_norm(ALPHA * x + (shared + routed), g_ref[...], b_ref[...])


def _combine(x1, ys, idx3, rank3, w_tk, pstart, wp):
    t = x1.shape[0]
    tm = COMBINE_TILE
    per = ROW_TILE // tm
    tile = pl.BlockSpec((1, TOP_K, tm), lambda i: (i // per, 0, i % per), memory_space=pltpu.SMEM)
    weights = [wp['wsg'], wp['wsu'], wp['wsd'], wp['ln2_g'], wp['ln2_b']]
    return pl.pallas_call(
        _combine_kernel,
        grid=(t // tm,),
        in_specs=[pl.BlockSpec(memory_space=pltpu.SMEM), tile, tile,
                  pl.BlockSpec((tm, D_MODEL), lambda i: (i, 0)), pl.BlockSpec((tm, TOP_K), lambda i: (i, 0))]
                 + [_full(w) for w in weights] + [pl.BlockSpec(memory_space=pl.ANY)],
        out_specs=pl.BlockSpec((tm, D_MODEL), lambda i: (i, 0)),
        out_shape=jax.ShapeDtypeStruct((t, D_MODEL), F32),
        scratch_shapes=[pltpu.VMEM((TOP_K * tm, D_MODEL), F32), pltpu.SemaphoreType.DMA(())],
        compiler_params=_params("arbitrary"),
        name="moe_combine",
    )(pstart, idx3, rank3, x1, w_tk, *weights, ys)


def _moe(x1, x1p, wp, w_e_gate, w_e_up, w_e_down, layer):
    t = x1.shape[0]
    rows = EXPERT_BLOCK
    n_blocks = -(-(t * TOP_K + N_EXPERTS * (rows - 1)) // rows)
    idx3, rank3, top_w, cnt = _router(x1, wp)
    counts = cnt[:, 0].astype(jnp.int32)
    padded = (counts + rows - 1) // rows * rows
    padded_end = jnp.cumsum(padded)
    pstart = (padded_end - padded).astype(jnp.int32)
    nused = (padded_end[-1] // rows).astype(jnp.int32).reshape(1)
    blocks = jnp.arange(n_blocks, dtype=jnp.int32)
    block_e = jnp.searchsorted(padded_end, jnp.minimum(blocks, nused[0] - 1) * rows, side='right')
    block_e = jnp.clip(block_e, 0, N_EXPERTS - 1).astype(jnp.int32)
    xs = _dispatch(x1p, idx3, rank3, pstart, n_blocks * rows)
    ys = _experts(xs, block_e, nused, w_e_gate, w_e_up, w_e_down, layer, n_blocks)
    return _combine(x1, ys, idx3, rank3, top_w.T, pstart, wp)


def _head_groups_t(w, used):
    k, h, _ = w.shape
    return jnp.pad(w, ((0, 0), (0, 0), (0, LANES - used))).reshape(k, h * LANES).T


def _prep_layer(l, w_in, b_gate, q_norm, w_uq, kv_norm, w_ukv, w_proj_a, w_proj_b, w_out, ln1_g, ln1_b,
                w_router, router_bias, w_s_gate, w_s_up, w_s_down, ln2_g, ln2_b):
    w = w_in[l]
    o = 0
    cols = {}
    for name, width in (('qa', WIDTH_A), ('ka', WIDTH_A), ('va', WIDTH_A), ('cq', Q_LORA_RANK),
                        ('ckv', KV_LORA_RANK), ('kr', QK_ROPE_DIM), ('ga', D_MODEL), ('gb', D_MODEL)):
        cols[name] = w[:, o:o + width]
        o += width
    wkr = jnp.zeros((D_MODEL, LANES), F32).at[:, QK_NOPE_DIM:QK_NOPE_DIM + QK_ROPE_DIM].set(cols['kr'])
    dqk = QK_NOPE_DIM + QK_ROPE_DIM
    wq = w_uq[l].reshape(Q_LORA_RANK, N_HEADS_B, dqk) * dqk ** -0.5
    wkv = w_ukv[l].reshape(KV_LORA_RANK, N_HEADS_B, QK_NOPE_DIM + V_HEAD_DIM)
    wuk = jnp.pad(wkv[:, :, :QK_NOPE_DIM], ((0, 0), (0, 0), (0, LANES - QK_NOPE_DIM))).reshape(KV_LORA_RANK, N_HEADS_B * LANES)
    ones = jnp.zeros((N_HEADS_B, LANES), F32).at[:, V_HEAD_DIM].set(1.0).reshape(N_HEADS_B * LANES, 1)
    wr_t = w_router[l].T
    wr_hi = wr_t.astype(BF16)
    return dict(
        wqt=(cols['qa'] * HEAD_DIM_A ** -0.5).T.astype(BF16), wk=cols['ka'].astype(BF16),
        wvt=_head_groups_t(cols['va'].reshape(D_MODEL, N_HEADS_A, HEAD_DIM_A), HEAD_DIM_A).astype(BF16), ones=ones,
        wcq=cols['cq'].astype(BF16), wckv=cols['ckv'].astype(BF16), wkr=wkr.astype(BF16),
        wg=jnp.concatenate([cols['ga'], cols['gb']], axis=1).astype(BF16),
        bg=b_gate[l].reshape(1, 2 * D_MODEL), qn=q_norm[l].reshape(1, Q_LORA_RANK), kvn=kv_norm[l].reshape(1, KV_LORA_RANK),
        wuqt=_head_groups_t(wq, dqk).astype(BF16), wuk=wuk.astype(BF16),
        wuvt=_head_groups_t(wkv[:, :, QK_NOPE_DIM:], V_HEAD_DIM).astype(BF16),
        wpa=w_proj_a[l].astype(BF16), wpb=w_proj_b[l].astype(BF16), wo=w_out[l].astype(BF16),
        ln1_g=ln1_g[l].reshape(1, D_MODEL), ln1_b=ln1_b[l].reshape(1, D_MODEL),
        wr_hi=wr_hi, wr_lo=(wr_t - wr_hi.astype(F32)).astype(BF16), rbias=router_bias[l].reshape(N_EXPERTS, 1),
        wsg=w_s_gate[l].astype(BF16), wsu=w_s_up[l].astype(BF16), wsd=w_s_down[l].astype(BF16),
        ln2_g=ln2_g[l].reshape(1, D_MODEL), ln2_b=ln2_b[l].reshape(1, D_MODEL),
    )


def _rope_tables(seq):
    pos = jnp.arange(seq, dtype=F32)
    inv_freq = ROPE_THETA ** (-jnp.arange(0, QK_ROPE_DIM, 2, dtype=F32) / QK_ROPE_DIM)
    ang = pos[:, None] * inv_freq[None, :]
    cos, sin = jnp.cos(ang), jnp.sin(ang)
    half = QK_ROPE_DIM // 2
    z = lambda n: jnp.zeros((seq, n), F32)
    c = jnp.concatenate([jnp.ones((seq, QK_NOPE_DIM), F32), cos, cos, z(LANES - QK_NOPE_DIM - QK_ROPE_DIM)], axis=1)
    s1 = jnp.concatenate([z(QK_NOPE_DIM), -sin, z(LANES - QK_NOPE_DIM - half)], axis=1)
    s2 = jnp.concatenate([z(QK_NOPE_DIM + half), sin, z(LANES - QK_NOPE_DIM - QK_ROPE_DIM)], axis=1)
    return c, s1, s2, c.T, s1.T, s2.T


def _moba_key_table(seq, slopes):
    blk = jnp.arange(seq, dtype=jnp.int32) // MOBA_BLOCK
    onehot = (blk[:, None] == jnp.arange(BIAS_LO, dtype=jnp.int32)[None, :]).astype(F32)
    inblk = (jnp.arange(seq, dtype=jnp.int32) % MOBA_BLOCK).astype(F32)[:, None] * slopes[None, :]
    hi = inblk.astype(BF16)
    lo = (inblk - hi.astype(F32)).astype(BF16)
    pad = jnp.zeros((seq, LANES - POS_LO - N_HEADS_A), BF16)
    return jnp.concatenate([onehot.astype(BF16), onehot.astype(BF16), hi, lo, pad], axis=1)


def kernel(x, w_in, b_gate, q_norm, w_uq, kv_norm, w_ukv, w_proj_a, w_proj_b, w_out, ln1_g, ln1_b, w_router, router_bias, w_e_gate, w_e_up, w_e_down, w_s_gate, w_s_up, w_s_down, ln2_g, ln2_b):
    batch, seq, d = x.shape
    assert d == D_MODEL and seq % MOBA_BLOCK == 0 and MOBA_TOPK <= seq // MOBA_BLOCK <= BIAS_LO
    assert POS_LO + N_HEADS_A <= LANES and POS_HI + N_HEADS_A <= POS_LO
    tabs = _rope_tables(seq)
    slopes = jnp.asarray(np.exp2(-8.0 * (np.arange(N_HEADS_A) + 1.0) / N_HEADS_A), F32)
    etab = _moba_key_table(seq, slopes)
    h = x.reshape(batch * seq, d)
    for l in range(DEPTH):
        wp = _prep_layer(l, w_in, b_gate, q_norm, w_uq, kv_norm, w_ukv, w_proj_a, w_proj_b, w_out, ln1_g, ln1_b,
                         w_router, router_bias, w_s_gate, w_s_up, w_s_down, ln2_g, ln2_b)
        qat, ka, vat, kmean, qmt, km, vmt, sa, sb = _inproj(h, wp, tabs, seq)
        oa = _moba(qat, ka, vat, kmean, etab, slopes, batch, seq)
        ob = _mla(qmt, km, vmt, batch, seq)
        x1, x1p = _merge(h, oa, ob, sa, sb, wp)
        h = _moe(x1, x1p, wp, w_e_gate, w_e_up, w_e_down, l)
    return h.reshape(batch, seq, d)
```

```python
import numpy as np

import jax
import jax.numpy as jnp
from jax import lax
from jax.experimental import pallas as pl
from jax.experimental.pallas import tpu as pltpu

D_MODEL = 1024
N_HEADS_A = 8
HEAD_DIM_A = 64
WIDTH_A = N_HEADS_A * HEAD_DIM_A
MOBA_BLOCK = 256
MOBA_TOPK = 3
N_HEADS_B = 8
QK_NOPE_DIM = 64
QK_ROPE_DIM = 32
V_HEAD_DIM = 64
Q_LORA_RANK = 384
KV_LORA_RANK = 256
WIDTH_B = N_HEADS_B * V_HEAD_DIM
ROPE_THETA = 10000.0
N_EXPERTS = 256
TOP_K = 8
N_GROUPS = 8
TOPK_GROUPS = 4
GROUP_SIZE = N_EXPERTS // N_GROUPS
D_EXPERT = 256
D_SHARED = 256
ROUTED_SCALE = 2.5
DEPTH = 2
ALPHA = (2 * DEPTH) ** 0.25
LN_EPS = 1e-5
RMS_EPS = 1e-6

LANES = 128
NEG = -1e30
ROW_TILE = 256
ATT_TILE = 256
EXPERT_BLOCK = 256
COMBINE_TILE = 128
VMEM_LIMIT = 56 * 1024 * 1024
HALF = D_MODEL // 2
PV_ROWS = 80
BIAS_HI, BIAS_LO, POS_HI, POS_LO = 0, 32, 64, 72

BF16 = jnp.bfloat16
F32 = jnp.float32
U32 = jnp.uint32


def _dot(a, b):
    return jnp.dot(a, b, preferred_element_type=F32)


def _dot_nt(a, b):
    return lax.dot_general(a, b, (((1,), (1,)), ((), ())), preferred_element_type=F32)


def _sigmoid(x):
    return 1.0 / (1.0 + jnp.exp(-x))


def _layer_norm(y, g, b):
    mu = jnp.mean(y, axis=-1, keepdims=True)
    d = y - mu
    var = jnp.mean(d * d, axis=-1, keepdims=True)
    return d * lax.rsqrt(var + LN_EPS) * g + b


def _params(*sem):
    return pltpu.CompilerParams(dimension_semantics=sem, vmem_limit_bytes=VMEM_LIMIT)


def _full(a):
    return pl.BlockSpec(a.shape, lambda *_: (0,) * a.ndim)


def _inproj_kernel(x_ref, wqt_ref, wk_ref, wvt_ref, ones_ref, wcq_ref, wckv_ref, wkr_ref, wg_ref, bg_ref,
                   qn_ref, kvn_ref, wuqt_ref, wuk_ref, wuvt_ref, cos_ref, s1_ref, s2_ref, cost_ref, s1t_ref, s2t_ref,
                   qat_ref, ka_ref, vat_ref, kmean_ref, qmt_ref, km_ref, vmt_ref, sa_ref, sb_ref):
    xb = x_ref[...].astype(BF16)
    half = QK_ROPE_DIM // 2
    qat_ref[0] = _dot_nt(wqt_ref[...], xb).astype(BF16)
    k = _dot(xb, wk_ref[...])
    ka_ref[...] = k.astype(BF16)
    kmean_ref[0] = jnp.mean(k, axis=0, keepdims=True)
    vat_ref[0] = (_dot_nt(wvt_ref[...], xb) + ones_ref[...]).astype(BF16)

    cq = _dot(xb, wcq_ref[...])
    cqn = (cq * lax.rsqrt(jnp.mean(cq * cq, axis=-1, keepdims=True) + RMS_EPS) * qn_ref[...]).astype(BF16)
    ckv = _dot(xb, wckv_ref[...])
    ckvn = (ckv * lax.rsqrt(jnp.mean(ckv * ckv, axis=-1, keepdims=True) + RMS_EPS) * kvn_ref[...]).astype(BF16)
    qt = _dot_nt(wuqt_ref[...], cqn)
    ct, s1t, s2t = cost_ref[...], s1t_ref[...], s2t_ref[...]
    for h in range(N_HEADS_B):
        t = qt[h * LANES:(h + 1) * LANES, :]
        rot = t * ct + pltpu.roll(t, LANES - half, 0) * s1t + pltpu.roll(t, half, 0) * s2t
        qmt_ref[0, h * LANES:(h + 1) * LANES, :] = rot.astype(BF16)
    kn = _dot(ckvn, wuk_ref[...])
    kr = _dot(xb, wkr_ref[...])
    c, s1, s2 = cos_ref[...], s1_ref[...], s2_ref[...]
    krot = kr * c + pltpu.roll(kr, LANES - half, 1) * s1 + pltpu.roll(kr, half, 1) * s2
    for h in range(N_HEADS_B):
        sl = slice(h * LANES, (h + 1) * LANES)
        km_ref[:, sl] = (kn[:, sl] + krot).astype(BF16)
    vmt_ref[0] = (_dot_nt(wuvt_ref[...], ckvn) + ones_ref[...]).astype(BF16)

    sig = _sigmoid(_dot(xb, wg_ref[...]) + bg_ref[...])
    sa_ref[...] = sig[:, :D_MODEL].astype(BF16)
    sb_ref[...] = sig[:, D_MODEL:].astype(BF16)


def _inproj(x, wp, tabs, seq):
    t = x.shape[0]
    tm = ROW_TILE
    nt = t // tm
    npos = seq // tm
    row = lambda w: pl.BlockSpec((tm, w), lambda i: (i, 0))
    tile = lambda r: pl.BlockSpec((1, r, tm), lambda i: (i, 0, 0))
    tab = pl.BlockSpec((tm, LANES), lambda i: (i % npos, 0))
    tabt = pl.BlockSpec((LANES, tm), lambda i: (0, i % npos))
    weights = [wp['wqt'], wp['wk'], wp['wvt'], wp['ones'], wp['wcq'], wp['wckv'], wp['wkr'], wp['wg'], wp['bg'],
               wp['qn'], wp['kvn'], wp['wuqt'], wp['wuk'], wp['wuvt']]
    hl = N_HEADS_B * LANES
    out_shape = [
        jax.ShapeDtypeStruct((nt, WIDTH_A, tm), BF16), jax.ShapeDtypeStruct((t, WIDTH_A), BF16),
        jax.ShapeDtypeStruct((nt, N_HEADS_A * LANES, tm), BF16), jax.ShapeDtypeStruct((nt, 1, WIDTH_A), F32),
        jax.ShapeDtypeStruct((nt, hl, tm), BF16), jax.ShapeDtypeStruct((t, hl), BF16),
        jax.ShapeDtypeStruct((nt, hl, tm), BF16),
        jax.ShapeDtypeStruct((t, D_MODEL), BF16), jax.ShapeDtypeStruct((t, D_MODEL), BF16),
    ]
    out_specs = [tile(WIDTH_A), row(WIDTH_A), tile(N_HEADS_A * LANES),
                 pl.BlockSpec((1, 1, WIDTH_A), lambda i: (i, 0, 0)),
                 tile(hl), row(hl), tile(hl), row(D_MODEL), row(D_MODEL)]
    return pl.pallas_call(
        _inproj_kernel,
        grid=(nt,),
        in_specs=[row(D_MODEL)] + [_full(w) for w in weights] + [tab, tab, tab, tabt, tabt, tabt],
        out_specs=out_specs,
        out_shape=out_shape,
        compiler_params=_params("parallel"),
        name="inproj",
    )(x, *weights, *tabs)


def _attend_init(tq):
    return jnp.full((1, tq), -jnp.inf, F32), jnp.zeros((PV_ROWS, tq), F32)


def _col_max(s_ref):
    return [jnp.max(s_ref[hh], axis=0, keepdims=True) for hh in range(2)]


def _attend_staged(cur_ref, cur_max, state, vts, nxt_ref=None, next_scores=None):
    heads = range(2)
    if nxt_ref is not None:
        for hh in heads:
            nxt_ref[hh] = next_scores(hh)
    new_m, scaled, pvs = [], [], []
    for hh in heads:
        m_i, acc = state[hh]
        m_new = jnp.maximum(m_i, cur_max[hh])
        new_m.append(m_new)
        scaled.append(jnp.exp(m_i - m_new) * acc)
        pvs.append(_dot(vts[hh], jnp.exp(cur_ref[hh] - m_new).astype(BF16)))
    nxt_max = _col_max(nxt_ref) if nxt_ref is not None else cur_max
    return tuple((m, a + pv) for m, a, pv in zip(new_m, scaled, pvs)), nxt_max


def _attention_loop(i, scores, values, causal, tq, sa_ref, sb_ref):
    heads = range(2)
    for hh in heads:
        sa_ref[hh] = jnp.where(causal, scores(hh, i), NEG)
    max_a = _col_max(sa_ref)
    state = tuple(_attend_init(tq) for _ in heads)
    npair = (i + 1) // 2

    def pair(n, c):
        state, max_a = c
        t0 = 2 * n
        first = jnp.where(n == 0, i, t0 - 1)
        state, max_b = _attend_staged(sa_ref, max_a, state, [values(hh, first) for hh in heads],
                                      sb_ref, lambda hh: scores(hh, t0))
        state, max_a = _attend_staged(sb_ref, max_b, state, [values(hh, t0) for hh in heads],
                                      sa_ref, lambda hh: scores(hh, jnp.minimum(t0 + 1, i - 1)))
        return state, max_a

    state, max_a = lax.fori_loop(0, npair, pair, (state, max_a))
    last = jnp.where(i == 0, i, i - 1)
    state = lax.cond(i % 2 == 0,
                     lambda s: _attend_staged(sa_ref, max_a, s, [values(hh, last) for hh in heads])[0],
                     lambda s: s, state)
    outs = []
    for hh in heads:
        _, acc = state[hh]
        outs.append(acc[:V_HEAD_DIM, :] / acc[V_HEAD_DIM:V_HEAD_DIM + 1, :])
    return jnp.concatenate(outs, axis=0).T


def _causal_t(tq):
    key = lax.broadcasted_iota(jnp.int32, (tq, tq), 0)
    qry = lax.broadcasted_iota(jnp.int32, (tq, tq), 1)
    return key <= qry


def _moba_kernel(slope_ref, qt_ref, k_ref, e_ref, vt_ref, km_ref, o_ref, sa_ref, sb_ref):
    tq = ATT_TILE
    hp = pl.program_id(1)
    i = pl.program_id(2)
    nblk = km_ref.shape[1]
    qt2 = qt_ref[0]
    row = lax.broadcasted_iota(jnp.int32, (LANES, tq), 0)
    kmean = jnp.concatenate([km_ref[0].astype(BF16), jnp.zeros((LANES - nblk, LANES), BF16)], axis=0)

    qaug = []
    for hh in range(2):
        head = hp * 2 + hh
        head_rows = (row >= hh * HEAD_DIM_A) & (row < (hh + 1) * HEAD_DIM_A)
        qh = jnp.where(head_rows, qt2, jnp.zeros_like(qt2))
        g = jnp.where(row < i, _dot(kmean, qh), -jnp.inf)
        picked = row == i
        for r in range(MOBA_TOPK):
            m = jnp.max(g, axis=0, keepdims=True)
            idx = jnp.min(jnp.where(g == m, row, LANES), axis=0, keepdims=True)
            hit = row == idx
            picked = picked | (hit & (r < i))
            g = jnp.where(hit, -jnp.inf, g)
        dist = (i - row).astype(F32) * (slope_ref[head] * MOBA_BLOCK)
        b = jnp.where(picked, -dist, NEG)
        b_hi = b.astype(BF16)
        b_lo = (b - b_hi.astype(F32)).astype(BF16)
        one_rows = (row == POS_HI + head) | (row == POS_LO + head)
        extra = jnp.where(row < BIAS_LO, b_hi,
                          jnp.where(row < POS_HI, pltpu.roll(b_lo.astype(F32), BIAS_LO, 0).astype(BF16),
                                    jnp.where(one_rows, 1.0, 0.0).astype(BF16)))
        qaug.append(jnp.concatenate([qh, extra], axis=0))

    def scores(hh, j):
        rows = pl.ds(pl.multiple_of(j * tq, tq), tq)
        return _dot(jnp.concatenate([k_ref[rows, :], e_ref[rows, :]], axis=1), qaug[hh])

    def values(hh, j):
        return vt_ref[j, hh * LANES:hh * LANES + PV_ROWS, :]

    o_ref[...] = _attention_loop(i, scores, values, _causal_t(tq), tq, sa_ref, sb_ref).astype(BF16)


def _moba(qat, ka, vat, kmean, etab, slopes, batch, seq):
    t = ka.shape[0]
    tq = ATT_TILE
    nq = seq // tq
    nblk = seq // MOBA_BLOCK
    npair = N_HEADS_A // 2
    km = kmean.reshape(batch, nblk, WIDTH_A)
    return pl.pallas_call(
        _moba_kernel,
        grid=(batch, npair, nq),
        in_specs=[
            pl.BlockSpec(memory_space=pltpu.SMEM),
            pl.BlockSpec((1, LANES, tq), lambda b, h, i: (b * nq + i, h, 0)),
            pl.BlockSpec((seq, LANES), lambda b, h, i: (b, h)),
            pl.BlockSpec((seq, LANES), lambda b, h, i: (0, 0)),
            pl.BlockSpec((nq, 2 * LANES, tq), lambda b, h, i: (b, h, 0)),
            pl.BlockSpec((1, nblk, LANES), lambda b, h, i: (b, 0, h)),
        ],
        out_specs=pl.BlockSpec((tq, LANES), lambda b, h, i: (b * nq + i, h)),
        out_shape=jax.ShapeDtypeStruct((t, WIDTH_A), BF16),
        scratch_shapes=[pltpu.VMEM((2, tq, tq), F32), pltpu.VMEM((2, tq, tq), F32)],
        compiler_params=_params("parallel", "parallel", "arbitrary"),
        name="moba",
    )(slopes, qat, ka, etab, vat, km)


def _mla_kernel(qt_ref, k_ref, vt_ref, o_ref, sa_ref, sb_ref):
    tq = ATT_TILE
    i = pl.program_id(2)
    qt = [qt_ref[0, hh * LANES:(hh + 1) * LANES, :] for hh in range(2)]

    def scores(hh, j):
        return _dot(k_ref[pl.ds(pl.multiple_of(j * tq, tq), tq), hh * LANES:(hh + 1) * LANES], qt[hh])

    def values(hh, j):
        return vt_ref[j, hh * LANES:hh * LANES + PV_ROWS, :]

    o_ref[...] = _attention_loop(i, scores, values, _causal_t(tq), tq, sa_ref, sb_ref).astype(BF16)


def _mla(qmt, km, vmt, batch, seq):
    t = km.shape[0]
    tq = ATT_TILE
    nq = seq // tq
    npair = N_HEADS_B // 2
    return pl.pallas_call(
        _mla_kernel,
        grid=(batch, npair, nq),
        in_specs=[
            pl.BlockSpec((1, 2 * LANES, tq), lambda b, h, i: (b * nq + i, h, 0)),
            pl.BlockSpec((seq, 2 * LANES), lambda b, h, i: (b, h)),
            pl.BlockSpec((nq, 2 * LANES, tq), lambda b, h, i: (b, h, 0)),
        ],
        out_specs=pl.BlockSpec((tq, LANES), lambda b, h, i: (b * nq + i, h)),
        out_shape=jax.ShapeDtypeStruct((t, WIDTH_B), BF16),
        scratch_shapes=[pltpu.VMEM((2, tq, tq), F32), pltpu.VMEM((2, tq, tq), F32)],
        compiler_params=_params("parallel", "parallel", "arbitrary"),
        name="mla",
    )(qmt, km, vmt)


def _pack_halves(y):
    lo = pltpu.bitcast(y[:, :HALF].astype(BF16).astype(F32), U32)
    hi = pltpu.bitcast(y[:, HALF:].astype(BF16).astype(F32), U32)
    return (hi & jnp.uint32(0xFFFF0000)) | (lo >> 16)


def _unpack_halves(w):
    return pltpu.bitcast(w << 16, F32), pltpu.bitcast(w & jnp.uint32(0xFFFF0000), F32)


def _merge_kernel(x_ref, oa_ref, ob_ref, sa_ref, sb_ref, wpa_ref, wpb_ref, wo_ref, g_ref, b_ref, o_ref, op_ref):
    pa = _dot(oa_ref[...], wpa_ref[...])
    pb = _dot(ob_ref[...], wpb_ref[...])
    merged = sa_ref[...].astype(F32) * pa + sb_ref[...].astype(F32) * pb
    hmix = _dot(merged.astype(BF16), wo_ref[...])
    y = _layer_norm(ALPHA * x_ref[...] + hmix, g_ref[...], b_ref[...])
    o_ref[...] = y
    op_ref[...] = _pack_halves(y)


def _merge(x, oa, ob, sa, sb, wp):
    t = x.shape[0]
    tm = ROW_TILE
    row = lambda w: pl.BlockSpec((tm, w), lambda i: (i, 0))
    weights = [wp['wpa'], wp['wpb'], wp['wo'], wp['ln1_g'], wp['ln1_b']]
    return pl.pallas_call(
        _merge_kernel,
        grid=(t // tm,),
        in_specs=[row(D_MODEL), row(WIDTH_A), row(WIDTH_B), row(D_MODEL), row(D_MODEL)] + [_full(w) for w in weights],
        out_specs=[row(D_MODEL), row(HALF)],
        out_shape=[jax.ShapeDtypeStruct((t, D_MODEL), F32), jax.ShapeDtypeStruct((t, HALF), U32)],
        compiler_params=_params("parallel"),
        name="merge",
    )(x, oa, ob, sa, sb, *weights)


def _router_kernel(x_ref, wh_ref, wl_ref, rb_ref, idx_ref, rank_ref, w_ref, cnt_ref):
    tm = x_ref.shape[0]

    @pl.when(pl.program_id(0) == 0)
    def _():
        cnt_ref[...] = jnp.zeros_like(cnt_ref)

    x = x_ref[...]
    xh = x.astype(BF16)
    xl = (x - xh.astype(F32)).astype(BF16)
    wh, wl = wh_ref[...], wl_ref[...]
    logits = _dot_nt(wh, xh) + (_dot_nt(wh, xl) + _dot_nt(wl, xh))
    scores = _sigmoid(logits)
    choice = scores + rb_ref[...]
    row = lax.broadcasted_iota(jnp.int32, (GROUP_SIZE, tm), 0)
    groups = [choice[g * GROUP_SIZE:(g + 1) * GROUP_SIZE, :] for g in range(N_GROUPS)]
    gscore = []
    for blk in groups:
        m1 = jnp.max(blk, axis=0, keepdims=True)
        first = jnp.min(jnp.where(blk == m1, row, GROUP_SIZE), axis=0, keepdims=True)
        m2 = jnp.max(jnp.where(row == first, -jnp.inf, blk), axis=0, keepdims=True)
        gscore.append(m1 + m2)
    masked = []
    for g in range(N_GROUPS):
        ahead = jnp.zeros((1, tm), jnp.int32)
        for o in range(N_GROUPS):
            if o < g:
                ahead += (gscore[o] >= gscore[g]).astype(jnp.int32)
            elif o > g:
                ahead += (gscore[o] > gscore[g]).astype(jnp.int32)
        masked.append(jnp.where(ahead < TOPK_GROUPS, groups[g], -jnp.inf))
    cur = jnp.concatenate(masked, axis=0)
    erow = lax.broadcasted_iota(jnp.int32, (N_EXPERTS, tm), 0)
    hits, idxs, ws = [], [], []
    for _ in range(TOP_K):
        m = jnp.max(cur, axis=0, keepdims=True)
        e = jnp.min(jnp.where(cur == m, erow, N_EXPERTS), axis=0, keepdims=True)
        hit = erow == e
        hits.append(hit)
        idxs.append(e)
        ws.append(jnp.sum(jnp.where(hit, scores, 0.0), axis=0, keepdims=True))
        cur = jnp.where(hit, -jnp.inf, cur)
    total = ws[0]
    for w in ws[1:]:
        total = total + w
    member = hits[0]
    for hit in hits[1:]:
        member = member | hit
    member = jnp.where(member, 1.0, 0.0).astype(BF16)
    t_src = lax.broadcasted_iota(jnp.int32, (tm, tm), 0)
    t_dst = lax.broadcasted_iota(jnp.int32, (tm, tm), 1)
    before = _dot(member, jnp.where(t_src < t_dst, 1.0, 0.0).astype(BF16))
    base = cnt_ref[...]
    before = before + jnp.concatenate([base] * (tm // LANES), axis=1)
    cnt_ref[...] = base + _dot(member, jnp.ones((tm, LANES), BF16))
    for r in range(TOP_K):
        idx_ref[0, r:r + 1, :] = idxs[r]
        rank_ref[0, r:r + 1, :] = jnp.sum(jnp.where(hits[r], before, 0.0), axis=0, keepdims=True).astype(jnp.int32)
        w_ref[r:r + 1, :] = ws[r] / total * ROUTED_SCALE


def _router(x1, wp):
    t = x1.shape[0]
    tm = ROW_TILE
    nt = t // tm
    weights = [wp['wr_hi'], wp['wr_lo'], wp['rbias']]
    tile = pl.BlockSpec((1, TOP_K, tm), lambda i: (i, 0, 0))
    return pl.pallas_call(
        _router_kernel,
        grid=(nt,),
        in_specs=[pl.BlockSpec((tm, D_MODEL), lambda i: (i, 0))] + [_full(w) for w in weights],
        out_specs=[tile, tile, pl.BlockSpec((TOP_K, tm), lambda i: (0, i)),
                   pl.BlockSpec((N_EXPERTS, LANES), lambda i: (0, 0))],
        out_shape=[jax.ShapeDtypeStruct((nt, TOP_K, tm), jnp.int32), jax.ShapeDtypeStruct((nt, TOP_K, tm), jnp.int32),
                   jax.ShapeDtypeStruct((TOP_K, t), F32), jax.ShapeDtypeStruct((N_EXPERTS, LANES), F32)],
        compiler_params=_params("arbitrary"),
        name="router",
    )(x1, *weights)


def _row_copy(src, src_row, dst, dst_row, sem):
    return pltpu.make_async_copy(src.at[pl.ds(src_row, 1), :], dst.at[pl.ds(dst_row, 1), :], sem)


def _dispatch_kernel(pstart_ref, idx_ref, rank_ref, x_ref, xs_in, xs_out, sem):
    del xs_in
    tm = x_ref.shape[0]

    def issue(r, c):
        for k in range(TOP_K):
            dst = pstart_ref[idx_ref[0, k, r]] + rank_ref[0, k, r]
            _row_copy(x_ref, r, xs_out, dst, sem).start()
        return c

    lax.fori_loop(0, tm, issue, 0)

    def drain(r, c):
        _row_copy(x_ref, 0, xs_out, 0, sem).wait()
        return c

    lax.fori_loop(0, tm * TOP_K, drain, 0, unroll=8)


def _dispatch(x1p, idx3, rank3, pstart, n_rows):
    t = x1p.shape[0]
    tm = ROW_TILE
    tile = pl.BlockSpec((1, TOP_K, tm), lambda i: (i, 0, 0), memory_space=pltpu.SMEM)
    return pl.pallas_call(
        _dispatch_kernel,
        grid=(t // tm,),
        in_specs=[pl.BlockSpec(memory_space=pltpu.SMEM), tile, tile,
                  pl.BlockSpec((tm, HALF), lambda i: (i, 0)), pl.BlockSpec(memory_space=pl.ANY)],
        out_specs=pl.BlockSpec(memory_space=pl.ANY),
        out_shape=jax.ShapeDtypeStruct((n_rows, HALF), U32),
        scratch_shapes=[pltpu.SemaphoreType.DMA(())],
        input_output_aliases={4: 0},
        compiler_params=_params("arbitrary"),
        name="moe_dispatch",
    )(pstart, idx3, rank3, x1p, jnp.zeros((n_rows, HALF), U32))


def _expert_kernel(be_ref, nused_ref, x_ref, wg_ref, wu_ref, wd_ref, y_ref):
    @pl.when(pl.program_id(0) < nused_ref[0])
    def _():
        xlo, xhi = (h.astype(BF16) for h in _unpack_halves(x_ref[...]))
        wg = wg_ref[0, 0].astype(BF16)
        wu = wu_ref[0, 0].astype(BF16)
        g = _dot(xlo, wg[:HALF]) + _dot(xhi, wg[HALF:])
        u = _dot(xlo, wu[:HALF]) + _dot(xhi, wu[HALF:])
        a = (g * _sigmoid(g) * u).astype(BF16)
        y_ref[...] = _pack_halves(_dot(a, wd_ref[0, 0].astype(BF16)))

    @pl.when(pl.program_id(0) >= nused_ref[0])
    def _():
        y_ref[...] = jnp.zeros_like(y_ref)


def _experts(xs, block_e, nused, w_gate, w_up, w_down, layer, n_blocks):
    rows = EXPERT_BLOCK
    used = lambda b, be, nu: jnp.minimum(b, nu[0] - 1)
    grid_spec = pltpu.PrefetchScalarGridSpec(
        num_scalar_prefetch=2,
        grid=(n_blocks,),
        in_specs=[
            pl.BlockSpec((rows, HALF), lambda b, be, nu: (used(b, be, nu), 0)),
            pl.BlockSpec((1, 1, D_MODEL, D_EXPERT), lambda b, be, nu: (layer, be[b], 0, 0)),
            pl.BlockSpec((1, 1, D_MODEL, D_EXPERT), lambda b, be, nu: (layer, be[b], 0, 0)),
            pl.BlockSpec((1, 1, D_EXPERT, D_MODEL), lambda b, be, nu: (layer, be[b], 0, 0)),
        ],
        out_specs=pl.BlockSpec((rows, HALF), lambda b, be, nu: (b, 0)),
    )
    return pl.pallas_call(
        _expert_kernel,
        grid_spec=grid_spec,
        out_shape=jax.ShapeDtypeStruct((n_blocks * rows, HALF), U32),
        compiler_params=_params("arbitrary"),
        name="moe_experts",
    )(block_e, nused, xs, w_gate, w_up, w_down)


def _combine_kernel(pstart_ref, idx_ref, rank_ref, x_ref, w_ref, wsg_ref, wsu_ref, wsd_ref, g_ref, b_ref, y_hbm,
                    o_ref, buf, sem):
    tm = x_ref.shape[0]

    def issue(r, c):
        for k in range(TOP_K):
            src = pstart_ref[idx_ref[0, k, r]] + rank_ref[0, k, r]
            _row_copy(y_hbm, src, buf, k * tm + r, sem).start()
        return c

    lax.fori_loop(0, tm, issue, 0)

    x = x_ref[...]
    xb = x.astype(BF16)
    g = _dot(xb, wsg_ref[...])
    u = _dot(xb, wsu_ref[...])
    shared = _dot((g * _sigmoid(g) * u).astype(BF16), wsd_ref[...])

    def drain(r, c):
        _row_copy(y_hbm, 0, buf, r, sem).wait()
        return c

    lax.fori_loop(0, tm * TOP_K, drain, 0, unroll=8)

    w = w_ref[...]
    lo, hi = (h * w[:, 0:1] for h in _unpack_halves(buf[0:tm, :]))
    for k in range(1, TOP_K):
        lo_k, hi_k = _unpack_halves(buf[k * tm:(k + 1) * tm, :])
        lo = lo + lo_k * w[:, k:k + 1]
        hi = hi + hi_k * w[:, k:k + 1]
    routed = jnp.concatenate([lo, hi], axis=1)
    o_ref[...] = _layer_norm(ALPHA * x + (shared + routed), g_ref[...], b_ref[...])


def _combine(x1, ys, idx3, rank3, w_tk, pstart, wp):
    t = x1.shape[0]
    tm = COMBINE_TILE
    per = ROW_TILE // tm
    tile = pl.BlockSpec((1, TOP_K, tm), lambda i: (i // per, 0, i % per), memory_space=pltpu.SMEM)
    weights = [wp['wsg'], wp['wsu'], wp['wsd'], wp['ln2_g'], wp['ln2_b']]
    return pl.pallas_call(
        _combine_kernel,
        grid=(t // tm,),
        in_specs=[pl.BlockSpec(memory_space=pltpu.SMEM), tile, tile,
                  pl.BlockSpec((tm, D_MODEL), lambda i: (i, 0)), pl.BlockSpec((tm, TOP_K), lambda i: (i, 0))]
                 + [_full(w) for w in weights] + [pl.BlockSpec(memory_space=pl.ANY)],
        out_specs=pl.BlockSpec((tm, D_MODEL), lambda i: (i, 0)),
        out_shape=jax.ShapeDtypeStruct((t, D_MODEL), F32),
        scratch_shapes=[pltpu.VMEM((TOP_K * tm, HALF), U32), pltpu.SemaphoreType.DMA(())],
        compiler_params=_params("arbitrary"),
        name="moe_combine",
    )(pstart, idx3, rank3, x1, w_tk, *weights, ys)


def _moe(x1, x1p, wp, w_e_gate, w_e_up, w_e_down, layer):
    t = x1.shape[0]
    rows = EXPERT_BLOCK
    n_blocks = -(-(t * TOP_K + N_EXPERTS * (rows - 1)) // rows)
    idx3, rank3, top_w, cnt = _router(x1, wp)
    counts = cnt[:, 0].astype(jnp.int32)
    padded = (counts + rows - 1) // rows * rows
    padded_end = jnp.cumsum(padded)
    pstart = (padded_end - padded).astype(jnp.int32)
    nused = (padded_end[-1] // rows).astype(jnp.int32).reshape(1)
    blocks = jnp.arange(n_blocks, dtype=jnp.int32)
    block_e = jnp.searchsorted(padded_end, jnp.minimum(blocks, nused[0] - 1) * rows, side='right')
    block_e = jnp.clip(block_e, 0, N_EXPERTS - 1).astype(jnp.int32)
    xs = _dispatch(x1p, idx3, rank3, pstart, n_blocks * rows)
    ys = _experts(xs, block_e, nused, w_e_gate, w_e_up, w_e_down, layer, n_blocks)
    return _combine(x1, ys, idx3, rank3, top_w.T, pstart, wp)


def _head_groups_t(w, used):
    k, h, _ = w.shape
    return jnp.pad(w, ((0, 0), (0, 0), (0, LANES - used))).reshape(k, h * LANES).T


def _prep_layer(l, w_in, b_gate, q_norm, w_uq, kv_norm, w_ukv, w_proj_a, w_proj_b, w_out, ln1_g, ln1_b,
                w_router, router_bias, w_s_gate, w_s_up, w_s_down, ln2_g, ln2_b):
    w = w_in[l]
    o = 0
    cols = {}
    for name, width in (('qa', WIDTH_A), ('ka', WIDTH_A), ('va', WIDTH_A), ('cq', Q_LORA_RANK),
                        ('ckv', KV_LORA_RANK), ('kr', QK_ROPE_DIM), ('ga', D_MODEL), ('gb', D_MODEL)):
        cols[name] = w[:, o:o + width]
        o += width
    wkr = jnp.zeros((D_MODEL, LANES), F32).at[:, QK_NOPE_DIM:QK_NOPE_DIM + QK_ROPE_DIM].set(cols['kr'])
    dqk = QK_NOPE_DIM + QK_ROPE_DIM
    wq = w_uq[l].reshape(Q_LORA_RANK, N_HEADS_B, dqk) * dqk ** -0.5
    wkv = w_ukv[l].reshape(KV_LORA_RANK, N_HEADS_B, QK_NOPE_DIM + V_HEAD_DIM)
    wuk = jnp.pad(wkv[:, :, :QK_NOPE_DIM], ((0, 0), (0, 0), (0, LANES - QK_NOPE_DIM))).reshape(KV_LORA_RANK, N_HEADS_B * LANES)
    ones = jnp.zeros((N_HEADS_B, LANES), F32).at[:, V_HEAD_DIM].set(1.0).reshape(N_HEADS_B * LANES, 1)
    wr_t = w_router[l].T
    wr_hi = wr_t.astype(BF16)
    return dict(
        wqt=(cols['qa'] * HEAD_DIM_A ** -0.5).T.astype(BF16), wk=cols['ka'].astype(BF16),
        wvt=_head_groups_t(cols['va'].reshape(D_MODEL, N_HEADS_A, HEAD_DIM_A), HEAD_DIM_A).astype(BF16), ones=ones,
        wcq=cols['cq'].astype(BF16), wckv=cols['ckv'].astype(BF16), wkr=wkr.astype(BF16),
        wg=jnp.concatenate([cols['ga'], cols['gb']], axis=1).astype(BF16),
        bg=b_gate[l].reshape(1, 2 * D_MODEL), qn=q_norm[l].reshape(1, Q_LORA_RANK), kvn=kv_norm[l].reshape(1, KV_LORA_RANK),
        wuqt=_head_groups_t(wq, dqk).astype(BF16), wuk=wuk.astype(BF16),
        wuvt=_head_groups_t(wkv[:, :, QK_NOPE_DIM:], V_HEAD_DIM).astype(BF16),
        wpa=w_proj_a[l].astype(BF16), wpb=w_proj_b[l].astype(BF16), wo=w_out[l].astype(BF16),
        ln1_g=ln1_g[l].reshape(1, D_MODEL), ln1_b=ln1_b[l].reshape(1, D_MODEL),
        wr_hi=wr_hi, wr_lo=(wr_t - wr_hi.astype(F32)).astype(BF16), rbias=router_bias[l].reshape(N_EXPERTS, 1),
        wsg=w_s_gate[l].astype(BF16), wsu=w_s_up[l].astype(BF16), wsd=w_s_down[l].astype(BF16),
        ln2_g=ln2_g[l].reshape(1, D_MODEL), ln2_b=ln2_b[l].reshape(1, D_MODEL),
    )


def _rope_tables(seq):
    pos = jnp.arange(seq, dtype=F32)
    inv_freq = ROPE_THETA ** (-jnp.arange(0, QK_ROPE_DIM, 2, dtype=F32) / QK_ROPE_DIM)
    ang = pos[:, None] * inv_freq[None, :]
    cos, sin = jnp.cos(ang), jnp.sin(ang)
    half = QK_ROPE_DIM // 2
    z = lambda n: jnp.zeros((seq, n), F32)
    c = jnp.concatenate([jnp.ones((seq, QK_NOPE_DIM), F32), cos, cos, z(LANES - QK_NOPE_DIM - QK_ROPE_DIM)], axis=1)
    s1 = jnp.concatenate([z(QK_NOPE_DIM), -sin, z(LANES - QK_NOPE_DIM - half)], axis=1)
    s2 = jnp.concatenate([z(QK_NOPE_DIM + half), sin, z(LANES - QK_NOPE_DIM - QK_ROPE_DIM)], axis=1)
    return c, s1, s2, c.T, s1.T, s2.T


def _moba_key_table(seq, slopes):
    blk = jnp.arange(seq, dtype=jnp.int32) // MOBA_BLOCK
    onehot = (blk[:, None] == jnp.arange(BIAS_LO, dtype=jnp.int32)[None, :]).astype(F32)
    inblk = (jnp.arange(seq, dtype=jnp.int32) % MOBA_BLOCK).astype(F32)[:, None] * slopes[None, :]
    hi = inblk.astype(BF16)
    lo = (inblk - hi.astype(F32)).astype(BF16)
    pad = jnp.zeros((seq, LANES - POS_LO - N_HEADS_A), BF16)
    return jnp.concatenate([onehot.astype(BF16), onehot.astype(BF16), hi, lo, pad], axis=1)


def kernel(x, w_in, b_gate, q_norm, w_uq, kv_norm, w_ukv, w_proj_a, w_proj_b, w_out, ln1_g, ln1_b, w_router, router_bias, w_e_gate, w_e_up, w_e_down, w_s_gate, w_s_up, w_s_down, ln2_g, ln2_b):
    batch, seq, d = x.shape
    assert d == D_MODEL and seq % MOBA_BLOCK == 0 and MOBA_TOPK <= seq // MOBA_BLOCK <= BIAS_LO
    assert POS_LO + N_HEADS_A <= LANES and POS_HI + N_HEADS_A <= POS_LO
    tabs = _rope_tables(seq)
    slopes = jnp.asarray(np.exp2(-8.0 * (np.arange(N_HEADS_A) + 1.0) / N_HEADS_A), F32)
    etab = _moba_key_table(seq, slopes)
    h = x.reshape(batch * seq, d)
    for l in range(DEPTH):
        wp = _prep_layer(l, w_in, b_gate, q_norm, w_uq, kv_norm, w_ukv, w_proj_a, w_proj_b, w_out, ln1_g, ln1_b,
                         w_router, router_bias, w_s_gate, w_s_up, w_s_down, ln2_g, ln2_b)
        qat, ka, vat, kmean, qmt, km, vmt, sa, sb = _inproj(h, wp, tabs, seq)
        oa = _moba(qat, ka, vat, kmean, etab, slopes, batch, seq)
        ob = _mla(qmt, km, vmt, batch, seq)
        x1, x1p = _merge(h, oa, ob, sa, sb, wp)
        h = _moe(x1, x1p, wp, w_e_gate, w_e_up, w_e_down, l)
    return h.reshape(batch, seq, d)
```

```python
import numpy as np

import jax
import jax.numpy as jnp
from jax import lax
from jax.experimental import pallas as pl
from jax.experimental.pallas import tpu as pltpu

D_MODEL = 1024
N_HEADS_A = 8
HEAD_DIM_A = 64
WIDTH_A = N_HEADS_A * HEAD_DIM_A
MOBA_BLOCK = 256
MOBA_TOPK = 3
N_HEADS_B = 8
QK_NOPE_DIM = 64
QK_ROPE_DIM = 32
V_HEAD_DIM = 64
Q_LORA_RANK = 384
KV_LORA_RANK = 256
WIDTH_B = N_HEADS_B * V_HEAD_DIM
ROPE_THETA = 10000.0
N_EXPERTS = 256
TOP_K = 8
N_GROUPS = 8
TOPK_GROUPS = 4
GROUP_SIZE = N_EXPERTS // N_GROUPS
D_EXPERT = 256
D_SHARED = 256
ROUTED_SCALE = 2.5
DEPTH = 2
ALPHA = (2 * DEPTH) ** 0.25
LN_EPS = 1e-5
RMS_EPS = 1e-6

LANES = 128
NEG = -1e30
ROW_TILE = 256
ATT_TILE = 256
EXPERT_BLOCK = 256
COMBINE_TILE = 128
VMEM_LIMIT = 56 * 1024 * 1024
HALF = D_MODEL // 2
ATT_HEADS = 4
PV_ROWS = 80
BIAS_HI, BIAS_LO, POS_HI, POS_LO = 0, 32, 64, 72

BF16 = jnp.bfloat16
F32 = jnp.float32
U32 = jnp.uint32


def _dot(a, b):
    return jnp.dot(a, b, preferred_element_type=F32)


def _dot_nt(a, b):
    return lax.dot_general(a, b, (((1,), (1,)), ((), ())), preferred_element_type=F32)


def _sigmoid(x):
    return 1.0 / (1.0 + jnp.exp(-x))


def _layer_norm(y, g, b):
    mu = jnp.mean(y, axis=-1, keepdims=True)
    d = y - mu
    var = jnp.mean(d * d, axis=-1, keepdims=True)
    return d * lax.rsqrt(var + LN_EPS) * g + b


def _params(*sem):
    return pltpu.CompilerParams(dimension_semantics=sem, vmem_limit_bytes=VMEM_LIMIT)


def _full(a):
    return pl.BlockSpec(a.shape, lambda *_: (0,) * a.ndim)


def _inproj_kernel(x_ref, wqt_ref, wk_ref, wvt_ref, ones_ref, wcq_ref, wckv_ref, wkr_ref, wg_ref, bg_ref,
                   qn_ref, kvn_ref, wuqt_ref, wuk_ref, wuvt_ref, cos_ref, s1_ref, s2_ref, cost_ref, s1t_ref, s2t_ref,
                   qat_ref, ka_ref, vat_ref, kmean_ref, qmt_ref, km_ref, vmt_ref, sa_ref, sb_ref):
    xb = x_ref[...].astype(BF16)
    half = QK_ROPE_DIM // 2
    qat_ref[0] = _dot_nt(wqt_ref[...], xb).astype(BF16)
    k = _dot(xb, wk_ref[...])
    ka_ref[...] = k.astype(BF16)
    kmean_ref[0] = jnp.mean(k, axis=0, keepdims=True)
    vat_ref[0] = (_dot_nt(wvt_ref[...], xb) + ones_ref[...]).astype(BF16)

    cq = _dot(xb, wcq_ref[...])
    cqn = (cq * lax.rsqrt(jnp.mean(cq * cq, axis=-1, keepdims=True) + RMS_EPS) * qn_ref[...]).astype(BF16)
    ckv = _dot(xb, wckv_ref[...])
    ckvn = (ckv * lax.rsqrt(jnp.mean(ckv * ckv, axis=-1, keepdims=True) + RMS_EPS) * kvn_ref[...]).astype(BF16)
    qt = _dot_nt(wuqt_ref[...], cqn)
    ct, s1t, s2t = cost_ref[...], s1t_ref[...], s2t_ref[...]
    for h in range(N_HEADS_B):
        t = qt[h * LANES:(h + 1) * LANES, :]
        rot = t * ct + pltpu.roll(t, LANES - half, 0) * s1t + pltpu.roll(t, half, 0) * s2t
        qmt_ref[0, h * LANES:(h + 1) * LANES, :] = rot.astype(BF16)
    kn = _dot(ckvn, wuk_ref[...])
    kr = _dot(xb, wkr_ref[...])
    c, s1, s2 = cos_ref[...], s1_ref[...], s2_ref[...]
    krot = kr * c + pltpu.roll(kr, LANES - half, 1) * s1 + pltpu.roll(kr, half, 1) * s2
    for h in range(N_HEADS_B):
        sl = slice(h * LANES, (h + 1) * LANES)
        km_ref[:, sl] = (kn[:, sl] + krot).astype(BF16)
    vmt_ref[0] = (_dot_nt(wuvt_ref[...], ckvn) + ones_ref[...]).astype(BF16)

    sig = _sigmoid(_dot(xb, wg_ref[...]) + bg_ref[...])
    sa_ref[...] = sig[:, :D_MODEL].astype(BF16)
    sb_ref[...] = sig[:, D_MODEL:].astype(BF16)


def _inproj(x, wp, tabs, seq):
    t = x.shape[0]
    tm = ROW_TILE
    nt = t // tm
    npos = seq // tm
    row = lambda w: pl.BlockSpec((tm, w), lambda i: (i, 0))
    tile = lambda r: pl.BlockSpec((1, r, tm), lambda i: (i, 0, 0))
    tab = pl.BlockSpec((tm, LANES), lambda i: (i % npos, 0))
    tabt = pl.BlockSpec((LANES, tm), lambda i: (0, i % npos))
    weights = [wp['wqt'], wp['wk'], wp['wvt'], wp['ones'], wp['wcq'], wp['wckv'], wp['wkr'], wp['wg'], wp['bg'],
               wp['qn'], wp['kvn'], wp['wuqt'], wp['wuk'], wp['wuvt']]
    hl = N_HEADS_B * LANES
    out_shape = [
        jax.ShapeDtypeStruct((nt, WIDTH_A, tm), BF16), jax.ShapeDtypeStruct((t, WIDTH_A), BF16),
        jax.ShapeDtypeStruct((nt, N_HEADS_A * LANES, tm), BF16), jax.ShapeDtypeStruct((nt, 1, WIDTH_A), F32),
        jax.ShapeDtypeStruct((nt, hl, tm), BF16), jax.ShapeDtypeStruct((t, hl), BF16),
        jax.ShapeDtypeStruct((nt, hl, tm), BF16),
        jax.ShapeDtypeStruct((t, D_MODEL), BF16), jax.ShapeDtypeStruct((t, D_MODEL), BF16),
    ]
    out_specs = [tile(WIDTH_A), row(WIDTH_A), tile(N_HEADS_A * LANES),
                 pl.BlockSpec((1, 1, WIDTH_A), lambda i: (i, 0, 0)),
                 tile(hl), row(hl), tile(hl), row(D_MODEL), row(D_MODEL)]
    return pl.pallas_call(
        _inproj_kernel,
        grid=(nt,),
        in_specs=[row(D_MODEL)] + [_full(w) for w in weights] + [tab, tab, tab, tabt, tabt, tabt],
        out_specs=out_specs,
        out_shape=out_shape,
        compiler_params=_params("parallel"),
        name="inproj",
    )(x, *weights, *tabs)


def _attend_init(tq):
    return jnp.full((1, tq), -jnp.inf, F32), jnp.zeros((PV_ROWS, tq), F32)


def _col_max(s_ref):
    return [jnp.max(s_ref[hh], axis=0, keepdims=True) for hh in range(ATT_HEADS)]


def _attend_staged(cur_ref, cur_max, state, vts, nxt_ref=None, next_scores=None):
    heads = range(ATT_HEADS)
    if nxt_ref is not None:
        for hh in heads:
            nxt_ref[hh] = next_scores(hh)
    new_m, scaled, pvs = [], [], []
    for hh in heads:
        m_i, acc = state[hh]
        m_new = jnp.maximum(m_i, cur_max[hh])
        new_m.append(m_new)
        scaled.append(jnp.exp(m_i - m_new) * acc)
        pvs.append(_dot(vts[hh], jnp.exp(cur_ref[hh] - m_new).astype(BF16)))
    nxt_max = _col_max(nxt_ref) if nxt_ref is not None else cur_max
    return tuple((m, a + pv) for m, a, pv in zip(new_m, scaled, pvs)), nxt_max


def _attention_loop(i, scores, values, causal, tq, sa_ref, sb_ref):
    heads = range(ATT_HEADS)
    for hh in heads:
        sa_ref[hh] = jnp.where(causal, scores(hh, i), NEG)
    max_a = _col_max(sa_ref)
    state = tuple(_attend_init(tq) for _ in heads)
    npair = (i + 1) // 2

    def pair(n, c):
        state, max_a = c
        t0 = 2 * n
        first = jnp.where(n == 0, i, t0 - 1)
        state, max_b = _attend_staged(sa_ref, max_a, state, [values(hh, first) for hh in heads],
                                      sb_ref, lambda hh: scores(hh, t0))
        state, max_a = _attend_staged(sb_ref, max_b, state, [values(hh, t0) for hh in heads],
                                      sa_ref, lambda hh: scores(hh, jnp.minimum(t0 + 1, i - 1)))
        return state, max_a

    state, max_a = lax.fori_loop(0, npair, pair, (state, max_a))
    last = jnp.where(i == 0, i, i - 1)
    state = lax.cond(i % 2 == 0,
                     lambda s: _attend_staged(sa_ref, max_a, s, [values(hh, last) for hh in heads])[0],
                     lambda s: s, state)
    outs = []
    for hh in heads:
        _, acc = state[hh]
        outs.append(acc[:V_HEAD_DIM, :] / acc[V_HEAD_DIM:V_HEAD_DIM + 1, :])
    return jnp.concatenate(outs, axis=0).T


def _causal_t(tq):
    key = lax.broadcasted_iota(jnp.int32, (tq, tq), 0)
    qry = lax.broadcasted_iota(jnp.int32, (tq, tq), 1)
    return key <= qry


def _moba_kernel(slope_ref, qt_ref, k_ref, e_ref, vt_ref, km_ref, o_ref, sa_ref, sb_ref):
    tq = ATT_TILE
    hp = pl.program_id(1)
    i = pl.program_id(2)
    nblk = km_ref.shape[1]
    row = lax.broadcasted_iota(jnp.int32, (LANES, tq), 0)
    pad = jnp.zeros((LANES - nblk, LANES), BF16)

    qaug = []
    for hh in range(ATT_HEADS):
        head = hp * ATT_HEADS + hh
        grp = slice(hh // 2 * LANES, (hh // 2 + 1) * LANES)
        qt2 = qt_ref[0, grp, :]
        head_rows = (row >= hh % 2 * HEAD_DIM_A) & (row < (hh % 2 + 1) * HEAD_DIM_A)
        qh = jnp.where(head_rows, qt2, jnp.zeros_like(qt2))
        kmean = jnp.concatenate([km_ref[0, :, grp].astype(BF16), pad], axis=0)
        g = jnp.where(row < i, _dot(kmean, qh), -jnp.inf)
        picked = row == i
        for r in range(MOBA_TOPK):
            m = jnp.max(g, axis=0, keepdims=True)
            idx = jnp.min(jnp.where(g == m, row, LANES), axis=0, keepdims=True)
            hit = row == idx
            picked = picked | (hit & (r < i))
            g = jnp.where(hit, -jnp.inf, g)
        dist = (i - row).astype(F32) * (slope_ref[head] * MOBA_BLOCK)
        b = jnp.where(picked, -dist, NEG)
        b_hi = b.astype(BF16)
        b_lo = (b - b_hi.astype(F32)).astype(BF16)
        one_rows = (row == POS_HI + head) | (row == POS_LO + head)
        extra = jnp.where(row < BIAS_LO, b_hi,
                          jnp.where(row < POS_HI, pltpu.roll(b_lo.astype(F32), BIAS_LO, 0).astype(BF16),
                                    jnp.where(one_rows, 1.0, 0.0).astype(BF16)))
        qaug.append(jnp.concatenate([qh, extra], axis=0))

    def scores(hh, j):
        rows = pl.ds(pl.multiple_of(j * tq, tq), tq)
        grp = slice(hh // 2 * LANES, (hh // 2 + 1) * LANES)
        return _dot(jnp.concatenate([k_ref[rows, grp], e_ref[rows, :]], axis=1), qaug[hh])

    def values(hh, j):
        return vt_ref[j, hh * LANES:hh * LANES + PV_ROWS, :]

    o_ref[...] = _attention_loop(i, scores, values, _causal_t(tq), tq, sa_ref, sb_ref).astype(BF16)


def _moba(qat, ka, vat, kmean, etab, slopes, batch, seq):
    t = ka.shape[0]
    tq = ATT_TILE
    nq = seq // tq
    nblk = seq // MOBA_BLOCK
    nh = ATT_HEADS
    km = kmean.reshape(batch, nblk, WIDTH_A)
    return pl.pallas_call(
        _moba_kernel,
        grid=(batch, N_HEADS_A // nh, nq),
        in_specs=[
            pl.BlockSpec(memory_space=pltpu.SMEM),
            pl.BlockSpec((1, nh * HEAD_DIM_A, tq), lambda b, h, i: (b * nq + i, h, 0)),
            pl.BlockSpec((seq, nh * HEAD_DIM_A), lambda b, h, i: (b, h)),
            pl.BlockSpec((seq, LANES), lambda b, h, i: (0, 0)),
            pl.BlockSpec((nq, nh * LANES, tq), lambda b, h, i: (b, h, 0)),
            pl.BlockSpec((1, nblk, nh * HEAD_DIM_A), lambda b, h, i: (b, 0, h)),
        ],
        out_specs=pl.BlockSpec((tq, nh * HEAD_DIM_A), lambda b, h, i: (b * nq + i, h)),
        out_shape=jax.ShapeDtypeStruct((t, WIDTH_A), BF16),
        scratch_shapes=[pltpu.VMEM((nh, tq, tq), F32), pltpu.VMEM((nh, tq, tq), F32)],
        compiler_params=_params("parallel", "parallel", "arbitrary"),
        name="moba",
    )(slopes, qat, ka, etab, vat, km)


def _mla_kernel(qt_ref, k_ref, vt_ref, o_ref, sa_ref, sb_ref):
    tq = ATT_TILE
    i = pl.program_id(2)
    qt = [qt_ref[0, hh * LANES:(hh + 1) * LANES, :] for hh in range(ATT_HEADS)]

    def scores(hh, j):
        return _dot(k_ref[pl.ds(pl.multiple_of(j * tq, tq), tq), hh * LANES:(hh + 1) * LANES], qt[hh])

    def values(hh, j):
        return vt_ref[j, hh * LANES:hh * LANES + PV_ROWS, :]

    o_ref[...] = _attention_loop(i, scores, values, _causal_t(tq), tq, sa_ref, sb_ref).astype(BF16)


def _mla(qmt, km, vmt, batch, seq):
    t = km.shape[0]
    tq = ATT_TILE
    nq = seq // tq
    nh = ATT_HEADS
    return pl.pallas_call(
        _mla_kernel,
        grid=(batch, N_HEADS_B // nh, nq),
        in_specs=[
            pl.BlockSpec((1, nh * LANES, tq), lambda b, h, i: (b * nq + i, h, 0)),
            pl.BlockSpec((seq, nh * LANES), lambda b, h, i: (b, h)),
            pl.BlockSpec((nq, nh * LANES, tq), lambda b, h, i: (b, h, 0)),
        ],
        out_specs=pl.BlockSpec((tq, nh * V_HEAD_DIM), lambda b, h, i: (b * nq + i, h)),
        out_shape=jax.ShapeDtypeStruct((t, WIDTH_B), BF16),
        scratch_shapes=[pltpu.VMEM((nh, tq, tq), F32), pltpu.VMEM((nh, tq, tq), F32)],
        compiler_params=_params("parallel", "parallel", "arbitrary"),
        name="mla",
    )(qmt, km, vmt)


def _pack_halves(y):
    lo = pltpu.bitcast(y[:, :HALF].astype(BF16).astype(F32), U32)
    hi = pltpu.bitcast(y[:, HALF:].astype(BF16).astype(F32), U32)
    return (hi & jnp.uint32(0xFFFF0000)) | (lo >> 16)


def _unpack_halves(w):
    return pltpu.bitcast(w << 16, F32), pltpu.bitcast(w & jnp.uint32(0xFFFF0000), F32)


def _merge_kernel(x_ref, oa_ref, ob_ref, sa_ref, sb_ref, wpa_ref, wpb_ref, wo_ref, g_ref, b_ref, o_ref, op_ref):
    pa = _dot(oa_ref[...], wpa_ref[...])
    pb = _dot(ob_ref[...], wpb_ref[...])
    merged = sa_ref[...].astype(F32) * pa + sb_ref[...].astype(F32) * pb
    hmix = _dot(merged.astype(BF16), wo_ref[...])
    y = _layer_norm(ALPHA * x_ref[...] + hmix, g_ref[...], b_ref[...])
    o_ref[...] = y
    op_ref[...] = _pack_halves(y)


def _merge(x, oa, ob, sa, sb, wp):
    t = x.shape[0]
    tm = ROW_TILE
    row = lambda w: pl.BlockSpec((tm, w), lambda i: (i, 0))
    weights = [wp['wpa'], wp['wpb'], wp['wo'], wp['ln1_g'], wp['ln1_b']]
    return pl.pallas_call(
        _merge_kernel,
        grid=(t // tm,),
        in_specs=[row(D_MODEL), row(WIDTH_A), row(WIDTH_B), row(D_MODEL), row(D_MODEL)] + [_full(w) for w in weights],
        out_specs=[row(D_MODEL), row(HALF)],
        out_shape=[jax.ShapeDtypeStruct((t, D_MODEL), F32), jax.ShapeDtypeStruct((t, HALF), U32)],
        compiler_params=_params("parallel"),
        name="merge",
    )(x, oa, ob, sa, sb, *weights)


def _router_kernel(x_ref, wh_ref, wl_ref, rb_ref, idx_ref, rank_ref, w_ref, cnt_ref):
    tm = x_ref.shape[0]

    @pl.when(pl.program_id(0) == 0)
    def _():
        cnt_ref[...] = jnp.zeros_like(cnt_ref)

    x = x_ref[...]
    xh = x.astype(BF16)
    xl = (x - xh.astype(F32)).astype(BF16)
    wh, wl = wh_ref[...], wl_ref[...]
    logits = _dot_nt(wh, xh) + (_dot_nt(wh, xl) + _dot_nt(wl, xh))
    scores = _sigmoid(logits)
    choice = scores + rb_ref[...]
    row = lax.broadcasted_iota(jnp.int32, (GROUP_SIZE, tm), 0)
    groups = [choice[g * GROUP_SIZE:(g + 1) * GROUP_SIZE, :] for g in range(N_GROUPS)]
    gscore = []
    for blk in groups:
        m1 = jnp.max(blk, axis=0, keepdims=True)
        first = jnp.min(jnp.where(blk == m1, row, GROUP_SIZE), axis=0, keepdims=True)
        m2 = jnp.max(jnp.where(row == first, -jnp.inf, blk), axis=0, keepdims=True)
        gscore.append(m1 + m2)
    masked = []
    for g in range(N_GROUPS):
        ahead = jnp.zeros((1, tm), jnp.int32)
        for o in range(N_GROUPS):
            if o < g:
                ahead += (gscore[o] >= gscore[g]).astype(jnp.int32)
            elif o > g:
                ahead += (gscore[o] > gscore[g]).astype(jnp.int32)
        masked.append(jnp.where(ahead < TOPK_GROUPS, groups[g], -jnp.inf))
    cur = jnp.concatenate(masked, axis=0)
    erow = lax.broadcasted_iota(jnp.int32, (N_EXPERTS, tm), 0)
    hits, idxs, ws = [], [], []
    for _ in range(TOP_K):
        m = jnp.max(cur, axis=0, keepdims=True)
        e = jnp.min(jnp.where(cur == m, erow, N_EXPERTS), axis=0, keepdims=True)
        hit = erow == e
        hits.append(hit)
        idxs.append(e)
        ws.append(jnp.sum(jnp.where(hit, scores, 0.0), axis=0, keepdims=True))
        cur = jnp.where(hit, -jnp.inf, cur)
    total = ws[0]
    for w in ws[1:]:
        total = total + w
    member = hits[0]
    for hit in hits[1:]:
        member = member | hit
    member = jnp.where(member, 1.0, 0.0).astype(BF16)
    t_src = lax.broadcasted_iota(jnp.int32, (tm, tm), 0)
    t_dst = lax.broadcasted_iota(jnp.int32, (tm, tm), 1)
    before = _dot(member, jnp.where(t_src < t_dst, 1.0, 0.0).astype(BF16))
    base = cnt_ref[...]
    before = before + jnp.concatenate([base] * (tm // LANES), axis=1)
    cnt_ref[...] = base + _dot(member, jnp.ones((tm, LANES), BF16))
    for r in range(TOP_K):
        idx_ref[0, r:r + 1, :] = idxs[r]
        rank_ref[0, r:r + 1, :] = jnp.sum(jnp.where(hits[r], before, 0.0), axis=0, keepdims=True).astype(jnp.int32)
        w_ref[r:r + 1, :] = ws[r] / total * ROUTED_SCALE


def _router(x1, wp):
    t = x1.shape[0]
    tm = ROW_TILE
    nt = t // tm
    weights = [wp['wr_hi'], wp['wr_lo'], wp['rbias']]
    tile = pl.BlockSpec((1, TOP_K, tm), lambda i: (i, 0, 0))
    return pl.pallas_call(
        _router_kernel,
        grid=(nt,),
        in_specs=[pl.BlockSpec((tm, D_MODEL), lambda i: (i, 0))] + [_full(w) for w in weights],
        out_specs=[tile, tile, pl.BlockSpec((TOP_K, tm), lambda i: (0, i)),
                   pl.BlockSpec((N_EXPERTS, LANES), lambda i: (0, 0))],
        out_shape=[jax.ShapeDtypeStruct((nt, TOP_K, tm), jnp.int32), jax.ShapeDtypeStruct((nt, TOP_K, tm), jnp.int32),
                   jax.ShapeDtypeStruct((TOP_K, t), F32), jax.ShapeDtypeStruct((N_EXPERTS, LANES), F32)],
        compiler_params=_params("arbitrary"),
        name="router",
    )(x1, *weights)


def _row_copy(src, src_row, dst, dst_row, sem):
    return pltpu.make_async_copy(src.at[pl.ds(src_row, 1), :], dst.at[pl.ds(dst_row, 1), :], sem)


def _dispatch_kernel(pstart_ref, idx_ref, rank_ref, x_ref, xs_in, xs_out, sem):
    del xs_in
    tm = x_ref.shape[0]

    def issue(r, c):
        for k in range(TOP_K):
            dst = pstart_ref[idx_ref[0, k, r]] + rank_ref[0, k, r]
            _row_copy(x_ref, r, xs_out, dst, sem).start()
        return c

    lax.fori_loop(0, tm, issue, 0)

    def drain(r, c):
        _row_copy(x_ref, 0, xs_out, 0, sem).wait()
        return c

    lax.fori_loop(0, tm * TOP_K, drain, 0, unroll=8)


def _dispatch(x1p, idx3, rank3, pstart, n_rows):
    t = x1p.shape[0]
    tm = ROW_TILE
    tile = pl.BlockSpec((1, TOP_K, tm), lambda i: (i, 0, 0), memory_space=pltpu.SMEM)
    return pl.pallas_call(
        _dispatch_kernel,
        grid=(t // tm,),
        in_specs=[pl.BlockSpec(memory_space=pltpu.SMEM), tile, tile,
                  pl.BlockSpec((tm, HALF), lambda i: (i, 0)), pl.BlockSpec(memory_space=pl.ANY)],
        out_specs=pl.BlockSpec(memory_space=pl.ANY),
        out_shape=jax.ShapeDtypeStruct((n_rows, HALF), U32),
        scratch_shapes=[pltpu.SemaphoreType.DMA(())],
        input_output_aliases={4: 0},
        compiler_params=_params("arbitrary"),
        name="moe_dispatch",
    )(pstart, idx3, rank3, x1p, jnp.zeros((n_rows, HALF), U32))


def _expert_kernel(be_ref, nused_ref, x_ref, wg_ref, wu_ref, wd_ref, y_ref):
    @pl.when(pl.program_id(0) < nused_ref[0])
    def _():
        xlo, xhi = (h.astype(BF16) for h in _unpack_halves(x_ref[...]))
        wg = wg_ref[0, 0].astype(BF16)
        wu = wu_ref[0, 0].astype(BF16)
        g = _dot(xlo, wg[:HALF]) + _dot(xhi, wg[HALF:])
        u = _dot(xlo, wu[:HALF]) + _dot(xhi, wu[HALF:])
        a = (g * _sigmoid(g) * u).astype(BF16)
        y_ref[...] = _pack_halves(_dot(a, wd_ref[0, 0].astype(BF16)))

    @pl.when(pl.program_id(0) >= nused_ref[0])
    def _():
        y_ref[...] = jnp.zeros_like(y_ref)


def _experts(xs, block_e, nused, w_gate, w_up, w_down, layer, n_blocks):
    rows = EXPERT_BLOCK
    used = lambda b, be, nu: jnp.minimum(b, nu[0] - 1)
    grid_spec = pltpu.PrefetchScalarGridSpec(
        num_scalar_prefetch=2,
        grid=(n_blocks,),
        in_specs=[
            pl.BlockSpec((rows, HALF), lambda b, be, nu: (used(b, be, nu), 0)),
            pl.BlockSpec((1, 1, D_MODEL, D_EXPERT), lambda b, be, nu: (layer, be[b], 0, 0)),
            pl.BlockSpec((1, 1, D_MODEL, D_EXPERT), lambda b, be, nu: (layer, be[b], 0, 0)),
            pl.BlockSpec((1, 1, D_EXPERT, D_MODEL), lambda b, be, nu: (layer, be[b], 0, 0)),
        ],
        out_specs=pl.BlockSpec((rows, HALF), lambda b, be, nu: (b, 0)),
    )
    return pl.pallas_call(
        _expert_kernel,
        grid_spec=grid_spec,
        out_shape=jax.ShapeDtypeStruct((n_blocks * rows, HALF), U32),
        compiler_params=_params("arbitrary"),
        name="moe_experts",
    )(block_e, nused, xs, w_gate, w_up, w_down)


def _combine_kernel(pstart_ref, idx_ref, rank_ref, x_ref, w_ref, wsg_ref, wsu_ref, wsd_ref, g_ref, b_ref, y_hbm,
                    o_ref, buf, sem):
    tm = x_ref.shape[0]

    def issue(r, c):
        for k in range(TOP_K):
            src = pstart_ref[idx_ref[0, k, r]] + rank_ref[0, k, r]
            _row_copy(y_hbm, src, buf, k * tm + r, sem).start()
        return c

    lax.fori_loop(0, tm, issue, 0)

    x = x_ref[...]
    xb = x.astype(BF16)
    g = _dot(xb, wsg_ref[...])
    u = _dot(xb, wsu_ref[...])
    shared = _dot((g * _sigmoid(g) * u).astype(BF16), wsd_ref[...])

    def drain(r, c):
        _row_copy(y_hbm, 0, buf, r, sem).wait()
        return c

    lax.fori_loop(0, tm * TOP_K, drain, 0, unroll=8)

    w = w_ref[...]
    lo, hi = (h * w[:, 0:1] for h in _unpack_halves(buf[0:tm, :]))
    for k in range(1, TOP_K):
        lo_k, hi_k = _unpack_halves(buf[k * tm:(k + 1) * tm, :])
        lo = lo + lo_k * w[:, k:k + 1]
        hi = hi + hi_k * w[:, k:k + 1]
    routed = jnp.concatenate([lo, hi], axis=1)
    o_ref[...] = _layer_norm(ALPHA * x + (shared + routed), g_ref[...], b_ref[...])


def _combine(x1, ys, idx3, rank3, w_tk, pstart, wp):
    t = x1.shape[0]
    tm = COMBINE_TILE
    per = ROW_TILE // tm
    tile = pl.BlockSpec((1, TOP_K, tm), lambda i: (i // per, 0, i % per), memory_space=pltpu.SMEM)
    weights = [wp['wsg'], wp['wsu'], wp['wsd'], wp['ln2_g'], wp['ln2_b']]
    return pl.pallas_call(
        _combine_kernel,
        grid=(t // tm,),
        in_specs=[pl.BlockSpec(memory_space=pltpu.SMEM), tile, tile,
                  pl.BlockSpec((tm, D_MODEL), lambda i: (i, 0)), pl.BlockSpec((tm, TOP_K), lambda i: (i, 0))]
                 + [_full(w) for w in weights] + [pl.BlockSpec(memory_space=pl.ANY)],
        out_specs=pl.BlockSpec((tm, D_MODEL), lambda i: (i, 0)),
        out_shape=jax.ShapeDtypeStruct((t, D_MODEL), F32),
        scratch_shapes=[pltpu.VMEM((TOP_K * tm, HALF), U32), pltpu.SemaphoreType.DMA(())],
        compiler_params=_params("arbitrary"),
        name="moe_combine",
    )(pstart, idx3, rank3, x1, w_tk, *weights, ys)


def _moe(x1, x1p, wp, w_e_gate, w_e_up, w_e_down, layer):
    t = x1.shape[0]
    rows = EXPERT_BLOCK
    n_blocks = -(-(t * TOP_K + N_EXPERTS * (rows - 1)) // rows)
    idx3, rank3, top_w, cnt = _router(x1, wp)
    counts = cnt[:, 0].astype(jnp.int32)
    padded = (counts + rows - 1) // rows * rows
    padded_end = jnp.cumsum(padded)
    pstart = (padded_end - padded).astype(jnp.int32)
    nused = (padded_end[-1] // rows).astype(jnp.int32).reshape(1)
    blocks = jnp.arange(n_blocks, dtype=jnp.int32)
    block_e = jnp.searchsorted(padded_end, jnp.minimum(blocks, nused[0] - 1) * rows, side='right')
    block_e = jnp.clip(block_e, 0, N_EXPERTS - 1).astype(jnp.int32)
    xs = _dispatch(x1p, idx3, rank3, pstart, n_blocks * rows)
    ys = _experts(xs, block_e, nused, w_e_gate, w_e_up, w_e_down, layer, n_blocks)
    return _combine(x1, ys, idx3, rank3, top_w.T, pstart, wp)


def _head_groups_t(w, used):
    k, h, _ = w.shape
    return jnp.pad(w, ((0, 0), (0, 0), (0, LANES - used))).reshape(k, h * LANES).T


def _prep_layer(l, w_in, b_gate, q_norm, w_uq, kv_norm, w_ukv, w_proj_a, w_proj_b, w_out, ln1_g, ln1_b,
                w_router, router_bias, w_s_gate, w_s_up, w_s_down, ln2_g, ln2_b):
    w = w_in[l]
    o = 0
    cols = {}
    for name, width in (('qa', WIDTH_A), ('ka', WIDTH_A), ('va', WIDTH_A), ('cq', Q_LORA_RANK),
                        ('ckv', KV_LORA_RANK), ('kr', QK_ROPE_DIM), ('ga', D_MODEL), ('gb', D_MODEL)):
        cols[name] = w[:, o:o + width]
        o += width
    wkr = jnp.zeros((D_MODEL, LANES), F32).at[:, QK_NOPE_DIM:QK_NOPE_DIM + QK_ROPE_DIM].set(cols['kr'])
    dqk = QK_NOPE_DIM + QK_ROPE_DIM
    wq = w_uq[l].reshape(Q_LORA_RANK, N_HEADS_B, dqk) * dqk ** -0.5
    wkv = w_ukv[l].reshape(KV_LORA_RANK, N_HEADS_B, QK_NOPE_DIM + V_HEAD_DIM)
    wuk = jnp.pad(wkv[:, :, :QK_NOPE_DIM], ((0, 0), (0, 0), (0, LANES - QK_NOPE_DIM))).reshape(KV_LORA_RANK, N_HEADS_B * LANES)
    ones = jnp.zeros((N_HEADS_B, LANES), F32).at[:, V_HEAD_DIM].set(1.0).reshape(N_HEADS_B * LANES, 1)
    wr_t = w_router[l].T
    wr_hi = wr_t.astype(BF16)
    return dict(
        wqt=(cols['qa'] * HEAD_DIM_A ** -0.5).T.astype(BF16), wk=cols['ka'].astype(BF16),
        wvt=_head_groups_t(cols['va'].reshape(D_MODEL, N_HEADS_A, HEAD_DIM_A), HEAD_DIM_A).astype(BF16), ones=ones,
        wcq=cols['cq'].astype(BF16), wckv=cols['ckv'].astype(BF16), wkr=wkr.astype(BF16),
        wg=jnp.concatenate([cols['ga'], cols['gb']], axis=1).astype(BF16),
        bg=b_gate[l].reshape(1, 2 * D_MODEL), qn=q_norm[l].reshape(1, Q_LORA_RANK), kvn=kv_norm[l].reshape(1, KV_LORA_RANK),
        wuqt=_head_groups_t(wq, dqk).astype(BF16), wuk=wuk.astype(BF16),
        wuvt=_head_groups_t(wkv[:, :, QK_NOPE_DIM:], V_HEAD_DIM).astype(BF16),
        wpa=w_proj_a[l].astype(BF16), wpb=w_proj_b[l].astype(BF16), wo=w_out[l].astype(BF16),
        ln1_g=ln1_g[l].reshape(1, D_MODEL), ln1_b=ln1_b[l].reshape(1, D_MODEL),
        wr_hi=wr_hi, wr_lo=(wr_t - wr_hi.astype(F32)).astype(BF16), rbias=router_bias[l].reshape(N_EXPERTS, 1),
        wsg=w_s_gate[l].astype(BF16), wsu=w_s_up[l].astype(BF16), wsd=w_s_down[l].astype(BF16),
        ln2_g=ln2_g[l].reshape(1, D_MODEL), ln2_b=ln2_b[l].reshape(1, D_MODEL),
    )


def _rope_tables(seq):
    pos = jnp.arange(seq, dtype=F32)
    inv_freq = ROPE_THETA ** (-jnp.arange(0, QK_ROPE_DIM, 2, dtype=F32) / QK_ROPE_DIM)
    ang = pos[:, None] * inv_freq[None, :]
    cos, sin = jnp.cos(ang), jnp.sin(ang)
    half = QK_ROPE_DIM // 2
    z = lambda n: jnp.zeros((seq, n), F32)
    c = jnp.concatenate([jnp.ones((seq, QK_NOPE_DIM), F32), cos, cos, z(LANES - QK_NOPE_DIM - QK_ROPE_DIM)], axis=1)
    s1 = jnp.concatenate([z(QK_NOPE_DIM), -sin, z(LANES - QK_NOPE_DIM - half)], axis=1)
    s2 = jnp.concatenate([z(QK_NOPE_DIM + half), sin, z(LANES - QK_NOPE_DIM - QK_ROPE_DIM)], axis=1)
    return c, s1, s2, c.T, s1.T, s2.T


def _moba_key_table(seq, slopes):
    blk = jnp.arange(seq, dtype=jnp.int32) // MOBA_BLOCK
    onehot = (blk[:, None] == jnp.arange(BIAS_LO, dtype=jnp.int32)[None, :]).astype(F32)
    inblk = (jnp.arange(seq, dtype=jnp.int32) % MOBA_BLOCK).astype(F32)[:, None] * slopes[None, :]
    hi = inblk.astype(BF16)
    lo = (inblk - hi.astype(F32)).astype(BF16)
    pad = jnp.zeros((seq, LANES - POS_LO - N_HEADS_A), BF16)
    return jnp.concatenate([onehot.astype(BF16), onehot.astype(BF16), hi, lo, pad], axis=1)


def kernel(x, w_in, b_gate, q_norm, w_uq, kv_norm, w_ukv, w_proj_a, w_proj_b, w_out, ln1_g, ln1_b, w_router, router_bias, w_e_gate, w_e_up, w_e_down, w_s_gate, w_s_up, w_s_down, ln2_g, ln2_b):
    batch, seq, d = x.shape
    assert d == D_MODEL and seq % MOBA_BLOCK == 0 and MOBA_TOPK <= seq // MOBA_BLOCK <= BIAS_LO
    assert POS_LO + N_HEADS_A <= LANES and POS_HI + N_HEADS_A <= POS_LO
    tabs = _rope_tables(seq)
    slopes = jnp.asarray(np.exp2(-8.0 * (np.arange(N_HEADS_A) + 1.0) / N_HEADS_A), F32)
    etab = _moba_key_table(seq, slopes)
    h = x.reshape(batch * seq, d)
    for l in range(DEPTH):
        wp = _prep_layer(l, w_in, b_gate, q_norm, w_uq, kv_norm, w_ukv, w_proj_a, w_proj_b, w_out, ln1_g, ln1_b,
                         w_router, router_bias, w_s_gate, w_s_up, w_s_down, ln2_g, ln2_b)
        qat, ka, vat, kmean, qmt, km, vmt, sa, sb = _inproj(h, wp, tabs, seq)
        oa = _moba(qat, ka, vat, kmean, etab, slopes, batch, seq)
        ob = _mla(qmt, km, vmt, batch, seq)
        x1, x1p = _merge(h, oa, ob, sa, sb, wp)
        h = _moe(x1, x1p, wp, w_e_gate, w_e_up, w_e_down, l)
    return h.reshape(batch, seq, d)
```

```python
import numpy as np

import jax
import jax.numpy as jnp
from jax import lax
from jax.experimental import pallas as pl
from jax.experimental.pallas import tpu as pltpu
from jax.experimental.pallas import tpu_sc as plsc

D_MODEL = 1024
N_HEADS_A = 8
HEAD_DIM_A = 64
WIDTH_A = N_HEADS_A * HEAD_DIM_A
MOBA_BLOCK = 256
MOBA_TOPK = 3
N_HEADS_B = 8
QK_NOPE_DIM = 64
QK_ROPE_DIM = 32
V_HEAD_DIM = 64
Q_LORA_RANK = 384
KV_LORA_RANK = 256
WIDTH_B = N_HEADS_B * V_HEAD_DIM
ROPE_THETA = 10000.0
N_EXPERTS = 256
TOP_K = 8
N_GROUPS = 8
TOPK_GROUPS = 4
GROUP_SIZE = N_EXPERTS // N_GROUPS
D_EXPERT = 256
D_SHARED = 256
ROUTED_SCALE = 2.5
DEPTH = 2
ALPHA = (2 * DEPTH) ** 0.25
LN_EPS = 1e-5
RMS_EPS = 1e-6

LANES = 128
NEG = -1e30
ROW_TILE = 256
ATT_TILE = 256
EXPERT_BLOCK = 256
SC_CORES = 2
SC_SUBCORES = 16
SC_CHUNK = 64
VMEM_LIMIT = 56 * 1024 * 1024
HALF = D_MODEL // 2
ATT_HEADS = 4
PV_ROWS = 80
BIAS_HI, BIAS_LO, POS_HI, POS_LO = 0, 32, 64, 72

BF16 = jnp.bfloat16
F32 = jnp.float32
U32 = jnp.uint32


def _dot(a, b):
    return jnp.dot(a, b, preferred_element_type=F32)


def _dot_nt(a, b):
    return lax.dot_general(a, b, (((1,), (1,)), ((), ())), preferred_element_type=F32)


def _sigmoid(x):
    return 1.0 / (1.0 + jnp.exp(-x))


def _layer_norm(y, g, b):
    mu = jnp.mean(y, axis=-1, keepdims=True)
    d = y - mu
    var = jnp.mean(d * d, axis=-1, keepdims=True)
    return d * lax.rsqrt(var + LN_EPS) * g + b


def _params(*sem):
    return pltpu.CompilerParams(dimension_semantics=sem, vmem_limit_bytes=VMEM_LIMIT)


def _full(a):
    return pl.BlockSpec(a.shape, lambda *_: (0,) * a.ndim)


def _inproj_kernel(x_ref, wqt_ref, wk_ref, wvt_ref, ones_ref, wcq_ref, wckv_ref, wkr_ref, wg_ref, bg_ref,
                   qn_ref, kvn_ref, wuqt_ref, wuk_ref, wuvt_ref, cos_ref, s1_ref, s2_ref, cost_ref, s1t_ref, s2t_ref,
                   qat_ref, ka_ref, vat_ref, kmean_ref, qmt_ref, km_ref, vmt_ref, sa_ref, sb_ref):
    xb = x_ref[...].astype(BF16)
    half = QK_ROPE_DIM // 2
    qat_ref[0] = _dot_nt(wqt_ref[...], xb).astype(BF16)
    k = _dot(xb, wk_ref[...])
    ka_ref[...] = k.astype(BF16)
    kmean_ref[0] = jnp.mean(k, axis=0, keepdims=True)
    vat_ref[0] = (_dot_nt(wvt_ref[...], xb) + ones_ref[...]).astype(BF16)

    cq = _dot(xb, wcq_ref[...])
    cqn = (cq * lax.rsqrt(jnp.mean(cq * cq, axis=-1, keepdims=True) + RMS_EPS) * qn_ref[...]).astype(BF16)
    ckv = _dot(xb, wckv_ref[...])
    ckvn = (ckv * lax.rsqrt(jnp.mean(ckv * ckv, axis=-1, keepdims=True) + RMS_EPS) * kvn_ref[...]).astype(BF16)
    qt = _dot_nt(wuqt_ref[...], cqn)
    ct, s1t, s2t = cost_ref[...], s1t_ref[...], s2t_ref[...]
    for h in range(N_HEADS_B):
        t = qt[h * LANES:(h + 1) * LANES, :]
        rot = t * ct + pltpu.roll(t, LANES - half, 0) * s1t + pltpu.roll(t, half, 0) * s2t
        qmt_ref[0, h * LANES:(h + 1) * LANES, :] = rot.astype(BF16)
    kn = _dot(ckvn, wuk_ref[...])
    kr = _dot(xb, wkr_ref[...])
    c, s1, s2 = cos_ref[...], s1_ref[...], s2_ref[...]
    krot = kr * c + pltpu.roll(kr, LANES - half, 1) * s1 + pltpu.roll(kr, half, 1) * s2
    for h in range(N_HEADS_B):
        sl = slice(h * LANES, (h + 1) * LANES)
        km_ref[:, sl] = (kn[:, sl] + krot).astype(BF16)
    vmt_ref[0] = (_dot_nt(wuvt_ref[...], ckvn) + ones_ref[...]).astype(BF16)

    sig = _sigmoid(_dot(xb, wg_ref[...]) + bg_ref[...])
    sa_ref[...] = sig[:, :D_MODEL].astype(BF16)
    sb_ref[...] = sig[:, D_MODEL:].astype(BF16)


def _inproj(x, wp, tabs, seq):
    t = x.shape[0]
    tm = ROW_TILE
    nt = t // tm
    npos = seq // tm
    row = lambda w: pl.BlockSpec((tm, w), lambda i: (i, 0))
    tile = lambda r: pl.BlockSpec((1, r, tm), lambda i: (i, 0, 0))
    tab = pl.BlockSpec((tm, LANES), lambda i: (i % npos, 0))
    tabt = pl.BlockSpec((LANES, tm), lambda i: (0, i % npos))
    weights = [wp['wqt'], wp['wk'], wp['wvt'], wp['ones'], wp['wcq'], wp['wckv'], wp['wkr'], wp['wg'], wp['bg'],
               wp['qn'], wp['kvn'], wp['wuqt'], wp['wuk'], wp['wuvt']]
    hl = N_HEADS_B * LANES
    out_shape = [
        jax.ShapeDtypeStruct((nt, WIDTH_A, tm), BF16), jax.ShapeDtypeStruct((t, WIDTH_A), BF16),
        jax.ShapeDtypeStruct((nt, N_HEADS_A * LANES, tm), BF16), jax.ShapeDtypeStruct((nt, 1, WIDTH_A), F32),
        jax.ShapeDtypeStruct((nt, hl, tm), BF16), jax.ShapeDtypeStruct((t, hl), BF16),
        jax.ShapeDtypeStruct((nt, hl, tm), BF16),
        jax.ShapeDtypeStruct((t, D_MODEL), BF16), jax.ShapeDtypeStruct((t, D_MODEL), BF16),
    ]
    out_specs = [tile(WIDTH_A), row(WIDTH_A), tile(N_HEADS_A * LANES),
                 pl.BlockSpec((1, 1, WIDTH_A), lambda i: (i, 0, 0)),
                 tile(hl), row(hl), tile(hl), row(D_MODEL), row(D_MODEL)]
    return pl.pallas_call(
        _inproj_kernel,
        grid=(nt,),
        in_specs=[row(D_MODEL)] + [_full(w) for w in weights] + [tab, tab, tab, tabt, tabt, tabt],
        out_specs=out_specs,
        out_shape=out_shape,
        compiler_params=_params("parallel"),
        name="inproj",
    )(x, *weights, *tabs)


def _attend_init(tq):
    return jnp.full((1, tq), -jnp.inf, F32), jnp.zeros((PV_ROWS, tq), F32)


def _col_max(s_ref):
    return [jnp.max(s_ref[hh], axis=0, keepdims=True) for hh in range(ATT_HEADS)]


def _attend_staged(cur_ref, cur_max, state, vts, nxt_ref=None, next_scores=None):
    heads = range(ATT_HEADS)
    if nxt_ref is not None:
        for hh in heads:
            nxt_ref[hh] = next_scores(hh)
    new_m, scaled, pvs = [], [], []
    for hh in heads:
        m_i, acc = state[hh]
        m_new = jnp.maximum(m_i, cur_max[hh])
        new_m.append(m_new)
        scaled.append(jnp.exp(m_i - m_new) * acc)
        pvs.append(_dot(vts[hh], jnp.exp(cur_ref[hh] - m_new).astype(BF16)))
    nxt_max = _col_max(nxt_ref) if nxt_ref is not None else cur_max
    return tuple((m, a + pv) for m, a, pv in zip(new_m, scaled, pvs)), nxt_max


def _attention_loop(i, scores, values, causal, tq, sa_ref, sb_ref):
    heads = range(ATT_HEADS)
    for hh in heads:
        sa_ref[hh] = jnp.where(causal, scores(hh, i), NEG)
    max_a = _col_max(sa_ref)
    state = tuple(_attend_init(tq) for _ in heads)
    npair = (i + 1) // 2

    def pair(n, c):
        state, max_a = c
        t0 = 2 * n
        first = jnp.where(n == 0, i, t0 - 1)
        state, max_b = _attend_staged(sa_ref, max_a, state, [values(hh, first) for hh in heads],
                                      sb_ref, lambda hh: scores(hh, t0))
        state, max_a = _attend_staged(sb_ref, max_b, state, [values(hh, t0) for hh in heads],
                                      sa_ref, lambda hh: scores(hh, jnp.minimum(t0 + 1, i - 1)))
        return state, max_a

    state, max_a = lax.fori_loop(0, npair, pair, (state, max_a))
    last = jnp.where(i == 0, i, i - 1)
    state = lax.cond(i % 2 == 0,
                     lambda s: _attend_staged(sa_ref, max_a, s, [values(hh, last) for hh in heads])[0],
                     lambda s: s, state)
    outs = []
    for hh in heads:
        _, acc = state[hh]
        outs.append(acc[:V_HEAD_DIM, :] / acc[V_HEAD_DIM:V_HEAD_DIM + 1, :])
    return jnp.concatenate(outs, axis=0).T


def _causal_t(tq):
    key = lax.broadcasted_iota(jnp.int32, (tq, tq), 0)
    qry = lax.broadcasted_iota(jnp.int32, (tq, tq), 1)
    return key <= qry


def _moba_kernel(slope_ref, qt_ref, k_ref, e_ref, vt_ref, km_ref, o_ref, sa_ref, sb_ref):
    tq = ATT_TILE
    hp = pl.program_id(1)
    i = pl.program_id(2)
    nblk = km_ref.shape[1]
    row = lax.broadcasted_iota(jnp.int32, (LANES, tq), 0)
    pad = jnp.zeros((LANES - nblk, LANES), BF16)

    qaug = []
    for hh in range(ATT_HEADS):
        head = hp * ATT_HEADS + hh
        grp = slice(hh // 2 * LANES, (hh // 2 + 1) * LANES)
        qt2 = qt_ref[0, grp, :]
        head_rows = (row >= hh % 2 * HEAD_DIM_A) & (row < (hh % 2 + 1) * HEAD_DIM_A)
        qh = jnp.where(head_rows, qt2, jnp.zeros_like(qt2))
        kmean = jnp.concatenate([km_ref[0, :, grp].astype(BF16), pad], axis=0)
        g = jnp.where(row < i, _dot(kmean, qh), -jnp.inf)
        picked = row == i
        for r in range(MOBA_TOPK):
            m = jnp.max(g, axis=0, keepdims=True)
            idx = jnp.min(jnp.where(g == m, row, LANES), axis=0, keepdims=True)
            hit = row == idx
            picked = picked | (hit & (r < i))
            g = jnp.where(hit, -jnp.inf, g)
        dist = (i - row).astype(F32) * (slope_ref[head] * MOBA_BLOCK)
        b = jnp.where(picked, -dist, NEG)
        b_hi = b.astype(BF16)
        b_lo = (b - b_hi.astype(F32)).astype(BF16)
        one_rows = (row == POS_HI + head) | (row == POS_LO + head)
        extra = jnp.where(row < BIAS_LO, b_hi,
                          jnp.where(row < POS_HI, pltpu.roll(b_lo.astype(F32), BIAS_LO, 0).astype(BF16),
                                    jnp.where(one_rows, 1.0, 0.0).astype(BF16)))
        qaug.append(jnp.concatenate([qh, extra], axis=0))

    def scores(hh, j):
        rows = pl.ds(pl.multiple_of(j * tq, tq), tq)
        grp = slice(hh // 2 * LANES, (hh // 2 + 1) * LANES)
        return _dot(jnp.concatenate([k_ref[rows, grp], e_ref[rows, :]], axis=1), qaug[hh])

    def values(hh, j):
        return vt_ref[j, hh * LANES:hh * LANES + PV_ROWS, :]

    o_ref[...] = _attention_loop(i, scores, values, _causal_t(tq), tq, sa_ref, sb_ref).astype(BF16)


def _moba(qat, ka, vat, kmean, etab, slopes, batch, seq):
    t = ka.shape[0]
    tq = ATT_TILE
    nq = seq // tq
    nblk = seq // MOBA_BLOCK
    nh = ATT_HEADS
    km = kmean.reshape(batch, nblk, WIDTH_A)
    return pl.pallas_call(
        _moba_kernel,
        grid=(batch, N_HEADS_A // nh, nq),
        in_specs=[
            pl.BlockSpec(memory_space=pltpu.SMEM),
            pl.BlockSpec((1, nh * HEAD_DIM_A, tq), lambda b, h, i: (b * nq + i, h, 0)),
            pl.BlockSpec((seq, nh * HEAD_DIM_A), lambda b, h, i: (b, h)),
            pl.BlockSpec((seq, LANES), lambda b, h, i: (0, 0)),
            pl.BlockSpec((nq, nh * LANES, tq), lambda b, h, i: (b, h, 0)),
            pl.BlockSpec((1, nblk, nh * HEAD_DIM_A), lambda b, h, i: (b, 0, h)),
        ],
        out_specs=pl.BlockSpec((tq, nh * HEAD_DIM_A), lambda b, h, i: (b * nq + i, h)),
        out_shape=jax.ShapeDtypeStruct((t, WIDTH_A), BF16),
        scratch_shapes=[pltpu.VMEM((nh, tq, tq), F32), pltpu.VMEM((nh, tq, tq), F32)],
        compiler_params=_params("parallel", "parallel", "arbitrary"),
        name="moba",
    )(slopes, qat, ka, etab, vat, km)


def _mla_kernel(qt_ref, k_ref, vt_ref, o_ref, sa_ref, sb_ref):
    tq = ATT_TILE
    i = pl.program_id(2)
    qt = [qt_ref[0, hh * LANES:(hh + 1) * LANES, :] for hh in range(ATT_HEADS)]

    def scores(hh, j):
        return _dot(k_ref[pl.ds(pl.multiple_of(j * tq, tq), tq), hh * LANES:(hh + 1) * LANES], qt[hh])

    def values(hh, j):
        return vt_ref[j, hh * LANES:hh * LANES + PV_ROWS, :]

    o_ref[...] = _attention_loop(i, scores, values, _causal_t(tq), tq, sa_ref, sb_ref).astype(BF16)


def _mla(qmt, km, vmt, batch, seq):
    t = km.shape[0]
    tq = ATT_TILE
    nq = seq // tq
    nh = ATT_HEADS
    return pl.pallas_call(
        _mla_kernel,
        grid=(batch, N_HEADS_B // nh, nq),
        in_specs=[
            pl.BlockSpec((1, nh * LANES, tq), lambda b, h, i: (b * nq + i, h, 0)),
            pl.BlockSpec((seq, nh * LANES), lambda b, h, i: (b, h)),
            pl.BlockSpec((nq, nh * LANES, tq), lambda b, h, i: (b, h, 0)),
        ],
        out_specs=pl.BlockSpec((tq, nh * V_HEAD_DIM), lambda b, h, i: (b * nq + i, h)),
        out_shape=jax.ShapeDtypeStruct((t, WIDTH_B), BF16),
        scratch_shapes=[pltpu.VMEM((nh, tq, tq), F32), pltpu.VMEM((nh, tq, tq), F32)],
        compiler_params=_params("parallel", "parallel", "arbitrary"),
        name="mla",
    )(qmt, km, vmt)


def _pack_halves(y):
    lo = pltpu.bitcast(y[:, :HALF].astype(BF16).astype(F32), U32)
    hi = pltpu.bitcast(y[:, HALF:].astype(BF16).astype(F32), U32)
    return (hi & jnp.uint32(0xFFFF0000)) | (lo >> 16)


def _unpack_halves(w):
    return pltpu.bitcast(w << 16, F32), pltpu.bitcast(w & jnp.uint32(0xFFFF0000), F32)


def _merge_kernel(x_ref, oa_ref, ob_ref, sa_ref, sb_ref, wpa_ref, wpb_ref, wo_ref, g_ref, b_ref, o_ref, op_ref):
    pa = _dot(oa_ref[...], wpa_ref[...])
    pb = _dot(ob_ref[...], wpb_ref[...])
    merged = sa_ref[...].astype(F32) * pa + sb_ref[...].astype(F32) * pb
    hmix = _dot(merged.astype(BF16), wo_ref[...])
    y = _layer_norm(ALPHA * x_ref[...] + hmix, g_ref[...], b_ref[...])
    o_ref[...] = y
    op_ref[...] = _pack_halves(y)


def _merge(x, oa, ob, sa, sb, wp):
    t = x.shape[0]
    tm = ROW_TILE
    row = lambda w: pl.BlockSpec((tm, w), lambda i: (i, 0))
    weights = [wp['wpa'], wp['wpb'], wp['wo'], wp['ln1_g'], wp['ln1_b']]
    return pl.pallas_call(
        _merge_kernel,
        grid=(t // tm,),
        in_specs=[row(D_MODEL), row(WIDTH_A), row(WIDTH_B), row(D_MODEL), row(D_MODEL)] + [_full(w) for w in weights],
        out_specs=[row(D_MODEL), row(HALF)],
        out_shape=[jax.ShapeDtypeStruct((t, D_MODEL), F32), jax.ShapeDtypeStruct((t, HALF), U32)],
        compiler_params=_params("parallel"),
        name="merge",
    )(x, oa, ob, sa, sb, *weights)


def _router_kernel(x_ref, wh_ref, wl_ref, rb_ref, idx_ref, rank_ref, w_ref, cnt_ref):
    tm = x_ref.shape[0]

    @pl.when(pl.program_id(0) == 0)
    def _():
        cnt_ref[...] = jnp.zeros_like(cnt_ref)

    x = x_ref[...]
    xh = x.astype(BF16)
    xl = (x - xh.astype(F32)).astype(BF16)
    wh, wl = wh_ref[...], wl_ref[...]
    logits = _dot_nt(wh, xh) + (_dot_nt(wh, xl) + _dot_nt(wl, xh))
    scores = _sigmoid(logits)
    choice = scores + rb_ref[...]
    row = lax.broadcasted_iota(jnp.int32, (GROUP_SIZE, tm), 0)
    groups = [choice[g * GROUP_SIZE:(g + 1) * GROUP_SIZE, :] for g in range(N_GROUPS)]
    gscore = []
    for blk in groups:
        m1 = jnp.max(blk, axis=0, keepdims=True)
        first = jnp.min(jnp.where(blk == m1, row, GROUP_SIZE), axis=0, keepdims=True)
        m2 = jnp.max(jnp.where(row == first, -jnp.inf, blk), axis=0, keepdims=True)
        gscore.append(m1 + m2)
    masked = []
    for g in range(N_GROUPS):
        ahead = jnp.zeros((1, tm), jnp.int32)
        for o in range(N_GROUPS):
            if o < g:
                ahead += (gscore[o] >= gscore[g]).astype(jnp.int32)
            elif o > g:
                ahead += (gscore[o] > gscore[g]).astype(jnp.int32)
        masked.append(jnp.where(ahead < TOPK_GROUPS, groups[g], -jnp.inf))
    cur = jnp.concatenate(masked, axis=0)
    erow = lax.broadcasted_iota(jnp.int32, (N_EXPERTS, tm), 0)
    hits, idxs, ws = [], [], []
    for _ in range(TOP_K):
        m = jnp.max(cur, axis=0, keepdims=True)
        e = jnp.min(jnp.where(cur == m, erow, N_EXPERTS), axis=0, keepdims=True)
        hit = erow == e
        hits.append(hit)
        idxs.append(e)
        ws.append(jnp.sum(jnp.where(hit, scores, 0.0), axis=0, keepdims=True))
        cur = jnp.where(hit, -jnp.inf, cur)
    total = ws[0]
    for w in ws[1:]:
        total = total + w
    member = hits[0]
    for hit in hits[1:]:
        member = member | hit
    member = jnp.where(member, 1.0, 0.0).astype(BF16)
    t_src = lax.broadcasted_iota(jnp.int32, (tm, tm), 0)
    t_dst = lax.broadcasted_iota(jnp.int32, (tm, tm), 1)
    before = _dot(member, jnp.where(t_src < t_dst, 1.0, 0.0).astype(BF16))
    base = cnt_ref[...]
    before = before + jnp.concatenate([base] * (tm // LANES), axis=1)
    cnt_ref[...] = base + _dot(member, jnp.ones((tm, LANES), BF16))
    for r in range(TOP_K):
        idx_ref[0, r:r + 1, :] = idxs[r]
        rank_ref[0, r:r + 1, :] = jnp.sum(jnp.where(hits[r], before, 0.0), axis=0, keepdims=True).astype(jnp.int32)
        w_ref[r:r + 1, :] = ws[r] / total * ROUTED_SCALE


def _router(x1, wp):
    t = x1.shape[0]
    tm = ROW_TILE
    nt = t // tm
    weights = [wp['wr_hi'], wp['wr_lo'], wp['rbias']]
    tile = pl.BlockSpec((1, TOP_K, tm), lambda i: (i, 0, 0))
    return pl.pallas_call(
        _router_kernel,
        grid=(nt,),
        in_specs=[pl.BlockSpec((tm, D_MODEL), lambda i: (i, 0))] + [_full(w) for w in weights],
        out_specs=[tile, tile, pl.BlockSpec((TOP_K, tm), lambda i: (0, i)),
                   pl.BlockSpec((N_EXPERTS, LANES), lambda i: (0, 0))],
        out_shape=[jax.ShapeDtypeStruct((nt, TOP_K, tm), jnp.int32), jax.ShapeDtypeStruct((nt, TOP_K, tm), jnp.int32),
                   jax.ShapeDtypeStruct((TOP_K, t), F32), jax.ShapeDtypeStruct((N_EXPERTS, LANES), F32)],
        compiler_params=_params("arbitrary"),
        name="router",
    )(x1, *weights)


def _row_copy(src, src_row, dst, dst_row, sem):
    return pltpu.make_async_copy(src.at[pl.ds(src_row, 1), :], dst.at[pl.ds(dst_row, 1), :], sem)


def _dispatch_kernel(pstart_ref, idx_ref, rank_ref, x_ref, xs_in, xs_out, sem):
    del xs_in
    tm = x_ref.shape[0]

    def issue(r, c):
        for k in range(TOP_K):
            dst = pstart_ref[idx_ref[0, k, r]] + rank_ref[0, k, r]
            _row_copy(x_ref, r, xs_out, dst, sem).start()
        return c

    lax.fori_loop(0, tm, issue, 0)

    def drain(r, c):
        _row_copy(x_ref, 0, xs_out, 0, sem).wait()
        return c

    lax.fori_loop(0, tm * TOP_K, drain, 0, unroll=8)


def _dispatch(x1p, idx3, rank3, pstart, n_rows):
    t = x1p.shape[0]
    tm = ROW_TILE
    tile = pl.BlockSpec((1, TOP_K, tm), lambda i: (i, 0, 0), memory_space=pltpu.SMEM)
    return pl.pallas_call(
        _dispatch_kernel,
        grid=(t // tm,),
        in_specs=[pl.BlockSpec(memory_space=pltpu.SMEM), tile, tile,
                  pl.BlockSpec((tm, HALF), lambda i: (i, 0)), pl.BlockSpec(memory_space=pl.ANY)],
        out_specs=pl.BlockSpec(memory_space=pl.ANY),
        out_shape=jax.ShapeDtypeStruct((n_rows, HALF), U32),
        scratch_shapes=[pltpu.SemaphoreType.DMA(())],
        input_output_aliases={4: 0},
        compiler_params=_params("arbitrary"),
        name="moe_dispatch",
    )(pstart, idx3, rank3, x1p, jnp.zeros((n_rows, HALF), U32))


def _expert_kernel(be_ref, nused_ref, x_ref, wg_ref, wu_ref, wd_ref, y_ref):
    @pl.when(pl.program_id(0) < nused_ref[0])
    def _():
        xlo, xhi = (h.astype(BF16) for h in _unpack_halves(x_ref[...]))
        wg = wg_ref[0, 0].astype(BF16)
        wu = wu_ref[0, 0].astype(BF16)
        g = _dot(xlo, wg[:HALF]) + _dot(xhi, wg[HALF:])
        u = _dot(xlo, wu[:HALF]) + _dot(xhi, wu[HALF:])
        a = (g * _sigmoid(g) * u).astype(BF16)
        y_ref[...] = _pack_halves(_dot(a, wd_ref[0, 0].astype(BF16)))

    @pl.when(pl.program_id(0) >= nused_ref[0])
    def _():
        y_ref[...] = jnp.zeros_like(y_ref)


def _experts(xs, block_e, nused, w_gate, w_up, w_down, layer, n_blocks):
    rows = EXPERT_BLOCK
    used = lambda b, be, nu: jnp.minimum(b, nu[0] - 1)
    grid_spec = pltpu.PrefetchScalarGridSpec(
        num_scalar_prefetch=2,
        grid=(n_blocks,),
        in_specs=[
            pl.BlockSpec((rows, HALF), lambda b, be, nu: (used(b, be, nu), 0)),
            pl.BlockSpec((1, 1, D_MODEL, D_EXPERT), lambda b, be, nu: (layer, be[b], 0, 0)),
            pl.BlockSpec((1, 1, D_MODEL, D_EXPERT), lambda b, be, nu: (layer, be[b], 0, 0)),
            pl.BlockSpec((1, 1, D_EXPERT, D_MODEL), lambda b, be, nu: (layer, be[b], 0, 0)),
        ],
        out_specs=pl.BlockSpec((rows, HALF), lambda b, be, nu: (b, 0)),
    )
    return pl.pallas_call(
        _expert_kernel,
        grid_spec=grid_spec,
        out_shape=jax.ShapeDtypeStruct((n_blocks * rows, HALF), U32),
        compiler_params=_params("arbitrary"),
        name="moe_experts",
    )(block_e, nused, xs, w_gate, w_up, w_down)


def _sc_gather_kernel(table_hbm, idx_hbm, out_hbm, idx_v, rows_v, sem):
    per_worker = idx_v.shape[0]
    base = (lax.axis_index("s") * SC_CORES + lax.axis_index("c")) * per_worker
    pltpu.sync_copy(idx_hbm.at[pl.ds(base, per_worker)], idx_v)

    @pl.loop(0, per_worker // SC_CHUNK)
    def _(c):
        off = c * SC_CHUNK
        pltpu.async_copy(table_hbm.at[idx_v.at[pl.ds(off, SC_CHUNK)]], rows_v, sem).wait()
        pltpu.sync_copy(rows_v, out_hbm.at[pl.ds(base + off, SC_CHUNK)])


def _sc_gather_rows(table, idx):
    n = idx.shape[0]
    workers = SC_CORES * SC_SUBCORES
    assert n % (workers * SC_CHUNK) == 0
    mesh = plsc.VectorSubcoreMesh(core_axis_name="c", subcore_axis_name="s", num_cores=SC_CORES,
                                  num_subcores=SC_SUBCORES)
    return pl.kernel(
        _sc_gather_kernel,
        out_type=jax.ShapeDtypeStruct((n, HALF), U32),
        mesh=mesh,
        scratch_types=[pltpu.VMEM((n // workers,), jnp.int32), pltpu.VMEM((SC_CHUNK, HALF), U32),
                       pltpu.SemaphoreType.DMA],
        name="moe_gather_sc",
    )(table, idx)


def _combine_kernel(x_ref, w_ref, *refs):
    y_refs, (wsg_ref, wsu_ref, wsd_ref, g_ref, b_ref, o_ref) = refs[:TOP_K], refs[TOP_K:]
    x = x_ref[...]
    xb = x.astype(BF16)
    g = _dot(xb, wsg_ref[...])
    u = _dot(xb, wsu_ref[...])
    shared = _dot((g * _sigmoid(g) * u).astype(BF16), wsd_ref[...])
    w = w_ref[...]
    lo, hi = (h * w[:, 0:1] for h in _unpack_halves(y_refs[0][...]))
    for k in range(1, TOP_K):
        lo_k, hi_k = _unpack_halves(y_refs[k][...])
        lo = lo + lo_k * w[:, k:k + 1]
        hi = hi + hi_k * w[:, k:k + 1]
    routed = jnp.concatenate([lo, hi], axis=1)
    o_ref[...] = _layer_norm(ALPHA * x + (shared + routed), g_ref[...], b_ref[...])


def _combine(x1, y8, w_tk, wp):
    t = x1.shape[0]
    tm = ROW_TILE
    nt = t // tm
    weights = [wp['wsg'], wp['wsu'], wp['wsd'], wp['ln2_g'], wp['ln2_b']]
    y_specs = [pl.BlockSpec((tm, HALF), lambda i, k=k: (k * nt + i, 0)) for k in range(TOP_K)]
    return pl.pallas_call(
        _combine_kernel,
        grid=(nt,),
        in_specs=[pl.BlockSpec((tm, D_MODEL), lambda i: (i, 0)), pl.BlockSpec((tm, TOP_K), lambda i: (i, 0))]
                 + y_specs + [_full(w) for w in weights],
        out_specs=pl.BlockSpec((tm, D_MODEL), lambda i: (i, 0)),
        out_shape=jax.ShapeDtypeStruct((t, D_MODEL), F32),
        compiler_params=_params("parallel"),
        name="moe_combine",
    )(x1, w_tk, *([y8] * TOP_K), *weights)


def _moe(x1, x1p, wp, w_e_gate, w_e_up, w_e_down, layer):
    t = x1.shape[0]
    rows = EXPERT_BLOCK
    n_blocks = -(-(t * TOP_K + N_EXPERTS * (rows - 1)) // rows)
    idx3, rank3, top_w, cnt = _router(x1, wp)
    counts = cnt[:, 0].astype(jnp.int32)
    padded = (counts + rows - 1) // rows * rows
    padded_end = jnp.cumsum(padded)
    pstart = (padded_end - padded).astype(jnp.int32)
    nused = (padded_end[-1] // rows).astype(jnp.int32).reshape(1)
    blocks = jnp.arange(n_blocks, dtype=jnp.int32)
    block_e = jnp.searchsorted(padded_end, jnp.minimum(blocks, nused[0] - 1) * rows, side='right')
    block_e = jnp.clip(block_e, 0, N_EXPERTS - 1).astype(jnp.int32)
    pos_kt = (pstart[idx3] + rank3).transpose(1, 0, 2).reshape(TOP_K * t)
    xs = _dispatch(x1p, idx3, rank3, pstart, n_blocks * rows)
    ys = _experts(xs, block_e, nused, w_e_gate, w_e_up, w_e_down, layer, n_blocks)
    return _combine(x1, _sc_gather_rows(ys, pos_kt), top_w.T, wp)


def _head_groups_t(w, used):
    k, h, _ = w.shape
    return jnp.pad(w, ((0, 0), (0, 0), (0, LANES - used))).reshape(k, h * LANES).T


def _prep_layer(l, w_in, b_gate, q_norm, w_uq, kv_norm, w_ukv, w_proj_a, w_proj_b, w_out, ln1_g, ln1_b,
                w_router, router_bias, w_s_gate, w_s_up, w_s_down, ln2_g, ln2_b):
    w = w_in[l]
    o = 0
    cols = {}
    for name, width in (('qa', WIDTH_A), ('ka', WIDTH_A), ('va', WIDTH_A), ('cq', Q_LORA_RANK),
                        ('ckv', KV_LORA_RANK), ('kr', QK_ROPE_DIM), ('ga', D_MODEL), ('gb', D_MODEL)):
        cols[name] = w[:, o:o + width]
        o += width
    wkr = jnp.zeros((D_MODEL, LANES), F32).at[:, QK_NOPE_DIM:QK_NOPE_DIM + QK_ROPE_DIM].set(cols['kr'])
    dqk = QK_NOPE_DIM + QK_ROPE_DIM
    wq = w_uq[l].reshape(Q_LORA_RANK, N_HEADS_B, dqk) * dqk ** -0.5
    wkv = w_ukv[l].reshape(KV_LORA_RANK, N_HEADS_B, QK_NOPE_DIM + V_HEAD_DIM)
    wuk = jnp.pad(wkv[:, :, :QK_NOPE_DIM], ((0, 0), (0, 0), (0, LANES - QK_NOPE_DIM))).reshape(KV_LORA_RANK, N_HEADS_B * LANES)
    ones = jnp.zeros((N_HEADS_B, LANES), F32).at[:, V_HEAD_DIM].set(1.0).reshape(N_HEADS_B * LANES, 1)
    wr_t = w_router[l].T
    wr_hi = wr_t.astype(BF16)
    return dict(
        wqt=(cols['qa'] * HEAD_DIM_A ** -0.5).T.astype(BF16), wk=cols['ka'].astype(BF16),
        wvt=_head_groups_t(cols['va'].reshape(D_MODEL, N_HEADS_A, HEAD_DIM_A), HEAD_DIM_A).astype(BF16), ones=ones,
        wcq=cols['cq'].astype(BF16), wckv=cols['ckv'].astype(BF16), wkr=wkr.astype(BF16),
        wg=jnp.concatenate([cols['ga'], cols['gb']], axis=1).astype(BF16),
        bg=b_gate[l].reshape(1, 2 * D_MODEL), qn=q_norm[l].reshape(1, Q_LORA_RANK), kvn=kv_norm[l].reshape(1, KV_LORA_RANK),
        wuqt=_head_groups_t(wq, dqk).astype(BF16), wuk=wuk.astype(BF16),
        wuvt=_head_groups_t(wkv[:, :, QK_NOPE_DIM:], V_HEAD_DIM).astype(BF16),
        wpa=w_proj_a[l].astype(BF16), wpb=w_proj_b[l].astype(BF16), wo=w_out[l].astype(BF16),
        ln1_g=ln1_g[l].reshape(1, D_MODEL), ln1_b=ln1_b[l].reshape(1, D_MODEL),
        wr_hi=wr_hi, wr_lo=(wr_t - wr_hi.astype(F32)).astype(BF16), rbias=router_bias[l].reshape(N_EXPERTS, 1),
        wsg=w_s_gate[l].astype(BF16), wsu=w_s_up[l].astype(BF16), wsd=w_s_down[l].astype(BF16),
        ln2_g=ln2_g[l].reshape(1, D_MODEL), ln2_b=ln2_b[l].reshape(1, D_MODEL),
    )


def _rope_tables(seq):
    pos = jnp.arange(seq, dtype=F32)
    inv_freq = ROPE_THETA ** (-jnp.arange(0, QK_ROPE_DIM, 2, dtype=F32) / QK_ROPE_DIM)
    ang = pos[:, None] * inv_freq[None, :]
    cos, sin = jnp.cos(ang), jnp.sin(ang)
    half = QK_ROPE_DIM // 2
    z = lambda n: jnp.zeros((seq, n), F32)
    c = jnp.concatenate([jnp.ones((seq, QK_NOPE_DIM), F32), cos, cos, z(LANES - QK_NOPE_DIM - QK_ROPE_DIM)], axis=1)
    s1 = jnp.concatenate([z(QK_NOPE_DIM), -sin, z(LANES - QK_NOPE_DIM - half)], axis=1)
    s2 = jnp.concatenate([z(QK_NOPE_DIM + half), sin, z(LANES - QK_NOPE_DIM - QK_ROPE_DIM)], axis=1)
    return c, s1, s2, c.T, s1.T, s2.T


def _moba_key_table(seq, slopes):
    blk = jnp.arange(seq, dtype=jnp.int32) // MOBA_BLOCK
    onehot = (blk[:, None] == jnp.arange(BIAS_LO, dtype=jnp.int32)[None, :]).astype(F32)
    inblk = (jnp.arange(seq, dtype=jnp.int32) % MOBA_BLOCK).astype(F32)[:, None] * slopes[None, :]
    hi = inblk.astype(BF16)
    lo = (inblk - hi.astype(F32)).astype(BF16)
    pad = jnp.zeros((seq, LANES - POS_LO - N_HEADS_A), BF16)
    return jnp.concatenate([onehot.astype(BF16), onehot.astype(BF16), hi, lo, pad], axis=1)


def kernel(x, w_in, b_gate, q_norm, w_uq, kv_norm, w_ukv, w_proj_a, w_proj_b, w_out, ln1_g, ln1_b, w_router, router_bias, w_e_gate, w_e_up, w_e_down, w_s_gate, w_s_up, w_s_down, ln2_g, ln2_b):
    batch, seq, d = x.shape
    assert d == D_MODEL and seq % MOBA_BLOCK == 0 and MOBA_TOPK <= seq // MOBA_BLOCK <= BIAS_LO
    assert POS_LO + N_HEADS_A <= LANES and POS_HI + N_HEADS_A <= POS_LO
    tabs = _rope_tables(seq)
    slopes = jnp.asarray(np.exp2(-8.0 * (np.arange(N_HEADS_A) + 1.0) / N_HEADS_A), F32)
    etab = _moba_key_table(seq, slopes)
    h = x.reshape(batch * seq, d)
    for l in range(DEPTH):
        wp = _prep_layer(l, w_in, b_gate, q_norm, w_uq, kv_norm, w_ukv, w_proj_a, w_proj_b, w_out, ln1_g, ln1_b,
                         w_router, router_bias, w_s_gate, w_s_up, w_s_down, ln2_g, ln2_b)
        qat, ka, vat, kmean, qmt, km, vmt, sa, sb = _inproj(h, wp, tabs, seq)
        oa = _moba(qat, ka, vat, kmean, etab, slopes, batch, seq)
        ob = _mla(qmt, km, vmt, batch, seq)
        x1, x1p = _merge(h, oa, ob, sa, sb, wp)
        h = _moe(x1, x1p, wp, w_e_gate, w_e_up, w_e_down, l)
    return h.reshape(batch, seq, d)
```

```python
import numpy as np

import jax
import jax.numpy as jnp
from jax import lax
from jax.experimental import pallas as pl
from jax.experimental.pallas import tpu as pltpu
from jax.experimental.pallas import tpu_sc as plsc

D_MODEL = 1024
N_HEADS_A = 8
HEAD_DIM_A = 64
WIDTH_A = N_HEADS_A * HEAD_DIM_A
MOBA_BLOCK = 256
MOBA_TOPK = 3
N_HEADS_B = 8
QK_NOPE_DIM = 64
QK_ROPE_DIM = 32
V_HEAD_DIM = 64
Q_LORA_RANK = 384
KV_LORA_RANK = 256
WIDTH_B = N_HEADS_B * V_HEAD_DIM
ROPE_THETA = 10000.0
N_EXPERTS = 256
TOP_K = 8
N_GROUPS = 8
TOPK_GROUPS = 4
GROUP_SIZE = N_EXPERTS // N_GROUPS
D_EXPERT = 256
D_SHARED = 256
ROUTED_SCALE = 2.5
DEPTH = 2
ALPHA = (2 * DEPTH) ** 0.25
LN_EPS = 1e-5
RMS_EPS = 1e-6

LANES = 128
NEG = -1e30
ROW_TILE = 256
ATT_TILE = 256
EXPERT_BLOCK = 256
SC_CORES = 2
SC_SUBCORES = 16
SC_CHUNK = 64
VMEM_LIMIT = 56 * 1024 * 1024
HALF = D_MODEL // 2
ATT_HEADS = 4
PV_ROWS = 80
BIAS_HI, BIAS_LO, POS_HI, POS_LO = 0, 32, 64, 72

BF16 = jnp.bfloat16
F32 = jnp.float32
U32 = jnp.uint32


def _dot(a, b):
    return jnp.dot(a, b, preferred_element_type=F32)


def _dot_nt(a, b):
    return lax.dot_general(a, b, (((1,), (1,)), ((), ())), preferred_element_type=F32)


def _sigmoid(x):
    return 1.0 / (1.0 + jnp.exp(-x))


def _layer_norm(y, g, b):
    mu = jnp.mean(y, axis=-1, keepdims=True)
    d = y - mu
    var = jnp.mean(d * d, axis=-1, keepdims=True)
    return d * lax.rsqrt(var + LN_EPS) * g + b


def _params(*sem):
    return pltpu.CompilerParams(dimension_semantics=sem, vmem_limit_bytes=VMEM_LIMIT)


def _full(a):
    return pl.BlockSpec(a.shape, lambda *_: (0,) * a.ndim)


def _inproj_kernel(x_ref, wqt_ref, wk_ref, wvt_ref, ones_ref, wcq_ref, wckv_ref, wkr_ref, wg_ref, bg_ref,
                   qn_ref, kvn_ref, wuqt_ref, wuk_ref, wuvt_ref, cos_ref, s1_ref, s2_ref, cost_ref, s1t_ref, s2t_ref,
                   qat_ref, ka_ref, vat_ref, kmean_ref, qmt_ref, km_ref, vmt_ref, sa_ref, sb_ref):
    xb = x_ref[...].astype(BF16)
    half = QK_ROPE_DIM // 2
    qat_ref[0] = _dot_nt(wqt_ref[...], xb).astype(BF16)
    k = _dot(xb, wk_ref[...])
    ka_ref[...] = k.astype(BF16)
    kmean_ref[0] = jnp.mean(k, axis=0, keepdims=True)
    vat_ref[0] = (_dot_nt(wvt_ref[...], xb) + ones_ref[...]).astype(BF16)

    cq = _dot(xb, wcq_ref[...])
    cqn = (cq * lax.rsqrt(jnp.mean(cq * cq, axis=-1, keepdims=True) + RMS_EPS) * qn_ref[...]).astype(BF16)
    ckv = _dot(xb, wckv_ref[...])
    ckvn = (ckv * lax.rsqrt(jnp.mean(ckv * ckv, axis=-1, keepdims=True) + RMS_EPS) * kvn_ref[...]).astype(BF16)
    qt = _dot_nt(wuqt_ref[...], cqn)
    ct, s1t, s2t = cost_ref[...], s1t_ref[...], s2t_ref[...]
    for h in range(N_HEADS_B):
        t = qt[h * LANES:(h + 1) * LANES, :]
        rot = t * ct + pltpu.roll(t, LANES - half, 0) * s1t + pltpu.roll(t, half, 0) * s2t
        qmt_ref[0, h * LANES:(h + 1) * LANES, :] = rot.astype(BF16)
    kn = _dot(ckvn, wuk_ref[...])
    kr = _dot(xb, wkr_ref[...])
    c, s1, s2 = cos_ref[...], s1_ref[...], s2_ref[...]
    krot = kr * c + pltpu.roll(kr, LANES - half, 1) * s1 + pltpu.roll(kr, half, 1) * s2
    for h in range(N_HEADS_B):
        sl = slice(h * LANES, (h + 1) * LANES)
        km_ref[:, sl] = (kn[:, sl] + krot).astype(BF16)
    vmt_ref[0] = (_dot_nt(wuvt_ref[...], ckvn) + ones_ref[...]).astype(BF16)

    sig = _sigmoid(_dot(xb, wg_ref[...]) + bg_ref[...])
    sa_ref[...] = sig[:, :D_MODEL].astype(BF16)
    sb_ref[...] = sig[:, D_MODEL:].astype(BF16)


def _inproj(x, wp, tabs, seq):
    t = x.shape[0]
    tm = ROW_TILE
    nt = t // tm
    npos = seq // tm
    row = lambda w: pl.BlockSpec((tm, w), lambda i: (i, 0))
    tile = lambda r: pl.BlockSpec((1, r, tm), lambda i: (i, 0, 0))
    tab = pl.BlockSpec((tm, LANES), lambda i: (i % npos, 0))
    tabt = pl.BlockSpec((LANES, tm), lambda i: (0, i % npos))
    weights = [wp['wqt'], wp['wk'], wp['wvt'], wp['ones'], wp['wcq'], wp['wckv'], wp['wkr'], wp['wg'], wp['bg'],
               wp['qn'], wp['kvn'], wp['wuqt'], wp['wuk'], wp['wuvt']]
    hl = N_HEADS_B * LANES
    out_shape = [
        jax.ShapeDtypeStruct((nt, WIDTH_A, tm), BF16), jax.ShapeDtypeStruct((t, WIDTH_A), BF16),
        jax.ShapeDtypeStruct((nt, N_HEADS_A * LANES, tm), BF16), jax.ShapeDtypeStruct((nt, 1, WIDTH_A), F32),
        jax.ShapeDtypeStruct((nt, hl, tm), BF16), jax.ShapeDtypeStruct((t, hl), BF16),
        jax.ShapeDtypeStruct((nt, hl, tm), BF16),
        jax.ShapeDtypeStruct((t, D_MODEL), BF16), jax.ShapeDtypeStruct((t, D_MODEL), BF16),
    ]
    out_specs = [tile(WIDTH_A), row(WIDTH_A), tile(N_HEADS_A * LANES),
                 pl.BlockSpec((1, 1, WIDTH_A), lambda i: (i, 0, 0)),
                 tile(hl), row(hl), tile(hl), row(D_MODEL), row(D_MODEL)]
    return pl.pallas_call(
        _inproj_kernel,
        grid=(nt,),
        in_specs=[row(D_MODEL)] + [_full(w) for w in weights] + [tab, tab, tab, tabt, tabt, tabt],
        out_specs=out_specs,
        out_shape=out_shape,
        compiler_params=_params("parallel"),
        name="inproj",
    )(x, *weights, *tabs)


def _attend_init(tq):
    return jnp.full((1, tq), -jnp.inf, F32), jnp.zeros((PV_ROWS, tq), F32)


def _col_max(s_ref):
    return [jnp.max(s_ref[hh], axis=0, keepdims=True) for hh in range(ATT_HEADS)]


def _attend_staged(cur_ref, cur_max, state, vts, nxt_ref=None, next_scores=None):
    heads = range(ATT_HEADS)
    if nxt_ref is not None:
        for hh in heads:
            nxt_ref[hh] = next_scores(hh)
    new_m, scaled, pvs = [], [], []
    for hh in heads:
        m_i, acc = state[hh]
        m_new = jnp.maximum(m_i, cur_max[hh])
        new_m.append(m_new)
        scaled.append(jnp.exp(m_i - m_new) * acc)
        pvs.append(_dot(vts[hh], jnp.exp(cur_ref[hh] - m_new).astype(BF16)))
    nxt_max = _col_max(nxt_ref) if nxt_ref is not None else cur_max
    return tuple((m, a + pv) for m, a, pv in zip(new_m, scaled, pvs)), nxt_max


def _attention_loop(i, scores, values, causal, tq, sa_ref, sb_ref):
    heads = range(ATT_HEADS)
    for hh in heads:
        sa_ref[hh] = jnp.where(causal, scores(hh, i), NEG)
    max_a = _col_max(sa_ref)
    state = tuple(_attend_init(tq) for _ in heads)
    npair = (i + 1) // 2

    def pair(n, c):
        state, max_a = c
        t0 = 2 * n
        first = jnp.where(n == 0, i, t0 - 1)
        state, max_b = _attend_staged(sa_ref, max_a, state, [values(hh, first) for hh in heads],
                                      sb_ref, lambda hh: scores(hh, t0))
        state, max_a = _attend_staged(sb_ref, max_b, state, [values(hh, t0) for hh in heads],
                                      sa_ref, lambda hh: scores(hh, jnp.minimum(t0 + 1, i - 1)))
        return state, max_a

    state, max_a = lax.fori_loop(0, npair, pair, (state, max_a))
    last = jnp.where(i == 0, i, i - 1)
    state = lax.cond(i % 2 == 0,
                     lambda s: _attend_staged(sa_ref, max_a, s, [values(hh, last) for hh in heads])[0],
                     lambda s: s, state)
    outs = []
    for hh in heads:
        _, acc = state[hh]
        outs.append(acc[:V_HEAD_DIM, :] / acc[V_HEAD_DIM:V_HEAD_DIM + 1, :])
    return jnp.concatenate(outs, axis=0).T


def _causal_t(tq):
    key = lax.broadcasted_iota(jnp.int32, (tq, tq), 0)
    qry = lax.broadcasted_iota(jnp.int32, (tq, tq), 1)
    return key <= qry


def _moba_kernel(slope_ref, qt_ref, k_ref, e_ref, vt_ref, km_ref, o_ref, sa_ref, sb_ref):
    tq = ATT_TILE
    hp = pl.program_id(1)
    i = pl.program_id(2)
    nblk = km_ref.shape[1]
    row = lax.broadcasted_iota(jnp.int32, (LANES, tq), 0)
    pad = jnp.zeros((LANES - nblk, LANES), BF16)

    qaug = []
    for hh in range(ATT_HEADS):
        head = hp * ATT_HEADS + hh
        grp = slice(hh // 2 * LANES, (hh // 2 + 1) * LANES)
        qt2 = qt_ref[0, grp, :]
        head_rows = (row >= hh % 2 * HEAD_DIM_A) & (row < (hh % 2 + 1) * HEAD_DIM_A)
        qh = jnp.where(head_rows, qt2, jnp.zeros_like(qt2))
        kmean = jnp.concatenate([km_ref[0, :, grp].astype(BF16), pad], axis=0)
        g = jnp.where(row < i, _dot(kmean, qh), -jnp.inf)
        picked = row == i
        for r in range(MOBA_TOPK):
            m = jnp.max(g, axis=0, keepdims=True)
            idx = jnp.min(jnp.where(g == m, row, LANES), axis=0, keepdims=True)
            hit = row == idx
            picked = picked | (hit & (r < i))
            g = jnp.where(hit, -jnp.inf, g)
        dist = (i - row).astype(F32) * (slope_ref[head] * MOBA_BLOCK)
        b = jnp.where(picked, -dist, NEG)
        b_hi = b.astype(BF16)
        b_lo = (b - b_hi.astype(F32)).astype(BF16)
        one_rows = (row == POS_HI + head) | (row == POS_LO + head)
        extra = jnp.where(row < BIAS_LO, b_hi,
                          jnp.where(row < POS_HI, pltpu.roll(b_lo.astype(F32), BIAS_LO, 0).astype(BF16),
                                    jnp.where(one_rows, 1.0, 0.0).astype(BF16)))
        qaug.append(jnp.concatenate([qh, extra], axis=0))

    def scores(hh, j):
        rows = pl.ds(pl.multiple_of(j * tq, tq), tq)
        grp = slice(hh // 2 * LANES, (hh // 2 + 1) * LANES)
        return _dot(jnp.concatenate([k_ref[rows, grp], e_ref[rows, :]], axis=1), qaug[hh])

    def values(hh, j):
        return vt_ref[j, hh * LANES:hh * LANES + PV_ROWS, :]

    o_ref[...] = _attention_loop(i, scores, values, _causal_t(tq), tq, sa_ref, sb_ref).astype(BF16)


def _moba(qat, ka, vat, kmean, etab, slopes, batch, seq):
    t = ka.shape[0]
    tq = ATT_TILE
    nq = seq // tq
    nblk = seq // MOBA_BLOCK
    nh = ATT_HEADS
    km = kmean.reshape(batch, nblk, WIDTH_A)
    return pl.pallas_call(
        _moba_kernel,
        grid=(batch, N_HEADS_A // nh, nq),
        in_specs=[
            pl.BlockSpec(memory_space=pltpu.SMEM),
            pl.BlockSpec((1, nh * HEAD_DIM_A, tq), lambda b, h, i: (b * nq + i, h, 0)),
            pl.BlockSpec((seq, nh * HEAD_DIM_A), lambda b, h, i: (b, h)),
            pl.BlockSpec((seq, LANES), lambda b, h, i: (0, 0)),
            pl.BlockSpec((nq, nh * LANES, tq), lambda b, h, i: (b, h, 0)),
            pl.BlockSpec((1, nblk, nh * HEAD_DIM_A), lambda b, h, i: (b, 0, h)),
        ],
        out_specs=pl.BlockSpec((tq, nh * HEAD_DIM_A), lambda b, h, i: (b * nq + i, h)),
        out_shape=jax.ShapeDtypeStruct((t, WIDTH_A), BF16),
        scratch_shapes=[pltpu.VMEM((nh, tq, tq), F32), pltpu.VMEM((nh, tq, tq), F32)],
        compiler_params=_params("parallel", "parallel", "arbitrary"),
        name="moba",
    )(slopes, qat, ka, etab, vat, km)


def _mla_kernel(qt_ref, k_ref, vt_ref, o_ref, sa_ref, sb_ref):
    tq = ATT_TILE
    i = pl.program_id(2)
    qt = [qt_ref[0, hh * LANES:(hh + 1) * LANES, :] for hh in range(ATT_HEADS)]

    def scores(hh, j):
        return _dot(k_ref[pl.ds(pl.multiple_of(j * tq, tq), tq), hh * LANES:(hh + 1) * LANES], qt[hh])

    def values(hh, j):
        return vt_ref[j, hh * LANES:hh * LANES + PV_ROWS, :]

    o_ref[...] = _attention_loop(i, scores, values, _causal_t(tq), tq, sa_ref, sb_ref).astype(BF16)


def _mla(qmt, km, vmt, batch, seq):
    t = km.shape[0]
    tq = ATT_TILE
    nq = seq // tq
    nh = ATT_HEADS
    return pl.pallas_call(
        _mla_kernel,
        grid=(batch, N_HEADS_B // nh, nq),
        in_specs=[
            pl.BlockSpec((1, nh * LANES, tq), lambda b, h, i: (b * nq + i, h, 0)),
            pl.BlockSpec((seq, nh * LANES), lambda b, h, i: (b, h)),
            pl.BlockSpec((nq, nh * LANES, tq), lambda b, h, i: (b, h, 0)),
        ],
        out_specs=pl.BlockSpec((tq, nh * V_HEAD_DIM), lambda b, h, i: (b * nq + i, h)),
        out_shape=jax.ShapeDtypeStruct((t, WIDTH_B), BF16),
        scratch_shapes=[pltpu.VMEM((nh, tq, tq), F32), pltpu.VMEM((nh, tq, tq), F32)],
        compiler_params=_params("parallel", "parallel", "arbitrary"),
        name="mla",
    )(qmt, km, vmt)


def _pack_halves(y):
    lo = pltpu.bitcast(y[:, :HALF].astype(BF16).astype(F32), U32)
    hi = pltpu.bitcast(y[:, HALF:].astype(BF16).astype(F32), U32)
    return (hi & jnp.uint32(0xFFFF0000)) | (lo >> 16)


def _unpack_halves(w):
    return pltpu.bitcast(w << 16, F32), pltpu.bitcast(w & jnp.uint32(0xFFFF0000), F32)


def _merge_kernel(x_ref, oa_ref, ob_ref, sa_ref, sb_ref, wpa_ref, wpb_ref, wo_ref, g_ref, b_ref, o_ref, op_ref):
    pa = _dot(oa_ref[...], wpa_ref[...])
    pb = _dot(ob_ref[...], wpb_ref[...])
    merged = sa_ref[...].astype(F32) * pa + sb_ref[...].astype(F32) * pb
    hmix = _dot(merged.astype(BF16), wo_ref[...])
    y = _layer_norm(ALPHA * x_ref[...] + hmix, g_ref[...], b_ref[...])
    o_ref[...] = y
    op_ref[...] = _pack_halves(y)


def _merge(x, oa, ob, sa, sb, wp):
    t = x.shape[0]
    tm = ROW_TILE
    row = lambda w: pl.BlockSpec((tm, w), lambda i: (i, 0))
    weights = [wp['wpa'], wp['wpb'], wp['wo'], wp['ln1_g'], wp['ln1_b']]
    return pl.pallas_call(
        _merge_kernel,
        grid=(t // tm,),
        in_specs=[row(D_MODEL), row(WIDTH_A), row(WIDTH_B), row(D_MODEL), row(D_MODEL)] + [_full(w) for w in weights],
        out_specs=[row(D_MODEL), row(HALF)],
        out_shape=[jax.ShapeDtypeStruct((t, D_MODEL), F32), jax.ShapeDtypeStruct((t, HALF), U32)],
        compiler_params=_params("parallel"),
        name="merge",
    )(x, oa, ob, sa, sb, *weights)


def _router_kernel(x_ref, wh_ref, wl_ref, rb_ref, idx_ref, rank_ref, w_ref, cnt_ref):
    tm = x_ref.shape[0]

    @pl.when(pl.program_id(0) == 0)
    def _():
        cnt_ref[...] = jnp.zeros_like(cnt_ref)

    x = x_ref[...]
    xh = x.astype(BF16)
    xl = (x - xh.astype(F32)).astype(BF16)
    wh, wl = wh_ref[...], wl_ref[...]
    logits = _dot_nt(wh, xh) + (_dot_nt(wh, xl) + _dot_nt(wl, xh))
    scores = _sigmoid(logits)
    choice = scores + rb_ref[...]
    row = lax.broadcasted_iota(jnp.int32, (GROUP_SIZE, tm), 0)
    groups = [choice[g * GROUP_SIZE:(g + 1) * GROUP_SIZE, :] for g in range(N_GROUPS)]
    gscore = []
    for blk in groups:
        m1 = jnp.max(blk, axis=0, keepdims=True)
        first = jnp.min(jnp.where(blk == m1, row, GROUP_SIZE), axis=0, keepdims=True)
        m2 = jnp.max(jnp.where(row == first, -jnp.inf, blk), axis=0, keepdims=True)
        gscore.append(m1 + m2)
    masked = []
    for g in range(N_GROUPS):
        ahead = jnp.zeros((1, tm), jnp.int32)
        for o in range(N_GROUPS):
            if o < g:
                ahead += (gscore[o] >= gscore[g]).astype(jnp.int32)
            elif o > g:
                ahead += (gscore[o] > gscore[g]).astype(jnp.int32)
        masked.append(jnp.where(ahead < TOPK_GROUPS, groups[g], -jnp.inf))
    cur = jnp.concatenate(masked, axis=0)
    erow = lax.broadcasted_iota(jnp.int32, (N_EXPERTS, tm), 0)
    hits, idxs, ws = [], [], []
    for _ in range(TOP_K):
        m = jnp.max(cur, axis=0, keepdims=True)
        e = jnp.min(jnp.where(cur == m, erow, N_EXPERTS), axis=0, keepdims=True)
        hit = erow == e
        hits.append(hit)
        idxs.append(e)
        ws.append(jnp.sum(jnp.where(hit, scores, 0.0), axis=0, keepdims=True))
        cur = jnp.where(hit, -jnp.inf, cur)
    total = ws[0]
    for w in ws[1:]:
        total = total + w
    member = hits[0]
    for hit in hits[1:]:
        member = member | hit
    member = jnp.where(member, 1.0, 0.0).astype(BF16)
    t_src = lax.broadcasted_iota(jnp.int32, (tm, tm), 0)
    t_dst = lax.broadcasted_iota(jnp.int32, (tm, tm), 1)
    before = _dot(member, jnp.where(t_src < t_dst, 1.0, 0.0).astype(BF16))
    base = cnt_ref[...]
    before = before + jnp.concatenate([base] * (tm // LANES), axis=1)
    cnt_ref[...] = base + _dot(member, jnp.ones((tm, LANES), BF16))
    for r in range(TOP_K):
        idx_ref[0, r:r + 1, :] = idxs[r]
        rank_ref[0, r:r + 1, :] = jnp.sum(jnp.where(hits[r], before, 0.0), axis=0, keepdims=True).astype(jnp.int32)
        w_ref[r:r + 1, :] = ws[r] / total * ROUTED_SCALE


def _router(x1, wp):
    t = x1.shape[0]
    tm = ROW_TILE
    nt = t // tm
    weights = [wp['wr_hi'], wp['wr_lo'], wp['rbias']]
    tile = pl.BlockSpec((1, TOP_K, tm), lambda i: (i, 0, 0))
    return pl.pallas_call(
        _router_kernel,
        grid=(nt,),
        in_specs=[pl.BlockSpec((tm, D_MODEL), lambda i: (i, 0))] + [_full(w) for w in weights],
        out_specs=[tile, tile, pl.BlockSpec((TOP_K, tm), lambda i: (0, i)),
                   pl.BlockSpec((N_EXPERTS, LANES), lambda i: (0, 0))],
        out_shape=[jax.ShapeDtypeStruct((nt, TOP_K, tm), jnp.int32), jax.ShapeDtypeStruct((nt, TOP_K, tm), jnp.int32),
                   jax.ShapeDtypeStruct((TOP_K, t), F32), jax.ShapeDtypeStruct((N_EXPERTS, LANES), F32)],
        compiler_params=_params("arbitrary"),
        name="router",
    )(x1, *weights)


def _pos_kernel(idx_ref, rank_ref, pq_ref, pos_ref):
    tm = idx_ref.shape[2]
    erow = lax.broadcasted_iota(jnp.int32, (N_EXPERTS, tm), 0)
    for k in range(TOP_K):
        onehot = jnp.where(erow == idx_ref[0, k:k + 1, :], 1.0, 0.0).astype(BF16)
        q = _dot(pq_ref[...], onehot)
        blk = (q[0:1, :] * 32.0 + q[1:2, :]).astype(jnp.int32)
        pos_ref[k:k + 1, :] = blk * EXPERT_BLOCK + rank_ref[0, k:k + 1, :]


def _positions(idx3, rank3, pstart):
    nt, _, tm = idx3.shape
    blk = pstart // EXPERT_BLOCK
    pq = jnp.zeros((8, N_EXPERTS), F32).at[0].set((blk // 32).astype(F32)).at[1].set((blk % 32).astype(F32))
    tile = pl.BlockSpec((1, TOP_K, tm), lambda i: (i, 0, 0))
    return pl.pallas_call(
        _pos_kernel,
        grid=(nt,),
        in_specs=[tile, tile, pl.BlockSpec((8, N_EXPERTS), lambda i: (0, 0))],
        out_specs=pl.BlockSpec((TOP_K, tm), lambda i: (0, i)),
        out_shape=jax.ShapeDtypeStruct((TOP_K, nt * tm), jnp.int32),
        compiler_params=_params("parallel"),
        name="moe_positions",
    )(idx3, rank3, pq.astype(BF16))


def _sc_mesh():
    return plsc.VectorSubcoreMesh(core_axis_name="c", subcore_axis_name="s", num_cores=SC_CORES,
                                  num_subcores=SC_SUBCORES)


def _sc_worker():
    return lax.axis_index("s") * SC_CORES + lax.axis_index("c")


def _sc_scatter_kernel(x_hbm, pos_hbm, out_hbm, idx_v, rows_v):
    nchunk = idx_v.shape[1]
    w = _sc_worker()
    pltpu.sync_copy(pos_hbm.at[w], idx_v)

    @pl.loop(0, nchunk)
    def _(c):
        pltpu.sync_copy(x_hbm.at[pl.ds((w * nchunk + c) * SC_CHUNK, SC_CHUNK)], rows_v)
        for k in range(TOP_K):
            pltpu.sync_copy(rows_v, out_hbm.at[idx_v.at[k, c]])


def _sc_dispatch(x1p, pos_kt, n_rows):
    t = x1p.shape[0]
    workers = SC_CORES * SC_SUBCORES
    nchunk = t // (workers * SC_CHUNK)
    assert t == workers * nchunk * SC_CHUNK
    pos4 = pos_kt.reshape(TOP_K, workers, nchunk, SC_CHUNK).transpose(1, 0, 2, 3)
    return pl.kernel(
        _sc_scatter_kernel,
        out_type=jax.ShapeDtypeStruct((n_rows, HALF), U32),
        mesh=_sc_mesh(),
        scratch_types=[pltpu.VMEM((TOP_K, nchunk, SC_CHUNK), jnp.int32), pltpu.VMEM((SC_CHUNK, HALF), U32)],
        name="moe_dispatch_sc",
    )(x1p, pos4)


def _expert_kernel(be_ref, nused_ref, nvalid_ref, x_ref, wg_ref, wu_ref, wd_ref, y_ref):
    @pl.when(pl.program_id(0) < nused_ref[0])
    def _():
        live = lax.broadcasted_iota(jnp.int32, x_ref.shape, 0) < nvalid_ref[pl.program_id(0)]
        xlo, xhi = (h.astype(BF16) for h in _unpack_halves(jnp.where(live, x_ref[...], jnp.uint32(0))))
        wg = wg_ref[0, 0].astype(BF16)
        wu = wu_ref[0, 0].astype(BF16)
        g = _dot(xlo, wg[:HALF]) + _dot(xhi, wg[HALF:])
        u = _dot(xlo, wu[:HALF]) + _dot(xhi, wu[HALF:])
        a = (g * _sigmoid(g) * u).astype(BF16)
        y_ref[...] = _pack_halves(_dot(a, wd_ref[0, 0].astype(BF16)))

    @pl.when(pl.program_id(0) >= nused_ref[0])
    def _():
        y_ref[...] = jnp.zeros_like(y_ref)


def _experts(xs, block_e, nused, nvalid, w_gate, w_up, w_down, layer, n_blocks):
    rows = EXPERT_BLOCK
    used = lambda b, be, nu: jnp.minimum(b, nu[0] - 1)
    ix = lambda f: (lambda b, be, nu, nv: f(b, be, nu))
    grid_spec = pltpu.PrefetchScalarGridSpec(
        num_scalar_prefetch=3,
        grid=(n_blocks,),
        in_specs=[
            pl.BlockSpec((rows, HALF), ix(lambda b, be, nu: (used(b, be, nu), 0))),
            pl.BlockSpec((1, 1, D_MODEL, D_EXPERT), ix(lambda b, be, nu: (layer, be[b], 0, 0))),
            pl.BlockSpec((1, 1, D_MODEL, D_EXPERT), ix(lambda b, be, nu: (layer, be[b], 0, 0))),
            pl.BlockSpec((1, 1, D_EXPERT, D_MODEL), ix(lambda b, be, nu: (layer, be[b], 0, 0))),
        ],
        out_specs=pl.BlockSpec((rows, HALF), ix(lambda b, be, nu: (b, 0))),
    )
    return pl.pallas_call(
        _expert_kernel,
        grid_spec=grid_spec,
        out_shape=jax.ShapeDtypeStruct((n_blocks * rows, HALF), U32),
        compiler_params=_params("arbitrary"),
        name="moe_experts",
    )(block_e, nused, nvalid, xs, w_gate, w_up, w_down)


def _sc_gather_kernel(table_hbm, idx_hbm, out_hbm, idx_v, rows_v, sem):
    per_worker = idx_v.shape[0]
    base = _sc_worker() * per_worker
    pltpu.sync_copy(idx_hbm.at[pl.ds(base, per_worker)], idx_v)

    @pl.loop(0, per_worker // SC_CHUNK)
    def _(c):
        off = c * SC_CHUNK
        pltpu.async_copy(table_hbm.at[idx_v.at[pl.ds(off, SC_CHUNK)]], rows_v, sem).wait()
        pltpu.sync_copy(rows_v, out_hbm.at[pl.ds(base + off, SC_CHUNK)])


def _sc_gather_rows(table, idx):
    n = idx.shape[0]
    workers = SC_CORES * SC_SUBCORES
    assert n % (workers * SC_CHUNK) == 0
    return pl.kernel(
        _sc_gather_kernel,
        out_type=jax.ShapeDtypeStruct((n, HALF), U32),
        mesh=_sc_mesh(),
        scratch_types=[pltpu.VMEM((n // workers,), jnp.int32), pltpu.VMEM((SC_CHUNK, HALF), U32),
                       pltpu.SemaphoreType.DMA],
        name="moe_gather_sc",
    )(table, idx)


def _combine_kernel(x_ref, w_ref, *refs):
    y_refs, (wsg_ref, wsu_ref, wsd_ref, g_ref, b_ref, o_ref) = refs[:TOP_K], refs[TOP_K:]
    x = x_ref[...]
    xb = x.astype(BF16)
    g = _dot(xb, wsg_ref[...])
    u = _dot(xb, wsu_ref[...])
    shared = _dot((g * _sigmoid(g) * u).astype(BF16), wsd_ref[...])
    w = w_ref[...]
    lo, hi = (h * w[:, 0:1] for h in _unpack_halves(y_refs[0][...]))
    for k in range(1, TOP_K):
        lo_k, hi_k = _unpack_halves(y_refs[k][...])
        lo = lo + lo_k * w[:, k:k + 1]
        hi = hi + hi_k * w[:, k:k + 1]
    routed = jnp.concatenate([lo, hi], axis=1)
    o_ref[...] = _layer_norm(ALPHA * x + (shared + routed), g_ref[...], b_ref[...])


def _combine(x1, y8, w_tk, wp):
    t = x1.shape[0]
    tm = ROW_TILE
    nt = t // tm
    weights = [wp['wsg'], wp['wsu'], wp['wsd'], wp['ln2_g'], wp['ln2_b']]
    y_specs = [pl.BlockSpec((tm, HALF), lambda i, k=k: (k * nt + i, 0)) for k in range(TOP_K)]
    return pl.pallas_call(
        _combine_kernel,
        grid=(nt,),
        in_specs=[pl.BlockSpec((tm, D_MODEL), lambda i: (i, 0)), pl.BlockSpec((tm, TOP_K), lambda i: (i, 0))]
                 + y_specs + [_full(w) for w in weights],
        out_specs=pl.BlockSpec((tm, D_MODEL), lambda i: (i, 0)),
        out_shape=jax.ShapeDtypeStruct((t, D_MODEL), F32),
        compiler_params=_params("parallel"),
        name="moe_combine",
    )(x1, w_tk, *([y8] * TOP_K), *weights)


def _moe(x1, x1p, wp, w_e_gate, w_e_up, w_e_down, layer):
    t = x1.shape[0]
    rows = EXPERT_BLOCK
    n_blocks = -(-(t * TOP_K + N_EXPERTS * (rows - 1)) // rows)
    idx3, rank3, top_w, cnt = _router(x1, wp)
    counts = cnt[:, 0].astype(jnp.int32)
    padded = (counts + rows - 1) // rows * rows
    padded_end = jnp.cumsum(padded)
    pstart = (padded_end - padded).astype(jnp.int32)
    nused = (padded_end[-1] // rows).astype(jnp.int32).reshape(1)
    blocks = jnp.arange(n_blocks, dtype=jnp.int32)
    block_e = jnp.searchsorted(padded_end, jnp.minimum(blocks, nused[0] - 1) * rows, side='right')
    block_e = jnp.clip(block_e, 0, N_EXPERTS - 1).astype(jnp.int32)
    nvalid = jnp.clip(counts[block_e] - (blocks * rows - pstart[block_e]), 0, rows).astype(jnp.int32)
    pos_kt = _positions(idx3, rank3, pstart)
    xs = _sc_dispatch(x1p, pos_kt, n_blocks * rows)
    ys = _experts(xs, block_e, nused, nvalid, w_e_gate, w_e_up, w_e_down, layer, n_blocks)
    return _combine(x1, _sc_gather_rows(ys, pos_kt.reshape(TOP_K * t)), top_w.T, wp)


def _head_groups_t(w, used):
    k, h, _ = w.shape
    return jnp.pad(w, ((0, 0), (0, 0), (0, LANES - used))).reshape(k, h * LANES).T


def _prep_layer(l, w_in, b_gate, q_norm, w_uq, kv_norm, w_ukv, w_proj_a, w_proj_b, w_out, ln1_g, ln1_b,
                w_router, router_bias, w_s_gate, w_s_up, w_s_down, ln2_g, ln2_b):
    w = w_in[l]
    o = 0
    cols = {}
    for name, width in (('qa', WIDTH_A), ('ka', WIDTH_A), ('va', WIDTH_A), ('cq', Q_LORA_RANK),
                        ('ckv', KV_LORA_RANK), ('kr', QK_ROPE_DIM), ('ga', D_MODEL), ('gb', D_MODEL)):
        cols[name] = w[:, o:o + width]
        o += width
    wkr = jnp.zeros((D_MODEL, LANES), F32).at[:, QK_NOPE_DIM:QK_NOPE_DIM + QK_ROPE_DIM].set(cols['kr'])
    dqk = QK_NOPE_DIM + QK_ROPE_DIM
    wq = w_uq[l].reshape(Q_LORA_RANK, N_HEADS_B, dqk) * dqk ** -0.5
    wkv = w_ukv[l].reshape(KV_LORA_RANK, N_HEADS_B, QK_NOPE_DIM + V_HEAD_DIM)
    wuk = jnp.pad(wkv[:, :, :QK_NOPE_DIM], ((0, 0), (0, 0), (0, LANES - QK_NOPE_DIM))).reshape(KV_LORA_RANK, N_HEADS_B * LANES)
    ones = jnp.zeros((N_HEADS_B, LANES), F32).at[:, V_HEAD_DIM].set(1.0).reshape(N_HEADS_B * LANES, 1)
    wr_t = w_router[l].T
    wr_hi = wr_t.astype(BF16)
    return dict(
        wqt=(cols['qa'] * HEAD_DIM_A ** -0.5).T.astype(BF16), wk=cols['ka'].astype(BF16),
        wvt=_head_groups_t(cols['va'].reshape(D_MODEL, N_HEADS_A, HEAD_DIM_A), HEAD_DIM_A).astype(BF16), ones=ones,
        wcq=cols['cq'].astype(BF16), wckv=cols['ckv'].astype(BF16), wkr=wkr.astype(BF16),
        wg=jnp.concatenate([cols['ga'], cols['gb']], axis=1).astype(BF16),
        bg=b_gate[l].reshape(1, 2 * D_MODEL), qn=q_norm[l].reshape(1, Q_LORA_RANK), kvn=kv_norm[l].reshape(1, KV_LORA_RANK),
        wuqt=_head_groups_t(wq, dqk).astype(BF16), wuk=wuk.astype(BF16),
        wuvt=_head_groups_t(wkv[:, :, QK_NOPE_DIM:], V_HEAD_DIM).astype(BF16),
        wpa=w_proj_a[l].astype(BF16), wpb=w_proj_b[l].astype(BF16), wo=w_out[l].astype(BF16),
        ln1_g=ln1_g[l].reshape(1, D_MODEL), ln1_b=ln1_b[l].reshape(1, D_MODEL),
        wr_hi=wr_hi, wr_lo=(wr_t - wr_hi.astype(F32)).astype(BF16), rbias=router_bias[l].reshape(N_EXPERTS, 1),
        wsg=w_s_gate[l].astype(BF16), wsu=w_s_up[l].astype(BF16), wsd=w_s_down[l].astype(BF16),
        ln2_g=ln2_g[l].reshape(1, D_MODEL), ln2_b=ln2_b[l].reshape(1, D_MODEL),
    )


def _rope_tables(seq):
    pos = jnp.arange(seq, dtype=F32)
    inv_freq = ROPE_THETA ** (-jnp.arange(0, QK_ROPE_DIM, 2, dtype=F32) / QK_ROPE_DIM)
    ang = pos[:, None] * inv_freq[None, :]
    cos, sin = jnp.cos(ang), jnp.sin(ang)
    half = QK_ROPE_DIM // 2
    z = lambda n: jnp.zeros((seq, n), F32)
    c = jnp.concatenate([jnp.ones((seq, QK_NOPE_DIM), F32), cos, cos, z(LANES - QK_NOPE_DIM - QK_ROPE_DIM)], axis=1)
    s1 = jnp.concatenate([z(QK_NOPE_DIM), -sin, z(LANES - QK_NOPE_DIM - half)], axis=1)
    s2 = jnp.concatenate([z(QK_NOPE_DIM + half), sin, z(LANES - QK_NOPE_DIM - QK_ROPE_DIM)], axis=1)
    return c, s1, s2, c.T, s1.T, s2.T


def _moba_key_table(seq, slopes):
    blk = jnp.arange(seq, dtype=jnp.int32) // MOBA_BLOCK
    onehot = (blk[:, None] == jnp.arange(BIAS_LO, dtype=jnp.int32)[None, :]).astype(F32)
    inblk = (jnp.arange(seq, dtype=jnp.int32) % MOBA_BLOCK).astype(F32)[:, None] * slopes[None, :]
    hi = inblk.astype(BF16)
    lo = (inblk - hi.astype(F32)).astype(BF16)
    pad = jnp.zeros((seq, LANES - POS_LO - N_HEADS_A), BF16)
    return jnp.concatenate([onehot.astype(BF16), onehot.astype(BF16), hi, lo, pad], axis=1)


def kernel(x, w_in, b_gate, q_norm, w_uq, kv_norm, w_ukv, w_proj_a, w_proj_b, w_out, ln1_g, ln1_b, w_router, router_bias, w_e_gate, w_e_up, w_e_down, w_s_gate, w_s_up, w_s_down, ln2_g, ln2_b):
    batch, seq, d = x.shape
    assert d == D_MODEL and seq % MOBA_BLOCK == 0 and MOBA_TOPK <= seq // MOBA_BLOCK <= BIAS_LO
    assert POS_LO + N_HEADS_A <= LANES and POS_HI + N_HEADS_A <= POS_LO
    tabs = _rope_tables(seq)
    slopes = jnp.asarray(np.exp2(-8.0 * (np.arange(N_HEADS_A) + 1.0) / N_HEADS_A), F32)
    etab = _moba_key_table(seq, slopes)
    h = x.reshape(batch * seq, d)
    for l in range(DEPTH):
        wp = _prep_layer(l, w_in, b_gate, q_norm, w_uq, kv_norm, w_ukv, w_proj_a, w_proj_b, w_out, ln1_g, ln1_b,
                         w_router, router_bias, w_s_gate, w_s_up, w_s_down, ln2_g, ln2_b)
        qat, ka, vat, kmean, qmt, km, vmt, sa, sb = _inproj(h, wp, tabs, seq)
        oa = _moba(qat, ka, vat, kmean, etab, slopes, batch, seq)
        ob = _mla(qmt, km, vmt, batch, seq)
        x1, x1p = _merge(h, oa, ob, sa, sb, wp)
        h = _moe(x1, x1p, wp, w_e_gate, w_e_up, w_e_down, l)
    return h.reshape(batch, seq, d)
```

```python
import numpy as np

import jax
import jax.numpy as jnp
from jax import lax
from jax.experimental import pallas as pl
from jax.experimental.pallas import tpu as pltpu
from jax.experimental.pallas import tpu_sc as plsc

D_MODEL = 1024
N_HEADS_A = 8
HEAD_DIM_A = 64
WIDTH_A = N_HEADS_A * HEAD_DIM_A
MOBA_BLOCK = 256
MOBA_TOPK = 3
N_HEADS_B = 8
QK_NOPE_DIM = 64
QK_ROPE_DIM = 32
V_HEAD_DIM = 64
Q_LORA_RANK = 384
KV_LORA_RANK = 256
WIDTH_B = N_HEADS_B * V_HEAD_DIM
ROPE_THETA = 10000.0
N_EXPERTS = 256
TOP_K = 8
N_GROUPS = 8
TOPK_GROUPS = 4
GROUP_SIZE = N_EXPERTS // N_GROUPS
D_EXPERT = 256
D_SHARED = 256
ROUTED_SCALE = 2.5
DEPTH = 2
ALPHA = (2 * DEPTH) ** 0.25
LN_EPS = 1e-5
RMS_EPS = 1e-6

LANES = 128
NEG = -1e30
LOG2E = float(np.log2(np.e))
ROW_TILE = 256
ATT_TILE = 256
EXPERT_BLOCK = 256
SC_CORES = 2
SC_SUBCORES = 16
SC_CHUNK = 64
VMEM_LIMIT = 56 * 1024 * 1024
HALF = D_MODEL // 2
ATT_HEADS = 4
PV_ROWS = 80
BIAS_HI, BIAS_LO, POS_HI, POS_LO = 0, 32, 64, 72

BF16 = jnp.bfloat16
F32 = jnp.float32
U32 = jnp.uint32


def _dot(a, b):
    return jnp.dot(a, b, preferred_element_type=F32)


def _dot_nt(a, b):
    return lax.dot_general(a, b, (((1,), (1,)), ((), ())), preferred_element_type=F32)


def _sigmoid(x):
    return 1.0 / (1.0 + jnp.exp(-x))


def _layer_norm(y, g, b):
    mu = jnp.mean(y, axis=-1, keepdims=True)
    d = y - mu
    var = jnp.mean(d * d, axis=-1, keepdims=True)
    return d * lax.rsqrt(var + LN_EPS) * g + b


def _params(*sem):
    return pltpu.CompilerParams(dimension_semantics=sem, vmem_limit_bytes=VMEM_LIMIT)


def _full(a):
    return pl.BlockSpec(a.shape, lambda *_: (0,) * a.ndim)


def _inproj_kernel(x_ref, wqt_ref, wk_ref, wvt_ref, ones_ref, wcq_ref, wckv_ref, wkr_ref, wg_ref, bg_ref,
                   qn_ref, kvn_ref, wuqt_ref, wuk_ref, wuvt_ref, cos_ref, s1_ref, s2_ref, cost_ref, s1t_ref, s2t_ref,
                   qat_ref, ka_ref, vat_ref, kmean_ref, qmt_ref, km_ref, vmt_ref, sa_ref, sb_ref):
    xb = x_ref[...].astype(BF16)
    half = QK_ROPE_DIM // 2
    qat_ref[0] = _dot_nt(wqt_ref[...], xb).astype(BF16)
    k = _dot(xb, wk_ref[...])
    ka_ref[...] = k.astype(BF16)
    kmean_ref[0] = jnp.mean(k, axis=0, keepdims=True)
    vat_ref[0] = (_dot_nt(wvt_ref[...], xb) + ones_ref[...]).astype(BF16)

    cq = _dot(xb, wcq_ref[...])
    cqn = (cq * lax.rsqrt(jnp.mean(cq * cq, axis=-1, keepdims=True) + RMS_EPS) * qn_ref[...]).astype(BF16)
    ckv = _dot(xb, wckv_ref[...])
    ckvn = (ckv * lax.rsqrt(jnp.mean(ckv * ckv, axis=-1, keepdims=True) + RMS_EPS) * kvn_ref[...]).astype(BF16)
    qt = _dot_nt(wuqt_ref[...], cqn)
    ct, s1t, s2t = cost_ref[...], s1t_ref[...], s2t_ref[...]
    for h in range(N_HEADS_B):
        t = qt[h * LANES:(h + 1) * LANES, :]
        rot = t * ct + pltpu.roll(t, LANES - half, 0) * s1t + pltpu.roll(t, half, 0) * s2t
        qmt_ref[0, h * LANES:(h + 1) * LANES, :] = rot.astype(BF16)
    kn = _dot(ckvn, wuk_ref[...])
    kr = _dot(xb, wkr_ref[...])
    c, s1, s2 = cos_ref[...], s1_ref[...], s2_ref[...]
    krot = kr * c + pltpu.roll(kr, LANES - half, 1) * s1 + pltpu.roll(kr, half, 1) * s2
    for h in range(N_HEADS_B):
        sl = slice(h * LANES, (h + 1) * LANES)
        km_ref[:, sl] = (kn[:, sl] + krot).astype(BF16)
    vmt_ref[0] = (_dot_nt(wuvt_ref[...], ckvn) + ones_ref[...]).astype(BF16)

    sig = _sigmoid(_dot(xb, wg_ref[...]) + bg_ref[...])
    sa_ref[...] = sig[:, :D_MODEL].astype(BF16)
    sb_ref[...] = sig[:, D_MODEL:].astype(BF16)


def _inproj(x, wp, tabs, seq):
    t = x.shape[0]
    tm = ROW_TILE
    nt = t // tm
    npos = seq // tm
    row = lambda w: pl.BlockSpec((tm, w), lambda i: (i, 0))
    tile = lambda r: pl.BlockSpec((1, r, tm), lambda i: (i, 0, 0))
    tab = pl.BlockSpec((tm, LANES), lambda i: (i % npos, 0))
    tabt = pl.BlockSpec((LANES, tm), lambda i: (0, i % npos))
    weights = [wp['wqt'], wp['wk'], wp['wvt'], wp['ones'], wp['wcq'], wp['wckv'], wp['wkr'], wp['wg'], wp['bg'],
               wp['qn'], wp['kvn'], wp['wuqt'], wp['wuk'], wp['wuvt']]
    hl = N_HEADS_B * LANES
    out_shape = [
        jax.ShapeDtypeStruct((nt, WIDTH_A, tm), BF16), jax.ShapeDtypeStruct((t, WIDTH_A), BF16),
        jax.ShapeDtypeStruct((nt, N_HEADS_A * LANES, tm), BF16), jax.ShapeDtypeStruct((nt, 1, WIDTH_A), F32),
        jax.ShapeDtypeStruct((nt, hl, tm), BF16), jax.ShapeDtypeStruct((t, hl), BF16),
        jax.ShapeDtypeStruct((nt, hl, tm), BF16),
        jax.ShapeDtypeStruct((t, D_MODEL), BF16), jax.ShapeDtypeStruct((t, D_MODEL), BF16),
    ]
    out_specs = [tile(WIDTH_A), row(WIDTH_A), tile(N_HEADS_A * LANES),
                 pl.BlockSpec((1, 1, WIDTH_A), lambda i: (i, 0, 0)),
                 tile(hl), row(hl), tile(hl), row(D_MODEL), row(D_MODEL)]
    return pl.pallas_call(
        _inproj_kernel,
        grid=(nt,),
        in_specs=[row(D_MODEL)] + [_full(w) for w in weights] + [tab, tab, tab, tabt, tabt, tabt],
        out_specs=out_specs,
        out_shape=out_shape,
        compiler_params=_params("parallel"),
        name="inproj",
    )(x, *weights, *tabs)


def _attend_init(tq):
    return jnp.full((1, tq), -jnp.inf, F32), jnp.zeros((PV_ROWS, tq), F32)


def _col_max(s_ref):
    return [jnp.max(s_ref[hh], axis=0, keepdims=True) for hh in range(ATT_HEADS)]


def _attend_staged(cur_ref, cur_max, state, vts, nxt_ref=None, next_scores=None):
    heads = range(ATT_HEADS)
    if nxt_ref is not None:
        for hh in heads:
            nxt_ref[hh] = next_scores(hh)
    new_m, scaled, pvs = [], [], []
    for hh in heads:
        m_i, acc = state[hh]
        m_new = jnp.maximum(m_i, cur_max[hh])
        new_m.append(m_new)
        scaled.append(jnp.exp2(m_i - m_new) * acc)
        pvs.append(_dot(vts[hh], jnp.exp2(cur_ref[hh] - m_new).astype(BF16)))
    nxt_max = _col_max(nxt_ref) if nxt_ref is not None else cur_max
    return tuple((m, a + pv) for m, a, pv in zip(new_m, scaled, pvs)), nxt_max


def _attention_loop(i, scores, values, causal, tq, sa_ref, sb_ref):
    heads = range(ATT_HEADS)
    for hh in heads:
        sa_ref[hh] = jnp.where(causal, scores(hh, i), NEG)
    max_a = _col_max(sa_ref)
    state = tuple(_attend_init(tq) for _ in heads)
    npair = (i + 1) // 2

    def pair(n, c):
        state, max_a = c
        t0 = 2 * n
        first = jnp.where(n == 0, i, t0 - 1)
        state, max_b = _attend_staged(sa_ref, max_a, state, [values(hh, first) for hh in heads],
                                      sb_ref, lambda hh: scores(hh, t0))
        state, max_a = _attend_staged(sb_ref, max_b, state, [values(hh, t0) for hh in heads],
                                      sa_ref, lambda hh: scores(hh, jnp.minimum(t0 + 1, i - 1)))
        return state, max_a

    state, max_a = lax.fori_loop(0, npair, pair, (state, max_a))
    last = jnp.where(i == 0, i, i - 1)
    state = lax.cond(i % 2 == 0,
                     lambda s: _attend_staged(sa_ref, max_a, s, [values(hh, last) for hh in heads])[0],
                     lambda s: s, state)
    outs = []
    for hh in heads:
        _, acc = state[hh]
        outs.append(acc[:V_HEAD_DIM, :] / acc[V_HEAD_DIM:V_HEAD_DIM + 1, :])
    return jnp.concatenate(outs, axis=0).T


def _causal_t(tq):
    key = lax.broadcasted_iota(jnp.int32, (tq, tq), 0)
    qry = lax.broadcasted_iota(jnp.int32, (tq, tq), 1)
    return key <= qry


def _moba_kernel(slope_ref, qt_ref, k_ref, e_ref, vt_ref, km_ref, o_ref, sa_ref, sb_ref):
    tq = ATT_TILE
    hp = pl.program_id(1)
    i = pl.program_id(2)
    nblk = km_ref.shape[1]
    frow = lax.broadcasted_iota(jnp.int32, (LANES, tq), 0)
    row = lax.broadcasted_iota(jnp.int32, (BIAS_LO, tq), 0)
    prow = lax.broadcasted_iota(jnp.int32, (LANES - POS_HI, tq), 0) + POS_HI

    qaug = []
    for hh in range(ATT_HEADS):
        head = hp * ATT_HEADS + hh
        grp = slice(hh // 2 * LANES, (hh // 2 + 1) * LANES)
        qt2 = qt_ref[0, grp, :]
        head_rows = (frow >= hh % 2 * HEAD_DIM_A) & (frow < (hh % 2 + 1) * HEAD_DIM_A)
        qh = jnp.where(head_rows, qt2, jnp.zeros_like(qt2))
        kmean = km_ref[0, :, grp].astype(BF16)
        if nblk < BIAS_LO:
            kmean = jnp.concatenate([kmean, jnp.zeros((BIAS_LO - nblk, LANES), BF16)], axis=0)
        g = jnp.where(row < i, _dot(kmean, qh), -jnp.inf)
        picked = row == i
        for r in range(MOBA_TOPK):
            m = jnp.max(g, axis=0, keepdims=True)
            idx = jnp.min(jnp.where(g == m, row, BIAS_LO), axis=0, keepdims=True)
            hit = row == idx
            picked = picked | (hit & (r < i))
            g = jnp.where(hit, -jnp.inf, g)
        dist = (i - row).astype(F32) * (slope_ref[head] * (MOBA_BLOCK * LOG2E))
        b = jnp.where(picked, -dist, NEG)
        b_hi = b.astype(BF16)
        b_lo = (b - b_hi.astype(F32)).astype(BF16)
        ones = jnp.where((prow == POS_HI + head) | (prow == POS_LO + head), 1.0, 0.0).astype(BF16)
        qaug.append(jnp.concatenate([qh, b_hi, b_lo, ones], axis=0))

    def scores(hh, j):
        rows = pl.ds(pl.multiple_of(j * tq, tq), tq)
        grp = slice(hh // 2 * LANES, (hh // 2 + 1) * LANES)
        return _dot(jnp.concatenate([k_ref[rows, grp], e_ref[rows, :]], axis=1), qaug[hh])

    def values(hh, j):
        return vt_ref[j, hh * LANES:hh * LANES + PV_ROWS, :]

    o_ref[...] = _attention_loop(i, scores, values, _causal_t(tq), tq, sa_ref, sb_ref).astype(BF16)


def _moba(qat, ka, vat, kmean, etab, slopes, batch, seq):
    t = ka.shape[0]
    tq = ATT_TILE
    nq = seq // tq
    nblk = seq // MOBA_BLOCK
    nh = ATT_HEADS
    km = kmean.reshape(batch, nblk, WIDTH_A)
    return pl.pallas_call(
        _moba_kernel,
        grid=(batch, N_HEADS_A // nh, nq),
        in_specs=[
            pl.BlockSpec(memory_space=pltpu.SMEM),
            pl.BlockSpec((1, nh * HEAD_DIM_A, tq), lambda b, h, i: (b * nq + i, h, 0)),
            pl.BlockSpec((seq, nh * HEAD_DIM_A), lambda b, h, i: (b, h)),
            pl.BlockSpec((seq, LANES), lambda b, h, i: (0, 0)),
            pl.BlockSpec((nq, nh * LANES, tq), lambda b, h, i: (b, h, 0)),
            pl.BlockSpec((1, nblk, nh * HEAD_DIM_A), lambda b, h, i: (b, 0, h)),
        ],
        out_specs=pl.BlockSpec((tq, nh * HEAD_DIM_A), lambda b, h, i: (b * nq + i, h)),
        out_shape=jax.ShapeDtypeStruct((t, WIDTH_A), BF16),
        scratch_shapes=[pltpu.VMEM((nh, tq, tq), F32), pltpu.VMEM((nh, tq, tq), F32)],
        compiler_params=_params("parallel", "parallel", "arbitrary"),
        name="moba",
    )(slopes, qat, ka, etab, vat, km)


def _mla_kernel(qt_ref, k_ref, vt_ref, o_ref, sa_ref, sb_ref):
    tq = ATT_TILE
    i = pl.program_id(2)
    qt = [qt_ref[0, hh * LANES:(hh + 1) * LANES, :] for hh in range(ATT_HEADS)]

    def scores(hh, j):
        return _dot(k_ref[pl.ds(pl.multiple_of(j * tq, tq), tq), hh * LANES:(hh + 1) * LANES], qt[hh])

    def values(hh, j):
        return vt_ref[j, hh * LANES:hh * LANES + PV_ROWS, :]

    o_ref[...] = _attention_loop(i, scores, values, _causal_t(tq), tq, sa_ref, sb_ref).astype(BF16)


def _mla(qmt, km, vmt, batch, seq):
    t = km.shape[0]
    tq = ATT_TILE
    nq = seq // tq
    nh = ATT_HEADS
    return pl.pallas_call(
        _mla_kernel,
        grid=(batch, N_HEADS_B // nh, nq),
        in_specs=[
            pl.BlockSpec((1, nh * LANES, tq), lambda b, h, i: (b * nq + i, h, 0)),
            pl.BlockSpec((seq, nh * LANES), lambda b, h, i: (b, h)),
            pl.BlockSpec((nq, nh * LANES, tq), lambda b, h, i: (b, h, 0)),
        ],
        out_specs=pl.BlockSpec((tq, nh * V_HEAD_DIM), lambda b, h, i: (b * nq + i, h)),
        out_shape=jax.ShapeDtypeStruct((t, WIDTH_B), BF16),
        scratch_shapes=[pltpu.VMEM((nh, tq, tq), F32), pltpu.VMEM((nh, tq, tq), F32)],
        compiler_params=_params("parallel", "parallel", "arbitrary"),
        name="mla",
    )(qmt, km, vmt)


def _pack_halves(y):
    lo = pltpu.bitcast(y[:, :HALF].astype(BF16).astype(F32), U32)
    hi = pltpu.bitcast(y[:, HALF:].astype(BF16).astype(F32), U32)
    return (hi & jnp.uint32(0xFFFF0000)) | (lo >> 16)


def _unpack_halves(w):
    return pltpu.bitcast(w << 16, F32), pltpu.bitcast(w & jnp.uint32(0xFFFF0000), F32)


def _merge_kernel(x_ref, oa_ref, ob_ref, sa_ref, sb_ref, wpa_ref, wpb_ref, wo_ref, g_ref, b_ref, o_ref, op_ref):
    pa = _dot(oa_ref[...], wpa_ref[...])
    pb = _dot(ob_ref[...], wpb_ref[...])
    merged = sa_ref[...].astype(F32) * pa + sb_ref[...].astype(F32) * pb
    hmix = _dot(merged.astype(BF16), wo_ref[...])
    y = _layer_norm(ALPHA * x_ref[...] + hmix, g_ref[...], b_ref[...])
    o_ref[...] = y
    op_ref[...] = _pack_halves(y)


def _merge(x, oa, ob, sa, sb, wp):
    t = x.shape[0]
    tm = ROW_TILE
    row = lambda w: pl.BlockSpec((tm, w), lambda i: (i, 0))
    weights = [wp['wpa'], wp['wpb'], wp['wo'], wp['ln1_g'], wp['ln1_b']]
    return pl.pallas_call(
        _merge_kernel,
        grid=(t // tm,),
        in_specs=[row(D_MODEL), row(WIDTH_A), row(WIDTH_B), row(D_MODEL), row(D_MODEL)] + [_full(w) for w in weights],
        out_specs=[row(D_MODEL), row(HALF)],
        out_shape=[jax.ShapeDtypeStruct((t, D_MODEL), F32), jax.ShapeDtypeStruct((t, HALF), U32)],
        compiler_params=_params("parallel"),
        name="merge",
    )(x, oa, ob, sa, sb, *weights)


def _router_kernel(x_ref, wh_ref, wl_ref, rb_ref, idx_ref, rank_ref, w_ref, cnt_ref):
    tm = x_ref.shape[0]

    @pl.when(pl.program_id(0) == 0)
    def _():
        cnt_ref[...] = jnp.zeros_like(cnt_ref)

    x = x_ref[...]
    xh = x.astype(BF16)
    xl = (x - xh.astype(F32)).astype(BF16)
    wh, wl = wh_ref[...], wl_ref[...]
    logits = _dot_nt(wh, xh) + (_dot_nt(wh, xl) + _dot_nt(wl, xh))
    scores = _sigmoid(logits)
    choice = scores + rb_ref[...]
    row = lax.broadcasted_iota(jnp.int32, (GROUP_SIZE, tm), 0)
    groups = [choice[g * GROUP_SIZE:(g + 1) * GROUP_SIZE, :] for g in range(N_GROUPS)]
    gscore = []
    for blk in groups:
        m1 = jnp.max(blk, axis=0, keepdims=True)
        first = jnp.min(jnp.where(blk == m1, row, GROUP_SIZE), axis=0, keepdims=True)
        m2 = jnp.max(jnp.where(row == first, -jnp.inf, blk), axis=0, keepdims=True)
        gscore.append(m1 + m2)
    masked = []
    for g in range(N_GROUPS):
        ahead = jnp.zeros((1, tm), jnp.int32)
        for o in range(N_GROUPS):
            if o < g:
                ahead += (gscore[o] >= gscore[g]).astype(jnp.int32)
            elif o > g:
                ahead += (gscore[o] > gscore[g]).astype(jnp.int32)
        masked.append(jnp.where(ahead < TOPK_GROUPS, groups[g], -jnp.inf))
    cur = jnp.concatenate(masked, axis=0)
    erow = lax.broadcasted_iota(jnp.int32, (N_EXPERTS, tm), 0)
    hits, idxs, ws = [], [], []
    for _ in range(TOP_K):
        m = jnp.max(cur, axis=0, keepdims=True)
        e = jnp.min(jnp.where(cur == m, erow, N_EXPERTS), axis=0, keepdims=True)
        hit = erow == e
        hits.append(hit)
        idxs.append(e)
        ws.append(jnp.sum(jnp.where(hit, scores, 0.0), axis=0, keepdims=True))
        cur = jnp.where(hit, -jnp.inf, cur)
    total = ws[0]
    for w in ws[1:]:
        total = total + w
    member = hits[0]
    for hit in hits[1:]:
        member = member | hit
    member = jnp.where(member, 1.0, 0.0).astype(BF16)
    t_src = lax.broadcasted_iota(jnp.int32, (tm, tm), 0)
    t_dst = lax.broadcasted_iota(jnp.int32, (tm, tm), 1)
    before = _dot(member, jnp.where(t_src < t_dst, 1.0, 0.0).astype(BF16))
    base = cnt_ref[...]
    before = before + jnp.concatenate([base] * (tm // LANES), axis=1)
    cnt_ref[...] = base + _dot(member, jnp.ones((tm, LANES), BF16))
    for r in range(TOP_K):
        idx_ref[0, r:r + 1, :] = idxs[r]
        rank_ref[0, r:r + 1, :] = jnp.sum(jnp.where(hits[r], before, 0.0), axis=0, keepdims=True).astype(jnp.int32)
        w_ref[r:r + 1, :] = ws[r] / total * ROUTED_SCALE


def _router(x1, wp):
    t = x1.shape[0]
    tm = ROW_TILE
    nt = t // tm
    weights = [wp['wr_hi'], wp['wr_lo'], wp['rbias']]
    tile = pl.BlockSpec((1, TOP_K, tm), lambda i: (i, 0, 0))
    return pl.pallas_call(
        _router_kernel,
        grid=(nt,),
        in_specs=[pl.BlockSpec((tm, D_MODEL), lambda i: (i, 0))] + [_full(w) for w in weights],
        out_specs=[tile, tile, pl.BlockSpec((TOP_K, tm), lambda i: (0, i)),
                   pl.BlockSpec((N_EXPERTS, LANES), lambda i: (0, 0))],
        out_shape=[jax.ShapeDtypeStruct((nt, TOP_K, tm), jnp.int32), jax.ShapeDtypeStruct((nt, TOP_K, tm), jnp.int32),
                   jax.ShapeDtypeStruct((TOP_K, t), F32), jax.ShapeDtypeStruct((N_EXPERTS, LANES), F32)],
        compiler_params=_params("arbitrary"),
        name="router",
    )(x1, *weights)


def _pos_kernel(idx_ref, rank_ref, pq_ref, pos_ref):
    tm = idx_ref.shape[2]
    erow = lax.broadcasted_iota(jnp.int32, (N_EXPERTS, tm), 0)
    for k in range(TOP_K):
        onehot = jnp.where(erow == idx_ref[0, k:k + 1, :], 1.0, 0.0).astype(BF16)
        q = _dot(pq_ref[...], onehot)
        blk = (q[0:1, :] * 32.0 + q[1:2, :]).astype(jnp.int32)
        pos_ref[k:k + 1, :] = blk * EXPERT_BLOCK + rank_ref[0, k:k + 1, :]


def _positions(idx3, rank3, pstart):
    nt, _, tm = idx3.shape
    blk = pstart // EXPERT_BLOCK
    pq = jnp.zeros((8, N_EXPERTS), F32).at[0].set((blk // 32).astype(F32)).at[1].set((blk % 32).astype(F32))
    tile = pl.BlockSpec((1, TOP_K, tm), lambda i: (i, 0, 0))
    return pl.pallas_call(
        _pos_kernel,
        grid=(nt,),
        in_specs=[tile, tile, pl.BlockSpec((8, N_EXPERTS), lambda i: (0, 0))],
        out_specs=pl.BlockSpec((TOP_K, tm), lambda i: (0, i)),
        out_shape=jax.ShapeDtypeStruct((TOP_K, nt * tm), jnp.int32),
        compiler_params=_params("parallel"),
        name="moe_positions",
    )(idx3, rank3, pq.astype(BF16))


def _sc_mesh():
    return plsc.VectorSubcoreMesh(core_axis_name="c", subcore_axis_name="s", num_cores=SC_CORES,
                                  num_subcores=SC_SUBCORES)


def _sc_worker():
    return lax.axis_index("s") * SC_CORES + lax.axis_index("c")


def _sc_scatter_kernel(x_hbm, pos_hbm, out_hbm, idx_v, rows_v):
    nchunk = idx_v.shape[1]
    w = _sc_worker()
    pltpu.sync_copy(pos_hbm.at[w], idx_v)

    @pl.loop(0, nchunk)
    def _(c):
        pltpu.sync_copy(x_hbm.at[pl.ds((w * nchunk + c) * SC_CHUNK, SC_CHUNK)], rows_v)
        for k in range(TOP_K):
            pltpu.sync_copy(rows_v, out_hbm.at[idx_v.at[k, c]])


def _sc_dispatch(x1p, pos_kt, n_rows):
    t = x1p.shape[0]
    workers = SC_CORES * SC_SUBCORES
    nchunk = t // (workers * SC_CHUNK)
    assert t == workers * nchunk * SC_CHUNK
    pos4 = pos_kt.reshape(TOP_K, workers, nchunk, SC_CHUNK).transpose(1, 0, 2, 3)
    return pl.kernel(
        _sc_scatter_kernel,
        out_type=jax.ShapeDtypeStruct((n_rows, HALF), U32),
        mesh=_sc_mesh(),
        scratch_types=[pltpu.VMEM((TOP_K, nchunk, SC_CHUNK), jnp.int32), pltpu.VMEM((SC_CHUNK, HALF), U32)],
        name="moe_dispatch_sc",
    )(x1p, pos4)


def _expert_kernel(be_ref, nused_ref, nvalid_ref, x_ref, wg_ref, wu_ref, wd_ref, y_ref):
    @pl.when(pl.program_id(0) < nused_ref[0])
    def _():
        live = lax.broadcasted_iota(jnp.int32, x_ref.shape, 0) < nvalid_ref[pl.program_id(0)]
        xlo, xhi = (h.astype(BF16) for h in _unpack_halves(jnp.where(live, x_ref[...], jnp.uint32(0))))
        wg = wg_ref[0, 0].astype(BF16)
        wu = wu_ref[0, 0].astype(BF16)
        g = _dot(xlo, wg[:HALF]) + _dot(xhi, wg[HALF:])
        u = _dot(xlo, wu[:HALF]) + _dot(xhi, wu[HALF:])
        a = (g * _sigmoid(g) * u).astype(BF16)
        y_ref[...] = _pack_halves(_dot(a, wd_ref[0, 0].astype(BF16)))

    @pl.when(pl.program_id(0) >= nused_ref[0])
    def _():
        y_ref[...] = jnp.zeros_like(y_ref)


def _experts(xs, block_e, nused, nvalid, w_gate, w_up, w_down, layer, n_blocks):
    rows = EXPERT_BLOCK
    used = lambda b, be, nu: jnp.minimum(b, nu[0] - 1)
    ix = lambda f: (lambda b, be, nu, nv: f(b, be, nu))
    grid_spec = pltpu.PrefetchScalarGridSpec(
        num_scalar_prefetch=3,
        grid=(n_blocks,),
        in_specs=[
            pl.BlockSpec((rows, HALF), ix(lambda b, be, nu: (used(b, be, nu), 0))),
            pl.BlockSpec((1, 1, D_MODEL, D_EXPERT), ix(lambda b, be, nu: (layer, be[b], 0, 0))),
            pl.BlockSpec((1, 1, D_MODEL, D_EXPERT), ix(lambda b, be, nu: (layer, be[b], 0, 0))),
            pl.BlockSpec((1, 1, D_EXPERT, D_MODEL), ix(lambda b, be, nu: (layer, be[b], 0, 0))),
        ],
        out_specs=pl.BlockSpec((rows, HALF), ix(lambda b, be, nu: (b, 0))),
    )
    return pl.pallas_call(
        _expert_kernel,
        grid_spec=grid_spec,
        out_shape=jax.ShapeDtypeStruct((n_blocks * rows, HALF), U32),
        compiler_params=_params("arbitrary"),
        name="moe_experts",
    )(block_e, nused, nvalid, xs, w_gate, w_up, w_down)


def _sc_gather_kernel(table_hbm, idx_hbm, out_hbm, idx_v, rows_v, sem):
    per_worker = idx_v.shape[0]
    base = _sc_worker() * per_worker
    pltpu.sync_copy(idx_hbm.at[pl.ds(base, per_worker)], idx_v)

    @pl.loop(0, per_worker // SC_CHUNK)
    def _(c):
        off = c * SC_CHUNK
        pltpu.async_copy(table_hbm.at[idx_v.at[pl.ds(off, SC_CHUNK)]], rows_v, sem).wait()
        pltpu.sync_copy(rows_v, out_hbm.at[pl.ds(base + off, SC_CHUNK)])


def _sc_gather_rows(table, idx):
    n = idx.shape[0]
    workers = SC_CORES * SC_SUBCORES
    assert n % (workers * SC_CHUNK) == 0
    return pl.kernel(
        _sc_gather_kernel,
        out_type=jax.ShapeDtypeStruct((n, HALF), U32),
        mesh=_sc_mesh(),
        scratch_types=[pltpu.VMEM((n // workers,), jnp.int32), pltpu.VMEM((SC_CHUNK, HALF), U32),
                       pltpu.SemaphoreType.DMA],
        name="moe_gather_sc",
    )(table, idx)


def _combine_kernel(x_ref, w_ref, *refs):
    y_refs, (wsg_ref, wsu_ref, wsd_ref, g_ref, b_ref, o_ref) = refs[:TOP_K], refs[TOP_K:]
    x = x_ref[...]
    xb = x.astype(BF16)
    g = _dot(xb, wsg_ref[...])
    u = _dot(xb, wsu_ref[...])
    shared = _dot((g * _sigmoid(g) * u).astype(BF16), wsd_ref[...])
    w = w_ref[...]
    lo, hi = (h * w[:, 0:1] for h in _unpack_halves(y_refs[0][...]))
    for k in range(1, TOP_K):
        lo_k, hi_k = _unpack_halves(y_refs[k][...])
        lo = lo + lo_k * w[:, k:k + 1]
        hi = hi + hi_k * w[:, k:k + 1]
    routed = jnp.concatenate([lo, hi], axis=1)
    o_ref[...] = _layer_norm(ALPHA * x + (shared + routed), g_ref[...], b_ref[...])


def _combine(x1, y8, w_tk, wp):
    t = x1.shape[0]
    tm = ROW_TILE
    nt = t // tm
    weights = [wp['wsg'], wp['wsu'], wp['wsd'], wp['ln2_g'], wp['ln2_b']]
    y_specs = [pl.BlockSpec((tm, HALF), lambda i, k=k: (k * nt + i, 0)) for k in range(TOP_K)]
    return pl.pallas_call(
        _combine_kernel,
        grid=(nt,),
        in_specs=[pl.BlockSpec((tm, D_MODEL), lambda i: (i, 0)), pl.BlockSpec((tm, TOP_K), lambda i: (i, 0))]
                 + y_specs + [_full(w) for w in weights],
        out_specs=pl.BlockSpec((tm, D_MODEL), lambda i: (i, 0)),
        out_shape=jax.ShapeDtypeStruct((t, D_MODEL), F32),
        compiler_params=_params("parallel"),
        name="moe_combine",
    )(x1, w_tk, *([y8] * TOP_K), *weights)


def _moe(x1, x1p, wp, w_e_gate, w_e_up, w_e_down, layer):
    t = x1.shape[0]
    rows = EXPERT_BLOCK
    n_blocks = -(-(t * TOP_K + N_EXPERTS * (rows - 1)) // rows)
    idx3, rank3, top_w, cnt = _router(x1, wp)
    counts = cnt[:, 0].astype(jnp.int32)
    padded = (counts + rows - 1) // rows * rows
    padded_end = jnp.cumsum(padded)
    pstart = (padded_end - padded).astype(jnp.int32)
    nused = (padded_end[-1] // rows).astype(jnp.int32).reshape(1)
    blocks = jnp.arange(n_blocks, dtype=jnp.int32)
    block_e = jnp.searchsorted(padded_end, jnp.minimum(blocks, nused[0] - 1) * rows, side='right')
    block_e = jnp.clip(block_e, 0, N_EXPERTS - 1).astype(jnp.int32)
    nvalid = jnp.clip(counts[block_e] - (blocks * rows - pstart[block_e]), 0, rows).astype(jnp.int32)
    pos_kt = _positions(idx3, rank3, pstart)
    xs = _sc_dispatch(x1p, pos_kt, n_blocks * rows)
    ys = _experts(xs, block_e, nused, nvalid, w_e_gate, w_e_up, w_e_down, layer, n_blocks)
    return _combine(x1, _sc_gather_rows(ys, pos_kt.reshape(TOP_K * t)), top_w.T, wp)


def _head_groups_t(w, used):
    k, h, _ = w.shape
    return jnp.pad(w, ((0, 0), (0, 0), (0, LANES - used))).reshape(k, h * LANES).T


def _prep_layer(l, w_in, b_gate, q_norm, w_uq, kv_norm, w_ukv, w_proj_a, w_proj_b, w_out, ln1_g, ln1_b,
                w_router, router_bias, w_s_gate, w_s_up, w_s_down, ln2_g, ln2_b):
    w = w_in[l]
    o = 0
    cols = {}
    for name, width in (('qa', WIDTH_A), ('ka', WIDTH_A), ('va', WIDTH_A), ('cq', Q_LORA_RANK),
                        ('ckv', KV_LORA_RANK), ('kr', QK_ROPE_DIM), ('ga', D_MODEL), ('gb', D_MODEL)):
        cols[name] = w[:, o:o + width]
        o += width
    wkr = jnp.zeros((D_MODEL, LANES), F32).at[:, QK_NOPE_DIM:QK_NOPE_DIM + QK_ROPE_DIM].set(cols['kr'])
    dqk = QK_NOPE_DIM + QK_ROPE_DIM
    wq = w_uq[l].reshape(Q_LORA_RANK, N_HEADS_B, dqk) * (dqk ** -0.5 * LOG2E)
    wkv = w_ukv[l].reshape(KV_LORA_RANK, N_HEADS_B, QK_NOPE_DIM + V_HEAD_DIM)
    wuk = jnp.pad(wkv[:, :, :QK_NOPE_DIM], ((0, 0), (0, 0), (0, LANES - QK_NOPE_DIM))).reshape(KV_LORA_RANK, N_HEADS_B * LANES)
    ones = jnp.zeros((N_HEADS_B, LANES), F32).at[:, V_HEAD_DIM].set(1.0).reshape(N_HEADS_B * LANES, 1)
    wr_t = w_router[l].T
    wr_hi = wr_t.astype(BF16)
    return dict(
        wqt=(cols['qa'] * (HEAD_DIM_A ** -0.5 * LOG2E)).T.astype(BF16), wk=cols['ka'].astype(BF16),
        wvt=_head_groups_t(cols['va'].reshape(D_MODEL, N_HEADS_A, HEAD_DIM_A), HEAD_DIM_A).astype(BF16), ones=ones,
        wcq=cols['cq'].astype(BF16), wckv=cols['ckv'].astype(BF16), wkr=wkr.astype(BF16),
        wg=jnp.concatenate([cols['ga'], cols['gb']], axis=1).astype(BF16),
        bg=b_gate[l].reshape(1, 2 * D_MODEL), qn=q_norm[l].reshape(1, Q_LORA_RANK), kvn=kv_norm[l].reshape(1, KV_LORA_RANK),
        wuqt=_head_groups_t(wq, dqk).astype(BF16), wuk=wuk.astype(BF16),
        wuvt=_head_groups_t(wkv[:, :, QK_NOPE_DIM:], V_HEAD_DIM).astype(BF16),
        wpa=w_proj_a[l].astype(BF16), wpb=w_proj_b[l].astype(BF16), wo=w_out[l].astype(BF16),
        ln1_g=ln1_g[l].reshape(1, D_MODEL), ln1_b=ln1_b[l].reshape(1, D_MODEL),
        wr_hi=wr_hi, wr_lo=(wr_t - wr_hi.astype(F32)).astype(BF16), rbias=router_bias[l].reshape(N_EXPERTS, 1),
        wsg=w_s_gate[l].astype(BF16), wsu=w_s_up[l].astype(BF16), wsd=w_s_down[l].astype(BF16),
        ln2_g=ln2_g[l].reshape(1, D_MODEL), ln2_b=ln2_b[l].reshape(1, D_MODEL),
    )


def _rope_tables(seq):
    pos = jnp.arange(seq, dtype=F32)
    inv_freq = ROPE_THETA ** (-jnp.arange(0, QK_ROPE_DIM, 2, dtype=F32) / QK_ROPE_DIM)
    ang = pos[:, None] * inv_freq[None, :]
    cos, sin = jnp.cos(ang), jnp.sin(ang)
    half = QK_ROPE_DIM // 2
    z = lambda n: jnp.zeros((seq, n), F32)
    c = jnp.concatenate([jnp.ones((seq, QK_NOPE_DIM), F32), cos, cos, z(LANES - QK_NOPE_DIM - QK_ROPE_DIM)], axis=1)
    s1 = jnp.concatenate([z(QK_NOPE_DIM), -sin, z(LANES - QK_NOPE_DIM - half)], axis=1)
    s2 = jnp.concatenate([z(QK_NOPE_DIM + half), sin, z(LANES - QK_NOPE_DIM - QK_ROPE_DIM)], axis=1)
    return c, s1, s2, c.T, s1.T, s2.T


def _moba_key_table(seq, slopes):
    blk = jnp.arange(seq, dtype=jnp.int32) // MOBA_BLOCK
    onehot = (blk[:, None] == jnp.arange(BIAS_LO, dtype=jnp.int32)[None, :]).astype(F32)
    inblk = (jnp.arange(seq, dtype=jnp.int32) % MOBA_BLOCK).astype(F32)[:, None] * (slopes * LOG2E)[None, :]
    hi = inblk.astype(BF16)
    lo = (inblk - hi.astype(F32)).astype(BF16)
    pad = jnp.zeros((seq, LANES - POS_LO - N_HEADS_A), BF16)
    return jnp.concatenate([onehot.astype(BF16), onehot.astype(BF16), hi, lo, pad], axis=1)


def kernel(x, w_in, b_gate, q_norm, w_uq, kv_norm, w_ukv, w_proj_a, w_proj_b, w_out, ln1_g, ln1_b, w_router, router_bias, w_e_gate, w_e_up, w_e_down, w_s_gate, w_s_up, w_s_down, ln2_g, ln2_b):
    batch, seq, d = x.shape
    assert d == D_MODEL and seq % MOBA_BLOCK == 0 and MOBA_TOPK <= seq // MOBA_BLOCK <= BIAS_LO
    assert POS_LO + N_HEADS_A <= LANES and POS_HI + N_HEADS_A <= POS_LO
    tabs = _rope_tables(seq)
    slopes = jnp.asarray(np.exp2(-8.0 * (np.arange(N_HEADS_A) + 1.0) / N_HEADS_A), F32)
    etab = _moba_key_table(seq, slopes)
    h = x.reshape(batch * seq, d)
    for l in range(DEPTH):
        wp = _prep_layer(l, w_in, b_gate, q_norm, w_uq, kv_norm, w_ukv, w_proj_a, w_proj_b, w_out, ln1_g, ln1_b,
                         w_router, router_bias, w_s_gate, w_s_up, w_s_down, ln2_g, ln2_b)
        qat, ka, vat, kmean, qmt, km, vmt, sa, sb = _inproj(h, wp, tabs, seq)
        oa = _moba(qat, ka, vat, kmean, etab, slopes, batch, seq)
        ob = _mla(qmt, km, vmt, batch, seq)
        x1, x1p = _merge(h, oa, ob, sa, sb, wp)
        h = _moe(x1, x1p, wp, w_e_gate, w_e_up, w_e_down, l)
    return h.reshape(batch, seq, d)
```

```python
import functools

import numpy as np

import jax
import jax.numpy as jnp
from jax import lax
from jax.experimental import pallas as pl
from jax.experimental.pallas import tpu as pltpu
from jax.experimental.pallas import tpu_sc as plsc

D_MODEL = 1024
N_HEADS_A = 8
HEAD_DIM_A = 64
WIDTH_A = N_HEADS_A * HEAD_DIM_A
MOBA_BLOCK = 256
MOBA_TOPK = 3
N_HEADS_B = 8
QK_NOPE_DIM = 64
QK_ROPE_DIM = 32
V_HEAD_DIM = 64
Q_LORA_RANK = 384
KV_LORA_RANK = 256
WIDTH_B = N_HEADS_B * V_HEAD_DIM
ROPE_THETA = 10000.0
N_EXPERTS = 256
TOP_K = 8
N_GROUPS = 8
TOPK_GROUPS = 4
GROUP_SIZE = N_EXPERTS // N_GROUPS
D_EXPERT = 256
D_SHARED = 256
ROUTED_SCALE = 2.5
DEPTH = 2
ALPHA = (2 * DEPTH) ** 0.25
LN_EPS = 1e-5
RMS_EPS = 1e-6

LANES = 128
NEG = -1e30
LOG2E = float(np.log2(np.e))
ROW_TILE = 256
ATT_TILE = 256
EXPERT_BLOCK = 256
SC_CORES = 2
SC_SUBCORES = 16
SC_CHUNK = 64
VMEM_LIMIT = 56 * 1024 * 1024
HALF = D_MODEL // 2
ATT_HEADS = 4
PV_ROWS = 80
BIAS_HI, BIAS_LO, POS_HI, POS_LO = 0, 32, 64, 72

BF16 = jnp.bfloat16
F32 = jnp.float32
U32 = jnp.uint32


def _dot(a, b):
    return jnp.dot(a, b, preferred_element_type=F32)


def _dot_nt(a, b):
    return lax.dot_general(a, b, (((1,), (1,)), ((), ())), preferred_element_type=F32)


def _sigmoid(x):
    return 1.0 / (1.0 + jnp.exp(-x))


def _layer_norm(y, g, b):
    mu = jnp.mean(y, axis=-1, keepdims=True)
    d = y - mu
    var = jnp.mean(d * d, axis=-1, keepdims=True)
    return d * lax.rsqrt(var + LN_EPS) * g + b


def _params(*sem):
    return pltpu.CompilerParams(dimension_semantics=sem, vmem_limit_bytes=VMEM_LIMIT)


def _full(a):
    return pl.BlockSpec(a.shape, lambda *_: (0,) * a.ndim)


def _inproj_kernel(x_ref, wqt_ref, wk_ref, wvt_ref, ones_ref, wcq_ref, wckv_ref, wkr_ref, wg_ref, bg_ref,
                   qn_ref, kvn_ref, wuqt_ref, wuk_ref, wuvt_ref, cos_ref, s1_ref, s2_ref, cost_ref, s1t_ref, s2t_ref,
                   qat_ref, ka_ref, vat_ref, kmean_ref, qmt_ref, km_ref, vmt_ref, sa_ref, sb_ref):
    xb = x_ref[...].astype(BF16)
    half = QK_ROPE_DIM // 2
    qat_ref[0] = _dot_nt(wqt_ref[...], xb).astype(BF16)
    k = _dot(xb, wk_ref[...])
    ka_ref[...] = k.astype(BF16)
    kmean_ref[0] = jnp.mean(k, axis=0, keepdims=True)
    vat_ref[0] = (_dot_nt(wvt_ref[...], xb) + ones_ref[...]).astype(BF16)

    cq = _dot(xb, wcq_ref[...])
    cqn = (cq * lax.rsqrt(jnp.mean(cq * cq, axis=-1, keepdims=True) + RMS_EPS) * qn_ref[...]).astype(BF16)
    ckv = _dot(xb, wckv_ref[...])
    ckvn = (ckv * lax.rsqrt(jnp.mean(ckv * ckv, axis=-1, keepdims=True) + RMS_EPS) * kvn_ref[...]).astype(BF16)
    qt = _dot_nt(wuqt_ref[...], cqn)
    ct, s1t, s2t = cost_ref[...], s1t_ref[...], s2t_ref[...]
    for h in range(N_HEADS_B):
        t = qt[h * LANES:(h + 1) * LANES, :]
        rot = t * ct + pltpu.roll(t, LANES - half, 0) * s1t + pltpu.roll(t, half, 0) * s2t
        qmt_ref[0, h * LANES:(h + 1) * LANES, :] = rot.astype(BF16)
    kn = _dot(ckvn, wuk_ref[...])
    kr = _dot(xb, wkr_ref[...])
    c, s1, s2 = cos_ref[...], s1_ref[...], s2_ref[...]
    krot = kr * c + pltpu.roll(kr, LANES - half, 1) * s1 + pltpu.roll(kr, half, 1) * s2
    for h in range(N_HEADS_B):
        sl = slice(h * LANES, (h + 1) * LANES)
        km_ref[:, sl] = (kn[:, sl] + krot).astype(BF16)
    vmt_ref[0] = (_dot_nt(wuvt_ref[...], ckvn) + ones_ref[...]).astype(BF16)

    sig = _sigmoid(_dot(xb, wg_ref[...]) + bg_ref[...])
    sa_ref[...] = sig[:, :D_MODEL].astype(BF16)
    sb_ref[...] = sig[:, D_MODEL:].astype(BF16)


def _inproj(x, wp, tabs, seq):
    t = x.shape[0]
    tm = ROW_TILE
    nt = t // tm
    npos = seq // tm
    row = lambda w: pl.BlockSpec((tm, w), lambda i: (i, 0))
    tile = lambda r: pl.BlockSpec((1, r, tm), lambda i: (i, 0, 0))
    tab = pl.BlockSpec((tm, LANES), lambda i: (i % npos, 0))
    tabt = pl.BlockSpec((LANES, tm), lambda i: (0, i % npos))
    weights = [wp['wqt'], wp['wk'], wp['wvt'], wp['ones'], wp['wcq'], wp['wckv'], wp['wkr'], wp['wg'], wp['bg'],
               wp['qn'], wp['kvn'], wp['wuqt'], wp['wuk'], wp['wuvt']]
    hl = N_HEADS_B * LANES
    out_shape = [
        jax.ShapeDtypeStruct((nt, WIDTH_A, tm), BF16), jax.ShapeDtypeStruct((t, WIDTH_A), BF16),
        jax.ShapeDtypeStruct((nt, N_HEADS_A * LANES, tm), BF16), jax.ShapeDtypeStruct((nt, 1, WIDTH_A), F32),
        jax.ShapeDtypeStruct((nt, hl, tm), BF16), jax.ShapeDtypeStruct((t, hl), BF16),
        jax.ShapeDtypeStruct((nt, hl, tm), BF16),
        jax.ShapeDtypeStruct((t, D_MODEL), BF16), jax.ShapeDtypeStruct((t, D_MODEL), BF16),
    ]
    out_specs = [tile(WIDTH_A), row(WIDTH_A), tile(N_HEADS_A * LANES),
                 pl.BlockSpec((1, 1, WIDTH_A), lambda i: (i, 0, 0)),
                 tile(hl), row(hl), tile(hl), row(D_MODEL), row(D_MODEL)]
    return pl.pallas_call(
        _inproj_kernel,
        grid=(nt,),
        in_specs=[row(D_MODEL)] + [_full(w) for w in weights] + [tab, tab, tab, tabt, tabt, tabt],
        out_specs=out_specs,
        out_shape=out_shape,
        compiler_params=_params("parallel"),
        name="inproj",
    )(x, *weights, *tabs)


def _attend_init(tq):
    return jnp.full((1, tq), -jnp.inf, F32), jnp.zeros((PV_ROWS, tq), F32)


def _col_max(s_ref):
    return [jnp.max(s_ref[hh], axis=0, keepdims=True) for hh in range(ATT_HEADS)]


def _attend_staged(cur_ref, cur_max, state, vts, nxt_ref=None, next_scores=None):
    heads = range(ATT_HEADS)
    if nxt_ref is not None:
        for hh in heads:
            nxt_ref[hh] = next_scores(hh)
    new_m, scaled, pvs = [], [], []
    for hh in heads:
        m_i, acc = state[hh]
        m_new = jnp.maximum(m_i, cur_max[hh])
        new_m.append(m_new)
        scaled.append(jnp.exp2(m_i - m_new) * acc)
        pvs.append(_dot(vts[hh], jnp.exp2(cur_ref[hh] - m_new).astype(BF16)))
    nxt_max = _col_max(nxt_ref) if nxt_ref is not None else cur_max
    return tuple((m, a + pv) for m, a, pv in zip(new_m, scaled, pvs)), nxt_max


def _attention_loop(i, scores, values, causal, tq, sa_ref, sb_ref):
    heads = range(ATT_HEADS)
    for hh in heads:
        sa_ref[hh] = jnp.where(causal, scores(hh, i), NEG)
    max_a = _col_max(sa_ref)
    state = tuple(_attend_init(tq) for _ in heads)
    npair = (i + 1) // 2

    def pair(n, c):
        state, max_a = c
        t0 = 2 * n
        first = jnp.where(n == 0, i, t0 - 1)
        state, max_b = _attend_staged(sa_ref, max_a, state, [values(hh, first) for hh in heads],
                                      sb_ref, lambda hh: scores(hh, t0))
        state, max_a = _attend_staged(sb_ref, max_b, state, [values(hh, t0) for hh in heads],
                                      sa_ref, lambda hh: scores(hh, jnp.minimum(t0 + 1, i - 1)))
        return state, max_a

    state, max_a = lax.fori_loop(0, npair, pair, (state, max_a))
    last = jnp.where(i == 0, i, i - 1)
    state = lax.cond(i % 2 == 0,
                     lambda s: _attend_staged(sa_ref, max_a, s, [values(hh, last) for hh in heads])[0],
                     lambda s: s, state)
    outs = []
    for hh in heads:
        _, acc = state[hh]
        outs.append(acc[:V_HEAD_DIM, :] / acc[V_HEAD_DIM:V_HEAD_DIM + 1, :])
    return jnp.concatenate(outs, axis=0).T


def _causal_t(tq):
    key = lax.broadcasted_iota(jnp.int32, (tq, tq), 0)
    qry = lax.broadcasted_iota(jnp.int32, (tq, tq), 1)
    return key <= qry


def _moba_kernel(slope_ref, qt_ref, k_ref, e_ref, vt_ref, km_ref, o_ref, sa_ref, sb_ref):
    tq = ATT_TILE
    hp = pl.program_id(1)
    i = pl.program_id(2)
    nblk = km_ref.shape[1]
    frow = lax.broadcasted_iota(jnp.int32, (LANES, tq), 0)
    row = lax.broadcasted_iota(jnp.int32, (BIAS_LO, tq), 0)
    prow = lax.broadcasted_iota(jnp.int32, (LANES - POS_HI, tq), 0) + POS_HI

    qaug = []
    for hh in range(ATT_HEADS):
        head = hp * ATT_HEADS + hh
        grp = slice(hh // 2 * LANES, (hh // 2 + 1) * LANES)
        qt2 = qt_ref[0, grp, :]
        head_rows = (frow >= hh % 2 * HEAD_DIM_A) & (frow < (hh % 2 + 1) * HEAD_DIM_A)
        qh = jnp.where(head_rows, qt2, jnp.zeros_like(qt2))
        kmean = km_ref[0, :, grp].astype(BF16)
        if nblk < BIAS_LO:
            kmean = jnp.concatenate([kmean, jnp.zeros((BIAS_LO - nblk, LANES), BF16)], axis=0)
        g = jnp.where(row < i, _dot(kmean, qh), -jnp.inf)
        picked = row == i
        for r in range(MOBA_TOPK):
            m = jnp.max(g, axis=0, keepdims=True)
            idx = jnp.min(jnp.where(g == m, row, BIAS_LO), axis=0, keepdims=True)
            hit = row == idx
            picked = picked | (hit & (r < i))
            g = jnp.where(hit, -jnp.inf, g)
        dist = (i - row).astype(F32) * (slope_ref[head] * (MOBA_BLOCK * LOG2E))
        b = jnp.where(picked, -dist, NEG)
        b_hi = b.astype(BF16)
        b_lo = (b - b_hi.astype(F32)).astype(BF16)
        ones = jnp.where((prow == POS_HI + head) | (prow == POS_LO + head), 1.0, 0.0).astype(BF16)
        qs = (qh.astype(F32) * LOG2E).astype(BF16)
        qaug.append(jnp.concatenate([qs, b_hi, b_lo, ones], axis=0))

    def scores(hh, j):
        rows = pl.ds(pl.multiple_of(j * tq, tq), tq)
        grp = slice(hh // 2 * LANES, (hh // 2 + 1) * LANES)
        return _dot(jnp.concatenate([k_ref[rows, grp], e_ref[rows, :]], axis=1), qaug[hh])

    def values(hh, j):
        return vt_ref[j, hh * LANES:hh * LANES + PV_ROWS, :]

    o_ref[...] = _attention_loop(i, scores, values, _causal_t(tq), tq, sa_ref, sb_ref).astype(BF16)


def _moba(qat, ka, vat, kmean, etab, slopes, batch, seq):
    t = ka.shape[0]
    tq = ATT_TILE
    nq = seq // tq
    nblk = seq // MOBA_BLOCK
    nh = ATT_HEADS
    km = kmean.reshape(batch, nblk, WIDTH_A)
    return pl.pallas_call(
        _moba_kernel,
        grid=(batch, N_HEADS_A // nh, nq),
        in_specs=[
            pl.BlockSpec(memory_space=pltpu.SMEM),
            pl.BlockSpec((1, nh * HEAD_DIM_A, tq), lambda b, h, i: (b * nq + i, h, 0)),
            pl.BlockSpec((seq, nh * HEAD_DIM_A), lambda b, h, i: (b, h)),
            pl.BlockSpec((seq, LANES), lambda b, h, i: (0, 0)),
            pl.BlockSpec((nq, nh * LANES, tq), lambda b, h, i: (b, h, 0)),
            pl.BlockSpec((1, nblk, nh * HEAD_DIM_A), lambda b, h, i: (b, 0, h)),
        ],
        out_specs=pl.BlockSpec((tq, nh * HEAD_DIM_A), lambda b, h, i: (b * nq + i, h)),
        out_shape=jax.ShapeDtypeStruct((t, WIDTH_A), BF16),
        scratch_shapes=[pltpu.VMEM((nh, tq, tq), F32), pltpu.VMEM((nh, tq, tq), F32)],
        compiler_params=_params("parallel", "parallel", "arbitrary"),
        name="moba",
    )(slopes, qat, ka, etab, vat, km)


def _mla_kernel(qt_ref, k_ref, vt_ref, o_ref, sa_ref, sb_ref):
    tq = ATT_TILE
    i = pl.program_id(2)
    qt = [qt_ref[0, hh * LANES:(hh + 1) * LANES, :] for hh in range(ATT_HEADS)]

    def scores(hh, j):
        return _dot(k_ref[pl.ds(pl.multiple_of(j * tq, tq), tq), hh * LANES:(hh + 1) * LANES], qt[hh])

    def values(hh, j):
        return vt_ref[j, hh * LANES:hh * LANES + PV_ROWS, :]

    o_ref[...] = _attention_loop(i, scores, values, _causal_t(tq), tq, sa_ref, sb_ref).astype(BF16)


def _mla(qmt, km, vmt, batch, seq):
    t = km.shape[0]
    tq = ATT_TILE
    nq = seq // tq
    nh = ATT_HEADS
    return pl.pallas_call(
        _mla_kernel,
        grid=(batch, N_HEADS_B // nh, nq),
        in_specs=[
            pl.BlockSpec((1, nh * LANES, tq), lambda b, h, i: (b * nq + i, h, 0)),
            pl.BlockSpec((seq, nh * LANES), lambda b, h, i: (b, h)),
            pl.BlockSpec((nq, nh * LANES, tq), lambda b, h, i: (b, h, 0)),
        ],
        out_specs=pl.BlockSpec((tq, nh * V_HEAD_DIM), lambda b, h, i: (b * nq + i, h)),
        out_shape=jax.ShapeDtypeStruct((t, WIDTH_B), BF16),
        scratch_shapes=[pltpu.VMEM((nh, tq, tq), F32), pltpu.VMEM((nh, tq, tq), F32)],
        compiler_params=_params("parallel", "parallel", "arbitrary"),
        name="mla",
    )(qmt, km, vmt)


def _pack_halves(y):
    lo = pltpu.bitcast(y[:, :HALF].astype(BF16).astype(F32), U32)
    hi = pltpu.bitcast(y[:, HALF:].astype(BF16).astype(F32), U32)
    return (hi & jnp.uint32(0xFFFF0000)) | (lo >> 16)


def _unpack_halves(w):
    return pltpu.bitcast(w << 16, F32), pltpu.bitcast(w & jnp.uint32(0xFFFF0000), F32)


def _merge_kernel(x_ref, oa_ref, ob_ref, sa_ref, sb_ref, wpa_ref, wpb_ref, wo_ref, g_ref, b_ref, o_ref, op_ref):
    pa = _dot(oa_ref[...], wpa_ref[...])
    pb = _dot(ob_ref[...], wpb_ref[...])
    merged = sa_ref[...].astype(F32) * pa + sb_ref[...].astype(F32) * pb
    hmix = _dot(merged.astype(BF16), wo_ref[...])
    y = _layer_norm(ALPHA * x_ref[...] + hmix, g_ref[...], b_ref[...])
    o_ref[...] = y
    op_ref[...] = _pack_halves(y)


def _merge(x, oa, ob, sa, sb, wp):
    t = x.shape[0]
    tm = ROW_TILE
    row = lambda w: pl.BlockSpec((tm, w), lambda i: (i, 0))
    weights = [wp['wpa'], wp['wpb'], wp['wo'], wp['ln1_g'], wp['ln1_b']]
    return pl.pallas_call(
        _merge_kernel,
        grid=(t // tm,),
        in_specs=[row(D_MODEL), row(WIDTH_A), row(WIDTH_B), row(D_MODEL), row(D_MODEL)] + [_full(w) for w in weights],
        out_specs=[row(D_MODEL), row(HALF)],
        out_shape=[jax.ShapeDtypeStruct((t, D_MODEL), F32), jax.ShapeDtypeStruct((t, HALF), U32)],
        compiler_params=_params("parallel"),
        name="merge",
    )(x, oa, ob, sa, sb, *weights)


def _router_kernel(x_ref, wh_ref, wl_ref, rb_ref, idx_ref, rank_ref, w_ref, cnt_ref):
    tm = x_ref.shape[0]

    @pl.when(pl.program_id(0) == 0)
    def _():
        cnt_ref[...] = jnp.zeros_like(cnt_ref)

    x = x_ref[...]
    xh = x.astype(BF16)
    xl = (x - xh.astype(F32)).astype(BF16)
    wh, wl = wh_ref[...], wl_ref[...]
    logits = _dot_nt(wh, xh) + (_dot_nt(wh, xl) + _dot_nt(wl, xh))
    scores = _sigmoid(logits)
    choice = scores + rb_ref[...]
    row = lax.broadcasted_iota(jnp.int32, (GROUP_SIZE, tm), 0)
    groups = [choice[g * GROUP_SIZE:(g + 1) * GROUP_SIZE, :] for g in range(N_GROUPS)]
    gscore = []
    for blk in groups:
        m1 = jnp.max(blk, axis=0, keepdims=True)
        first = jnp.min(jnp.where(blk == m1, row, GROUP_SIZE), axis=0, keepdims=True)
        m2 = jnp.max(jnp.where(row == first, -jnp.inf, blk), axis=0, keepdims=True)
        gscore.append(m1 + m2)
    masked = []
    for g in range(N_GROUPS):
        ahead = jnp.zeros((1, tm), jnp.int32)
        for o in range(N_GROUPS):
            if o < g:
                ahead += (gscore[o] >= gscore[g]).astype(jnp.int32)
            elif o > g:
                ahead += (gscore[o] > gscore[g]).astype(jnp.int32)
        masked.append(jnp.where(ahead < TOPK_GROUPS, groups[g], -jnp.inf))
    cur = jnp.concatenate(masked, axis=0)
    erow = lax.broadcasted_iota(jnp.int32, (N_EXPERTS, tm), 0)
    hits, idxs, ws = [], [], []
    for _ in range(TOP_K):
        m = jnp.max(cur, axis=0, keepdims=True)
        e = jnp.min(jnp.where(cur == m, erow, N_EXPERTS), axis=0, keepdims=True)
        hit = erow == e
        hits.append(hit)
        idxs.append(e)
        ws.append(jnp.sum(jnp.where(hit, scores, 0.0), axis=0, keepdims=True))
        cur = jnp.where(hit, -jnp.inf, cur)
    total = ws[0]
    for w in ws[1:]:
        total = total + w
    member = hits[0]
    for hit in hits[1:]:
        member = member | hit
    member = jnp.where(member, 1.0, 0.0).astype(BF16)
    t_src = lax.broadcasted_iota(jnp.int32, (tm, tm), 0)
    t_dst = lax.broadcasted_iota(jnp.int32, (tm, tm), 1)
    before = _dot(member, jnp.where(t_src < t_dst, 1.0, 0.0).astype(BF16))
    base = cnt_ref[...]
    before = before + jnp.concatenate([base] * (tm // LANES), axis=1)
    cnt_ref[...] = base + _dot(member, jnp.ones((tm, LANES), BF16))
    for r in range(TOP_K):
        idx_ref[0, r:r + 1, :] = idxs[r]
        rank_ref[0, r:r + 1, :] = jnp.sum(jnp.where(hits[r], before, 0.0), axis=0, keepdims=True).astype(jnp.int32)
        w_ref[r:r + 1, :] = ws[r] / total * ROUTED_SCALE


def _router(x1, wp):
    t = x1.shape[0]
    tm = ROW_TILE
    nt = t // tm
    weights = [wp['wr_hi'], wp['wr_lo'], wp['rbias']]
    tile = pl.BlockSpec((1, TOP_K, tm), lambda i: (i, 0, 0))
    return pl.pallas_call(
        _router_kernel,
        grid=(nt,),
        in_specs=[pl.BlockSpec((tm, D_MODEL), lambda i: (i, 0))] + [_full(w) for w in weights],
        out_specs=[tile, tile, pl.BlockSpec((TOP_K, tm), lambda i: (0, i)),
                   pl.BlockSpec((N_EXPERTS, LANES), lambda i: (0, 0))],
        out_shape=[jax.ShapeDtypeStruct((nt, TOP_K, tm), jnp.int32), jax.ShapeDtypeStruct((nt, TOP_K, tm), jnp.int32),
                   jax.ShapeDtypeStruct((TOP_K, t), F32), jax.ShapeDtypeStruct((N_EXPERTS, LANES), F32)],
        compiler_params=_params("arbitrary"),
        name="router",
    )(x1, *weights)


def _pos_kernel(idx_ref, rank_ref, pq_ref, pos_ref):
    tm = idx_ref.shape[2]
    erow = lax.broadcasted_iota(jnp.int32, (N_EXPERTS, tm), 0)
    for k in range(TOP_K):
        onehot = jnp.where(erow == idx_ref[0, k:k + 1, :], 1.0, 0.0).astype(BF16)
        q = _dot(pq_ref[...], onehot)
        blk = (q[0:1, :] * 32.0 + q[1:2, :]).astype(jnp.int32)
        pos_ref[k:k + 1, :] = blk * EXPERT_BLOCK + rank_ref[0, k:k + 1, :]


def _positions(idx3, rank3, pstart):
    nt, _, tm = idx3.shape
    blk = pstart // EXPERT_BLOCK
    pq = jnp.zeros((8, N_EXPERTS), F32).at[0].set((blk // 32).astype(F32)).at[1].set((blk % 32).astype(F32))
    tile = pl.BlockSpec((1, TOP_K, tm), lambda i: (i, 0, 0))
    return pl.pallas_call(
        _pos_kernel,
        grid=(nt,),
        in_specs=[tile, tile, pl.BlockSpec((8, N_EXPERTS), lambda i: (0, 0))],
        out_specs=pl.BlockSpec((TOP_K, tm), lambda i: (0, i)),
        out_shape=jax.ShapeDtypeStruct((TOP_K, nt * tm), jnp.int32),
        compiler_params=_params("parallel"),
        name="moe_positions",
    )(idx3, rank3, pq.astype(BF16))


def _sc_mesh():
    return plsc.VectorSubcoreMesh(core_axis_name="c", subcore_axis_name="s", num_cores=SC_CORES,
                                  num_subcores=SC_SUBCORES)


def _sc_worker():
    return lax.axis_index("s") * SC_CORES + lax.axis_index("c")


def _sc_scatter_kernel(x_hbm, pos_hbm, out_hbm, idx_v, rows_v):
    nchunk = idx_v.shape[1]
    w = _sc_worker()
    pltpu.sync_copy(pos_hbm.at[w], idx_v)

    @pl.loop(0, nchunk)
    def _(c):
        pltpu.sync_copy(x_hbm.at[pl.ds((w * nchunk + c) * SC_CHUNK, SC_CHUNK)], rows_v)
        for k in range(TOP_K):
            pltpu.sync_copy(rows_v, out_hbm.at[idx_v.at[k, c]])


def _sc_dispatch(x1p, pos_kt, n_rows):
    t = x1p.shape[0]
    workers = SC_CORES * SC_SUBCORES
    nchunk = t // (workers * SC_CHUNK)
    assert t == workers * nchunk * SC_CHUNK
    pos4 = pos_kt.reshape(TOP_K, workers, nchunk, SC_CHUNK).transpose(1, 0, 2, 3)
    return pl.kernel(
        _sc_scatter_kernel,
        out_type=jax.ShapeDtypeStruct((n_rows, HALF), U32),
        mesh=_sc_mesh(),
        scratch_types=[pltpu.VMEM((TOP_K, nchunk, SC_CHUNK), jnp.int32), pltpu.VMEM((SC_CHUNK, HALF), U32)],
        name="moe_dispatch_sc",
    )(x1p, pos4)


def _expert_kernel(layer, be_ref, nused_ref, nvalid_ref, first_ref, slot_ref, ahead_ref, head_ref,
                   x_ref, wg_hbm, wu_hbm, wd_hbm, y_ref, wg_f, wu_f, wd_f, wg_b, wu_b, wd_b, sem):
    b = pl.program_id(0)

    def fetch(e, slot):
        return [pltpu.make_async_copy(src.at[layer, e], dst.at[slot], sem.at[slot, n])
                for n, (src, dst) in enumerate(((wg_hbm, wg_f), (wu_hbm, wu_f), (wd_hbm, wd_f)))]

    @pl.when(b == 0)
    def _():
        for copy in fetch(head_ref[0], 0):
            copy.start()

        @pl.when(head_ref[1] >= 0)
        def _():
            for copy in fetch(head_ref[1], 1):
                copy.start()

    @pl.when((b < nused_ref[0]) & (first_ref[b] == 1))
    def _():
        slot = slot_ref[b]
        for copy in fetch(be_ref[b], slot):
            copy.wait()
        wg_b[...] = wg_f[slot].astype(BF16)
        wu_b[...] = wu_f[slot].astype(BF16)
        wd_b[...] = wd_f[slot].astype(BF16)

        @pl.when(ahead_ref[b] >= 0)
        def _():
            for copy in fetch(ahead_ref[b], slot):
                copy.start()

    @pl.when(b < nused_ref[0])
    def _():
        live = lax.broadcasted_iota(jnp.int32, x_ref.shape, 0) < nvalid_ref[b]
        xlo, xhi = (h.astype(BF16) for h in _unpack_halves(jnp.where(live, x_ref[...], jnp.uint32(0))))
        g = _dot(xlo, wg_b[:HALF, :]) + _dot(xhi, wg_b[HALF:, :])
        u = _dot(xlo, wu_b[:HALF, :]) + _dot(xhi, wu_b[HALF:, :])
        a = (g * _sigmoid(g) * u).astype(BF16)
        y_ref[...] = _pack_halves(_dot(a, wd_b[...]))

    @pl.when(b >= nused_ref[0])
    def _():
        y_ref[...] = jnp.zeros_like(y_ref)


def _experts(xs, block_e, nused, nvalid, counts, w_gate, w_up, w_down, layer, n_blocks):
    rows = EXPERT_BLOCK
    blocks = jnp.arange(n_blocks, dtype=jnp.int32)
    first = ((blocks == 0) | (block_e != jnp.roll(block_e, 1))) & (blocks < nused[0])
    run = jnp.cumsum(first.astype(jnp.int32)) - 1
    run_e = jnp.nonzero(counts > 0, size=N_EXPERTS, fill_value=-1)[0].astype(jnp.int32)
    ahead = jnp.concatenate([run_e, jnp.full((2,), -1, jnp.int32)])[jnp.clip(run, 0, N_EXPERTS - 1) + 2]
    grid_spec = pltpu.PrefetchScalarGridSpec(
        num_scalar_prefetch=7,
        grid=(n_blocks,),
        in_specs=[
            pl.BlockSpec((rows, HALF), lambda b, be, nu, *_: (jnp.minimum(b, nu[0] - 1), 0)),
            pl.BlockSpec(memory_space=pl.ANY), pl.BlockSpec(memory_space=pl.ANY), pl.BlockSpec(memory_space=pl.ANY),
        ],
        out_specs=pl.BlockSpec((rows, HALF), lambda b, *_: (b, 0)),
        scratch_shapes=[
            pltpu.VMEM((2, D_MODEL, D_EXPERT), F32), pltpu.VMEM((2, D_MODEL, D_EXPERT), F32),
            pltpu.VMEM((2, D_EXPERT, D_MODEL), F32),
            pltpu.VMEM((D_MODEL, D_EXPERT), BF16), pltpu.VMEM((D_MODEL, D_EXPERT), BF16),
            pltpu.VMEM((D_EXPERT, D_MODEL), BF16),
            pltpu.SemaphoreType.DMA((2, 3)),
        ],
    )
    return pl.pallas_call(
        functools.partial(_expert_kernel, layer),
        grid_spec=grid_spec,
        out_shape=jax.ShapeDtypeStruct((n_blocks * rows, HALF), U32),
        compiler_params=_params("arbitrary"),
        name="moe_experts",
    )(block_e, nused, nvalid, first.astype(jnp.int32), (run % 2).astype(jnp.int32), ahead, run_e[:2],
      xs, w_gate, w_up, w_down)


def _sc_gather_kernel(table_hbm, idx_hbm, out_hbm, idx_v, rows_v, sem):
    per_worker = idx_v.shape[0]
    base = _sc_worker() * per_worker
    pltpu.sync_copy(idx_hbm.at[pl.ds(base, per_worker)], idx_v)

    @pl.loop(0, per_worker // SC_CHUNK)
    def _(c):
        off = c * SC_CHUNK
        pltpu.async_copy(table_hbm.at[idx_v.at[pl.ds(off, SC_CHUNK)]], rows_v, sem).wait()
        pltpu.sync_copy(rows_v, out_hbm.at[pl.ds(base + off, SC_CHUNK)])


def _sc_gather_rows(table, idx):
    n = idx.shape[0]
    workers = SC_CORES * SC_SUBCORES
    assert n % (workers * SC_CHUNK) == 0
    return pl.kernel(
        _sc_gather_kernel,
        out_type=jax.ShapeDtypeStruct((n, HALF), U32),
        mesh=_sc_mesh(),
        scratch_types=[pltpu.VMEM((n // workers,), jnp.int32), pltpu.VMEM((SC_CHUNK, HALF), U32),
                       pltpu.SemaphoreType.DMA],
        name="moe_gather_sc",
    )(table, idx)


def _combine_kernel(x_ref, w_ref, *refs):
    y_refs, (wsg_ref, wsu_ref, wsd_ref, g_ref, b_ref, o_ref) = refs[:TOP_K], refs[TOP_K:]
    x = x_ref[...]
    xb = x.astype(BF16)
    g = _dot(xb, wsg_ref[...])
    u = _dot(xb, wsu_ref[...])
    shared = _dot((g * _sigmoid(g) * u).astype(BF16), wsd_ref[...])
    w = w_ref[...]
    lo, hi = (h * w[:, 0:1] for h in _unpack_halves(y_refs[0][...]))
    for k in range(1, TOP_K):
        lo_k, hi_k = _unpack_halves(y_refs[k][...])
        lo = lo + lo_k * w[:, k:k + 1]
        hi = hi + hi_k * w[:, k:k + 1]
    routed = jnp.concatenate([lo, hi], axis=1)
    o_ref[...] = _layer_norm(ALPHA * x + (shared + routed), g_ref[...], b_ref[...])


def _combine(x1, y8, w_tk, wp):
    t = x1.shape[0]
    tm = ROW_TILE
    nt = t // tm
    weights = [wp['wsg'], wp['wsu'], wp['wsd'], wp['ln2_g'], wp['ln2_b']]
    y_specs = [pl.BlockSpec((tm, HALF), lambda i, k=k: (k * nt + i, 0)) for k in range(TOP_K)]
    return pl.pallas_call(
        _combine_kernel,
        grid=(nt,),
        in_specs=[pl.BlockSpec((tm, D_MODEL), lambda i: (i, 0)), pl.BlockSpec((tm, TOP_K), lambda i: (i, 0))]
                 + y_specs + [_full(w) for w in weights],
        out_specs=pl.BlockSpec((tm, D_MODEL), lambda i: (i, 0)),
        out_shape=jax.ShapeDtypeStruct((t, D_MODEL), F32),
        compiler_params=_params("parallel"),
        name="moe_combine",
    )(x1, w_tk, *([y8] * TOP_K), *weights)


def _moe(x1, x1p, wp, w_e_gate, w_e_up, w_e_down, layer):
    t = x1.shape[0]
    rows = EXPERT_BLOCK
    n_blocks = -(-(t * TOP_K + N_EXPERTS * (rows - 1)) // rows)
    idx3, rank3, top_w, cnt = _router(x1, wp)
    counts = cnt[:, 0].astype(jnp.int32)
    padded = (counts + rows - 1) // rows * rows
    padded_end = jnp.cumsum(padded)
    pstart = (padded_end - padded).astype(jnp.int32)
    nused = (padded_end[-1] // rows).astype(jnp.int32).reshape(1)
    blocks = jnp.arange(n_blocks, dtype=jnp.int32)
    block_e = jnp.searchsorted(padded_end, jnp.minimum(blocks, nused[0] - 1) * rows, side='right')
    block_e = jnp.clip(block_e, 0, N_EXPERTS - 1).astype(jnp.int32)
    nvalid = jnp.clip(counts[block_e] - (blocks * rows - pstart[block_e]), 0, rows).astype(jnp.int32)
    pos_kt = _positions(idx3, rank3, pstart)
    xs = _sc_dispatch(x1p, pos_kt, n_blocks * rows)
    ys = _experts(xs, block_e, nused, nvalid, counts, w_e_gate, w_e_up, w_e_down, layer, n_blocks)
    return _combine(x1, _sc_gather_rows(ys, pos_kt.reshape(TOP_K * t)), top_w.T, wp)


def _head_groups_t(w, used):
    k, h, _ = w.shape
    return jnp.pad(w, ((0, 0), (0, 0), (0, LANES - used))).reshape(k, h * LANES).T


def _prep_layer(l, w_in, b_gate, q_norm, w_uq, kv_norm, w_ukv, w_proj_a, w_proj_b, w_out, ln1_g, ln1_b,
                w_router, router_bias, w_s_gate, w_s_up, w_s_down, ln2_g, ln2_b):
    w = w_in[l]
    o = 0
    cols = {}
    for name, width in (('qa', WIDTH_A), ('ka', WIDTH_A), ('va', WIDTH_A), ('cq', Q_LORA_RANK),
                        ('ckv', KV_LORA_RANK), ('kr', QK_ROPE_DIM), ('ga', D_MODEL), ('gb', D_MODEL)):
        cols[name] = w[:, o:o + width]
        o += width
    wkr = jnp.zeros((D_MODEL, LANES), F32).at[:, QK_NOPE_DIM:QK_NOPE_DIM + QK_ROPE_DIM].set(cols['kr'])
    dqk = QK_NOPE_DIM + QK_ROPE_DIM
    wq = w_uq[l].reshape(Q_LORA_RANK, N_HEADS_B, dqk) * (dqk ** -0.5 * LOG2E)
    wkv = w_ukv[l].reshape(KV_LORA_RANK, N_HEADS_B, QK_NOPE_DIM + V_HEAD_DIM)
    wuk = jnp.pad(wkv[:, :, :QK_NOPE_DIM], ((0, 0), (0, 0), (0, LANES - QK_NOPE_DIM))).reshape(KV_LORA_RANK, N_HEADS_B * LANES)
    ones = jnp.zeros((N_HEADS_B, LANES), F32).at[:, V_HEAD_DIM].set(1.0).reshape(N_HEADS_B * LANES, 1)
    wr_t = w_router[l].T
    wr_hi = wr_t.astype(BF16)
    return dict(
        wqt=(cols['qa'] * HEAD_DIM_A ** -0.5).T.astype(BF16), wk=cols['ka'].astype(BF16),
        wvt=_head_groups_t(cols['va'].reshape(D_MODEL, N_HEADS_A, HEAD_DIM_A), HEAD_DIM_A).astype(BF16), ones=ones,
        wcq=cols['cq'].astype(BF16), wckv=cols['ckv'].astype(BF16), wkr=wkr.astype(BF16),
        wg=jnp.concatenate([cols['ga'], cols['gb']], axis=1).astype(BF16),
        bg=b_gate[l].reshape(1, 2 * D_MODEL), qn=q_norm[l].reshape(1, Q_LORA_RANK), kvn=kv_norm[l].reshape(1, KV_LORA_RANK),
        wuqt=_head_groups_t(wq, dqk).astype(BF16), wuk=wuk.astype(BF16),
        wuvt=_head_groups_t(wkv[:, :, QK_NOPE_DIM:], V_HEAD_DIM).astype(BF16),
        wpa=w_proj_a[l].astype(BF16), wpb=w_proj_b[l].astype(BF16), wo=w_out[l].astype(BF16),
        ln1_g=ln1_g[l].reshape(1, D_MODEL), ln1_b=ln1_b[l].reshape(1, D_MODEL),
        wr_hi=wr_hi, wr_lo=(wr_t - wr_hi.astype(F32)).astype(BF16), rbias=router_bias[l].reshape(N_EXPERTS, 1),
        wsg=w_s_gate[l].astype(BF16), wsu=w_s_up[l].astype(BF16), wsd=w_s_down[l].astype(BF16),
        ln2_g=ln2_g[l].reshape(1, D_MODEL), ln2_b=ln2_b[l].reshape(1, D_MODEL),
    )


def _rope_tables(seq):
    pos = jnp.arange(seq, dtype=F32)
    inv_freq = ROPE_THETA ** (-jnp.arange(0, QK_ROPE_DIM, 2, dtype=F32) / QK_ROPE_DIM)
    ang = pos[:, None] * inv_freq[None, :]
    cos, sin = jnp.cos(ang), jnp.sin(ang)
    half = QK_ROPE_DIM // 2
    z = lambda n: jnp.zeros((seq, n), F32)
    c = jnp.concatenate([jnp.ones((seq, QK_NOPE_DIM), F32), cos, cos, z(LANES - QK_NOPE_DIM - QK_ROPE_DIM)], axis=1)
    s1 = jnp.concatenate([z(QK_NOPE_DIM), -sin, z(LANES - QK_NOPE_DIM - half)], axis=1)
    s2 = jnp.concatenate([z(QK_NOPE_DIM + half), sin, z(LANES - QK_NOPE_DIM - QK_ROPE_DIM)], axis=1)
    return c, s1, s2, c.T, s1.T, s2.T


def _moba_key_table(seq, slopes):
    blk = jnp.arange(seq, dtype=jnp.int32) // MOBA_BLOCK
    onehot = (blk[:, None] == jnp.arange(BIAS_LO, dtype=jnp.int32)[None, :]).astype(F32)
    inblk = (jnp.arange(seq, dtype=jnp.int32) % MOBA_BLOCK).astype(F32)[:, None] * (slopes * LOG2E)[None, :]
    hi = inblk.astype(BF16)
    lo = (inblk - hi.astype(F32)).astype(BF16)
    pad = jnp.zeros((seq, LANES - POS_LO - N_HEADS_A), BF16)
    return jnp.concatenate([onehot.astype(BF16), onehot.astype(BF16), hi, lo, pad], axis=1)


def kernel(x, w_in, b_gate, q_norm, w_uq, kv_norm, w_ukv, w_proj_a, w_proj_b, w_out, ln1_g, ln1_b, w_router, router_bias, w_e_gate, w_e_up, w_e_down, w_s_gate, w_s_up, w_s_down, ln2_g, ln2_b):
    batch, seq, d = x.shape
    assert d == D_MODEL and seq % MOBA_BLOCK == 0 and MOBA_TOPK <= seq // MOBA_BLOCK <= BIAS_LO
    assert POS_LO + N_HEADS_A <= LANES and POS_HI + N_HEADS_A <= POS_LO
    tabs = _rope_tables(seq)
    slopes = jnp.asarray(np.exp2(-8.0 * (np.arange(N_HEADS_A) + 1.0) / N_HEADS_A), F32)
    etab = _moba_key_table(seq, slopes)
    h = x.reshape(batch * seq, d)
    for l in range(DEPTH):
        wp = _prep_layer(l, w_in, b_gate, q_norm, w_uq, kv_norm, w_ukv, w_proj_a, w_proj_b, w_out, ln1_g, ln1_b,
                         w_router, router_bias, w_s_gate, w_s_up, w_s_down, ln2_g, ln2_b)
        qat, ka, vat, kmean, qmt, km, vmt, sa, sb = _inproj(h, wp, tabs, seq)
        oa = _moba(qat, ka, vat, kmean, etab, slopes, batch, seq)
        ob = _mla(qmt, km, vmt, batch, seq)
        x1, x1p = _merge(h, oa, ob, sa, sb, wp)
        h = _moe(x1, x1p, wp, w_e_gate, w_e_up, w_e_down, l)
    return h.reshape(batch, seq, d)
```

```python
import functools

import numpy as np

import jax
import jax.numpy as jnp
from jax import lax
from jax.experimental import pallas as pl
from jax.experimental.pallas import tpu as pltpu
from jax.experimental.pallas import tpu_sc as plsc

D_MODEL = 1024
N_HEADS_A = 8
HEAD_DIM_A = 64
WIDTH_A = N_HEADS_A * HEAD_DIM_A
MOBA_BLOCK = 256
MOBA_TOPK = 3
N_HEADS_B = 8
QK_NOPE_DIM = 64
QK_ROPE_DIM = 32
V_HEAD_DIM = 64
Q_LORA_RANK = 384
KV_LORA_RANK = 256
WIDTH_B = N_HEADS_B * V_HEAD_DIM
ROPE_THETA = 10000.0
N_EXPERTS = 256
TOP_K = 8
N_GROUPS = 8
TOPK_GROUPS = 4
GROUP_SIZE = N_EXPERTS // N_GROUPS
D_EXPERT = 256
D_SHARED = 256
ROUTED_SCALE = 2.5
DEPTH = 2
ALPHA = (2 * DEPTH) ** 0.25
LN_EPS = 1e-5
RMS_EPS = 1e-6

LANES = 128
NEG = -1e30
LOG2E = float(np.log2(np.e))
ROW_TILE = 256
ATT_TILE = 256
EXPERT_BLOCK = 256
EXPERT_STEP = 2
SC_CORES = 2
SC_SUBCORES = 16
SC_CHUNK = 64
SC_BUFS = 2
VMEM_LIMIT = 56 * 1024 * 1024
HALF = D_MODEL // 2
ATT_HEADS = 4
PV_ROWS = 80
BIAS_HI, BIAS_LO, POS_HI, POS_LO = 0, 32, 64, 72

BF16 = jnp.bfloat16
F32 = jnp.float32
U32 = jnp.uint32


def _dot(a, b):
    return jnp.dot(a, b, preferred_element_type=F32)


def _dot_nt(a, b):
    return lax.dot_general(a, b, (((1,), (1,)), ((), ())), preferred_element_type=F32)


def _sigmoid(x):
    return 1.0 / (1.0 + jnp.exp(-x))


def _layer_norm(y, g, b):
    mu = jnp.mean(y, axis=-1, keepdims=True)
    d = y - mu
    var = jnp.mean(d * d, axis=-1, keepdims=True)
    return d * lax.rsqrt(var + LN_EPS) * g + b


def _params(*sem):
    return pltpu.CompilerParams(dimension_semantics=sem, vmem_limit_bytes=VMEM_LIMIT)


def _full(a):
    return pl.BlockSpec(a.shape, lambda *_: (0,) * a.ndim)


def _inproj_kernel(x_ref, wqt_ref, wk_ref, wvt_ref, ones_ref, wcq_ref, wckv_ref, wkr_ref, wg_ref, bg_ref,
                   qn_ref, kvn_ref, wuqt_ref, wuk_ref, wuvt_ref, cos_ref, s1_ref, s2_ref, cost_ref, s1t_ref, s2t_ref,
                   qat_ref, ka_ref, vat_ref, kmean_ref, qmt_ref, km_ref, vmt_ref, sa_ref, sb_ref):
    xb = x_ref[...].astype(BF16)
    half = QK_ROPE_DIM // 2
    qat_ref[0] = _dot_nt(wqt_ref[...], xb).astype(BF16)
    k = _dot(xb, wk_ref[...])
    ka_ref[...] = k.astype(BF16)
    kmean_ref[0] = jnp.mean(k, axis=0, keepdims=True)
    vat_ref[0] = (_dot_nt(wvt_ref[...], xb) + ones_ref[...]).astype(BF16)

    cq = _dot(xb, wcq_ref[...])
    cqn = (cq * lax.rsqrt(jnp.mean(cq * cq, axis=-1, keepdims=True) + RMS_EPS) * qn_ref[...]).astype(BF16)
    ckv = _dot(xb, wckv_ref[...])
    ckvn = (ckv * lax.rsqrt(jnp.mean(ckv * ckv, axis=-1, keepdims=True) + RMS_EPS) * kvn_ref[...]).astype(BF16)
    qt = _dot_nt(wuqt_ref[...], cqn)
    ct, s1t, s2t = cost_ref[...], s1t_ref[...], s2t_ref[...]
    for h in range(N_HEADS_B):
        t = qt[h * LANES:(h + 1) * LANES, :]
        rot = t * ct + pltpu.roll(t, LANES - half, 0) * s1t + pltpu.roll(t, half, 0) * s2t
        qmt_ref[0, h * LANES:(h + 1) * LANES, :] = rot.astype(BF16)
    kn = _dot(ckvn, wuk_ref[...])
    kr = _dot(xb, wkr_ref[...])
    c, s1, s2 = cos_ref[...], s1_ref[...], s2_ref[...]
    krot = kr * c + pltpu.roll(kr, LANES - half, 1) * s1 + pltpu.roll(kr, half, 1) * s2
    for h in range(N_HEADS_B):
        sl = slice(h * LANES, (h + 1) * LANES)
        km_ref[:, sl] = (kn[:, sl] + krot).astype(BF16)
    vmt_ref[0] = (_dot_nt(wuvt_ref[...], ckvn) + ones_ref[...]).astype(BF16)

    sig = _sigmoid(_dot(xb, wg_ref[...]) + bg_ref[...])
    sa_ref[...] = sig[:, :D_MODEL].astype(BF16)
    sb_ref[...] = sig[:, D_MODEL:].astype(BF16)


def _inproj(x, wp, tabs, seq):
    t = x.shape[0]
    tm = ROW_TILE
    nt = t // tm
    npos = seq // tm
    row = lambda w: pl.BlockSpec((tm, w), lambda i: (i, 0))
    tile = lambda r: pl.BlockSpec((1, r, tm), lambda i: (i, 0, 0))
    tab = pl.BlockSpec((tm, LANES), lambda i: (i % npos, 0))
    tabt = pl.BlockSpec((LANES, tm), lambda i: (0, i % npos))
    weights = [wp['wqt'], wp['wk'], wp['wvt'], wp['ones'], wp['wcq'], wp['wckv'], wp['wkr'], wp['wg'], wp['bg'],
               wp['qn'], wp['kvn'], wp['wuqt'], wp['wuk'], wp['wuvt']]
    hl = N_HEADS_B * LANES
    out_shape = [
        jax.ShapeDtypeStruct((nt, WIDTH_A, tm), BF16), jax.ShapeDtypeStruct((t, WIDTH_A), BF16),
        jax.ShapeDtypeStruct((nt, N_HEADS_A * LANES, tm), BF16), jax.ShapeDtypeStruct((nt, 1, WIDTH_A), F32),
        jax.ShapeDtypeStruct((nt, hl, tm), BF16), jax.ShapeDtypeStruct((t, hl), BF16),
        jax.ShapeDtypeStruct((nt, hl, tm), BF16),
        jax.ShapeDtypeStruct((t, D_MODEL), BF16), jax.ShapeDtypeStruct((t, D_MODEL), BF16),
    ]
    out_specs = [tile(WIDTH_A), row(WIDTH_A), tile(N_HEADS_A * LANES),
                 pl.BlockSpec((1, 1, WIDTH_A), lambda i: (i, 0, 0)),
                 tile(hl), row(hl), tile(hl), row(D_MODEL), row(D_MODEL)]
    return pl.pallas_call(
        _inproj_kernel,
        grid=(nt,),
        in_specs=[row(D_MODEL)] + [_full(w) for w in weights] + [tab, tab, tab, tabt, tabt, tabt],
        out_specs=out_specs,
        out_shape=out_shape,
        compiler_params=_params("parallel"),
        name="inproj",
    )(x, *weights, *tabs)


def _attend_init(tq):
    return jnp.full((1, tq), -jnp.inf, F32), jnp.zeros((PV_ROWS, tq), F32)


def _col_max(s_ref):
    return [jnp.max(s_ref[hh], axis=0, keepdims=True) for hh in range(ATT_HEADS)]


def _attend_staged(cur_ref, cur_max, state, vts, nxt_ref=None, next_scores=None):
    heads = range(ATT_HEADS)
    if nxt_ref is not None:
        for hh in heads:
            nxt_ref[hh] = next_scores(hh)
    new_m, scaled, pvs = [], [], []
    for hh in heads:
        m_i, acc = state[hh]
        m_new = jnp.maximum(m_i, cur_max[hh])
        new_m.append(m_new)
        scaled.append(jnp.exp2(m_i - m_new) * acc)
        pvs.append(_dot(vts[hh], jnp.exp2(cur_ref[hh] - m_new).astype(BF16)))
    nxt_max = _col_max(nxt_ref) if nxt_ref is not None else cur_max
    return tuple((m, a + pv) for m, a, pv in zip(new_m, scaled, pvs)), nxt_max


def _attention_loop(i, scores, values, causal, tq, sa_ref, sb_ref):
    heads = range(ATT_HEADS)
    for hh in heads:
        sa_ref[hh] = jnp.where(causal, scores(hh, i), NEG)
    max_a = _col_max(sa_ref)
    state = tuple(_attend_init(tq) for _ in heads)
    npair = (i + 1) // 2

    def pair(n, c):
        state, max_a = c
        t0 = 2 * n
        first = jnp.where(n == 0, i, t0 - 1)
        state, max_b = _attend_staged(sa_ref, max_a, state, [values(hh, first) for hh in heads],
                                      sb_ref, lambda hh: scores(hh, t0))
        state, max_a = _attend_staged(sb_ref, max_b, state, [values(hh, t0) for hh in heads],
                                      sa_ref, lambda hh: scores(hh, jnp.minimum(t0 + 1, i - 1)))
        return state, max_a

    state, max_a = lax.fori_loop(0, npair, pair, (state, max_a))
    last = jnp.where(i == 0, i, i - 1)
    state = lax.cond(i % 2 == 0,
                     lambda s: _attend_staged(sa_ref, max_a, s, [values(hh, last) for hh in heads])[0],
                     lambda s: s, state)
    outs = []
    for hh in heads:
        _, acc = state[hh]
        outs.append(acc[:V_HEAD_DIM, :] / acc[V_HEAD_DIM:V_HEAD_DIM + 1, :])
    return jnp.concatenate(outs, axis=0).T


def _causal_t(tq):
    key = lax.broadcasted_iota(jnp.int32, (tq, tq), 0)
    qry = lax.broadcasted_iota(jnp.int32, (tq, tq), 1)
    return key <= qry


def _moba_kernel(slope_ref, qt_ref, k_ref, e_ref, vt_ref, km_ref, o_ref, sa_ref, sb_ref):
    tq = ATT_TILE
    hp = pl.program_id(1)
    i = pl.program_id(2)
    nblk = km_ref.shape[1]
    frow = lax.broadcasted_iota(jnp.int32, (LANES, tq), 0)
    row = lax.broadcasted_iota(jnp.int32, (BIAS_LO, tq), 0)
    prow = lax.broadcasted_iota(jnp.int32, (LANES - POS_HI, tq), 0) + POS_HI

    qaug = []
    for hh in range(ATT_HEADS):
        head = hp * ATT_HEADS + hh
        grp = slice(hh // 2 * LANES, (hh // 2 + 1) * LANES)
        qt2 = qt_ref[0, grp, :]
        head_rows = (frow >= hh % 2 * HEAD_DIM_A) & (frow < (hh % 2 + 1) * HEAD_DIM_A)
        qh = jnp.where(head_rows, qt2, jnp.zeros_like(qt2))
        kmean = km_ref[0, :, grp].astype(BF16)
        if nblk < BIAS_LO:
            kmean = jnp.concatenate([kmean, jnp.zeros((BIAS_LO - nblk, LANES), BF16)], axis=0)
        g = jnp.where(row < i, _dot(kmean, qh), -jnp.inf)
        picked = row == i
        for r in range(MOBA_TOPK):
            m = jnp.max(g, axis=0, keepdims=True)
            idx = jnp.min(jnp.where(g == m, row, BIAS_LO), axis=0, keepdims=True)
            hit = row == idx
            picked = picked | (hit & (r < i))
            g = jnp.where(hit, -jnp.inf, g)
        dist = (i - row).astype(F32) * (slope_ref[head] * (MOBA_BLOCK * LOG2E))
        b = jnp.where(picked, -dist, NEG)
        b_hi = b.astype(BF16)
        b_lo = (b - b_hi.astype(F32)).astype(BF16)
        ones = jnp.where((prow == POS_HI + head) | (prow == POS_LO + head), 1.0, 0.0).astype(BF16)
        qs = (qh.astype(F32) * LOG2E).astype(BF16)
        qaug.append(jnp.concatenate([qs, b_hi, b_lo, ones], axis=0))

    def scores(hh, j):
        rows = pl.ds(pl.multiple_of(j * tq, tq), tq)
        grp = slice(hh // 2 * LANES, (hh // 2 + 1) * LANES)
        return _dot(jnp.concatenate([k_ref[rows, grp], e_ref[rows, :]], axis=1), qaug[hh])

    def values(hh, j):
        return vt_ref[j, hh * LANES:hh * LANES + PV_ROWS, :]

    o_ref[...] = _attention_loop(i, scores, values, _causal_t(tq), tq, sa_ref, sb_ref).astype(BF16)


def _moba(qat, ka, vat, kmean, etab, slopes, batch, seq):
    t = ka.shape[0]
    tq = ATT_TILE
    nq = seq // tq
    nblk = seq // MOBA_BLOCK
    nh = ATT_HEADS
    km = kmean.reshape(batch, nblk, WIDTH_A)
    return pl.pallas_call(
        _moba_kernel,
        grid=(batch, N_HEADS_A // nh, nq),
        in_specs=[
            pl.BlockSpec(memory_space=pltpu.SMEM),
            pl.BlockSpec((1, nh * HEAD_DIM_A, tq), lambda b, h, i: (b * nq + i, h, 0)),
            pl.BlockSpec((seq, nh * HEAD_DIM_A), lambda b, h, i: (b, h)),
            pl.BlockSpec((seq, LANES), lambda b, h, i: (0, 0)),
            pl.BlockSpec((nq, nh * LANES, tq), lambda b, h, i: (b, h, 0)),
            pl.BlockSpec((1, nblk, nh * HEAD_DIM_A), lambda b, h, i: (b, 0, h)),
        ],
        out_specs=pl.BlockSpec((tq, nh * HEAD_DIM_A), lambda b, h, i: (b * nq + i, h)),
        out_shape=jax.ShapeDtypeStruct((t, WIDTH_A), BF16),
        scratch_shapes=[pltpu.VMEM((nh, tq, tq), F32), pltpu.VMEM((nh, tq, tq), F32)],
        compiler_params=_params("parallel", "parallel", "arbitrary"),
        name="moba",
    )(slopes, qat, ka, etab, vat, km)


def _mla_kernel(qt_ref, k_ref, vt_ref, o_ref, sa_ref, sb_ref):
    tq = ATT_TILE
    i = pl.program_id(2)
    qt = [qt_ref[0, hh * LANES:(hh + 1) * LANES, :] for hh in range(ATT_HEADS)]

    def scores(hh, j):
        return _dot(k_ref[pl.ds(pl.multiple_of(j * tq, tq), tq), hh * LANES:(hh + 1) * LANES], qt[hh])

    def values(hh, j):
        return vt_ref[j, hh * LANES:hh * LANES + PV_ROWS, :]

    o_ref[...] = _attention_loop(i, scores, values, _causal_t(tq), tq, sa_ref, sb_ref).astype(BF16)


def _mla(qmt, km, vmt, batch, seq):
    t = km.shape[0]
    tq = ATT_TILE
    nq = seq // tq
    nh = ATT_HEADS
    return pl.pallas_call(
        _mla_kernel,
        grid=(batch, N_HEADS_B // nh, nq),
        in_specs=[
            pl.BlockSpec((1, nh * LANES, tq), lambda b, h, i: (b * nq + i, h, 0)),
            pl.BlockSpec((seq, nh * LANES), lambda b, h, i: (b, h)),
            pl.BlockSpec((nq, nh * LANES, tq), lambda b, h, i: (b, h, 0)),
        ],
        out_specs=pl.BlockSpec((tq, nh * V_HEAD_DIM), lambda b, h, i: (b * nq + i, h)),
        out_shape=jax.ShapeDtypeStruct((t, WIDTH_B), BF16),
        scratch_shapes=[pltpu.VMEM((nh, tq, tq), F32), pltpu.VMEM((nh, tq, tq), F32)],
        compiler_params=_params("parallel", "parallel", "arbitrary"),
        name="mla",
    )(qmt, km, vmt)


def _pack_halves(y):
    lo = pltpu.bitcast(y[:, :HALF].astype(BF16).astype(F32), U32)
    hi = pltpu.bitcast(y[:, HALF:].astype(BF16).astype(F32), U32)
    return (hi & jnp.uint32(0xFFFF0000)) | (lo >> 16)


def _unpack_halves(w):
    return pltpu.bitcast(w << 16, F32), pltpu.bitcast(w & jnp.uint32(0xFFFF0000), F32)


def _merge_kernel(x_ref, oa_ref, ob_ref, sa_ref, sb_ref, wpa_ref, wpb_ref, wo_ref, g_ref, b_ref, o_ref, op_ref):
    pa = _dot(oa_ref[...], wpa_ref[...])
    pb = _dot(ob_ref[...], wpb_ref[...])
    merged = sa_ref[...].astype(F32) * pa + sb_ref[...].astype(F32) * pb
    hmix = _dot(merged.astype(BF16), wo_ref[...])
    y = _layer_norm(ALPHA * x_ref[...] + hmix, g_ref[...], b_ref[...])
    o_ref[...] = y
    op_ref[...] = _pack_halves(y)


def _merge(x, oa, ob, sa, sb, wp):
    t = x.shape[0]
    tm = ROW_TILE
    row = lambda w: pl.BlockSpec((tm, w), lambda i: (i, 0))
    weights = [wp['wpa'], wp['wpb'], wp['wo'], wp['ln1_g'], wp['ln1_b']]
    return pl.pallas_call(
        _merge_kernel,
        grid=(t // tm,),
        in_specs=[row(D_MODEL), row(WIDTH_A), row(WIDTH_B), row(D_MODEL), row(D_MODEL)] + [_full(w) for w in weights],
        out_specs=[row(D_MODEL), row(HALF)],
        out_shape=[jax.ShapeDtypeStruct((t, D_MODEL), F32), jax.ShapeDtypeStruct((t, HALF), U32)],
        compiler_params=_params("parallel"),
        name="merge",
    )(x, oa, ob, sa, sb, *weights)


def _router_kernel(x_ref, wh_ref, wl_ref, rb_ref, idx_ref, rank_ref, w_ref, cnt_ref):
    tm = x_ref.shape[0]

    @pl.when(pl.program_id(0) == 0)
    def _():
        cnt_ref[...] = jnp.zeros_like(cnt_ref)

    x = x_ref[...]
    xh = x.astype(BF16)
    xl = (x - xh.astype(F32)).astype(BF16)
    wh, wl = wh_ref[...], wl_ref[...]
    logits = _dot_nt(wh, xh) + (_dot_nt(wh, xl) + _dot_nt(wl, xh))
    scores = _sigmoid(logits)
    choice = scores + rb_ref[...]
    row = lax.broadcasted_iota(jnp.int32, (GROUP_SIZE, tm), 0)
    groups = [choice[g * GROUP_SIZE:(g + 1) * GROUP_SIZE, :] for g in range(N_GROUPS)]
    gscore = []
    for blk in groups:
        m1 = jnp.max(blk, axis=0, keepdims=True)
        first = jnp.min(jnp.where(blk == m1, row, GROUP_SIZE), axis=0, keepdims=True)
        m2 = jnp.max(jnp.where(row == first, -jnp.inf, blk), axis=0, keepdims=True)
        gscore.append(m1 + m2)
    masked = []
    for g in range(N_GROUPS):
        ahead = jnp.zeros((1, tm), jnp.int32)
        for o in range(N_GROUPS):
            if o < g:
                ahead += (gscore[o] >= gscore[g]).astype(jnp.int32)
            elif o > g:
                ahead += (gscore[o] > gscore[g]).astype(jnp.int32)
        masked.append(jnp.where(ahead < TOPK_GROUPS, groups[g], -jnp.inf))
    cur = jnp.concatenate(masked, axis=0)
    erow = lax.broadcasted_iota(jnp.int32, (N_EXPERTS, tm), 0)
    hits, idxs, ws = [], [], []
    for _ in range(TOP_K):
        m = jnp.max(cur, axis=0, keepdims=True)
        e = jnp.min(jnp.where(cur == m, erow, N_EXPERTS), axis=0, keepdims=True)
        hit = erow == e
        hits.append(hit)
        idxs.append(e)
        ws.append(jnp.sum(jnp.where(hit, scores, 0.0), axis=0, keepdims=True))
        cur = jnp.where(hit, -jnp.inf, cur)
    total = ws[0]
    for w in ws[1:]:
        total = total + w
    member = hits[0]
    for hit in hits[1:]:
        member = member | hit
    member = jnp.where(member, 1.0, 0.0).astype(BF16)
    t_src = lax.broadcasted_iota(jnp.int32, (tm, tm), 0)
    t_dst = lax.broadcasted_iota(jnp.int32, (tm, tm), 1)
    before = _dot(member, jnp.where(t_src < t_dst, 1.0, 0.0).astype(BF16))
    base = cnt_ref[...]
    before = before + jnp.concatenate([base] * (tm // LANES), axis=1)
    cnt_ref[...] = base + _dot(member, jnp.ones((tm, LANES), BF16))
    for r in range(TOP_K):
        idx_ref[0, r:r + 1, :] = idxs[r]
        rank_ref[0, r:r + 1, :] = jnp.sum(jnp.where(hits[r], before, 0.0), axis=0, keepdims=True).astype(jnp.int32)
        w_ref[r:r + 1, :] = ws[r] / total * ROUTED_SCALE


def _router(x1, wp):
    t = x1.shape[0]
    tm = ROW_TILE
    nt = t // tm
    weights = [wp['wr_hi'], wp['wr_lo'], wp['rbias']]
    tile = pl.BlockSpec((1, TOP_K, tm), lambda i: (i, 0, 0))
    return pl.pallas_call(
        _router_kernel,
        grid=(nt,),
        in_specs=[pl.BlockSpec((tm, D_MODEL), lambda i: (i, 0))] + [_full(w) for w in weights],
        out_specs=[tile, tile, pl.BlockSpec((TOP_K, tm), lambda i: (0, i)),
                   pl.BlockSpec((N_EXPERTS, LANES), lambda i: (0, 0))],
        out_shape=[jax.ShapeDtypeStruct((nt, TOP_K, tm), jnp.int32), jax.ShapeDtypeStruct((nt, TOP_K, tm), jnp.int32),
                   jax.ShapeDtypeStruct((TOP_K, t), F32), jax.ShapeDtypeStruct((N_EXPERTS, LANES), F32)],
        compiler_params=_params("arbitrary"),
        name="router",
    )(x1, *weights)


def _pos_kernel(idx_ref, rank_ref, pq_ref, pos_ref):
    tm = idx_ref.shape[2]
    erow = lax.broadcasted_iota(jnp.int32, (N_EXPERTS, tm), 0)
    for k in range(TOP_K):
        onehot = jnp.where(erow == idx_ref[0, k:k + 1, :], 1.0, 0.0).astype(BF16)
        q = _dot(pq_ref[...], onehot)
        blk = (q[0:1, :] * 32.0 + q[1:2, :]).astype(jnp.int32)
        pos_ref[k:k + 1, :] = blk * EXPERT_BLOCK + rank_ref[0, k:k + 1, :]


def _positions(idx3, rank3, pstart):
    nt, _, tm = idx3.shape
    blk = pstart // EXPERT_BLOCK
    pq = jnp.zeros((8, N_EXPERTS), F32).at[0].set((blk // 32).astype(F32)).at[1].set((blk % 32).astype(F32))
    tile = pl.BlockSpec((1, TOP_K, tm), lambda i: (i, 0, 0))
    return pl.pallas_call(
        _pos_kernel,
        grid=(nt,),
        in_specs=[tile, tile, pl.BlockSpec((8, N_EXPERTS), lambda i: (0, 0))],
        out_specs=pl.BlockSpec((TOP_K, tm), lambda i: (0, i)),
        out_shape=jax.ShapeDtypeStruct((TOP_K, nt * tm), jnp.int32),
        compiler_params=_params("parallel"),
        name="moe_positions",
    )(idx3, rank3, pq.astype(BF16))


def _sc_mesh():
    return plsc.VectorSubcoreMesh(core_axis_name="c", subcore_axis_name="s", num_cores=SC_CORES,
                                  num_subcores=SC_SUBCORES)


def _sc_worker():
    return lax.axis_index("s") * SC_CORES + lax.axis_index("c")


def _sc_scatter_kernel(x_hbm, pos_hbm, out_hbm, idx_v, rows_v, sem):
    nchunk = idx_v.shape[1]
    w = _sc_worker()
    pltpu.sync_copy(pos_hbm.at[w], idx_v)

    @pl.loop(0, nchunk)
    def _(c):
        pltpu.sync_copy(x_hbm.at[pl.ds((w * nchunk + c) * SC_CHUNK, SC_CHUNK)], rows_v)
        scatters = [pltpu.async_copy(rows_v, out_hbm.at[idx_v.at[k, c]], sem) for k in range(TOP_K)]
        for scatter in scatters:
            scatter.wait()


def _sc_dispatch(x1p, pos_kt, n_rows):
    t = x1p.shape[0]
    workers = SC_CORES * SC_SUBCORES
    nchunk = t // (workers * SC_CHUNK)
    assert t == workers * nchunk * SC_CHUNK
    pos4 = pos_kt.reshape(TOP_K, workers, nchunk, SC_CHUNK).transpose(1, 0, 2, 3)
    return pl.kernel(
        _sc_scatter_kernel,
        out_type=jax.ShapeDtypeStruct((n_rows, HALF), U32),
        mesh=_sc_mesh(),
        scratch_types=[pltpu.VMEM((TOP_K, nchunk, SC_CHUNK), jnp.int32), pltpu.VMEM((SC_CHUNK, HALF), U32),
                       pltpu.SemaphoreType.DMA],
        name="moe_dispatch_sc",
    )(x1p, pos4)


def _expert_kernel(layer, be_ref, nused_ref, nvalid_ref, first_ref, slot_ref, ahead_ref, head_ref,
                   x_ref, wg_hbm, wu_hbm, wd_hbm, y_ref, wg_f, wu_f, wd_f, wg_b, wu_b, wd_b, sem):
    def fetch(e, slot):
        return [pltpu.make_async_copy(src.at[layer, e], dst.at[slot], sem.at[slot, n])
                for n, (src, dst) in enumerate(((wg_hbm, wg_f), (wu_hbm, wu_f), (wd_hbm, wd_f)))]

    @pl.when(pl.program_id(0) == 0)
    def _():
        for copy in fetch(head_ref[0], 0):
            copy.start()

        @pl.when(head_ref[1] >= 0)
        def _():
            for copy in fetch(head_ref[1], 1):
                copy.start()

    def one_block(b, rows):
        @pl.when((b < nused_ref[0]) & (first_ref[b] == 1))
        def _():
            slot = slot_ref[b]
            for copy in fetch(be_ref[b], slot):
                copy.wait()
            wg_b[...] = wg_f[slot].astype(BF16)
            wu_b[...] = wu_f[slot].astype(BF16)
            wd_b[...] = wd_f[slot].astype(BF16)

            @pl.when(ahead_ref[b] >= 0)
            def _():
                for copy in fetch(ahead_ref[b], slot):
                    copy.start()

        @pl.when(b < nused_ref[0])
        def _():
            live = lax.broadcasted_iota(jnp.int32, (EXPERT_BLOCK, HALF), 0) < nvalid_ref[b]
            xlo, xhi = (h.astype(BF16) for h in _unpack_halves(jnp.where(live, x_ref[rows, :], jnp.uint32(0))))
            g = _dot(xlo, wg_b[:HALF, :]) + _dot(xhi, wg_b[HALF:, :])
            u = _dot(xlo, wu_b[:HALF, :]) + _dot(xhi, wu_b[HALF:, :])
            a = (g * _sigmoid(g) * u).astype(BF16)
            y_ref[rows, :] = _pack_halves(_dot(a, wd_b[...]))

        @pl.when(b >= nused_ref[0])
        def _():
            y_ref[rows, :] = jnp.zeros((EXPERT_BLOCK, HALF), U32)

    for sub in range(EXPERT_STEP):
        one_block(pl.program_id(0) * EXPERT_STEP + sub, slice(sub * EXPERT_BLOCK, (sub + 1) * EXPERT_BLOCK))


def _experts(xs, block_e, nused, nvalid, counts, w_gate, w_up, w_down, layer, n_blocks):
    rows = EXPERT_BLOCK
    blocks = jnp.arange(n_blocks, dtype=jnp.int32)
    first = ((blocks == 0) | (block_e != jnp.roll(block_e, 1))) & (blocks < nused[0])
    run = jnp.cumsum(first.astype(jnp.int32)) - 1
    run_e = jnp.nonzero(counts > 0, size=N_EXPERTS, fill_value=-1)[0].astype(jnp.int32)
    ahead = jnp.concatenate([run_e, jnp.full((2,), -1, jnp.int32)])[jnp.clip(run, 0, N_EXPERTS - 1) + 2]
    step_rows = EXPERT_STEP * rows
    grid_spec = pltpu.PrefetchScalarGridSpec(
        num_scalar_prefetch=7,
        grid=(n_blocks // EXPERT_STEP,),
        in_specs=[
            pl.BlockSpec((step_rows, HALF), lambda s, be, nu, *_: (jnp.minimum(s, (nu[0] - 1) // EXPERT_STEP), 0)),
            pl.BlockSpec(memory_space=pl.ANY), pl.BlockSpec(memory_space=pl.ANY), pl.BlockSpec(memory_space=pl.ANY),
        ],
        out_specs=pl.BlockSpec((step_rows, HALF), lambda s, *_: (s, 0)),
        scratch_shapes=[
            pltpu.VMEM((2, D_MODEL, D_EXPERT), F32), pltpu.VMEM((2, D_MODEL, D_EXPERT), F32),
            pltpu.VMEM((2, D_EXPERT, D_MODEL), F32),
            pltpu.VMEM((D_MODEL, D_EXPERT), BF16), pltpu.VMEM((D_MODEL, D_EXPERT), BF16),
            pltpu.VMEM((D_EXPERT, D_MODEL), BF16),
            pltpu.SemaphoreType.DMA((2, 3)),
        ],
    )
    return pl.pallas_call(
        functools.partial(_expert_kernel, layer),
        grid_spec=grid_spec,
        out_shape=jax.ShapeDtypeStruct((n_blocks * rows, HALF), U32),
        compiler_params=_params("arbitrary"),
        name="moe_experts",
    )(block_e, nused, nvalid, first.astype(jnp.int32), (run % 2).astype(jnp.int32), ahead, run_e[:2],
      xs, w_gate, w_up, w_down)


def _sc_gather_kernel(table_hbm, idx_hbm, out_hbm, idx_v, rows_v, gsem, wsem):
    per_worker = idx_v.shape[0]
    base = _sc_worker() * per_worker
    pltpu.sync_copy(idx_hbm.at[pl.ds(base, per_worker)], idx_v)

    @pl.loop(0, per_worker // SC_CHUNK, step=SC_BUFS)
    def _(c):
        offs = [(c + n) * SC_CHUNK for n in range(SC_BUFS)]
        gathers = [pltpu.async_copy(table_hbm.at[idx_v.at[pl.ds(offs[n], SC_CHUNK)]], rows_v.at[n], gsem.at[n])
                   for n in range(SC_BUFS)]
        writes = []
        for n in range(SC_BUFS):
            gathers[n].wait()
            writes.append(pltpu.async_copy(rows_v.at[n], out_hbm.at[pl.ds(base + offs[n], SC_CHUNK)], wsem.at[n]))
        for write in writes:
            write.wait()


def _sc_gather_rows(table, idx):
    n = idx.shape[0]
    workers = SC_CORES * SC_SUBCORES
    assert n % (workers * SC_CHUNK * SC_BUFS) == 0
    return pl.kernel(
        _sc_gather_kernel,
        out_type=jax.ShapeDtypeStruct((n, HALF), U32),
        mesh=_sc_mesh(),
        scratch_types=[pltpu.VMEM((n // workers,), jnp.int32), pltpu.VMEM((SC_BUFS, SC_CHUNK, HALF), U32),
                       pltpu.SemaphoreType.DMA((SC_BUFS,)), pltpu.SemaphoreType.DMA((SC_BUFS,))],
        name="moe_gather_sc",
    )(table, idx)


def _combine_kernel(x_ref, w_ref, *refs):
    y_refs, (wsg_ref, wsu_ref, wsd_ref, g_ref, b_ref, o_ref) = refs[:TOP_K], refs[TOP_K:]
    x = x_ref[...]
    xb = x.astype(BF16)
    g = _dot(xb, wsg_ref[...])
    u = _dot(xb, wsu_ref[...])
    shared = _dot((g * _sigmoid(g) * u).astype(BF16), wsd_ref[...])
    w = w_ref[...]
    lo, hi = (h * w[:, 0:1] for h in _unpack_halves(y_refs[0][...]))
    for k in range(1, TOP_K):
        lo_k, hi_k = _unpack_halves(y_refs[k][...])
        lo = lo + lo_k * w[:, k:k + 1]
        hi = hi + hi_k * w[:, k:k + 1]
    routed = jnp.concatenate([lo, hi], axis=1)
    o_ref[...] = _layer_norm(ALPHA * x + (shared + routed), g_ref[...], b_ref[...])


def _combine(x1, y8, w_tk, wp):
    t = x1.shape[0]
    tm = ROW_TILE
    nt = t // tm
    weights = [wp['wsg'], wp['wsu'], wp['wsd'], wp['ln2_g'], wp['ln2_b']]
    y_specs = [pl.BlockSpec((tm, HALF), lambda i, k=k: (k * nt + i, 0)) for k in range(TOP_K)]
    return pl.pallas_call(
        _combine_kernel,
        grid=(nt,),
        in_specs=[pl.BlockSpec((tm, D_MODEL), lambda i: (i, 0)), pl.BlockSpec((tm, TOP_K), lambda i: (i, 0))]
                 + y_specs + [_full(w) for w in weights],
        out_specs=pl.BlockSpec((tm, D_MODEL), lambda i: (i, 0)),
        out_shape=jax.ShapeDtypeStruct((t, D_MODEL), F32),
        compiler_params=_params("parallel"),
        name="moe_combine",
    )(x1, w_tk, *([y8] * TOP_K), *weights)


def _moe(x1, x1p, wp, w_e_gate, w_e_up, w_e_down, layer):
    t = x1.shape[0]
    rows = EXPERT_BLOCK
    n_blocks = -(-(t * TOP_K + N_EXPERTS * (rows - 1)) // (rows * EXPERT_STEP)) * EXPERT_STEP
    idx3, rank3, top_w, cnt = _router(x1, wp)
    counts = cnt[:, 0].astype(jnp.int32)
    padded = (counts + rows - 1) // rows * rows
    padded_end = jnp.cumsum(padded)
    pstart = (padded_end - padded).astype(jnp.int32)
    nused = (padded_end[-1] // rows).astype(jnp.int32).reshape(1)
    blocks = jnp.arange(n_blocks, dtype=jnp.int32)
    block_row = jnp.minimum(blocks, nused[0] - 1) * rows
    block_e = jnp.sum((padded_end[None, :] <= block_row[:, None]).astype(jnp.int32), axis=1)
    block_e = jnp.minimum(block_e, N_EXPERTS - 1)
    nvalid = jnp.clip(counts[block_e] - (blocks * rows - pstart[block_e]), 0, rows).astype(jnp.int32)
    pos_kt = _positions(idx3, rank3, pstart)
    xs = _sc_dispatch(x1p, pos_kt, n_blocks * rows)
    ys = _experts(xs, block_e, nused, nvalid, counts, w_e_gate, w_e_up, w_e_down, layer, n_blocks)
    return _combine(x1, _sc_gather_rows(ys, pos_kt.reshape(TOP_K * t)), top_w.T, wp)


def _head_groups_t(w, used):
    k, h, _ = w.shape
    return jnp.pad(w, ((0, 0), (0, 0), (0, LANES - used))).reshape(k, h * LANES).T


def _prep_layer(l, w_in, b_gate, q_norm, w_uq, kv_norm, w_ukv, w_proj_a, w_proj_b, w_out, ln1_g, ln1_b,
                w_router, router_bias, w_s_gate, w_s_up, w_s_down, ln2_g, ln2_b):
    w = w_in[l]
    o = 0
    cols = {}
    for name, width in (('qa', WIDTH_A), ('ka', WIDTH_A), ('va', WIDTH_A), ('cq', Q_LORA_RANK),
                        ('ckv', KV_LORA_RANK), ('kr', QK_ROPE_DIM), ('ga', D_MODEL), ('gb', D_MODEL)):
        cols[name] = w[:, o:o + width]
        o += width
    wkr = jnp.zeros((D_MODEL, LANES), F32).at[:, QK_NOPE_DIM:QK_NOPE_DIM + QK_ROPE_DIM].set(cols['kr'])
    dqk = QK_NOPE_DIM + QK_ROPE_DIM
    wq = w_uq[l].reshape(Q_LORA_RANK, N_HEADS_B, dqk) * (dqk ** -0.5 * LOG2E)
    wkv = w_ukv[l].reshape(KV_LORA_RANK, N_HEADS_B, QK_NOPE_DIM + V_HEAD_DIM)
    wuk = jnp.pad(wkv[:, :, :QK_NOPE_DIM], ((0, 0), (0, 0), (0, LANES - QK_NOPE_DIM))).reshape(KV_LORA_RANK, N_HEADS_B * LANES)
    ones = jnp.zeros((N_HEADS_B, LANES), F32).at[:, V_HEAD_DIM].set(1.0).reshape(N_HEADS_B * LANES, 1)
    wr_t = w_router[l].T
    wr_hi = wr_t.astype(BF16)
    return dict(
        wqt=(cols['qa'] * HEAD_DIM_A ** -0.5).T.astype(BF16), wk=cols['ka'].astype(BF16),
        wvt=_head_groups_t(cols['va'].reshape(D_MODEL, N_HEADS_A, HEAD_DIM_A), HEAD_DIM_A).astype(BF16), ones=ones,
        wcq=cols['cq'].astype(BF16), wckv=cols['ckv'].astype(BF16), wkr=wkr.astype(BF16),
        wg=jnp.concatenate([cols['ga'], cols['gb']], axis=1).astype(BF16),
        bg=b_gate[l].reshape(1, 2 * D_MODEL), qn=q_norm[l].reshape(1, Q_LORA_RANK), kvn=kv_norm[l].reshape(1, KV_LORA_RANK),
        wuqt=_head_groups_t(wq, dqk).astype(BF16), wuk=wuk.astype(BF16),
        wuvt=_head_groups_t(wkv[:, :, QK_NOPE_DIM:], V_HEAD_DIM).astype(BF16),
        wpa=w_proj_a[l].astype(BF16), wpb=w_proj_b[l].astype(BF16), wo=w_out[l].astype(BF16),
        ln1_g=ln1_g[l].reshape(1, D_MODEL), ln1_b=ln1_b[l].reshape(1, D_MODEL),
        wr_hi=wr_hi, wr_lo=(wr_t - wr_hi.astype(F32)).astype(BF16), rbias=router_bias[l].reshape(N_EXPERTS, 1),
        wsg=w_s_gate[l].astype(BF16), wsu=w_s_up[l].astype(BF16), wsd=w_s_down[l].astype(BF16),
        ln2_g=ln2_g[l].reshape(1, D_MODEL), ln2_b=ln2_b[l].reshape(1, D_MODEL),
    )


def _rope_tables(seq):
    pos = jnp.arange(seq, dtype=F32)
    inv_freq = ROPE_THETA ** (-jnp.arange(0, QK_ROPE_DIM, 2, dtype=F32) / QK_ROPE_DIM)
    ang = pos[:, None] * inv_freq[None, :]
    cos, sin = jnp.cos(ang), jnp.sin(ang)
    half = QK_ROPE_DIM // 2
    z = lambda n: jnp.zeros((seq, n), F32)
    c = jnp.concatenate([jnp.ones((seq, QK_NOPE_DIM), F32), cos, cos, z(LANES - QK_NOPE_DIM - QK_ROPE_DIM)], axis=1)
    s1 = jnp.concatenate([z(QK_NOPE_DIM), -sin, z(LANES - QK_NOPE_DIM - half)], axis=1)
    s2 = jnp.concatenate([z(QK_NOPE_DIM + half), sin, z(LANES - QK_NOPE_DIM - QK_ROPE_DIM)], axis=1)
    return c, s1, s2, c.T, s1.T, s2.T


def _moba_key_table(seq, slopes):
    blk = jnp.arange(seq, dtype=jnp.int32) // MOBA_BLOCK
    onehot = (blk[:, None] == jnp.arange(BIAS_LO, dtype=jnp.int32)[None, :]).astype(F32)
    inblk = (jnp.arange(seq, dtype=jnp.int32) % MOBA_BLOCK).astype(F32)[:, None] * (slopes * LOG2E)[None, :]
    hi = inblk.astype(BF16)
    lo = (inblk - hi.astype(F32)).astype(BF16)
    pad = jnp.zeros((seq, LANES - POS_LO - N_HEADS_A), BF16)
    return jnp.concatenate([onehot.astype(BF16), onehot.astype(BF16), hi, lo, pad], axis=1)


def kernel(x, w_in, b_gate, q_norm, w_uq, kv_norm, w_ukv, w_proj_a, w_proj_b, w_out, ln1_g, ln1_b, w_router, router_bias, w_e_gate, w_e_up, w_e_down, w_s_gate, w_s_up, w_s_down, ln2_g, ln2_b):
    batch, seq, d = x.shape
    assert d == D_MODEL and seq % MOBA_BLOCK == 0 and MOBA_TOPK <= seq // MOBA_BLOCK <= BIAS_LO
    assert POS_LO + N_HEADS_A <= LANES and POS_HI + N_HEADS_A <= POS_LO
    tabs = _rope_tables(seq)
    slopes = jnp.asarray(np.exp2(-8.0 * (np.arange(N_HEADS_A) + 1.0) / N_HEADS_A), F32)
    etab = _moba_key_table(seq, slopes)
    h = x.reshape(batch * seq, d)
    for l in range(DEPTH):
        wp = _prep_layer(l, w_in, b_gate, q_norm, w_uq, kv_norm, w_ukv, w_proj_a, w_proj_b, w_out, ln1_g, ln1_b,
                         w_router, router_bias, w_s_gate, w_s_up, w_s_down, ln2_g, ln2_b)
        qat, ka, vat, kmean, qmt, km, vmt, sa, sb = _inproj(h, wp, tabs, seq)
        oa = _moba(qat, ka, vat, kmean, etab, slopes, batch, seq)
        ob = _mla(qmt, km, vmt, batch, seq)
        x1, x1p = _merge(h, oa, ob, sa, sb, wp)
        h = _moe(x1, x1p, wp, w_e_gate, w_e_up, w_e_down, l)
    return h.reshape(batch, seq, d)
```

```python
import functools

import numpy as np

import jax
import jax.numpy as jnp
from jax import lax
from jax.experimental import pallas as pl
from jax.experimental.pallas import tpu as pltpu
from jax.experimental.pallas import tpu_sc as plsc

D_MODEL = 1024
N_HEADS_A = 8
HEAD_DIM_A = 64
WIDTH_A = N_HEADS_A * HEAD_DIM_A
MOBA_BLOCK = 256
MOBA_TOPK = 3
N_HEADS_B = 8
QK_NOPE_DIM = 64
QK_ROPE_DIM = 32
V_HEAD_DIM = 64
Q_LORA_RANK = 384
KV_LORA_RANK = 256
WIDTH_B = N_HEADS_B * V_HEAD_DIM
ROPE_THETA = 10000.0
N_EXPERTS = 256
TOP_K = 8
N_GROUPS = 8
TOPK_GROUPS = 4
GROUP_SIZE = N_EXPERTS // N_GROUPS
D_EXPERT = 256
D_SHARED = 256
ROUTED_SCALE = 2.5
DEPTH = 2
ALPHA = (2 * DEPTH) ** 0.25
LN_EPS = 1e-5
RMS_EPS = 1e-6

LANES = 128
NEG = -1e30
LOG2E = float(np.log2(np.e))
ROW_TILE = 256
ATT_TILE = 256
EXPERT_BLOCK = 256
EXPERT_STEP = 4
SC_CORES = 2
SC_SUBCORES = 16
SC_CHUNK = 64
SC_BUFS = 2
VMEM_LIMIT = 56 * 1024 * 1024
HALF = D_MODEL // 2
ATT_HEADS = 4
ATT_GROUP = 4
PV_ROWS = 80
BIAS_HI, BIAS_LO, POS_HI, POS_LO = 0, 32, 64, 72

BF16 = jnp.bfloat16
F32 = jnp.float32
U32 = jnp.uint32


def _dot(a, b):
    return jnp.dot(a, b, preferred_element_type=F32)


def _dot_nt(a, b):
    return lax.dot_general(a, b, (((1,), (1,)), ((), ())), preferred_element_type=F32)


def _sigmoid(x):
    return 1.0 / (1.0 + jnp.exp(-x))


def _layer_norm(y, g, b):
    mu = jnp.mean(y, axis=-1, keepdims=True)
    d = y - mu
    var = jnp.mean(d * d, axis=-1, keepdims=True)
    return d * lax.rsqrt(var + LN_EPS) * g + b


def _params(*sem):
    return pltpu.CompilerParams(dimension_semantics=sem, vmem_limit_bytes=VMEM_LIMIT)


def _full(a):
    return pl.BlockSpec(a.shape, lambda *_: (0,) * a.ndim)


def _inproj_kernel(x_ref, wqt_ref, wk_ref, wvt_ref, ones_ref, wcq_ref, wckv_ref, wkr_ref, wg_ref, bg_ref,
                   qn_ref, kvn_ref, wuqt_ref, wuk_ref, wuvt_ref, cos_ref, s1_ref, s2_ref, cost_ref, s1t_ref, s2t_ref,
                   qat_ref, ka_ref, vat_ref, kmean_ref, qmt_ref, km_ref, vmt_ref, sa_ref, sb_ref):
    xb = x_ref[...].astype(BF16)
    half = QK_ROPE_DIM // 2
    qat_ref[0] = _dot_nt(wqt_ref[...], xb).astype(BF16)
    k = _dot(xb, wk_ref[...])
    ka_ref[...] = k.astype(BF16)
    kmean_ref[0] = jnp.mean(k, axis=0, keepdims=True)
    vat_ref[0] = (_dot_nt(wvt_ref[...], xb) + ones_ref[...]).astype(BF16)

    cq = _dot(xb, wcq_ref[...])
    cqn = (cq * lax.rsqrt(jnp.mean(cq * cq, axis=-1, keepdims=True) + RMS_EPS) * qn_ref[...]).astype(BF16)
    ckv = _dot(xb, wckv_ref[...])
    ckvn = (ckv * lax.rsqrt(jnp.mean(ckv * ckv, axis=-1, keepdims=True) + RMS_EPS) * kvn_ref[...]).astype(BF16)
    qt = _dot_nt(wuqt_ref[...], cqn)
    ct, s1t, s2t = cost_ref[...], s1t_ref[...], s2t_ref[...]
    for h in range(N_HEADS_B):
        t = qt[h * LANES:(h + 1) * LANES, :]
        rot = t * ct + pltpu.roll(t, LANES - half, 0) * s1t + pltpu.roll(t, half, 0) * s2t
        qmt_ref[0, h * LANES:(h + 1) * LANES, :] = rot.astype(BF16)
    kn = _dot(ckvn, wuk_ref[...])
    kr = _dot(xb, wkr_ref[...])
    c, s1, s2 = cos_ref[...], s1_ref[...], s2_ref[...]
    krot = kr * c + pltpu.roll(kr, LANES - half, 1) * s1 + pltpu.roll(kr, half, 1) * s2
    for h in range(N_HEADS_B):
        sl = slice(h * LANES, (h + 1) * LANES)
        km_ref[:, sl] = (kn[:, sl] + krot).astype(BF16)
    vmt_ref[0] = (_dot_nt(wuvt_ref[...], ckvn) + ones_ref[...]).astype(BF16)

    sig = _sigmoid(_dot(xb, wg_ref[...]) + bg_ref[...])
    sa_ref[...] = sig[:, :D_MODEL].astype(BF16)
    sb_ref[...] = sig[:, D_MODEL:].astype(BF16)


def _inproj(x, wp, tabs, seq):
    t = x.shape[0]
    tm = ROW_TILE
    nt = t // tm
    npos = seq // tm
    row = lambda w: pl.BlockSpec((tm, w), lambda i: (i, 0))
    tile = lambda r: pl.BlockSpec((1, r, tm), lambda i: (i, 0, 0))
    tab = pl.BlockSpec((tm, LANES), lambda i: (i % npos, 0))
    tabt = pl.BlockSpec((LANES, tm), lambda i: (0, i % npos))
    weights = [wp['wqt'], wp['wk'], wp['wvt'], wp['ones'], wp['wcq'], wp['wckv'], wp['wkr'], wp['wg'], wp['bg'],
               wp['qn'], wp['kvn'], wp['wuqt'], wp['wuk'], wp['wuvt']]
    hl = N_HEADS_B * LANES
    out_shape = [
        jax.ShapeDtypeStruct((nt, WIDTH_A, tm), BF16), jax.ShapeDtypeStruct((t, WIDTH_A), BF16),
        jax.ShapeDtypeStruct((nt, N_HEADS_A * LANES, tm), BF16), jax.ShapeDtypeStruct((nt, 1, WIDTH_A), F32),
        jax.ShapeDtypeStruct((nt, hl, tm), BF16), jax.ShapeDtypeStruct((t, hl), BF16),
        jax.ShapeDtypeStruct((nt, hl, tm), BF16),
        jax.ShapeDtypeStruct((t, D_MODEL), BF16), jax.ShapeDtypeStruct((t, D_MODEL), BF16),
    ]
    out_specs = [tile(WIDTH_A), row(WIDTH_A), tile(N_HEADS_A * LANES),
                 pl.BlockSpec((1, 1, WIDTH_A), lambda i: (i, 0, 0)),
                 tile(hl), row(hl), tile(hl), row(D_MODEL), row(D_MODEL)]
    return pl.pallas_call(
        _inproj_kernel,
        grid=(nt,),
        in_specs=[row(D_MODEL)] + [_full(w) for w in weights] + [tab, tab, tab, tabt, tabt, tabt],
        out_specs=out_specs,
        out_shape=out_shape,
        compiler_params=_params("parallel"),
        name="inproj",
    )(x, *weights, *tabs)


def _attend_init(tq):
    return jnp.full((1, tq), -jnp.inf, F32), jnp.zeros((PV_ROWS, tq), F32)


def _col_max(s_ref):
    return [jnp.max(s_ref[hh], axis=0, keepdims=True) for hh in range(ATT_HEADS)]


def _attend_staged(cur_ref, cur_max, state, vts, nxt_ref=None, next_scores=None):
    heads = range(ATT_HEADS)
    if nxt_ref is not None:
        for hh in heads:
            nxt_ref[hh] = next_scores(hh)
    new_m, scaled, pvs = [], [], []
    for hh in heads:
        m_i, acc = state[hh]
        m_new = jnp.maximum(m_i, cur_max[hh])
        new_m.append(m_new)
        scaled.append(jnp.exp2(m_i - m_new) * acc)
        pvs.append(_dot(vts[hh], jnp.exp2(cur_ref[hh] - m_new).astype(BF16)))
    nxt_max = _col_max(nxt_ref) if nxt_ref is not None else cur_max
    return tuple((m, a + pv) for m, a, pv in zip(new_m, scaled, pvs)), nxt_max


def _attention_loop(i, scores, values, causal, tq, sa_ref, sb_ref):
    heads = range(ATT_HEADS)
    bufs = (sa_ref, sb_ref)

    def vals(s):
        tile = jnp.where(s == 0, i, s - 1)
        return [values(hh, tile) for hh in heads]

    def ahead(s):
        return lambda hh: scores(hh, jnp.maximum(jnp.minimum(s, i - 1), 0))

    for hh in heads:
        sa_ref[hh] = jnp.where(causal, scores(hh, i), NEG)
    carry = (tuple(_attend_init(tq) for _ in heads), _col_max(sa_ref))
    nslots = i + 1

    def trip(n, c):
        for g in range(ATT_GROUP):
            s = n * ATT_GROUP + g
            c = _attend_staged(bufs[g % 2], c[1], c[0], vals(s), bufs[(g + 1) % 2], ahead(s))
        return c

    carry = lax.fori_loop(0, nslots // ATT_GROUP, trip, carry)
    done = nslots // ATT_GROUP * ATT_GROUP
    for g in range(ATT_GROUP - 1):
        def step(c, g=g):
            if g == ATT_GROUP - 2:
                return _attend_staged(bufs[g % 2], c[1], c[0], vals(done + g))
            return _attend_staged(bufs[g % 2], c[1], c[0], vals(done + g), bufs[(g + 1) % 2], ahead(done + g))

        carry = lax.cond(nslots - done > g, step, lambda c: c, carry)
    outs = []
    for hh in heads:
        _, acc = carry[0][hh]
        outs.append(acc[:V_HEAD_DIM, :] / acc[V_HEAD_DIM:V_HEAD_DIM + 1, :])
    return jnp.concatenate(outs, axis=0).T


def _causal_t(tq):
    key = lax.broadcasted_iota(jnp.int32, (tq, tq), 0)
    qry = lax.broadcasted_iota(jnp.int32, (tq, tq), 1)
    return key <= qry


def _moba_kernel(slope_ref, qt_ref, k_ref, e_ref, vt_ref, km_ref, o_ref, sa_ref, sb_ref):
    tq = ATT_TILE
    hp = pl.program_id(1)
    i = pl.program_id(2)
    nblk = km_ref.shape[1]
    frow = lax.broadcasted_iota(jnp.int32, (LANES, tq), 0)
    row = lax.broadcasted_iota(jnp.int32, (BIAS_LO, tq), 0)
    prow = lax.broadcasted_iota(jnp.int32, (LANES - POS_HI, tq), 0) + POS_HI

    qaug = []
    for hh in range(ATT_HEADS):
        head = hp * ATT_HEADS + hh
        grp = slice(hh // 2 * LANES, (hh // 2 + 1) * LANES)
        qt2 = qt_ref[0, grp, :]
        head_rows = (frow >= hh % 2 * HEAD_DIM_A) & (frow < (hh % 2 + 1) * HEAD_DIM_A)
        qh = jnp.where(head_rows, qt2, jnp.zeros_like(qt2))
        kmean = km_ref[0, :, grp].astype(BF16)
        if nblk < BIAS_LO:
            kmean = jnp.concatenate([kmean, jnp.zeros((BIAS_LO - nblk, LANES), BF16)], axis=0)
        g = jnp.where(row < i, _dot(kmean, qh), -jnp.inf)
        picked = row == i
        for r in range(MOBA_TOPK):
            m = jnp.max(g, axis=0, keepdims=True)
            idx = jnp.min(jnp.where(g == m, row, BIAS_LO), axis=0, keepdims=True)
            hit = row == idx
            picked = picked | (hit & (r < i))
            g = jnp.where(hit, -jnp.inf, g)
        dist = (i - row).astype(F32) * (slope_ref[head] * (MOBA_BLOCK * LOG2E))
        b = jnp.where(picked, -dist, NEG)
        b_hi = b.astype(BF16)
        b_lo = (b - b_hi.astype(F32)).astype(BF16)
        ones = jnp.where((prow == POS_HI + head) | (prow == POS_LO + head), 1.0, 0.0).astype(BF16)
        qs = (qh.astype(F32) * LOG2E).astype(BF16)
        qaug.append(jnp.concatenate([qs, b_hi, b_lo, ones], axis=0))

    def scores(hh, j):
        rows = pl.ds(pl.multiple_of(j * tq, tq), tq)
        grp = slice(hh // 2 * LANES, (hh // 2 + 1) * LANES)
        return _dot(jnp.concatenate([k_ref[rows, grp], e_ref[rows, :]], axis=1), qaug[hh])

    def values(hh, j):
        return vt_ref[j, hh * LANES:hh * LANES + PV_ROWS, :]

    o_ref[...] = _attention_loop(i, scores, values, _causal_t(tq), tq, sa_ref, sb_ref).astype(BF16)


def _moba(qat, ka, vat, kmean, etab, slopes, batch, seq):
    t = ka.shape[0]
    tq = ATT_TILE
    nq = seq // tq
    nblk = seq // MOBA_BLOCK
    nh = ATT_HEADS
    km = kmean.reshape(batch, nblk, WIDTH_A)
    return pl.pallas_call(
        _moba_kernel,
        grid=(batch, N_HEADS_A // nh, nq),
        in_specs=[
            pl.BlockSpec(memory_space=pltpu.SMEM),
            pl.BlockSpec((1, nh * HEAD_DIM_A, tq), lambda b, h, i: (b * nq + i, h, 0)),
            pl.BlockSpec((seq, nh * HEAD_DIM_A), lambda b, h, i: (b, h)),
            pl.BlockSpec((seq, LANES), lambda b, h, i: (0, 0)),
            pl.BlockSpec((nq, nh * LANES, tq), lambda b, h, i: (b, h, 0)),
            pl.BlockSpec((1, nblk, nh * HEAD_DIM_A), lambda b, h, i: (b, 0, h)),
        ],
        out_specs=pl.BlockSpec((tq, nh * HEAD_DIM_A), lambda b, h, i: (b * nq + i, h)),
        out_shape=jax.ShapeDtypeStruct((t, WIDTH_A), BF16),
        scratch_shapes=[pltpu.VMEM((nh, tq, tq), F32), pltpu.VMEM((nh, tq, tq), F32)],
        compiler_params=_params("parallel", "parallel", "arbitrary"),
        name="moba",
    )(slopes, qat, ka, etab, vat, km)


def _mla_kernel(qt_ref, k_ref, vt_ref, o_ref, sa_ref, sb_ref):
    tq = ATT_TILE
    i = pl.program_id(2)
    qt = [qt_ref[0, hh * LANES:(hh + 1) * LANES, :] for hh in range(ATT_HEADS)]

    def scores(hh, j):
        return _dot(k_ref[pl.ds(pl.multiple_of(j * tq, tq), tq), hh * LANES:(hh + 1) * LANES], qt[hh])

    def values(hh, j):
        return vt_ref[j, hh * LANES:hh * LANES + PV_ROWS, :]

    o_ref[...] = _attention_loop(i, scores, values, _causal_t(tq), tq, sa_ref, sb_ref).astype(BF16)


def _mla(qmt, km, vmt, batch, seq):
    t = km.shape[0]
    tq = ATT_TILE
    nq = seq // tq
    nh = ATT_HEADS
    return pl.pallas_call(
        _mla_kernel,
        grid=(batch, N_HEADS_B // nh, nq),
        in_specs=[
            pl.BlockSpec((1, nh * LANES, tq), lambda b, h, i: (b * nq + i, h, 0)),
            pl.BlockSpec((seq, nh * LANES), lambda b, h, i: (b, h)),
            pl.BlockSpec((nq, nh * LANES, tq), lambda b, h, i: (b, h, 0)),
        ],
        out_specs=pl.BlockSpec((tq, nh * V_HEAD_DIM), lambda b, h, i: (b * nq + i, h)),
        out_shape=jax.ShapeDtypeStruct((t, WIDTH_B), BF16),
        scratch_shapes=[pltpu.VMEM((nh, tq, tq), F32), pltpu.VMEM((nh, tq, tq), F32)],
        compiler_params=_params("parallel", "parallel", "arbitrary"),
        name="mla",
    )(qmt, km, vmt)


def _pack_halves(y):
    lo = pltpu.bitcast(y[:, :HALF].astype(BF16).astype(F32), U32)
    hi = pltpu.bitcast(y[:, HALF:].astype(BF16).astype(F32), U32)
    return (hi & jnp.uint32(0xFFFF0000)) | (lo >> 16)


def _unpack_halves(w):
    return pltpu.bitcast(w << 16, F32), pltpu.bitcast(w & jnp.uint32(0xFFFF0000), F32)


def _merge_kernel(x_ref, oa_ref, ob_ref, sa_ref, sb_ref, wpa_ref, wpb_ref, wo_ref, g_ref, b_ref, o_ref, op_ref):
    pa = _dot(oa_ref[...], wpa_ref[...])
    pb = _dot(ob_ref[...], wpb_ref[...])
    merged = sa_ref[...].astype(F32) * pa + sb_ref[...].astype(F32) * pb
    hmix = _dot(merged.astype(BF16), wo_ref[...])
    y = _layer_norm(ALPHA * x_ref[...] + hmix, g_ref[...], b_ref[...])
    o_ref[...] = y
    op_ref[...] = _pack_halves(y)


def _merge(x, oa, ob, sa, sb, wp):
    t = x.shape[0]
    tm = ROW_TILE
    row = lambda w: pl.BlockSpec((tm, w), lambda i: (i, 0))
    weights = [wp['wpa'], wp['wpb'], wp['wo'], wp['ln1_g'], wp['ln1_b']]
    return pl.pallas_call(
        _merge_kernel,
        grid=(t // tm,),
        in_specs=[row(D_MODEL), row(WIDTH_A), row(WIDTH_B), row(D_MODEL), row(D_MODEL)] + [_full(w) for w in weights],
        out_specs=[row(D_MODEL), row(HALF)],
        out_shape=[jax.ShapeDtypeStruct((t, D_MODEL), F32), jax.ShapeDtypeStruct((t, HALF), U32)],
        compiler_params=_params("parallel"),
        name="merge",
    )(x, oa, ob, sa, sb, *weights)


def _router_kernel(x_ref, wh_ref, wl_ref, rb_ref, idx_ref, rank_ref, w_ref, cnt_ref):
    tm = x_ref.shape[0]

    @pl.when(pl.program_id(0) == 0)
    def _():
        cnt_ref[...] = jnp.zeros_like(cnt_ref)

    x = x_ref[...]
    xh = x.astype(BF16)
    xl = (x - xh.astype(F32)).astype(BF16)
    wh, wl = wh_ref[...], wl_ref[...]
    logits = _dot_nt(wh, xh) + (_dot_nt(wh, xl) + _dot_nt(wl, xh))
    scores = _sigmoid(logits)
    choice = scores + rb_ref[...]
    row = lax.broadcasted_iota(jnp.int32, (GROUP_SIZE, tm), 0)
    groups = [choice[g * GROUP_SIZE:(g + 1) * GROUP_SIZE, :] for g in range(N_GROUPS)]
    gscore = []
    for blk in groups:
        m1 = jnp.max(blk, axis=0, keepdims=True)
        first = jnp.min(jnp.where(blk == m1, row, GROUP_SIZE), axis=0, keepdims=True)
        m2 = jnp.max(jnp.where(row == first, -jnp.inf, blk), axis=0, keepdims=True)
        gscore.append(m1 + m2)
    masked = []
    for g in range(N_GROUPS):
        ahead = jnp.zeros((1, tm), jnp.int32)
        for o in range(N_GROUPS):
            if o < g:
                ahead += (gscore[o] >= gscore[g]).astype(jnp.int32)
            elif o > g:
                ahead += (gscore[o] > gscore[g]).astype(jnp.int32)
        masked.append(jnp.where(ahead < TOPK_GROUPS, groups[g], -jnp.inf))
    cur = jnp.concatenate(masked, axis=0)
    erow = lax.broadcasted_iota(jnp.int32, (N_EXPERTS, tm), 0)
    hits, idxs, ws = [], [], []
    for _ in range(TOP_K):
        m = jnp.max(cur, axis=0, keepdims=True)
        e = jnp.min(jnp.where(cur == m, erow, N_EXPERTS), axis=0, keepdims=True)
        hit = erow == e
        hits.append(hit)
        idxs.append(e)
        ws.append(jnp.sum(jnp.where(hit, scores, 0.0), axis=0, keepdims=True))
        cur = jnp.where(hit, -jnp.inf, cur)
    total = ws[0]
    for w in ws[1:]:
        total = total + w
    member = hits[0]
    for hit in hits[1:]:
        member = member | hit
    member = jnp.where(member, 1.0, 0.0).astype(BF16)
    t_src = lax.broadcasted_iota(jnp.int32, (tm, tm), 0)
    t_dst = lax.broadcasted_iota(jnp.int32, (tm, tm), 1)
    before = _dot(member, jnp.where(t_src < t_dst, 1.0, 0.0).astype(BF16))
    base = cnt_ref[...]
    before = before + jnp.concatenate([base] * (tm // LANES), axis=1)
    cnt_ref[...] = base + _dot(member, jnp.ones((tm, LANES), BF16))
    for r in range(TOP_K):
        idx_ref[0, r:r + 1, :] = idxs[r]
        rank_ref[0, r:r + 1, :] = jnp.sum(jnp.where(hits[r], before, 0.0), axis=0, keepdims=True).astype(jnp.int32)
        w_ref[r:r + 1, :] = ws[r] / total * ROUTED_SCALE


def _router(x1, wp):
    t = x1.shape[0]
    tm = ROW_TILE
    nt = t // tm
    weights = [wp['wr_hi'], wp['wr_lo'], wp['rbias']]
    tile = pl.BlockSpec((1, TOP_K, tm), lambda i: (i, 0, 0))
    return pl.pallas_call(
        _router_kernel,
        grid=(nt,),
        in_specs=[pl.BlockSpec((tm, D_MODEL), lambda i: (i, 0))] + [_full(w) for w in weights],
        out_specs=[tile, tile, pl.BlockSpec((TOP_K, tm), lambda i: (0, i)),
                   pl.BlockSpec((N_EXPERTS, LANES), lambda i: (0, 0))],
        out_shape=[jax.ShapeDtypeStruct((nt, TOP_K, tm), jnp.int32), jax.ShapeDtypeStruct((nt, TOP_K, tm), jnp.int32),
                   jax.ShapeDtypeStruct((TOP_K, t), F32), jax.ShapeDtypeStruct((N_EXPERTS, LANES), F32)],
        compiler_params=_params("arbitrary"),
        name="router",
    )(x1, *weights)


def _pos_kernel(idx_ref, rank_ref, pq_ref, pos_ref):
    tm = idx_ref.shape[2]
    erow = lax.broadcasted_iota(jnp.int32, (N_EXPERTS, tm), 0)
    for k in range(TOP_K):
        onehot = jnp.where(erow == idx_ref[0, k:k + 1, :], 1.0, 0.0).astype(BF16)
        q = _dot(pq_ref[...], onehot)
        blk = (q[0:1, :] * 32.0 + q[1:2, :]).astype(jnp.int32)
        pos_ref[k:k + 1, :] = blk * EXPERT_BLOCK + rank_ref[0, k:k + 1, :]


def _positions(idx3, rank3, pstart):
    nt, _, tm = idx3.shape
    blk = pstart // EXPERT_BLOCK
    pq = jnp.zeros((8, N_EXPERTS), F32).at[0].set((blk // 32).astype(F32)).at[1].set((blk % 32).astype(F32))
    tile = pl.BlockSpec((1, TOP_K, tm), lambda i: (i, 0, 0))
    return pl.pallas_call(
        _pos_kernel,
        grid=(nt,),
        in_specs=[tile, tile, pl.BlockSpec((8, N_EXPERTS), lambda i: (0, 0))],
        out_specs=pl.BlockSpec((TOP_K, tm), lambda i: (0, i)),
        out_shape=jax.ShapeDtypeStruct((TOP_K, nt * tm), jnp.int32),
        compiler_params=_params("parallel"),
        name="moe_positions",
    )(idx3, rank3, pq.astype(BF16))


def _sc_mesh():
    return plsc.VectorSubcoreMesh(core_axis_name="c", subcore_axis_name="s", num_cores=SC_CORES,
                                  num_subcores=SC_SUBCORES)


def _sc_worker():
    return lax.axis_index("s") * SC_CORES + lax.axis_index("c")


def _sc_scatter_kernel(x_hbm, pos_hbm, out_hbm, idx_v, rows_v):
    nchunk = idx_v.shape[1]
    w = _sc_worker()
    pltpu.sync_copy(pos_hbm.at[w], idx_v)

    @pl.loop(0, nchunk)
    def _(c):
        pltpu.sync_copy(x_hbm.at[pl.ds((w * nchunk + c) * SC_CHUNK, SC_CHUNK)], rows_v)
        for k in range(TOP_K):
            pltpu.sync_copy(rows_v, out_hbm.at[idx_v.at[k, c]])


def _sc_dispatch(x1p, pos_kt, n_rows):
    t = x1p.shape[0]
    workers = SC_CORES * SC_SUBCORES
    nchunk = t // (workers * SC_CHUNK)
    assert t == workers * nchunk * SC_CHUNK
    pos4 = pos_kt.reshape(TOP_K, workers, nchunk, SC_CHUNK).transpose(1, 0, 2, 3)
    return pl.kernel(
        _sc_scatter_kernel,
        out_type=jax.ShapeDtypeStruct((n_rows, HALF), U32),
        mesh=_sc_mesh(),
        scratch_types=[pltpu.VMEM((TOP_K, nchunk, SC_CHUNK), jnp.int32), pltpu.VMEM((SC_CHUNK, HALF), U32)],
        name="moe_dispatch_sc",
    )(x1p, pos4)


def _expert_kernel(layer, be_ref, nused_ref, nvalid_ref, first_ref, slot_ref, ahead_ref, head_ref,
                   x_ref, wg_hbm, wu_hbm, wd_hbm, y_ref, wg_f, wu_f, wd_f, wg_b, wu_b, wd_b, sem):
    def fetch(e, slot):
        return [pltpu.make_async_copy(src.at[layer, e], dst.at[slot], sem.at[slot, n])
                for n, (src, dst) in enumerate(((wg_hbm, wg_f), (wu_hbm, wu_f), (wd_hbm, wd_f)))]

    @pl.when(pl.program_id(0) == 0)
    def _():
        for copy in fetch(head_ref[0], 0):
            copy.start()

        @pl.when(head_ref[1] >= 0)
        def _():
            for copy in fetch(head_ref[1], 1):
                copy.start()

    def one_block(b, rows):
        @pl.when((b < nused_ref[0]) & (first_ref[b] == 1))
        def _():
            slot = slot_ref[b]
            for copy in fetch(be_ref[b], slot):
                copy.wait()
            wg_b[...] = wg_f[slot].astype(BF16)
            wu_b[...] = wu_f[slot].astype(BF16)
            wd_b[...] = wd_f[slot].astype(BF16)

            @pl.when(ahead_ref[b] >= 0)
            def _():
                for copy in fetch(ahead_ref[b], slot):
                    copy.start()

        @pl.when(b < nused_ref[0])
        def _():
            live = lax.broadcasted_iota(jnp.int32, (EXPERT_BLOCK, HALF), 0) < nvalid_ref[b]
            xlo, xhi = (h.astype(BF16) for h in _unpack_halves(jnp.where(live, x_ref[rows, :], jnp.uint32(0))))
            g = _dot(xlo, wg_b[:HALF, :]) + _dot(xhi, wg_b[HALF:, :])
            u = _dot(xlo, wu_b[:HALF, :]) + _dot(xhi, wu_b[HALF:, :])
            a = (g * _sigmoid(g) * u).astype(BF16)
            y_ref[rows, :] = _pack_halves(_dot(a, wd_b[...]))

        @pl.when(b >= nused_ref[0])
        def _():
            y_ref[rows, :] = jnp.zeros((EXPERT_BLOCK, HALF), U32)

    for sub in range(EXPERT_STEP):
        one_block(pl.program_id(0) * EXPERT_STEP + sub, slice(sub * EXPERT_BLOCK, (sub + 1) * EXPERT_BLOCK))


def _experts(xs, block_e, nused, nvalid, counts, w_gate, w_up, w_down, layer, n_blocks):
    rows = EXPERT_BLOCK
    blocks = jnp.arange(n_blocks, dtype=jnp.int32)
    first = ((blocks == 0) | (block_e != jnp.roll(block_e, 1))) & (blocks < nused[0])
    run = jnp.cumsum(first.astype(jnp.int32)) - 1
    run_e = jnp.nonzero(counts > 0, size=N_EXPERTS, fill_value=-1)[0].astype(jnp.int32)
    ahead = jnp.concatenate([run_e, jnp.full((2,), -1, jnp.int32)])[jnp.clip(run, 0, N_EXPERTS - 1) + 2]
    step_rows = EXPERT_STEP * rows
    grid_spec = pltpu.PrefetchScalarGridSpec(
        num_scalar_prefetch=7,
        grid=(n_blocks // EXPERT_STEP,),
        in_specs=[
            pl.BlockSpec((step_rows, HALF), lambda s, be, nu, *_: (jnp.minimum(s, (nu[0] - 1) // EXPERT_STEP), 0)),
            pl.BlockSpec(memory_space=pl.ANY), pl.BlockSpec(memory_space=pl.ANY), pl.BlockSpec(memory_space=pl.ANY),
        ],
        out_specs=pl.BlockSpec((step_rows, HALF), lambda s, *_: (s, 0)),
        scratch_shapes=[
            pltpu.VMEM((2, D_MODEL, D_EXPERT), F32), pltpu.VMEM((2, D_MODEL, D_EXPERT), F32),
            pltpu.VMEM((2, D_EXPERT, D_MODEL), F32),
            pltpu.VMEM((D_MODEL, D_EXPERT), BF16), pltpu.VMEM((D_MODEL, D_EXPERT), BF16),
            pltpu.VMEM((D_EXPERT, D_MODEL), BF16),
            pltpu.SemaphoreType.DMA((2, 3)),
        ],
    )
    return pl.pallas_call(
        functools.partial(_expert_kernel, layer),
        grid_spec=grid_spec,
        out_shape=jax.ShapeDtypeStruct((n_blocks * rows, HALF), U32),
        compiler_params=_params("arbitrary"),
        name="moe_experts",
    )(block_e, nused, nvalid, first.astype(jnp.int32), (run % 2).astype(jnp.int32), ahead, run_e[:2],
      xs, w_gate, w_up, w_down)


def _sc_gather_kernel(table_hbm, idx_hbm, out_hbm, idx_v, rows_v, gsem, wsem):
    per_worker = idx_v.shape[0]
    base = _sc_worker() * per_worker
    pltpu.sync_copy(idx_hbm.at[pl.ds(base, per_worker)], idx_v)

    @pl.loop(0, per_worker // SC_CHUNK, step=SC_BUFS)
    def _(c):
        offs = [(c + n) * SC_CHUNK for n in range(SC_BUFS)]
        gathers = [pltpu.async_copy(table_hbm.at[idx_v.at[pl.ds(offs[n], SC_CHUNK)]], rows_v.at[n], gsem.at[n])
                   for n in range(SC_BUFS)]
        writes = []
        for n in range(SC_BUFS):
            gathers[n].wait()
            writes.append(pltpu.async_copy(rows_v.at[n], out_hbm.at[pl.ds(base + offs[n], SC_CHUNK)], wsem.at[n]))
        for write in writes:
            write.wait()


def _sc_gather_rows(table, idx):
    n = idx.shape[0]
    workers = SC_CORES * SC_SUBCORES
    assert n % (workers * SC_CHUNK * SC_BUFS) == 0
    return pl.kernel(
        _sc_gather_kernel,
        out_type=jax.ShapeDtypeStruct((n, HALF), U32),
        mesh=_sc_mesh(),
        scratch_types=[pltpu.VMEM((n // workers,), jnp.int32), pltpu.VMEM((SC_BUFS, SC_CHUNK, HALF), U32),
                       pltpu.SemaphoreType.DMA((SC_BUFS,)), pltpu.SemaphoreType.DMA((SC_BUFS,))],
        name="moe_gather_sc",
    )(table, idx)


def _combine_kernel(x_ref, w_ref, *refs):
    y_refs, (wsg_ref, wsu_ref, wsd_ref, g_ref, b_ref, o_ref) = refs[:TOP_K], refs[TOP_K:]
    x = x_ref[...]
    xb = x.astype(BF16)
    g = _dot(xb, wsg_ref[...])
    u = _dot(xb, wsu_ref[...])
    shared = _dot((g * _sigmoid(g) * u).astype(BF16), wsd_ref[...])
    w = w_ref[...]
    lo, hi = (h * w[:, 0:1] for h in _unpack_halves(y_refs[0][...]))
    for k in range(1, TOP_K):
        lo_k, hi_k = _unpack_halves(y_refs[k][...])
        lo = lo + lo_k * w[:, k:k + 1]
        hi = hi + hi_k * w[:, k:k + 1]
    routed = jnp.concatenate([lo, hi], axis=1)
    o_ref[...] = _layer_norm(ALPHA * x + (shared + routed), g_ref[...], b_ref[...])


def _combine(x1, y8, w_tk, wp):
    t = x1.shape[0]
    tm = ROW_TILE
    nt = t // tm
    weights = [wp['wsg'], wp['wsu'], wp['wsd'], wp['ln2_g'], wp['ln2_b']]
    y_specs = [pl.BlockSpec((tm, HALF), lambda i, k=k: (k * nt + i, 0)) for k in range(TOP_K)]
    return pl.pallas_call(
        _combine_kernel,
        grid=(nt,),
        in_specs=[pl.BlockSpec((tm, D_MODEL), lambda i: (i, 0)), pl.BlockSpec((tm, TOP_K), lambda i: (i, 0))]
                 + y_specs + [_full(w) for w in weights],
        out_specs=pl.BlockSpec((tm, D_MODEL), lambda i: (i, 0)),
        out_shape=jax.ShapeDtypeStruct((t, D_MODEL), F32),
        compiler_params=_params("parallel"),
        name="moe_combine",
    )(x1, w_tk, *([y8] * TOP_K), *weights)


def _moe(x1, x1p, wp, w_e_gate, w_e_up, w_e_down, layer):
    t = x1.shape[0]
    rows = EXPERT_BLOCK
    n_blocks = -(-(t * TOP_K + N_EXPERTS * (rows - 1)) // (rows * EXPERT_STEP)) * EXPERT_STEP
    idx3, rank3, top_w, cnt = _router(x1, wp)
    counts = cnt[:, 0].astype(jnp.int32)
    padded = (counts + rows - 1) // rows * rows
    padded_end = jnp.cumsum(padded)
    pstart = (padded_end - padded).astype(jnp.int32)
    nused = (padded_end[-1] // rows).astype(jnp.int32).reshape(1)
    blocks = jnp.arange(n_blocks, dtype=jnp.int32)
    block_row = jnp.minimum(blocks, nused[0] - 1) * rows
    block_e = jnp.sum((padded_end[None, :] <= block_row[:, None]).astype(jnp.int32), axis=1)
    block_e = jnp.minimum(block_e, N_EXPERTS - 1)
    nvalid = jnp.clip(counts[block_e] - (blocks * rows - pstart[block_e]), 0, rows).astype(jnp.int32)
    pos_kt = _positions(idx3, rank3, pstart)
    xs = _sc_dispatch(x1p, pos_kt, n_blocks * rows)
    ys = _experts(xs, block_e, nused, nvalid, counts, w_e_gate, w_e_up, w_e_down, layer, n_blocks)
    return _combine(x1, _sc_gather_rows(ys, pos_kt.reshape(TOP_K * t)), top_w.T, wp)


def _head_groups_t(w, used):
    k, h, _ = w.shape
    return jnp.pad(w, ((0, 0), (0, 0), (0, LANES - used))).reshape(k, h * LANES).T


def _prep_layer(l, w_in, b_gate, q_norm, w_uq, kv_norm, w_ukv, w_proj_a, w_proj_b, w_out, ln1_g, ln1_b,
                w_router, router_bias, w_s_gate, w_s_up, w_s_down, ln2_g, ln2_b):
    w = w_in[l]
    o = 0
    cols = {}
    for name, width in (('qa', WIDTH_A), ('ka', WIDTH_A), ('va', WIDTH_A), ('cq', Q_LORA_RANK),
                        ('ckv', KV_LORA_RANK), ('kr', QK_ROPE_DIM), ('ga', D_MODEL), ('gb', D_MODEL)):
        cols[name] = w[:, o:o + width]
        o += width
    wkr = jnp.zeros((D_MODEL, LANES), F32).at[:, QK_NOPE_DIM:QK_NOPE_DIM + QK_ROPE_DIM].set(cols['kr'])
    dqk = QK_NOPE_DIM + QK_ROPE_DIM
    wq = w_uq[l].reshape(Q_LORA_RANK, N_HEADS_B, dqk) * (dqk ** -0.5 * LOG2E)
    wkv = w_ukv[l].reshape(KV_LORA_RANK, N_HEADS_B, QK_NOPE_DIM + V_HEAD_DIM)
    wuk = jnp.pad(wkv[:, :, :QK_NOPE_DIM], ((0, 0), (0, 0), (0, LANES - QK_NOPE_DIM))).reshape(KV_LORA_RANK, N_HEADS_B * LANES)
    ones = jnp.zeros((N_HEADS_B, LANES), F32).at[:, V_HEAD_DIM].set(1.0).reshape(N_HEADS_B * LANES, 1)
    wr_t = w_router[l].T
    wr_hi = wr_t.astype(BF16)
    return dict(
        wqt=(cols['qa'] * HEAD_DIM_A ** -0.5).T.astype(BF16), wk=cols['ka'].astype(BF16),
        wvt=_head_groups_t(cols['va'].reshape(D_MODEL, N_HEADS_A, HEAD_DIM_A), HEAD_DIM_A).astype(BF16), ones=ones,
        wcq=cols['cq'].astype(BF16), wckv=cols['ckv'].astype(BF16), wkr=wkr.astype(BF16),
        wg=jnp.concatenate([cols['ga'], cols['gb']], axis=1).astype(BF16),
        bg=b_gate[l].reshape(1, 2 * D_MODEL), qn=q_norm[l].reshape(1, Q_LORA_RANK), kvn=kv_norm[l].reshape(1, KV_LORA_RANK),
        wuqt=_head_groups_t(wq, dqk).astype(BF16), wuk=wuk.astype(BF16),
        wuvt=_head_groups_t(wkv[:, :, QK_NOPE_DIM:], V_HEAD_DIM).astype(BF16),
        wpa=w_proj_a[l].astype(BF16), wpb=w_proj_b[l].astype(BF16), wo=w_out[l].astype(BF16),
        ln1_g=ln1_g[l].reshape(1, D_MODEL), ln1_b=ln1_b[l].reshape(1, D_MODEL),
        wr_hi=wr_hi, wr_lo=(wr_t - wr_hi.astype(F32)).astype(BF16), rbias=router_bias[l].reshape(N_EXPERTS, 1),
        wsg=w_s_gate[l].astype(BF16), wsu=w_s_up[l].astype(BF16), wsd=w_s_down[l].astype(BF16),
        ln2_g=ln2_g[l].reshape(1, D_MODEL), ln2_b=ln2_b[l].reshape(1, D_MODEL),
    )


def _rope_tables(seq):
    pos = jnp.arange(seq, dtype=F32)
    inv_freq = ROPE_THETA ** (-jnp.arange(0, QK_ROPE_DIM, 2, dtype=F32) / QK_ROPE_DIM)
    ang = pos[:, None] * inv_freq[None, :]
    cos, sin = jnp.cos(ang), jnp.sin(ang)
    half = QK_ROPE_DIM // 2
    z = lambda n: jnp.zeros((seq, n), F32)
    c = jnp.concatenate([jnp.ones((seq, QK_NOPE_DIM), F32), cos, cos, z(LANES - QK_NOPE_DIM - QK_ROPE_DIM)], axis=1)
    s1 = jnp.concatenate([z(QK_NOPE_DIM), -sin, z(LANES - QK_NOPE_DIM - half)], axis=1)
    s2 = jnp.concatenate([z(QK_NOPE_DIM + half), sin, z(LANES - QK_NOPE_DIM - QK_ROPE_DIM)], axis=1)
    return c, s1, s2, c.T, s1.T, s2.T


def _moba_key_table(seq, slopes):
    blk = jnp.arange(seq, dtype=jnp.int32) // MOBA_BLOCK
    onehot = (blk[:, None] == jnp.arange(BIAS_LO, dtype=jnp.int32)[None, :]).astype(F32)
    inblk = (jnp.arange(seq, dtype=jnp.int32) % MOBA_BLOCK).astype(F32)[:, None] * (slopes * LOG2E)[None, :]
    hi = inblk.astype(BF16)
    lo = (inblk - hi.astype(F32)).astype(BF16)
    pad = jnp.zeros((seq, LANES - POS_LO - N_HEADS_A), BF16)
    return jnp.concatenate([onehot.astype(BF16), onehot.astype(BF16), hi, lo, pad], axis=1)


def kernel(x, w_in, b_gate, q_norm, w_uq, kv_norm, w_ukv, w_proj_a, w_proj_b, w_out, ln1_g, ln1_b, w_router, router_bias, w_e_gate, w_e_up, w_e_down, w_s_gate, w_s_up, w_s_down, ln2_g, ln2_b):
    batch, seq, d = x.shape
    assert d == D_MODEL and seq % MOBA_BLOCK == 0 and MOBA_TOPK <= seq // MOBA_BLOCK <= BIAS_LO
    assert POS_LO + N_HEADS_A <= LANES and POS_HI + N_HEADS_A <= POS_LO
    tabs = _rope_tables(seq)
    slopes = jnp.asarray(np.exp2(-8.0 * (np.arange(N_HEADS_A) + 1.0) / N_HEADS_A), F32)
    etab = _moba_key_table(seq, slopes)
    h = x.reshape(batch * seq, d)
    for l in range(DEPTH):
        wp = _prep_layer(l, w_in, b_gate, q_norm, w_uq, kv_norm, w_ukv, w_proj_a, w_proj_b, w_out, ln1_g, ln1_b,
                         w_router, router_bias, w_s_gate, w_s_up, w_s_down, ln2_g, ln2_b)
        qat, ka, vat, kmean, qmt, km, vmt, sa, sb = _inproj(h, wp, tabs, seq)
        oa = _moba(qat, ka, vat, kmean, etab, slopes, batch, seq)
        ob = _mla(qmt, km, vmt, batch, seq)
        x1, x1p = _merge(h, oa, ob, sa, sb, wp)
        h = _moe(x1, x1p, wp, w_e_gate, w_e_up, w_e_down, l)
    return h.reshape(batch, seq, d)
```

```python
import functools

import numpy as np

import jax
import jax.numpy as jnp
from jax import lax
from jax.experimental import pallas as pl
from jax.experimental.pallas import tpu as pltpu
from jax.experimental.pallas import tpu_sc as plsc

D_MODEL = 1024
N_HEADS_A = 8
HEAD_DIM_A = 64
WIDTH_A = N_HEADS_A * HEAD_DIM_A
MOBA_BLOCK = 256
MOBA_TOPK = 3
N_HEADS_B = 8
QK_NOPE_DIM = 64
QK_ROPE_DIM = 32
V_HEAD_DIM = 64
Q_LORA_RANK = 384
KV_LORA_RANK = 256
WIDTH_B = N_HEADS_B * V_HEAD_DIM
ROPE_THETA = 10000.0
N_EXPERTS = 256
TOP_K = 8
N_GROUPS = 8
TOPK_GROUPS = 4
GROUP_SIZE = N_EXPERTS // N_GROUPS
D_EXPERT = 256
D_SHARED = 256
ROUTED_SCALE = 2.5
DEPTH = 2
ALPHA = (2 * DEPTH) ** 0.25
LN_EPS = 1e-5
RMS_EPS = 1e-6

LANES = 128
NEG = -1e30
LOG2E = float(np.log2(np.e))
ROW_TILE = 256
ATT_TILE = 256
EXPERT_BLOCK = 256
EXPERT_STEP = 4
SC_CORES = 2
SC_SUBCORES = 16
SC_CHUNK = 64
SC_BUFS = 2
VMEM_LIMIT = 56 * 1024 * 1024
HALF = D_MODEL // 2
ATT_HEADS = 4
ATT_GROUP = 4
PV_ROWS = 80
BIAS_HI, BIAS_LO, POS_HI, POS_LO = 0, 32, 64, 72

BF16 = jnp.bfloat16
F32 = jnp.float32
U32 = jnp.uint32


def _dot(a, b):
    return jnp.dot(a, b, preferred_element_type=F32)


def _dot_nt(a, b):
    return lax.dot_general(a, b, (((1,), (1,)), ((), ())), preferred_element_type=F32)


def _sigmoid(x):
    return 1.0 / (1.0 + jnp.exp(-x))


def _layer_norm(y, g, b):
    mu = jnp.mean(y, axis=-1, keepdims=True)
    d = y - mu
    var = jnp.mean(d * d, axis=-1, keepdims=True)
    return d * lax.rsqrt(var + LN_EPS) * g + b


def _params(*sem):
    return pltpu.CompilerParams(dimension_semantics=sem, vmem_limit_bytes=VMEM_LIMIT)


def _full(a):
    return pl.BlockSpec(a.shape, lambda *_: (0,) * a.ndim)


def _inproj_kernel(x_ref, wqt_ref, wk_ref, wvt_ref, ones_ref, wcq_ref, wckv_ref, wkr_ref, wg_ref, bg_ref,
                   qn_ref, kvn_ref, wuqt_ref, wuk_ref, wuvt_ref, cos_ref, s1_ref, s2_ref, cost_ref, s1t_ref, s2t_ref,
                   qat_ref, ka_ref, vat_ref, kmean_ref, qmt_ref, km_ref, vmt_ref, sa_ref, sb_ref):
    xb = x_ref[...].astype(BF16)
    half = QK_ROPE_DIM // 2
    qat_ref[0] = _dot_nt(wqt_ref[...], xb).astype(BF16)
    k = _dot(xb, wk_ref[...])
    ka_ref[...] = k.astype(BF16)
    kmean_ref[0] = jnp.mean(k, axis=0, keepdims=True)
    vat_ref[0] = (_dot_nt(wvt_ref[...], xb) + ones_ref[...]).astype(BF16)

    cq = _dot(xb, wcq_ref[...])
    cqn = (cq * lax.rsqrt(jnp.mean(cq * cq, axis=-1, keepdims=True) + RMS_EPS) * qn_ref[...]).astype(BF16)
    ckv = _dot(xb, wckv_ref[...])
    ckvn = (ckv * lax.rsqrt(jnp.mean(ckv * ckv, axis=-1, keepdims=True) + RMS_EPS) * kvn_ref[...]).astype(BF16)
    qt = _dot_nt(wuqt_ref[...], cqn)
    ct, s1t, s2t = cost_ref[...], s1t_ref[...], s2t_ref[...]
    for h in range(N_HEADS_B):
        t = qt[h * LANES:(h + 1) * LANES, :]
        rot = t * ct + pltpu.roll(t, LANES - half, 0) * s1t + pltpu.roll(t, half, 0) * s2t
        qmt_ref[0, h * LANES:(h + 1) * LANES, :] = rot.astype(BF16)
    kn = _dot(ckvn, wuk_ref[...])
    kr = _dot(xb, wkr_ref[...])
    c, s1, s2 = cos_ref[...], s1_ref[...], s2_ref[...]
    krot = kr * c + pltpu.roll(kr, LANES - half, 1) * s1 + pltpu.roll(kr, half, 1) * s2
    for h in range(N_HEADS_B):
        sl = slice(h * LANES, (h + 1) * LANES)
        km_ref[:, sl] = (kn[:, sl] + krot).astype(BF16)
    vmt_ref[0] = (_dot_nt(wuvt_ref[...], ckvn) + ones_ref[...]).astype(BF16)

    sig = _sigmoid(_dot(xb, wg_ref[...]) + bg_ref[...])
    sa_ref[...] = sig[:, :D_MODEL].astype(BF16)
    sb_ref[...] = sig[:, D_MODEL:].astype(BF16)


def _inproj(x, wp, tabs, seq):
    t = x.shape[0]
    tm = ROW_TILE
    nt = t // tm
    npos = seq // tm
    row = lambda w: pl.BlockSpec((tm, w), lambda i: (i, 0))
    tile = lambda r: pl.BlockSpec((1, r, tm), lambda i: (i, 0, 0))
    tab = pl.BlockSpec((tm, LANES), lambda i: (i % npos, 0))
    tabt = pl.BlockSpec((LANES, tm), lambda i: (0, i % npos))
    weights = [wp['wqt'], wp['wk'], wp['wvt'], wp['ones'], wp['wcq'], wp['wckv'], wp['wkr'], wp['wg'], wp['bg'],
               wp['qn'], wp['kvn'], wp['wuqt'], wp['wuk'], wp['wuvt']]
    hl = N_HEADS_B * LANES
    out_shape = [
        jax.ShapeDtypeStruct((nt, WIDTH_A, tm), BF16), jax.ShapeDtypeStruct((t, WIDTH_A), BF16),
        jax.ShapeDtypeStruct((nt, N_HEADS_A * LANES, tm), BF16), jax.ShapeDtypeStruct((nt, 1, WIDTH_A), F32),
        jax.ShapeDtypeStruct((nt, hl, tm), BF16), jax.ShapeDtypeStruct((t, hl), BF16),
        jax.ShapeDtypeStruct((nt, hl, tm), BF16),
        jax.ShapeDtypeStruct((t, D_MODEL), BF16), jax.ShapeDtypeStruct((t, D_MODEL), BF16),
    ]
    out_specs = [tile(WIDTH_A), row(WIDTH_A), tile(N_HEADS_A * LANES),
                 pl.BlockSpec((1, 1, WIDTH_A), lambda i: (i, 0, 0)),
                 tile(hl), row(hl), tile(hl), row(D_MODEL), row(D_MODEL)]
    return pl.pallas_call(
        _inproj_kernel,
        grid=(nt,),
        in_specs=[row(D_MODEL)] + [_full(w) for w in weights] + [tab, tab, tab, tabt, tabt, tabt],
        out_specs=out_specs,
        out_shape=out_shape,
        compiler_params=_params("parallel"),
        name="inproj",
    )(x, *weights, *tabs)


def _attend_init(tq):
    return jnp.full((1, tq), -jnp.inf, F32), jnp.zeros((PV_ROWS, tq), F32)


def _col_max(s_ref):
    return [jnp.max(s_ref[hh], axis=0, keepdims=True) for hh in range(ATT_HEADS)]


def _attend_staged(cur_ref, cur_max, state, vts, nxt_ref=None, next_scores=None):
    heads = range(ATT_HEADS)
    if nxt_ref is not None:
        for hh in heads:
            nxt_ref[hh] = next_scores(hh)
    new_m, scaled, pvs = [], [], []
    for hh in heads:
        m_i, acc = state[hh]
        m_new = jnp.maximum(m_i, cur_max[hh])
        new_m.append(m_new)
        scaled.append(jnp.exp2(m_i - m_new) * acc)
        pvs.append(_dot(vts[hh], jnp.exp2(cur_ref[hh] - m_new).astype(BF16)))
    nxt_max = _col_max(nxt_ref) if nxt_ref is not None else cur_max
    return tuple((m, a + pv) for m, a, pv in zip(new_m, scaled, pvs)), nxt_max


def _attention_loop(i, scores, values, causal, tq, sa_ref, sb_ref):
    heads = range(ATT_HEADS)
    bufs = (sa_ref, sb_ref)

    def vals(s):
        tile = jnp.where(s == 0, i, s - 1)
        return [values(hh, tile) for hh in heads]

    def ahead(s):
        return lambda hh: scores(hh, jnp.maximum(jnp.minimum(s, i - 1), 0))

    for hh in heads:
        sa_ref[hh] = jnp.where(causal, scores(hh, i), NEG)
    carry = (tuple(_attend_init(tq) for _ in heads), _col_max(sa_ref))
    nslots = i + 1

    def trip(n, c):
        for g in range(ATT_GROUP):
            s = n * ATT_GROUP + g
            c = _attend_staged(bufs[g % 2], c[1], c[0], vals(s), bufs[(g + 1) % 2], ahead(s))
        return c

    carry = lax.fori_loop(0, nslots // ATT_GROUP, trip, carry)
    done = nslots // ATT_GROUP * ATT_GROUP
    for g in range(ATT_GROUP - 1):
        def step(c, g=g):
            if g == ATT_GROUP - 2:
                return _attend_staged(bufs[g % 2], c[1], c[0], vals(done + g))
            return _attend_staged(bufs[g % 2], c[1], c[0], vals(done + g), bufs[(g + 1) % 2], ahead(done + g))

        carry = lax.cond(nslots - done > g, step, lambda c: c, carry)
    outs = []
    for hh in heads:
        _, acc = carry[0][hh]
        outs.append(acc[:V_HEAD_DIM, :] / acc[V_HEAD_DIM:V_HEAD_DIM + 1, :])
    return jnp.concatenate(outs, axis=0).T


def _causal_t(tq):
    key = lax.broadcasted_iota(jnp.int32, (tq, tq), 0)
    qry = lax.broadcasted_iota(jnp.int32, (tq, tq), 1)
    return key <= qry


def _moba_kernel(slope_ref, qt_ref, k_ref, e_ref, vt_ref, km_ref, o_ref, sa_ref, sb_ref):
    tq = ATT_TILE
    hp = pl.program_id(1)
    i = pl.program_id(2)
    nblk = km_ref.shape[1]
    frow = lax.broadcasted_iota(jnp.int32, (LANES, tq), 0)
    row = lax.broadcasted_iota(jnp.int32, (BIAS_LO, tq), 0)
    prow = lax.broadcasted_iota(jnp.int32, (LANES - POS_HI, tq), 0) + POS_HI

    qaug = []
    for hh in range(ATT_HEADS):
        head = hp * ATT_HEADS + hh
        grp = slice(hh // 2 * LANES, (hh // 2 + 1) * LANES)
        qt2 = qt_ref[0, grp, :]
        head_rows = (frow >= hh % 2 * HEAD_DIM_A) & (frow < (hh % 2 + 1) * HEAD_DIM_A)
        qh = jnp.where(head_rows, qt2, jnp.zeros_like(qt2))
        kmean = km_ref[0, :, grp].astype(BF16)
        if nblk < BIAS_LO:
            kmean = jnp.concatenate([kmean, jnp.zeros((BIAS_LO - nblk, LANES), BF16)], axis=0)
        g = jnp.where(row < i, _dot(kmean, qh), -jnp.inf)
        picked = row == i
        for r in range(MOBA_TOPK):
            m = jnp.max(g, axis=0, keepdims=True)
            idx = jnp.min(jnp.where(g == m, row, BIAS_LO), axis=0, keepdims=True)
            hit = row == idx
            picked = picked | (hit & (r < i))
            g = jnp.where(hit, -jnp.inf, g)
        dist = (i - row).astype(F32) * (slope_ref[head] * (MOBA_BLOCK * LOG2E))
        b = jnp.where(picked, -dist, NEG)
        b_hi = b.astype(BF16)
        b_lo = (b - b_hi.astype(F32)).astype(BF16)
        ones = jnp.where((prow == POS_HI + head) | (prow == POS_LO + head), 1.0, 0.0).astype(BF16)
        qs = (qh.astype(F32) * LOG2E).astype(BF16)
        qaug.append(jnp.concatenate([qs, b_hi, b_lo, ones], axis=0))

    def scores(hh, j):
        rows = pl.ds(pl.multiple_of(j * tq, tq), tq)
        grp = slice(hh // 2 * LANES, (hh // 2 + 1) * LANES)
        return _dot(jnp.concatenate([k_ref[rows, grp], e_ref[rows, :]], axis=1), qaug[hh])

    def values(hh, j):
        return vt_ref[j, hh * LANES:hh * LANES + PV_ROWS, :]

    o_ref[...] = _attention_loop(i, scores, values, _causal_t(tq), tq, sa_ref, sb_ref).astype(BF16)


def _moba(qat, ka, vat, kmean, etab, slopes, batch, seq):
    t = ka.shape[0]
    tq = ATT_TILE
    nq = seq // tq
    nblk = seq // MOBA_BLOCK
    nh = ATT_HEADS
    km = kmean.reshape(batch, nblk, WIDTH_A)
    return pl.pallas_call(
        _moba_kernel,
        grid=(batch, N_HEADS_A // nh, nq),
        in_specs=[
            pl.BlockSpec(memory_space=pltpu.SMEM),
            pl.BlockSpec((1, nh * HEAD_DIM_A, tq), lambda b, h, i: (b * nq + i, h, 0)),
            pl.BlockSpec((seq, nh * HEAD_DIM_A), lambda b, h, i: (b, h)),
            pl.BlockSpec((seq, LANES), lambda b, h, i: (0, 0)),
            pl.BlockSpec((nq, nh * LANES, tq), lambda b, h, i: (b, h, 0)),
            pl.BlockSpec((1, nblk, nh * HEAD_DIM_A), lambda b, h, i: (b, 0, h)),
        ],
        out_specs=pl.BlockSpec((tq, nh * HEAD_DIM_A), lambda b, h, i: (b * nq + i, h)),
        out_shape=jax.ShapeDtypeStruct((t, WIDTH_A), BF16),
        scratch_shapes=[pltpu.VMEM((nh, tq, tq), F32), pltpu.VMEM((nh, tq, tq), F32)],
        compiler_params=_params("parallel", "parallel", "arbitrary"),
        name="moba",
    )(slopes, qat, ka, etab, vat, km)


def _mla_kernel(qt_ref, k_ref, vt_ref, o_ref, sa_ref, sb_ref):
    tq = ATT_TILE
    i = pl.program_id(2)
    qt = [qt_ref[0, hh * LANES:(hh + 1) * LANES, :] for hh in range(ATT_HEADS)]

    def scores(hh, j):
        return _dot(k_ref[pl.ds(pl.multiple_of(j * tq, tq), tq), hh * LANES:(hh + 1) * LANES], qt[hh])

    def values(hh, j):
        return vt_ref[j, hh * LANES:hh * LANES + PV_ROWS, :]

    o_ref[...] = _attention_loop(i, scores, values, _causal_t(tq), tq, sa_ref, sb_ref).astype(BF16)


def _mla(qmt, km, vmt, batch, seq):
    t = km.shape[0]
    tq = ATT_TILE
    nq = seq // tq
    nh = ATT_HEADS
    return pl.pallas_call(
        _mla_kernel,
        grid=(batch, N_HEADS_B // nh, nq),
        in_specs=[
            pl.BlockSpec((1, nh * LANES, tq), lambda b, h, i: (b * nq + i, h, 0)),
            pl.BlockSpec((seq, nh * LANES), lambda b, h, i: (b, h)),
            pl.BlockSpec((nq, nh * LANES, tq), lambda b, h, i: (b, h, 0)),
        ],
        out_specs=pl.BlockSpec((tq, nh * V_HEAD_DIM), lambda b, h, i: (b * nq + i, h)),
        out_shape=jax.ShapeDtypeStruct((t, WIDTH_B), BF16),
        scratch_shapes=[pltpu.VMEM((nh, tq, tq), F32), pltpu.VMEM((nh, tq, tq), F32)],
        compiler_params=_params("parallel", "parallel", "arbitrary"),
        name="mla",
    )(qmt, km, vmt)


def _pack_halves(y):
    lo = pltpu.bitcast(y[:, :HALF].astype(BF16).astype(F32), U32)
    hi = pltpu.bitcast(y[:, HALF:].astype(BF16).astype(F32), U32)
    return (hi & jnp.uint32(0xFFFF0000)) | (lo >> 16)


def _unpack_halves(w):
    return pltpu.bitcast(w << 16, F32), pltpu.bitcast(w & jnp.uint32(0xFFFF0000), F32)


def _merge_kernel(x_ref, oa_ref, ob_ref, sa_ref, sb_ref, wpa_ref, wpb_ref, wo_ref, g_ref, b_ref, o_ref, op_ref):
    pa = _dot(oa_ref[...], wpa_ref[...])
    pb = _dot(ob_ref[...], wpb_ref[...])
    merged = sa_ref[...].astype(F32) * pa + sb_ref[...].astype(F32) * pb
    hmix = _dot(merged.astype(BF16), wo_ref[...])
    y = _layer_norm(ALPHA * x_ref[...] + hmix, g_ref[...], b_ref[...])
    o_ref[...] = y
    op_ref[...] = _pack_halves(y)


def _merge(x, oa, ob, sa, sb, wp):
    t = x.shape[0]
    tm = ROW_TILE
    row = lambda w: pl.BlockSpec((tm, w), lambda i: (i, 0))
    weights = [wp['wpa'], wp['wpb'], wp['wo'], wp['ln1_g'], wp['ln1_b']]
    return pl.pallas_call(
        _merge_kernel,
        grid=(t // tm,),
        in_specs=[row(D_MODEL), row(WIDTH_A), row(WIDTH_B), row(D_MODEL), row(D_MODEL)] + [_full(w) for w in weights],
        out_specs=[row(D_MODEL), row(HALF)],
        out_shape=[jax.ShapeDtypeStruct((t, D_MODEL), F32), jax.ShapeDtypeStruct((t, HALF), U32)],
        compiler_params=_params("parallel"),
        name="merge",
    )(x, oa, ob, sa, sb, *weights)


def _router_kernel(x_ref, wh_ref, wl_ref, rb_ref, idx_ref, rank_ref, w_ref, cnt_ref):
    tm = x_ref.shape[0]

    @pl.when(pl.program_id(0) == 0)
    def _():
        cnt_ref[...] = jnp.zeros_like(cnt_ref)

    x = x_ref[...]
    xh = x.astype(BF16)
    xl = (x - xh.astype(F32)).astype(BF16)
    wh, wl = wh_ref[...], wl_ref[...]
    logits = _dot_nt(wh, xh) + (_dot_nt(wh, xl) + _dot_nt(wl, xh))
    scores = _sigmoid(logits)
    choice = scores + rb_ref[...]
    row = lax.broadcasted_iota(jnp.int32, (GROUP_SIZE, tm), 0)
    groups = [choice[g * GROUP_SIZE:(g + 1) * GROUP_SIZE, :] for g in range(N_GROUPS)]
    gscore = []
    for blk in groups:
        m1 = jnp.max(blk, axis=0, keepdims=True)
        first = jnp.min(jnp.where(blk == m1, row, GROUP_SIZE), axis=0, keepdims=True)
        m2 = jnp.max(jnp.where(row == first, -jnp.inf, blk), axis=0, keepdims=True)
        gscore.append(m1 + m2)
    masked = []
    for g in range(N_GROUPS):
        ahead = jnp.zeros((1, tm), jnp.int32)
        for o in range(N_GROUPS):
            if o < g:
                ahead += (gscore[o] >= gscore[g]).astype(jnp.int32)
            elif o > g:
                ahead += (gscore[o] > gscore[g]).astype(jnp.int32)
        masked.append(jnp.where(ahead < TOPK_GROUPS, groups[g], -jnp.inf))
    cur = jnp.concatenate(masked, axis=0)
    erow = lax.broadcasted_iota(jnp.int32, (N_EXPERTS, tm), 0)
    hits, idxs, ws = [], [], []
    for _ in range(TOP_K):
        m = jnp.max(cur, axis=0, keepdims=True)
        e = jnp.min(jnp.where(cur == m, erow, N_EXPERTS), axis=0, keepdims=True)
        hit = erow == e
        hits.append(hit)
        idxs.append(e)
        ws.append(jnp.sum(jnp.where(hit, scores, 0.0), axis=0, keepdims=True))
        cur = jnp.where(hit, -jnp.inf, cur)
    total = ws[0]
    for w in ws[1:]:
        total = total + w
    member = hits[0]
    for hit in hits[1:]:
        member = member | hit
    member = jnp.where(member, 1.0, 0.0).astype(BF16)
    t_src = lax.broadcasted_iota(jnp.int32, (tm, tm), 0)
    t_dst = lax.broadcasted_iota(jnp.int32, (tm, tm), 1)
    before = _dot(member, jnp.where(t_src < t_dst, 1.0, 0.0).astype(BF16))
    base = cnt_ref[...]
    before = before + jnp.concatenate([base] * (tm // LANES), axis=1)
    cnt_ref[...] = base + _dot(member, jnp.ones((tm, LANES), BF16))
    for r in range(TOP_K):
        idx_ref[0, r:r + 1, :] = idxs[r]
        rank_ref[0, r:r + 1, :] = jnp.sum(jnp.where(hits[r], before, 0.0), axis=0, keepdims=True).astype(jnp.int32)
        w_ref[r:r + 1, :] = ws[r] / total * ROUTED_SCALE


def _router(x1, wp):
    t = x1.shape[0]
    tm = ROW_TILE
    nt = t // tm
    weights = [wp['wr_hi'], wp['wr_lo'], wp['rbias']]
    tile = pl.BlockSpec((1, TOP_K, tm), lambda i: (i, 0, 0))
    return pl.pallas_call(
        _router_kernel,
        grid=(nt,),
        in_specs=[pl.BlockSpec((tm, D_MODEL), lambda i: (i, 0))] + [_full(w) for w in weights],
        out_specs=[tile, tile, pl.BlockSpec((TOP_K, tm), lambda i: (0, i)),
                   pl.BlockSpec((N_EXPERTS, LANES), lambda i: (0, 0))],
        out_shape=[jax.ShapeDtypeStruct((nt, TOP_K, tm), jnp.int32), jax.ShapeDtypeStruct((nt, TOP_K, tm), jnp.int32),
                   jax.ShapeDtypeStruct((TOP_K, t), F32), jax.ShapeDtypeStruct((N_EXPERTS, LANES), F32)],
        compiler_params=_params("arbitrary"),
        name="router",
    )(x1, *weights)


def _pos_kernel(idx_ref, rank_ref, pq_ref, pos_ref):
    tm = idx_ref.shape[2]
    erow = lax.broadcasted_iota(jnp.int32, (N_EXPERTS, tm), 0)
    for k in range(TOP_K):
        onehot = jnp.where(erow == idx_ref[0, k:k + 1, :], 1.0, 0.0).astype(BF16)
        q = _dot(pq_ref[...], onehot)
        blk = (q[0:1, :] * 32.0 + q[1:2, :]).astype(jnp.int32)
        pos_ref[k:k + 1, :] = blk * EXPERT_BLOCK + rank_ref[0, k:k + 1, :]


def _positions(idx3, rank3, pstart):
    nt, _, tm = idx3.shape
    blk = pstart // EXPERT_BLOCK
    pq = jnp.zeros((8, N_EXPERTS), F32).at[0].set((blk // 32).astype(F32)).at[1].set((blk % 32).astype(F32))
    tile = pl.BlockSpec((1, TOP_K, tm), lambda i: (i, 0, 0))
    return pl.pallas_call(
        _pos_kernel,
        grid=(nt,),
        in_specs=[tile, tile, pl.BlockSpec((8, N_EXPERTS), lambda i: (0, 0))],
        out_specs=pl.BlockSpec((TOP_K, tm), lambda i: (0, i)),
        out_shape=jax.ShapeDtypeStruct((TOP_K, nt * tm), jnp.int32),
        compiler_params=_params("parallel"),
        name="moe_positions",
    )(idx3, rank3, pq.astype(BF16))


def _sc_mesh():
    return plsc.VectorSubcoreMesh(core_axis_name="c", subcore_axis_name="s", num_cores=SC_CORES,
                                  num_subcores=SC_SUBCORES)


def _sc_worker():
    return lax.axis_index("s") * SC_CORES + lax.axis_index("c")


def _sc_scatter_kernel(x_hbm, pos_hbm, out_hbm, idx_v, rows_v):
    nchunk = idx_v.shape[1]
    w = _sc_worker()
    pltpu.sync_copy(pos_hbm.at[w], idx_v)

    @pl.loop(0, nchunk)
    def _(c):
        pltpu.sync_copy(x_hbm.at[pl.ds((w * nchunk + c) * SC_CHUNK, SC_CHUNK)], rows_v)
        for k in range(TOP_K):
            pltpu.sync_copy(rows_v, out_hbm.at[idx_v.at[k, c]])


def _sc_dispatch(x1p, pos_kt, n_rows):
    t = x1p.shape[0]
    workers = SC_CORES * SC_SUBCORES
    nchunk = t // (workers * SC_CHUNK)
    assert t == workers * nchunk * SC_CHUNK
    pos4 = pos_kt.reshape(TOP_K, workers, nchunk, SC_CHUNK).transpose(1, 0, 2, 3)
    return pl.kernel(
        _sc_scatter_kernel,
        out_type=jax.ShapeDtypeStruct((n_rows, HALF), U32),
        mesh=_sc_mesh(),
        scratch_types=[pltpu.VMEM((TOP_K, nchunk, SC_CHUNK), jnp.int32), pltpu.VMEM((SC_CHUNK, HALF), U32)],
        name="moe_dispatch_sc",
    )(x1p, pos4)


def _expert_kernel(layer, be_ref, nused_ref, nvalid_ref, first_ref, slot_ref, ahead_ref, head_ref,
                   x_ref, wg_hbm, wu_hbm, wd_hbm, y_ref, wg_f, wu_f, wd_f, wg_b, wu_b, wd_b, sem):
    def fetch(e, slot):
        return [pltpu.make_async_copy(src.at[layer, e], dst.at[slot], sem.at[slot, n])
                for n, (src, dst) in enumerate(((wg_hbm, wg_f), (wu_hbm, wu_f), (wd_hbm, wd_f)))]

    @pl.when(pl.program_id(0) == 0)
    def _():
        for copy in fetch(head_ref[0], 0):
            copy.start()

        @pl.when(head_ref[1] >= 0)
        def _():
            for copy in fetch(head_ref[1], 1):
                copy.start()

    def one_block(b, rows):
        @pl.when((b < nused_ref[0]) & (first_ref[b] == 1))
        def _():
            slot = slot_ref[b]
            for copy in fetch(be_ref[b], slot):
                copy.wait()
            wg_b[...] = wg_f[slot].astype(BF16)
            wu_b[...] = wu_f[slot].astype(BF16)
            wd_b[...] = wd_f[slot].astype(BF16)

            @pl.when(ahead_ref[b] >= 0)
            def _():
                for copy in fetch(ahead_ref[b], slot):
                    copy.start()

        @pl.when(b < nused_ref[0])
        def _():
            live = lax.broadcasted_iota(jnp.int32, (EXPERT_BLOCK, HALF), 0) < nvalid_ref[b]
            xlo, xhi = (h.astype(BF16) for h in _unpack_halves(jnp.where(live, x_ref[rows, :], jnp.uint32(0))))
            g = _dot(xlo, wg_b[:HALF, :]) + _dot(xhi, wg_b[HALF:, :])
            u = _dot(xlo, wu_b[:HALF, :]) + _dot(xhi, wu_b[HALF:, :])
            a = (g * _sigmoid(g) * u).astype(BF16)
            y_ref[rows, :] = _pack_halves(_dot(a, wd_b[...]))

        @pl.when(b >= nused_ref[0])
        def _():
            y_ref[rows, :] = jnp.zeros((EXPERT_BLOCK, HALF), U32)

    for sub in range(EXPERT_STEP):
        one_block(pl.program_id(0) * EXPERT_STEP + sub, slice(sub * EXPERT_BLOCK, (sub + 1) * EXPERT_BLOCK))


def _experts(xs, block_e, nused, nvalid, counts, w_gate, w_up, w_down, layer, n_blocks):
    rows = EXPERT_BLOCK
    blocks = jnp.arange(n_blocks, dtype=jnp.int32)
    first = ((blocks == 0) | (block_e != jnp.roll(block_e, 1))) & (blocks < nused[0])
    run = jnp.cumsum(first.astype(jnp.int32)) - 1
    run_e = jnp.nonzero(counts > 0, size=N_EXPERTS, fill_value=-1)[0].astype(jnp.int32)
    ahead = jnp.concatenate([run_e, jnp.full((2,), -1, jnp.int32)])[jnp.clip(run, 0, N_EXPERTS - 1) + 2]
    step_rows = EXPERT_STEP * rows
    grid_spec = pltpu.PrefetchScalarGridSpec(
        num_scalar_prefetch=7,
        grid=(n_blocks // EXPERT_STEP,),
        in_specs=[
            pl.BlockSpec((step_rows, HALF), lambda s, be, nu, *_: (jnp.minimum(s, (nu[0] - 1) // EXPERT_STEP), 0)),
            pl.BlockSpec(memory_space=pl.ANY), pl.BlockSpec(memory_space=pl.ANY), pl.BlockSpec(memory_space=pl.ANY),
        ],
        out_specs=pl.BlockSpec((step_rows, HALF), lambda s, *_: (s, 0)),
        scratch_shapes=[
            pltpu.VMEM((2, D_MODEL, D_EXPERT), F32), pltpu.VMEM((2, D_MODEL, D_EXPERT), F32),
            pltpu.VMEM((2, D_EXPERT, D_MODEL), F32),
            pltpu.VMEM((D_MODEL, D_EXPERT), BF16), pltpu.VMEM((D_MODEL, D_EXPERT), BF16),
            pltpu.VMEM((D_EXPERT, D_MODEL), BF16),
            pltpu.SemaphoreType.DMA((2, 3)),
        ],
    )
    return pl.pallas_call(
        functools.partial(_expert_kernel, layer),
        grid_spec=grid_spec,
        out_shape=jax.ShapeDtypeStruct((n_blocks * rows, HALF), U32),
        compiler_params=_params("arbitrary"),
        name="moe_experts",
    )(block_e, nused, nvalid, first.astype(jnp.int32), (run % 2).astype(jnp.int32), ahead, run_e[:2],
      xs, w_gate, w_up, w_down)


def _sc_gather_kernel(table_hbm, idx_hbm, out_hbm, idx_v, rows_v, gsem, wsem):
    per_worker = idx_v.shape[0]
    base = _sc_worker() * per_worker
    pltpu.sync_copy(idx_hbm.at[pl.ds(base, per_worker)], idx_v)

    @pl.loop(0, per_worker // SC_CHUNK, step=SC_BUFS)
    def _(c):
        offs = [(c + n) * SC_CHUNK for n in range(SC_BUFS)]
        gathers = [pltpu.async_copy(table_hbm.at[idx_v.at[pl.ds(offs[n], SC_CHUNK)]], rows_v.at[n], gsem.at[n])
                   for n in range(SC_BUFS)]
        writes = []
        for n in range(SC_BUFS):
            gathers[n].wait()
            writes.append(pltpu.async_copy(rows_v.at[n], out_hbm.at[pl.ds(base + offs[n], SC_CHUNK)], wsem.at[n]))
        for write in writes:
            write.wait()


def _sc_gather_rows(table, idx):
    n = idx.shape[0]
    workers = SC_CORES * SC_SUBCORES
    assert n % (workers * SC_CHUNK * SC_BUFS) == 0
    return pl.kernel(
        _sc_gather_kernel,
        out_type=jax.ShapeDtypeStruct((n, HALF), U32),
        mesh=_sc_mesh(),
        scratch_types=[pltpu.VMEM((n // workers,), jnp.int32), pltpu.VMEM((SC_BUFS, SC_CHUNK, HALF), U32),
                       pltpu.SemaphoreType.DMA((SC_BUFS,)), pltpu.SemaphoreType.DMA((SC_BUFS,))],
        name="moe_gather_sc",
    )(table, idx)


def _shared_kernel(x_ref, wsg_ref, wsu_ref, wsd_ref, o_ref):
    x = x_ref[...]
    xb = x.astype(BF16)
    g = _dot(xb, wsg_ref[...])
    u = _dot(xb, wsu_ref[...])
    o_ref[...] = ALPHA * x + _dot((g * _sigmoid(g) * u).astype(BF16), wsd_ref[...])


def _shared(x1, wp):
    t = x1.shape[0]
    tm = ROW_TILE
    row = pl.BlockSpec((tm, D_MODEL), lambda i: (i, 0))
    weights = [wp['wsg'], wp['wsu'], wp['wsd']]
    return pl.pallas_call(
        _shared_kernel,
        grid=(t // tm,),
        in_specs=[row] + [_full(w) for w in weights],
        out_specs=row,
        out_shape=jax.ShapeDtypeStruct((t, D_MODEL), F32),
        compiler_params=_params("parallel"),
        name="moe_shared",
    )(x1, *weights)


def _combine_kernel(base_ref, w_ref, *refs):
    y_refs, (g_ref, b_ref, o_ref) = refs[:TOP_K], refs[TOP_K:]
    w = w_ref[...]
    lo, hi = (h * w[:, 0:1] for h in _unpack_halves(y_refs[0][...]))
    for k in range(1, TOP_K):
        lo_k, hi_k = _unpack_halves(y_refs[k][...])
        lo = lo + lo_k * w[:, k:k + 1]
        hi = hi + hi_k * w[:, k:k + 1]
    routed = jnp.concatenate([lo, hi], axis=1)
    o_ref[...] = _layer_norm(base_ref[...] + routed, g_ref[...], b_ref[...])


def _combine(base, y8, w_tk, wp):
    t = base.shape[0]
    tm = ROW_TILE
    nt = t // tm
    weights = [wp['ln2_g'], wp['ln2_b']]
    y_specs = [pl.BlockSpec((tm, HALF), lambda i, k=k: (k * nt + i, 0)) for k in range(TOP_K)]
    return pl.pallas_call(
        _combine_kernel,
        grid=(nt,),
        in_specs=[pl.BlockSpec((tm, D_MODEL), lambda i: (i, 0)), pl.BlockSpec((tm, TOP_K), lambda i: (i, 0))]
                 + y_specs + [_full(w) for w in weights],
        out_specs=pl.BlockSpec((tm, D_MODEL), lambda i: (i, 0)),
        out_shape=jax.ShapeDtypeStruct((t, D_MODEL), F32),
        compiler_params=_params("parallel"),
        name="moe_combine",
    )(base, w_tk, *([y8] * TOP_K), *weights)


def _moe(x1, x1p, wp, w_e_gate, w_e_up, w_e_down, layer):
    t = x1.shape[0]
    rows = EXPERT_BLOCK
    n_blocks = -(-(t * TOP_K + N_EXPERTS * (rows - 1)) // (rows * EXPERT_STEP)) * EXPERT_STEP
    idx3, rank3, top_w, cnt = _router(x1, wp)
    counts = cnt[:, 0].astype(jnp.int32)
    padded = (counts + rows - 1) // rows * rows
    padded_end = jnp.cumsum(padded)
    pstart = (padded_end - padded).astype(jnp.int32)
    nused = (padded_end[-1] // rows).astype(jnp.int32).reshape(1)
    blocks = jnp.arange(n_blocks, dtype=jnp.int32)
    block_row = jnp.minimum(blocks, nused[0] - 1) * rows
    block_e = jnp.sum((padded_end[None, :] <= block_row[:, None]).astype(jnp.int32), axis=1)
    block_e = jnp.minimum(block_e, N_EXPERTS - 1)
    nvalid = jnp.clip(counts[block_e] - (blocks * rows - pstart[block_e]), 0, rows).astype(jnp.int32)
    pos_kt = _positions(idx3, rank3, pstart)
    xs = _sc_dispatch(x1p, pos_kt, n_blocks * rows)
    ys = _experts(xs, block_e, nused, nvalid, counts, w_e_gate, w_e_up, w_e_down, layer, n_blocks)
    y8 = _sc_gather_rows(ys, pos_kt.reshape(TOP_K * t))
    return _combine(_shared(x1, wp), y8, top_w.T, wp)


def _head_groups_t(w, used):
    k, h, _ = w.shape
    return jnp.pad(w, ((0, 0), (0, 0), (0, LANES - used))).reshape(k, h * LANES).T


def _prep_layer(l, w_in, b_gate, q_norm, w_uq, kv_norm, w_ukv, w_proj_a, w_proj_b, w_out, ln1_g, ln1_b,
                w_router, router_bias, w_s_gate, w_s_up, w_s_down, ln2_g, ln2_b):
    w = w_in[l]
    o = 0
    cols = {}
    for name, width in (('qa', WIDTH_A), ('ka', WIDTH_A), ('va', WIDTH_A), ('cq', Q_LORA_RANK),
                        ('ckv', KV_LORA_RANK), ('kr', QK_ROPE_DIM), ('ga', D_MODEL), ('gb', D_MODEL)):
        cols[name] = w[:, o:o + width]
        o += width
    wkr = jnp.zeros((D_MODEL, LANES), F32).at[:, QK_NOPE_DIM:QK_NOPE_DIM + QK_ROPE_DIM].set(cols['kr'])
    dqk = QK_NOPE_DIM + QK_ROPE_DIM
    wq = w_uq[l].reshape(Q_LORA_RANK, N_HEADS_B, dqk) * (dqk ** -0.5 * LOG2E)
    wkv = w_ukv[l].reshape(KV_LORA_RANK, N_HEADS_B, QK_NOPE_DIM + V_HEAD_DIM)
    wuk = jnp.pad(wkv[:, :, :QK_NOPE_DIM], ((0, 0), (0, 0), (0, LANES - QK_NOPE_DIM))).reshape(KV_LORA_RANK, N_HEADS_B * LANES)
    ones = jnp.zeros((N_HEADS_B, LANES), F32).at[:, V_HEAD_DIM].set(1.0).reshape(N_HEADS_B * LANES, 1)
    wr_t = w_router[l].T
    wr_hi = wr_t.astype(BF16)
    return dict(
        wqt=(cols['qa'] * HEAD_DIM_A ** -0.5).T.astype(BF16), wk=cols['ka'].astype(BF16),
        wvt=_head_groups_t(cols['va'].reshape(D_MODEL, N_HEADS_A, HEAD_DIM_A), HEAD_DIM_A).astype(BF16), ones=ones,
        wcq=cols['cq'].astype(BF16), wckv=cols['ckv'].astype(BF16), wkr=wkr.astype(BF16),
        wg=jnp.concatenate([cols['ga'], cols['gb']], axis=1).astype(BF16),
        bg=b_gate[l].reshape(1, 2 * D_MODEL), qn=q_norm[l].reshape(1, Q_LORA_RANK), kvn=kv_norm[l].reshape(1, KV_LORA_RANK),
        wuqt=_head_groups_t(wq, dqk).astype(BF16), wuk=wuk.astype(BF16),
        wuvt=_head_groups_t(wkv[:, :, QK_NOPE_DIM:], V_HEAD_DIM).astype(BF16),
        wpa=w_proj_a[l].astype(BF16), wpb=w_proj_b[l].astype(BF16), wo=w_out[l].astype(BF16),
        ln1_g=ln1_g[l].reshape(1, D_MODEL), ln1_b=ln1_b[l].reshape(1, D_MODEL),
        wr_hi=wr_hi, wr_lo=(wr_t - wr_hi.astype(F32)).astype(BF16), rbias=router_bias[l].reshape(N_EXPERTS, 1),
        wsg=w_s_gate[l].astype(BF16), wsu=w_s_up[l].astype(BF16), wsd=w_s_down[l].astype(BF16),
        ln2_g=ln2_g[l].reshape(1, D_MODEL), ln2_b=ln2_b[l].reshape(1, D_MODEL),
    )


def _rope_tables(seq):
    pos = jnp.arange(seq, dtype=F32)
    inv_freq = ROPE_THETA ** (-jnp.arange(0, QK_ROPE_DIM, 2, dtype=F32) / QK_ROPE_DIM)
    ang = pos[:, None] * inv_freq[None, :]
    cos, sin = jnp.cos(ang), jnp.sin(ang)
    half = QK_ROPE_DIM // 2
    z = lambda n: jnp.zeros((seq, n), F32)
    c = jnp.concatenate([jnp.ones((seq, QK_NOPE_DIM), F32), cos, cos, z(LANES - QK_NOPE_DIM - QK_ROPE_DIM)], axis=1)
    s1 = jnp.concatenate([z(QK_NOPE_DIM), -sin, z(LANES - QK_NOPE_DIM - half)], axis=1)
    s2 = jnp.concatenate([z(QK_NOPE_DIM + half), sin, z(LANES - QK_NOPE_DIM - QK_ROPE_DIM)], axis=1)
    return c, s1, s2, c.T, s1.T, s2.T


def _moba_key_table(seq, slopes):
    blk = jnp.arange(seq, dtype=jnp.int32) // MOBA_BLOCK
    onehot = (blk[:, None] == jnp.arange(BIAS_LO, dtype=jnp.int32)[None, :]).astype(F32)
    inblk = (jnp.arange(seq, dtype=jnp.int32) % MOBA_BLOCK).astype(F32)[:, None] * (slopes * LOG2E)[None, :]
    hi = inblk.astype(BF16)
    lo = (inblk - hi.astype(F32)).astype(BF16)
    pad = jnp.zeros((seq, LANES - POS_LO - N_HEADS_A), BF16)
    return jnp.concatenate([onehot.astype(BF16), onehot.astype(BF16), hi, lo, pad], axis=1)


def kernel(x, w_in, b_gate, q_norm, w_uq, kv_norm, w_ukv, w_proj_a, w_proj_b, w_out, ln1_g, ln1_b, w_router, router_bias, w_e_gate, w_e_up, w_e_down, w_s_gate, w_s_up, w_s_down, ln2_g, ln2_b):
    batch, seq, d = x.shape
    assert d == D_MODEL and seq % MOBA_BLOCK == 0 and MOBA_TOPK <= seq // MOBA_BLOCK <= BIAS_LO
    assert POS_LO + N_HEADS_A <= LANES and POS_HI + N_HEADS_A <= POS_LO
    tabs = _rope_tables(seq)
    slopes = jnp.asarray(np.exp2(-8.0 * (np.arange(N_HEADS_A) + 1.0) / N_HEADS_A), F32)
    etab = _moba_key_table(seq, slopes)
    h = x.reshape(batch * seq, d)
    for l in range(DEPTH):
        wp = _prep_layer(l, w_in, b_gate, q_norm, w_uq, kv_norm, w_ukv, w_proj_a, w_proj_b, w_out, ln1_g, ln1_b,
                         w_router, router_bias, w_s_gate, w_s_up, w_s_down, ln2_g, ln2_b)
        qat, ka, vat, kmean, qmt, km, vmt, sa, sb = _inproj(h, wp, tabs, seq)
        oa = _moba(qat, ka, vat, kmean, etab, slopes, batch, seq)
        ob = _mla(qmt, km, vmt, batch, seq)
        x1, x1p = _merge(h, oa, ob, sa, sb, wp)
        h = _moe(x1, x1p, wp, w_e_gate, w_e_up, w_e_down, l)
    return h.reshape(batch, seq, d)
```

```python
import functools

import numpy as np

import jax
import jax.numpy as jnp
from jax import lax
from jax.experimental import pallas as pl
from jax.experimental.pallas import tpu as pltpu
from jax.experimental.pallas import tpu_sc as plsc

D_MODEL = 1024
N_HEADS_A = 8
HEAD_DIM_A = 64
WIDTH_A = N_HEADS_A * HEAD_DIM_A
MOBA_BLOCK = 256
MOBA_TOPK = 3
N_HEADS_B = 8
QK_NOPE_DIM = 64
QK_ROPE_DIM = 32
V_HEAD_DIM = 64
Q_LORA_RANK = 384
KV_LORA_RANK = 256
WIDTH_B = N_HEADS_B * V_HEAD_DIM
ROPE_THETA = 10000.0
N_EXPERTS = 256
TOP_K = 8
N_GROUPS = 8
TOPK_GROUPS = 4
GROUP_SIZE = N_EXPERTS // N_GROUPS
D_EXPERT = 256
D_SHARED = 256
ROUTED_SCALE = 2.5
DEPTH = 2
ALPHA = (2 * DEPTH) ** 0.25
LN_EPS = 1e-5
RMS_EPS = 1e-6

LANES = 128
NEG = -1e30
LOG2E = float(np.log2(np.e))
ROW_TILE = 256
WIDE_TILE = 512
ATT_TILE = 256
EXPERT_BLOCK = 256
EXPERT_STEP = 4
SC_CORES = 2
SC_SUBCORES = 16
SC_CHUNK = 64
SC_BUFS = 2
VMEM_LIMIT = 56 * 1024 * 1024
HALF = D_MODEL // 2
ATT_HEADS = 4
ATT_GROUP = 4
PV_ROWS = 80
BIAS_HI, BIAS_LO, POS_HI, POS_LO = 0, 32, 64, 72

BF16 = jnp.bfloat16
F32 = jnp.float32
U32 = jnp.uint32


def _dot(a, b):
    return jnp.dot(a, b, preferred_element_type=F32)


def _dot_nt(a, b):
    return lax.dot_general(a, b, (((1,), (1,)), ((), ())), preferred_element_type=F32)


def _sigmoid(x):
    return 1.0 / (1.0 + jnp.exp(-x))


def _layer_norm(y, g, b):
    mu = jnp.mean(y, axis=-1, keepdims=True)
    d = y - mu
    var = jnp.mean(d * d, axis=-1, keepdims=True)
    return d * lax.rsqrt(var + LN_EPS) * g + b


def _params(*sem):
    return pltpu.CompilerParams(dimension_semantics=sem, vmem_limit_bytes=VMEM_LIMIT)


def _full(a):
    return pl.BlockSpec(a.shape, lambda *_: (0,) * a.ndim)


def _inproj_kernel(x_ref, wqt_ref, wk_ref, wvt_ref, ones_ref, wcq_ref, wckv_ref, wkr_ref, wg_ref, bg_ref,
                   qn_ref, kvn_ref, wuqt_ref, wuk_ref, wuvt_ref, cos_ref, s1_ref, s2_ref, cost_ref, s1t_ref, s2t_ref,
                   qat_ref, ka_ref, vat_ref, kmean_ref, qmt_ref, km_ref, vmt_ref, sa_ref, sb_ref):
    xb = x_ref[...].astype(BF16)
    half = QK_ROPE_DIM // 2
    qat_ref[0] = _dot_nt(wqt_ref[...], xb).astype(BF16)
    k = _dot(xb, wk_ref[...])
    ka_ref[...] = k.astype(BF16)
    kmean_ref[0] = jnp.mean(k, axis=0, keepdims=True)
    vat_ref[0] = (_dot_nt(wvt_ref[...], xb) + ones_ref[...]).astype(BF16)

    cq = _dot(xb, wcq_ref[...])
    cqn = (cq * lax.rsqrt(jnp.mean(cq * cq, axis=-1, keepdims=True) + RMS_EPS) * qn_ref[...]).astype(BF16)
    ckv = _dot(xb, wckv_ref[...])
    ckvn = (ckv * lax.rsqrt(jnp.mean(ckv * ckv, axis=-1, keepdims=True) + RMS_EPS) * kvn_ref[...]).astype(BF16)
    qt = _dot_nt(wuqt_ref[...], cqn)
    ct, s1t, s2t = cost_ref[...], s1t_ref[...], s2t_ref[...]
    for h in range(N_HEADS_B):
        t = qt[h * LANES:(h + 1) * LANES, :]
        rot = t * ct + pltpu.roll(t, LANES - half, 0) * s1t + pltpu.roll(t, half, 0) * s2t
        qmt_ref[0, h * LANES:(h + 1) * LANES, :] = rot.astype(BF16)
    kn = _dot(ckvn, wuk_ref[...])
    kr = _dot(xb, wkr_ref[...])
    c, s1, s2 = cos_ref[...], s1_ref[...], s2_ref[...]
    krot = kr * c + pltpu.roll(kr, LANES - half, 1) * s1 + pltpu.roll(kr, half, 1) * s2
    for h in range(N_HEADS_B):
        sl = slice(h * LANES, (h + 1) * LANES)
        km_ref[:, sl] = (kn[:, sl] + krot).astype(BF16)
    vmt_ref[0] = (_dot_nt(wuvt_ref[...], ckvn) + ones_ref[...]).astype(BF16)

    sig = _sigmoid(_dot(xb, wg_ref[...]) + bg_ref[...])
    sa_ref[...] = sig[:, :D_MODEL].astype(BF16)
    sb_ref[...] = sig[:, D_MODEL:].astype(BF16)


def _inproj(x, wp, tabs, seq):
    t = x.shape[0]
    tm = ROW_TILE
    nt = t // tm
    npos = seq // tm
    row = lambda w: pl.BlockSpec((tm, w), lambda i: (i, 0))
    tile = lambda r: pl.BlockSpec((1, r, tm), lambda i: (i, 0, 0))
    tab = pl.BlockSpec((tm, LANES), lambda i: (i % npos, 0))
    tabt = pl.BlockSpec((LANES, tm), lambda i: (0, i % npos))
    weights = [wp['wqt'], wp['wk'], wp['wvt'], wp['ones'], wp['wcq'], wp['wckv'], wp['wkr'], wp['wg'], wp['bg'],
               wp['qn'], wp['kvn'], wp['wuqt'], wp['wuk'], wp['wuvt']]
    hl = N_HEADS_B * LANES
    out_shape = [
        jax.ShapeDtypeStruct((nt, WIDTH_A, tm), BF16), jax.ShapeDtypeStruct((t, WIDTH_A), BF16),
        jax.ShapeDtypeStruct((nt, N_HEADS_A * LANES, tm), BF16), jax.ShapeDtypeStruct((nt, 1, WIDTH_A), F32),
        jax.ShapeDtypeStruct((nt, hl, tm), BF16), jax.ShapeDtypeStruct((t, hl), BF16),
        jax.ShapeDtypeStruct((nt, hl, tm), BF16),
        jax.ShapeDtypeStruct((t, D_MODEL), BF16), jax.ShapeDtypeStruct((t, D_MODEL), BF16),
    ]
    out_specs = [tile(WIDTH_A), row(WIDTH_A), tile(N_HEADS_A * LANES),
                 pl.BlockSpec((1, 1, WIDTH_A), lambda i: (i, 0, 0)),
                 tile(hl), row(hl), tile(hl), row(D_MODEL), row(D_MODEL)]
    return pl.pallas_call(
        _inproj_kernel,
        grid=(nt,),
        in_specs=[row(D_MODEL)] + [_full(w) for w in weights] + [tab, tab, tab, tabt, tabt, tabt],
        out_specs=out_specs,
        out_shape=out_shape,
        compiler_params=_params("parallel"),
        name="inproj",
    )(x, *weights, *tabs)


def _attend_init(tq):
    return jnp.full((1, tq), -jnp.inf, F32), jnp.zeros((PV_ROWS, tq), F32)


def _col_max(s_ref):
    return [jnp.max(s_ref[hh], axis=0, keepdims=True) for hh in range(ATT_HEADS)]


def _attend_staged(cur_ref, cur_max, state, vts, nxt_ref=None, next_scores=None):
    heads = range(ATT_HEADS)
    if nxt_ref is not None:
        for hh in heads:
            nxt_ref[hh] = next_scores(hh)
    new_m, scaled, pvs = [], [], []
    for hh in heads:
        m_i, acc = state[hh]
        m_new = jnp.maximum(m_i, cur_max[hh])
        new_m.append(m_new)
        scaled.append(jnp.exp2(m_i - m_new) * acc)
        pvs.append(_dot(vts[hh], jnp.exp2(cur_ref[hh] - m_new).astype(BF16)))
    nxt_max = _col_max(nxt_ref) if nxt_ref is not None else cur_max
    return tuple((m, a + pv) for m, a, pv in zip(new_m, scaled, pvs)), nxt_max


def _attention_loop(i, scores, values, causal, tq, sa_ref, sb_ref):
    heads = range(ATT_HEADS)
    bufs = (sa_ref, sb_ref)

    def vals(s):
        tile = jnp.where(s == 0, i, s - 1)
        return [values(hh, tile) for hh in heads]

    def ahead(s):
        return lambda hh: scores(hh, jnp.maximum(jnp.minimum(s, i - 1), 0))

    for hh in heads:
        sa_ref[hh] = jnp.where(causal, scores(hh, i), NEG)
    carry = (tuple(_attend_init(tq) for _ in heads), _col_max(sa_ref))
    nslots = i + 1

    def trip(n, c):
        for g in range(ATT_GROUP):
            s = n * ATT_GROUP + g
            c = _attend_staged(bufs[g % 2], c[1], c[0], vals(s), bufs[(g + 1) % 2], ahead(s))
        return c

    carry = lax.fori_loop(0, nslots // ATT_GROUP, trip, carry)
    done = nslots // ATT_GROUP * ATT_GROUP
    for g in range(ATT_GROUP - 1):
        def step(c, g=g):
            if g == ATT_GROUP - 2:
                return _attend_staged(bufs[g % 2], c[1], c[0], vals(done + g))
            return _attend_staged(bufs[g % 2], c[1], c[0], vals(done + g), bufs[(g + 1) % 2], ahead(done + g))

        carry = lax.cond(nslots - done > g, step, lambda c: c, carry)
    outs = []
    for hh in heads:
        _, acc = carry[0][hh]
        outs.append(acc[:V_HEAD_DIM, :] / acc[V_HEAD_DIM:V_HEAD_DIM + 1, :])
    return jnp.concatenate(outs, axis=0).T


def _causal_t(tq):
    key = lax.broadcasted_iota(jnp.int32, (tq, tq), 0)
    qry = lax.broadcasted_iota(jnp.int32, (tq, tq), 1)
    return key <= qry


def _moba_kernel(slope_ref, qt_ref, k_ref, e_ref, vt_ref, km_ref, o_ref, sa_ref, sb_ref):
    tq = ATT_TILE
    hp = pl.program_id(1)
    i = pl.program_id(2)
    nblk = km_ref.shape[1]
    frow = lax.broadcasted_iota(jnp.int32, (LANES, tq), 0)
    row = lax.broadcasted_iota(jnp.int32, (BIAS_LO, tq), 0)
    prow = lax.broadcasted_iota(jnp.int32, (LANES - POS_HI, tq), 0) + POS_HI

    qaug = []
    for hh in range(ATT_HEADS):
        head = hp * ATT_HEADS + hh
        grp = slice(hh // 2 * LANES, (hh // 2 + 1) * LANES)
        qt2 = qt_ref[0, grp, :]
        head_rows = (frow >= hh % 2 * HEAD_DIM_A) & (frow < (hh % 2 + 1) * HEAD_DIM_A)
        qh = jnp.where(head_rows, qt2, jnp.zeros_like(qt2))
        kmean = km_ref[0, :, grp].astype(BF16)
        if nblk < BIAS_LO:
            kmean = jnp.concatenate([kmean, jnp.zeros((BIAS_LO - nblk, LANES), BF16)], axis=0)
        g = jnp.where(row < i, _dot(kmean, qh), -jnp.inf)
        picked = row == i
        for r in range(MOBA_TOPK):
            m = jnp.max(g, axis=0, keepdims=True)
            idx = jnp.min(jnp.where(g == m, row, BIAS_LO), axis=0, keepdims=True)
            hit = row == idx
            picked = picked | (hit & (r < i))
            g = jnp.where(hit, -jnp.inf, g)
        dist = (i - row).astype(F32) * (slope_ref[head] * (MOBA_BLOCK * LOG2E))
        b = jnp.where(picked, -dist, NEG)
        b_hi = b.astype(BF16)
        b_lo = (b - b_hi.astype(F32)).astype(BF16)
        ones = jnp.where((prow == POS_HI + head) | (prow == POS_LO + head), 1.0, 0.0).astype(BF16)
        qs = (qh.astype(F32) * LOG2E).astype(BF16)
        qaug.append(jnp.concatenate([qs, b_hi, b_lo, ones], axis=0))

    def scores(hh, j):
        rows = pl.ds(pl.multiple_of(j * tq, tq), tq)
        grp = slice(hh // 2 * LANES, (hh // 2 + 1) * LANES)
        return _dot(jnp.concatenate([k_ref[rows, grp], e_ref[rows, :]], axis=1), qaug[hh])

    def values(hh, j):
        return vt_ref[j, hh * LANES:hh * LANES + PV_ROWS, :]

    o_ref[...] = _attention_loop(i, scores, values, _causal_t(tq), tq, sa_ref, sb_ref).astype(BF16)


def _moba(qat, ka, vat, kmean, etab, slopes, batch, seq):
    t = ka.shape[0]
    tq = ATT_TILE
    nq = seq // tq
    nblk = seq // MOBA_BLOCK
    nh = ATT_HEADS
    km = kmean.reshape(batch, nblk, WIDTH_A)
    return pl.pallas_call(
        _moba_kernel,
        grid=(batch, N_HEADS_A // nh, nq),
        in_specs=[
            pl.BlockSpec(memory_space=pltpu.SMEM),
            pl.BlockSpec((1, nh * HEAD_DIM_A, tq), lambda b, h, i: (b * nq + i, h, 0)),
            pl.BlockSpec((seq, nh * HEAD_DIM_A), lambda b, h, i: (b, h)),
            pl.BlockSpec((seq, LANES), lambda b, h, i: (0, 0)),
            pl.BlockSpec((nq, nh * LANES, tq), lambda b, h, i: (b, h, 0)),
            pl.BlockSpec((1, nblk, nh * HEAD_DIM_A), lambda b, h, i: (b, 0, h)),
        ],
        out_specs=pl.BlockSpec((tq, nh * HEAD_DIM_A), lambda b, h, i: (b * nq + i, h)),
        out_shape=jax.ShapeDtypeStruct((t, WIDTH_A), BF16),
        scratch_shapes=[pltpu.VMEM((nh, tq, tq), F32), pltpu.VMEM((nh, tq, tq), F32)],
        compiler_params=_params("parallel", "parallel", "arbitrary"),
        name="moba",
    )(slopes, qat, ka, etab, vat, km)


def _mla_kernel(qt_ref, k_ref, vt_ref, o_ref, sa_ref, sb_ref):
    tq = ATT_TILE
    i = pl.program_id(2)
    qt = [qt_ref[0, hh * LANES:(hh + 1) * LANES, :] for hh in range(ATT_HEADS)]

    def scores(hh, j):
        return _dot(k_ref[pl.ds(pl.multiple_of(j * tq, tq), tq), hh * LANES:(hh + 1) * LANES], qt[hh])

    def values(hh, j):
        return vt_ref[j, hh * LANES:hh * LANES + PV_ROWS, :]

    o_ref[...] = _attention_loop(i, scores, values, _causal_t(tq), tq, sa_ref, sb_ref).astype(BF16)


def _mla(qmt, km, vmt, batch, seq):
    t = km.shape[0]
    tq = ATT_TILE
    nq = seq // tq
    nh = ATT_HEADS
    return pl.pallas_call(
        _mla_kernel,
        grid=(batch, N_HEADS_B // nh, nq),
        in_specs=[
            pl.BlockSpec((1, nh * LANES, tq), lambda b, h, i: (b * nq + i, h, 0)),
            pl.BlockSpec((seq, nh * LANES), lambda b, h, i: (b, h)),
            pl.BlockSpec((nq, nh * LANES, tq), lambda b, h, i: (b, h, 0)),
        ],
        out_specs=pl.BlockSpec((tq, nh * V_HEAD_DIM), lambda b, h, i: (b * nq + i, h)),
        out_shape=jax.ShapeDtypeStruct((t, WIDTH_B), BF16),
        scratch_shapes=[pltpu.VMEM((nh, tq, tq), F32), pltpu.VMEM((nh, tq, tq), F32)],
        compiler_params=_params("parallel", "parallel", "arbitrary"),
        name="mla",
    )(qmt, km, vmt)


def _pack_halves(y):
    lo = pltpu.bitcast(y[:, :HALF].astype(BF16).astype(F32), U32)
    hi = pltpu.bitcast(y[:, HALF:].astype(BF16).astype(F32), U32)
    return (hi & jnp.uint32(0xFFFF0000)) | (lo >> 16)


def _unpack_halves(w):
    return pltpu.bitcast(w << 16, F32), pltpu.bitcast(w & jnp.uint32(0xFFFF0000), F32)


def _merge_kernel(x_ref, oa_ref, ob_ref, sa_ref, sb_ref, wpa_ref, wpb_ref, wo_ref, g_ref, b_ref, o_ref, op_ref):
    pa = _dot(oa_ref[...], wpa_ref[...])
    pb = _dot(ob_ref[...], wpb_ref[...])
    merged = sa_ref[...].astype(F32) * pa + sb_ref[...].astype(F32) * pb
    hmix = _dot(merged.astype(BF16), wo_ref[...])
    y = _layer_norm(ALPHA * x_ref[...] + hmix, g_ref[...], b_ref[...])
    o_ref[...] = y
    op_ref[...] = _pack_halves(y)


def _merge(x, oa, ob, sa, sb, wp):
    t = x.shape[0]
    tm = WIDE_TILE
    row = lambda w: pl.BlockSpec((tm, w), lambda i: (i, 0))
    weights = [wp['wpa'], wp['wpb'], wp['wo'], wp['ln1_g'], wp['ln1_b']]
    return pl.pallas_call(
        _merge_kernel,
        grid=(t // tm,),
        in_specs=[row(D_MODEL), row(WIDTH_A), row(WIDTH_B), row(D_MODEL), row(D_MODEL)] + [_full(w) for w in weights],
        out_specs=[row(D_MODEL), row(HALF)],
        out_shape=[jax.ShapeDtypeStruct((t, D_MODEL), F32), jax.ShapeDtypeStruct((t, HALF), U32)],
        compiler_params=_params("parallel"),
        name="merge",
    )(x, oa, ob, sa, sb, *weights)


def _router_kernel(x_ref, wh_ref, wl_ref, rb_ref, idx_ref, rank_ref, w_ref, cnt_ref):
    tm = x_ref.shape[0]

    @pl.when(pl.program_id(0) == 0)
    def _():
        cnt_ref[...] = jnp.zeros_like(cnt_ref)

    x = x_ref[...]
    xh = x.astype(BF16)
    xl = (x - xh.astype(F32)).astype(BF16)
    wh, wl = wh_ref[...], wl_ref[...]
    logits = _dot_nt(wh, xh) + (_dot_nt(wh, xl) + _dot_nt(wl, xh))
    scores = _sigmoid(logits)
    choice = scores + rb_ref[...]
    row = lax.broadcasted_iota(jnp.int32, (GROUP_SIZE, tm), 0)
    groups = [choice[g * GROUP_SIZE:(g + 1) * GROUP_SIZE, :] for g in range(N_GROUPS)]
    gscore = []
    for blk in groups:
        m1 = jnp.max(blk, axis=0, keepdims=True)
        first = jnp.min(jnp.where(blk == m1, row, GROUP_SIZE), axis=0, keepdims=True)
        m2 = jnp.max(jnp.where(row == first, -jnp.inf, blk), axis=0, keepdims=True)
        gscore.append(m1 + m2)
    masked = []
    for g in range(N_GROUPS):
        ahead = jnp.zeros((1, tm), jnp.int32)
        for o in range(N_GROUPS):
            if o < g:
                ahead += (gscore[o] >= gscore[g]).astype(jnp.int32)
            elif o > g:
                ahead += (gscore[o] > gscore[g]).astype(jnp.int32)
        masked.append(jnp.where(ahead < TOPK_GROUPS, groups[g], -jnp.inf))
    cur = jnp.concatenate(masked, axis=0)
    erow = lax.broadcasted_iota(jnp.int32, (N_EXPERTS, tm), 0)
    hits, idxs, ws = [], [], []
    for _ in range(TOP_K):
        m = jnp.max(cur, axis=0, keepdims=True)
        e = jnp.min(jnp.where(cur == m, erow, N_EXPERTS), axis=0, keepdims=True)
        hit = erow == e
        hits.append(hit)
        idxs.append(e)
        ws.append(jnp.sum(jnp.where(hit, scores, 0.0), axis=0, keepdims=True))
        cur = jnp.where(hit, -jnp.inf, cur)
    total = ws[0]
    for w in ws[1:]:
        total = total + w
    member = hits[0]
    for hit in hits[1:]:
        member = member | hit
    member = jnp.where(member, 1.0, 0.0).astype(BF16)
    t_src = lax.broadcasted_iota(jnp.int32, (tm, tm), 0)
    t_dst = lax.broadcasted_iota(jnp.int32, (tm, tm), 1)
    before = _dot(member, jnp.where(t_src < t_dst, 1.0, 0.0).astype(BF16))
    base = cnt_ref[...]
    before = before + jnp.concatenate([base] * (tm // LANES), axis=1)
    cnt_ref[...] = base + _dot(member, jnp.ones((tm, LANES), BF16))
    for r in range(TOP_K):
        idx_ref[0, r:r + 1, :] = idxs[r]
        rank_ref[0, r:r + 1, :] = jnp.sum(jnp.where(hits[r], before, 0.0), axis=0, keepdims=True).astype(jnp.int32)
        w_ref[r:r + 1, :] = ws[r] / total * ROUTED_SCALE


def _router(x1, wp):
    t = x1.shape[0]
    tm = ROW_TILE
    nt = t // tm
    weights = [wp['wr_hi'], wp['wr_lo'], wp['rbias']]
    tile = pl.BlockSpec((1, TOP_K, tm), lambda i: (i, 0, 0))
    return pl.pallas_call(
        _router_kernel,
        grid=(nt,),
        in_specs=[pl.BlockSpec((tm, D_MODEL), lambda i: (i, 0))] + [_full(w) for w in weights],
        out_specs=[tile, tile, pl.BlockSpec((TOP_K, tm), lambda i: (0, i)),
                   pl.BlockSpec((N_EXPERTS, LANES), lambda i: (0, 0))],
        out_shape=[jax.ShapeDtypeStruct((nt, TOP_K, tm), jnp.int32), jax.ShapeDtypeStruct((nt, TOP_K, tm), jnp.int32),
                   jax.ShapeDtypeStruct((TOP_K, t), F32), jax.ShapeDtypeStruct((N_EXPERTS, LANES), F32)],
        compiler_params=_params("arbitrary"),
        name="router",
    )(x1, *weights)


def _pos_kernel(idx_ref, rank_ref, pq_ref, pos_ref):
    tm = idx_ref.shape[2]
    erow = lax.broadcasted_iota(jnp.int32, (N_EXPERTS, tm), 0)
    for k in range(TOP_K):
        onehot = jnp.where(erow == idx_ref[0, k:k + 1, :], 1.0, 0.0).astype(BF16)
        q = _dot(pq_ref[...], onehot)
        blk = (q[0:1, :] * 32.0 + q[1:2, :]).astype(jnp.int32)
        pos_ref[k:k + 1, :] = blk * EXPERT_BLOCK + rank_ref[0, k:k + 1, :]


def _positions(idx3, rank3, pstart):
    nt, _, tm = idx3.shape
    blk = pstart // EXPERT_BLOCK
    pq = jnp.zeros((8, N_EXPERTS), F32).at[0].set((blk // 32).astype(F32)).at[1].set((blk % 32).astype(F32))
    tile = pl.BlockSpec((1, TOP_K, tm), lambda i: (i, 0, 0))
    return pl.pallas_call(
        _pos_kernel,
        grid=(nt,),
        in_specs=[tile, tile, pl.BlockSpec((8, N_EXPERTS), lambda i: (0, 0))],
        out_specs=pl.BlockSpec((TOP_K, tm), lambda i: (0, i)),
        out_shape=jax.ShapeDtypeStruct((TOP_K, nt * tm), jnp.int32),
        compiler_params=_params("parallel"),
        name="moe_positions",
    )(idx3, rank3, pq.astype(BF16))


def _sc_mesh():
    return plsc.VectorSubcoreMesh(core_axis_name="c", subcore_axis_name="s", num_cores=SC_CORES,
                                  num_subcores=SC_SUBCORES)


def _sc_worker():
    return lax.axis_index("s") * SC_CORES + lax.axis_index("c")


def _sc_scatter_kernel(x_hbm, pos_hbm, out_hbm, idx_v, rows_v):
    nchunk = idx_v.shape[1]
    w = _sc_worker()
    pltpu.sync_copy(pos_hbm.at[w], idx_v)

    @pl.loop(0, nchunk)
    def _(c):
        pltpu.sync_copy(x_hbm.at[pl.ds((w * nchunk + c) * SC_CHUNK, SC_CHUNK)], rows_v)
        for k in range(TOP_K):
            pltpu.sync_copy(rows_v, out_hbm.at[idx_v.at[k, c]])


def _sc_dispatch(x1p, pos_kt, n_rows):
    t = x1p.shape[0]
    workers = SC_CORES * SC_SUBCORES
    nchunk = t // (workers * SC_CHUNK)
    assert t == workers * nchunk * SC_CHUNK
    pos4 = pos_kt.reshape(TOP_K, workers, nchunk, SC_CHUNK).transpose(1, 0, 2, 3)
    return pl.kernel(
        _sc_scatter_kernel,
        out_type=jax.ShapeDtypeStruct((n_rows, HALF), U32),
        mesh=_sc_mesh(),
        scratch_types=[pltpu.VMEM((TOP_K, nchunk, SC_CHUNK), jnp.int32), pltpu.VMEM((SC_CHUNK, HALF), U32)],
        name="moe_dispatch_sc",
    )(x1p, pos4)


def _expert_kernel(layer, be_ref, nused_ref, nvalid_ref, first_ref, slot_ref, ahead_ref, head_ref,
                   x_ref, wg_hbm, wu_hbm, wd_hbm, y_ref, wg_f, wu_f, wd_f, wg_b, wu_b, wd_b, sem):
    def fetch(e, slot):
        return [pltpu.make_async_copy(src.at[layer, e], dst.at[slot], sem.at[slot, n])
                for n, (src, dst) in enumerate(((wg_hbm, wg_f), (wu_hbm, wu_f), (wd_hbm, wd_f)))]

    @pl.when(pl.program_id(0) == 0)
    def _():
        for copy in fetch(head_ref[0], 0):
            copy.start()

        @pl.when(head_ref[1] >= 0)
        def _():
            for copy in fetch(head_ref[1], 1):
                copy.start()

    def one_block(b, rows):
        @pl.when((b < nused_ref[0]) & (first_ref[b] == 1))
        def _():
            slot = slot_ref[b]
            for copy in fetch(be_ref[b], slot):
                copy.wait()
            wg_b[...] = wg_f[slot].astype(BF16)
            wu_b[...] = wu_f[slot].astype(BF16)
            wd_b[...] = wd_f[slot].astype(BF16)

            @pl.when(ahead_ref[b] >= 0)
            def _():
                for copy in fetch(ahead_ref[b], slot):
                    copy.start()

        @pl.when(b < nused_ref[0])
        def _():
            live = lax.broadcasted_iota(jnp.int32, (EXPERT_BLOCK, HALF), 0) < nvalid_ref[b]
            xlo, xhi = (h.astype(BF16) for h in _unpack_halves(jnp.where(live, x_ref[rows, :], jnp.uint32(0))))
            g = _dot(xlo, wg_b[:HALF, :]) + _dot(xhi, wg_b[HALF:, :])
            u = _dot(xlo, wu_b[:HALF, :]) + _dot(xhi, wu_b[HALF:, :])
            a = (g * _sigmoid(g) * u).astype(BF16)
            y_ref[rows, :] = _pack_halves(_dot(a, wd_b[...]))

        @pl.when(b >= nused_ref[0])
        def _():
            y_ref[rows, :] = jnp.zeros((EXPERT_BLOCK, HALF), U32)

    for sub in range(EXPERT_STEP):
        one_block(pl.program_id(0) * EXPERT_STEP + sub, slice(sub * EXPERT_BLOCK, (sub + 1) * EXPERT_BLOCK))


def _experts(xs, block_e, nused, nvalid, counts, w_gate, w_up, w_down, layer, n_blocks):
    rows = EXPERT_BLOCK
    blocks = jnp.arange(n_blocks, dtype=jnp.int32)
    first = ((blocks == 0) | (block_e != jnp.roll(block_e, 1))) & (blocks < nused[0])
    run = jnp.cumsum(first.astype(jnp.int32)) - 1
    run_e = jnp.nonzero(counts > 0, size=N_EXPERTS, fill_value=-1)[0].astype(jnp.int32)
    ahead = jnp.concatenate([run_e, jnp.full((2,), -1, jnp.int32)])[jnp.clip(run, 0, N_EXPERTS - 1) + 2]
    step_rows = EXPERT_STEP * rows
    grid_spec = pltpu.PrefetchScalarGridSpec(
        num_scalar_prefetch=7,
        grid=(n_blocks // EXPERT_STEP,),
        in_specs=[
            pl.BlockSpec((step_rows, HALF), lambda s, be, nu, *_: (jnp.minimum(s, (nu[0] - 1) // EXPERT_STEP), 0)),
            pl.BlockSpec(memory_space=pl.ANY), pl.BlockSpec(memory_space=pl.ANY), pl.BlockSpec(memory_space=pl.ANY),
        ],
        out_specs=pl.BlockSpec((step_rows, HALF), lambda s, *_: (s, 0)),
        scratch_shapes=[
            pltpu.VMEM((2, D_MODEL, D_EXPERT), F32), pltpu.VMEM((2, D_MODEL, D_EXPERT), F32),
            pltpu.VMEM((2, D_EXPERT, D_MODEL), F32),
            pltpu.VMEM((D_MODEL, D_EXPERT), BF16), pltpu.VMEM((D_MODEL, D_EXPERT), BF16),
            pltpu.VMEM((D_EXPERT, D_MODEL), BF16),
            pltpu.SemaphoreType.DMA((2, 3)),
        ],
    )
    return pl.pallas_call(
        functools.partial(_expert_kernel, layer),
        grid_spec=grid_spec,
        out_shape=jax.ShapeDtypeStruct((n_blocks * rows, HALF), U32),
        compiler_params=_params("arbitrary"),
        name="moe_experts",
    )(block_e, nused, nvalid, first.astype(jnp.int32), (run % 2).astype(jnp.int32), ahead, run_e[:2],
      xs, w_gate, w_up, w_down)


def _sc_gather_kernel(table_hbm, idx_hbm, out_hbm, idx_v, rows_v, gsem, wsem):
    per_worker = idx_v.shape[0]
    base = _sc_worker() * per_worker
    pltpu.sync_copy(idx_hbm.at[pl.ds(base, per_worker)], idx_v)

    @pl.loop(0, per_worker // SC_CHUNK, step=SC_BUFS)
    def _(c):
        offs = [(c + n) * SC_CHUNK for n in range(SC_BUFS)]
        gathers = [pltpu.async_copy(table_hbm.at[idx_v.at[pl.ds(offs[n], SC_CHUNK)]], rows_v.at[n], gsem.at[n])
                   for n in range(SC_BUFS)]
        writes = []
        for n in range(SC_BUFS):
            gathers[n].wait()
            writes.append(pltpu.async_copy(rows_v.at[n], out_hbm.at[pl.ds(base + offs[n], SC_CHUNK)], wsem.at[n]))
        for write in writes:
            write.wait()


def _sc_gather_rows(table, idx):
    n = idx.shape[0]
    workers = SC_CORES * SC_SUBCORES
    assert n % (workers * SC_CHUNK * SC_BUFS) == 0
    return pl.kernel(
        _sc_gather_kernel,
        out_type=jax.ShapeDtypeStruct((n, HALF), U32),
        mesh=_sc_mesh(),
        scratch_types=[pltpu.VMEM((n // workers,), jnp.int32), pltpu.VMEM((SC_BUFS, SC_CHUNK, HALF), U32),
                       pltpu.SemaphoreType.DMA((SC_BUFS,)), pltpu.SemaphoreType.DMA((SC_BUFS,))],
        name="moe_gather_sc",
    )(table, idx)


def _combine_kernel(x_ref, w_ref, *refs):
    y_refs, (wsg_ref, wsu_ref, wsd_ref, g_ref, b_ref, o_ref) = refs[:TOP_K], refs[TOP_K:]
    x = x_ref[...]
    xb = x.astype(BF16)
    g = _dot(xb, wsg_ref[...])
    u = _dot(xb, wsu_ref[...])
    shared = _dot((g * _sigmoid(g) * u).astype(BF16), wsd_ref[...])
    w = w_ref[...]
    lo, hi = (h * w[:, 0:1] for h in _unpack_halves(y_refs[0][...]))
    for k in range(1, TOP_K):
        lo_k, hi_k = _unpack_halves(y_refs[k][...])
        lo = lo + lo_k * w[:, k:k + 1]
        hi = hi + hi_k * w[:, k:k + 1]
    routed = jnp.concatenate([lo, hi], axis=1)
    o_ref[...] = _layer_norm(ALPHA * x + (shared + routed), g_ref[...], b_ref[...])


def _combine(x1, y8, w_tk, wp):
    t = x1.shape[0]
    tm = WIDE_TILE
    nt = t // tm
    weights = [wp['wsg'], wp['wsu'], wp['wsd'], wp['ln2_g'], wp['ln2_b']]
    y_specs = [pl.BlockSpec((tm, HALF), lambda i, k=k: (k * nt + i, 0)) for k in range(TOP_K)]
    return pl.pallas_call(
        _combine_kernel,
        grid=(nt,),
        in_specs=[pl.BlockSpec((tm, D_MODEL), lambda i: (i, 0)), pl.BlockSpec((tm, TOP_K), lambda i: (i, 0))]
                 + y_specs + [_full(w) for w in weights],
        out_specs=pl.BlockSpec((tm, D_MODEL), lambda i: (i, 0)),
        out_shape=jax.ShapeDtypeStruct((t, D_MODEL), F32),
        compiler_params=_params("parallel"),
        name="moe_combine",
    )(x1, w_tk, *([y8] * TOP_K), *weights)


def _moe(x1, x1p, wp, w_e_gate, w_e_up, w_e_down, layer):
    t = x1.shape[0]
    rows = EXPERT_BLOCK
    n_blocks = -(-(t * TOP_K + N_EXPERTS * (rows - 1)) // (rows * EXPERT_STEP)) * EXPERT_STEP
    idx3, rank3, top_w, cnt = _router(x1, wp)
    counts = cnt[:, 0].astype(jnp.int32)
    padded = (counts + rows - 1) // rows * rows
    padded_end = jnp.cumsum(padded)
    pstart = (padded_end - padded).astype(jnp.int32)
    nused = (padded_end[-1] // rows).astype(jnp.int32).reshape(1)
    blocks = jnp.arange(n_blocks, dtype=jnp.int32)
    block_row = jnp.minimum(blocks, nused[0] - 1) * rows
    block_e = jnp.sum((padded_end[None, :] <= block_row[:, None]).astype(jnp.int32), axis=1)
    block_e = jnp.minimum(block_e, N_EXPERTS - 1)
    nvalid = jnp.clip(counts[block_e] - (blocks * rows - pstart[block_e]), 0, rows).astype(jnp.int32)
    pos_kt = _positions(idx3, rank3, pstart)
    xs = _sc_dispatch(x1p, pos_kt, n_blocks * rows)
    ys = _experts(xs, block_e, nused, nvalid, counts, w_e_gate, w_e_up, w_e_down, layer, n_blocks)
    return _combine(x1, _sc_gather_rows(ys, pos_kt.reshape(TOP_K * t)), top_w.T, wp)


def _head_groups_t(w, used):
    k, h, _ = w.shape
    return jnp.pad(w, ((0, 0), (0, 0), (0, LANES - used))).reshape(k, h * LANES).T


def _prep_layer(l, w_in, b_gate, q_norm, w_uq, kv_norm, w_ukv, w_proj_a, w_proj_b, w_out, ln1_g, ln1_b,
                w_router, router_bias, w_s_gate, w_s_up, w_s_down, ln2_g, ln2_b):
    w = w_in[l]
    o = 0
    cols = {}
    for name, width in (('qa', WIDTH_A), ('ka', WIDTH_A), ('va', WIDTH_A), ('cq', Q_LORA_RANK),
                        ('ckv', KV_LORA_RANK), ('kr', QK_ROPE_DIM), ('ga', D_MODEL), ('gb', D_MODEL)):
        cols[name] = w[:, o:o + width]
        o += width
    wkr = jnp.zeros((D_MODEL, LANES), F32).at[:, QK_NOPE_DIM:QK_NOPE_DIM + QK_ROPE_DIM].set(cols['kr'])
    dqk = QK_NOPE_DIM + QK_ROPE_DIM
    wq = w_uq[l].reshape(Q_LORA_RANK, N_HEADS_B, dqk) * (dqk ** -0.5 * LOG2E)
    wkv = w_ukv[l].reshape(KV_LORA_RANK, N_HEADS_B, QK_NOPE_DIM + V_HEAD_DIM)
    wuk = jnp.pad(wkv[:, :, :QK_NOPE_DIM], ((0, 0), (0, 0), (0, LANES - QK_NOPE_DIM))).reshape(KV_LORA_RANK, N_HEADS_B * LANES)
    ones = jnp.zeros((N_HEADS_B, LANES), F32).at[:, V_HEAD_DIM].set(1.0).reshape(N_HEADS_B * LANES, 1)
    wr_t = w_router[l].T
    wr_hi = wr_t.astype(BF16)
    return dict(
        wqt=(cols['qa'] * HEAD_DIM_A ** -0.5).T.astype(BF16), wk=cols['ka'].astype(BF16),
        wvt=_head_groups_t(cols['va'].reshape(D_MODEL, N_HEADS_A, HEAD_DIM_A), HEAD_DIM_A).astype(BF16), ones=ones,
        wcq=cols['cq'].astype(BF16), wckv=cols['ckv'].astype(BF16), wkr=wkr.astype(BF16),
        wg=jnp.concatenate([cols['ga'], cols['gb']], axis=1).astype(BF16),
        bg=b_gate[l].reshape(1, 2 * D_MODEL), qn=q_norm[l].reshape(1, Q_LORA_RANK), kvn=kv_norm[l].reshape(1, KV_LORA_RANK),
        wuqt=_head_groups_t(wq, dqk).astype(BF16), wuk=wuk.astype(BF16),
        wuvt=_head_groups_t(wkv[:, :, QK_NOPE_DIM:], V_HEAD_DIM).astype(BF16),
        wpa=w_proj_a[l].astype(BF16), wpb=w_proj_b[l].astype(BF16), wo=w_out[l].astype(BF16),
        ln1_g=ln1_g[l].reshape(1, D_MODEL), ln1_b=ln1_b[l].reshape(1, D_MODEL),
        wr_hi=wr_hi, wr_lo=(wr_t - wr_hi.astype(F32)).astype(BF16), rbias=router_bias[l].reshape(N_EXPERTS, 1),
        wsg=w_s_gate[l].astype(BF16), wsu=w_s_up[l].astype(BF16), wsd=w_s_down[l].astype(BF16),
        ln2_g=ln2_g[l].reshape(1, D_MODEL), ln2_b=ln2_b[l].reshape(1, D_MODEL),
    )


def _rope_tables(seq):
    pos = jnp.arange(seq, dtype=F32)
    inv_freq = ROPE_THETA ** (-jnp.arange(0, QK_ROPE_DIM, 2, dtype=F32) / QK_ROPE_DIM)
    ang = pos[:, None] * inv_freq[None, :]
    cos, sin = jnp.cos(ang), jnp.sin(ang)
    half = QK_ROPE_DIM // 2
    z = lambda n: jnp.zeros((seq, n), F32)
    c = jnp.concatenate([jnp.ones((seq, QK_NOPE_DIM), F32), cos, cos, z(LANES - QK_NOPE_DIM - QK_ROPE_DIM)], axis=1)
    s1 = jnp.concatenate([z(QK_NOPE_DIM), -sin, z(LANES - QK_NOPE_DIM - half)], axis=1)
    s2 = jnp.concatenate([z(QK_NOPE_DIM + half), sin, z(LANES - QK_NOPE_DIM - QK_ROPE_DIM)], axis=1)
    return c, s1, s2, c.T, s1.T, s2.T


def _moba_key_table(seq, slopes):
    blk = jnp.arange(seq, dtype=jnp.int32) // MOBA_BLOCK
    onehot = (blk[:, None] == jnp.arange(BIAS_LO, dtype=jnp.int32)[None, :]).astype(F32)
    inblk = (jnp.arange(seq, dtype=jnp.int32) % MOBA_BLOCK).astype(F32)[:, None] * (slopes * LOG2E)[None, :]
    hi = inblk.astype(BF16)
    lo = (inblk - hi.astype(F32)).astype(BF16)
    pad = jnp.zeros((seq, LANES - POS_LO - N_HEADS_A), BF16)
    return jnp.concatenate([onehot.astype(BF16), onehot.astype(BF16), hi, lo, pad], axis=1)


def kernel(x, w_in, b_gate, q_norm, w_uq, kv_norm, w_ukv, w_proj_a, w_proj_b, w_out, ln1_g, ln1_b, w_router, router_bias, w_e_gate, w_e_up, w_e_down, w_s_gate, w_s_up, w_s_down, ln2_g, ln2_b):
    batch, seq, d = x.shape
    assert d == D_MODEL and seq % MOBA_BLOCK == 0 and MOBA_TOPK <= seq // MOBA_BLOCK <= BIAS_LO
    assert POS_LO + N_HEADS_A <= LANES and POS_HI + N_HEADS_A <= POS_LO and (batch * seq) % WIDE_TILE == 0
    tabs = _rope_tables(seq)
    slopes = jnp.asarray(np.exp2(-8.0 * (np.arange(N_HEADS_A) + 1.0) / N_HEADS_A), F32)
    etab = _moba_key_table(seq, slopes)
    h = x.reshape(batch * seq, d)
    for l in range(DEPTH):
        wp = _prep_layer(l, w_in, b_gate, q_norm, w_uq, kv_norm, w_ukv, w_proj_a, w_proj_b, w_out, ln1_g, ln1_b,
                         w_router, router_bias, w_s_gate, w_s_up, w_s_down, ln2_g, ln2_b)
        qat, ka, vat, kmean, qmt, km, vmt, sa, sb = _inproj(h, wp, tabs, seq)
        oa = _moba(qat, ka, vat, kmean, etab, slopes, batch, seq)
        ob = _mla(qmt, km, vmt, batch, seq)
        x1, x1p = _merge(h, oa, ob, sa, sb, wp)
        h = _moe(x1, x1p, wp, w_e_gate, w_e_up, w_e_down, l)
    return h.reshape(batch, seq, d)
```

```python
import functools

import numpy as np

import jax
import jax.numpy as jnp
from jax import lax
from jax.experimental import pallas as pl
from jax.experimental.pallas import tpu as pltpu
from jax.experimental.pallas import tpu_sc as plsc

D_MODEL = 1024
N_HEADS_A = 8
HEAD_DIM_A = 64
WIDTH_A = N_HEADS_A * HEAD_DIM_A
MOBA_BLOCK = 256
MOBA_TOPK = 3
N_HEADS_B = 8
QK_NOPE_DIM = 64
QK_ROPE_DIM = 32
V_HEAD_DIM = 64
Q_LORA_RANK = 384
KV_LORA_RANK = 256
WIDTH_B = N_HEADS_B * V_HEAD_DIM
ROPE_THETA = 10000.0
N_EXPERTS = 256
TOP_K = 8
N_GROUPS = 8
TOPK_GROUPS = 4
GROUP_SIZE = N_EXPERTS // N_GROUPS
D_EXPERT = 256
D_SHARED = 256
ROUTED_SCALE = 2.5
DEPTH = 2
ALPHA = (2 * DEPTH) ** 0.25
LN_EPS = 1e-5
RMS_EPS = 1e-6

LANES = 128
NEG = -1e30
LOG2E = float(np.log2(np.e))
ROW_TILE = 256
WIDE_TILE = 512
ATT_TILE = 256
EXPERT_BLOCK = 256
EXPERT_STEP = 8
SC_CORES = 2
SC_SUBCORES = 16
SC_CHUNK = 64
SC_BUFS = 2
VMEM_LIMIT = 56 * 1024 * 1024
HALF = D_MODEL // 2
ATT_HEADS = 4
ATT_GROUP = 4
PV_ROWS = 80
BIAS_HI, BIAS_LO, POS_HI, POS_LO = 0, 32, 64, 72

BF16 = jnp.bfloat16
F32 = jnp.float32
U32 = jnp.uint32


def _dot(a, b):
    return jnp.dot(a, b, preferred_element_type=F32)


def _dot_nt(a, b):
    return lax.dot_general(a, b, (((1,), (1,)), ((), ())), preferred_element_type=F32)


def _sigmoid(x):
    return 1.0 / (1.0 + jnp.exp(-x))


def _layer_norm(y, g, b):
    mu = jnp.mean(y, axis=-1, keepdims=True)
    d = y - mu
    var = jnp.mean(d * d, axis=-1, keepdims=True)
    return d * lax.rsqrt(var + LN_EPS) * g + b


def _params(*sem):
    return pltpu.CompilerParams(dimension_semantics=sem, vmem_limit_bytes=VMEM_LIMIT)


def _full(a):
    return pl.BlockSpec(a.shape, lambda *_: (0,) * a.ndim)


def _inproj_kernel(x_ref, wqt_ref, wk_ref, wvt_ref, ones_ref, wcq_ref, wckv_ref, wkr_ref, wg_ref, bg_ref,
                   qn_ref, kvn_ref, wuqt_ref, wuk_ref, wuvt_ref, cos_ref, s1_ref, s2_ref, cost_ref, s1t_ref, s2t_ref,
                   qat_ref, ka_ref, vat_ref, kmean_ref, qmt_ref, km_ref, vmt_ref, sa_ref, sb_ref):
    xb = x_ref[...].astype(BF16)
    half = QK_ROPE_DIM // 2
    qat_ref[0] = _dot_nt(wqt_ref[...], xb).astype(BF16)
    k = _dot(xb, wk_ref[...])
    ka_ref[...] = k.astype(BF16)
    kmean_ref[0] = jnp.mean(k, axis=0, keepdims=True)
    vat_ref[0] = (_dot_nt(wvt_ref[...], xb) + ones_ref[...]).astype(BF16)

    cq = _dot(xb, wcq_ref[...])
    cqn = (cq * lax.rsqrt(jnp.mean(cq * cq, axis=-1, keepdims=True) + RMS_EPS) * qn_ref[...]).astype(BF16)
    ckv = _dot(xb, wckv_ref[...])
    ckvn = (ckv * lax.rsqrt(jnp.mean(ckv * ckv, axis=-1, keepdims=True) + RMS_EPS) * kvn_ref[...]).astype(BF16)
    qt = _dot_nt(wuqt_ref[...], cqn)
    ct, s1t, s2t = cost_ref[...], s1t_ref[...], s2t_ref[...]
    for h in range(N_HEADS_B):
        t = qt[h * LANES:(h + 1) * LANES, :]
        rot = t * ct + pltpu.roll(t, LANES - half, 0) * s1t + pltpu.roll(t, half, 0) * s2t
        qmt_ref[0, h * LANES:(h + 1) * LANES, :] = rot.astype(BF16)
    kn = _dot(ckvn, wuk_ref[...])
    kr = _dot(xb, wkr_ref[...])
    c, s1, s2 = cos_ref[...], s1_ref[...], s2_ref[...]
    krot = kr * c + pltpu.roll(kr, LANES - half, 1) * s1 + pltpu.roll(kr, half, 1) * s2
    for h in range(N_HEADS_B):
        sl = slice(h * LANES, (h + 1) * LANES)
        km_ref[:, sl] = (kn[:, sl] + krot).astype(BF16)
    vmt_ref[0] = (_dot_nt(wuvt_ref[...], ckvn) + ones_ref[...]).astype(BF16)

    sig = _sigmoid(_dot(xb, wg_ref[...]) + bg_ref[...])
    sa_ref[...] = sig[:, :D_MODEL].astype(BF16)
    sb_ref[...] = sig[:, D_MODEL:].astype(BF16)


def _inproj(x, wp, tabs, seq):
    t = x.shape[0]
    tm = ROW_TILE
    nt = t // tm
    npos = seq // tm
    row = lambda w: pl.BlockSpec((tm, w), lambda i: (i, 0))
    tile = lambda r: pl.BlockSpec((1, r, tm), lambda i: (i, 0, 0))
    tab = pl.BlockSpec((tm, LANES), lambda i: (i % npos, 0))
    tabt = pl.BlockSpec((LANES, tm), lambda i: (0, i % npos))
    weights = [wp['wqt'], wp['wk'], wp['wvt'], wp['ones'], wp['wcq'], wp['wckv'], wp['wkr'], wp['wg'], wp['bg'],
               wp['qn'], wp['kvn'], wp['wuqt'], wp['wuk'], wp['wuvt']]
    hl = N_HEADS_B * LANES
    out_shape = [
        jax.ShapeDtypeStruct((nt, WIDTH_A, tm), BF16), jax.ShapeDtypeStruct((t, WIDTH_A), BF16),
        jax.ShapeDtypeStruct((nt, N_HEADS_A * LANES, tm), BF16), jax.ShapeDtypeStruct((nt, 1, WIDTH_A), F32),
        jax.ShapeDtypeStruct((nt, hl, tm), BF16), jax.ShapeDtypeStruct((t, hl), BF16),
        jax.ShapeDtypeStruct((nt, hl, tm), BF16),
        jax.ShapeDtypeStruct((t, D_MODEL), BF16), jax.ShapeDtypeStruct((t, D_MODEL), BF16),
    ]
    out_specs = [tile(WIDTH_A), row(WIDTH_A), tile(N_HEADS_A * LANES),
                 pl.BlockSpec((1, 1, WIDTH_A), lambda i: (i, 0, 0)),
                 tile(hl), row(hl), tile(hl), row(D_MODEL), row(D_MODEL)]
    return pl.pallas_call(
        _inproj_kernel,
        grid=(nt,),
        in_specs=[row(D_MODEL)] + [_full(w) for w in weights] + [tab, tab, tab, tabt, tabt, tabt],
        out_specs=out_specs,
        out_shape=out_shape,
        compiler_params=_params("parallel"),
        name="inproj",
    )(x, *weights, *tabs)


def _attend_init(tq):
    return jnp.full((1, tq), -jnp.inf, F32), jnp.zeros((PV_ROWS, tq), F32)


def _col_max(s_ref):
    return [jnp.max(s_ref[hh], axis=0, keepdims=True) for hh in range(ATT_HEADS)]


def _attend_staged(cur_ref, cur_max, state, vts, nxt_ref=None, next_scores=None):
    heads = range(ATT_HEADS)
    if nxt_ref is not None:
        for hh in heads:
            nxt_ref[hh] = next_scores(hh)
    new_m, scaled, pvs = [], [], []
    for hh in heads:
        m_i, acc = state[hh]
        m_new = jnp.maximum(m_i, cur_max[hh])
        new_m.append(m_new)
        scaled.append(jnp.exp2(m_i - m_new) * acc)
        pvs.append(_dot(vts[hh], jnp.exp2(cur_ref[hh] - m_new).astype(BF16)))
    nxt_max = _col_max(nxt_ref) if nxt_ref is not None else cur_max
    return tuple((m, a + pv) for m, a, pv in zip(new_m, scaled, pvs)), nxt_max


def _attention_loop(i, scores, values, causal, tq, sa_ref, sb_ref):
    heads = range(ATT_HEADS)
    bufs = (sa_ref, sb_ref)

    def vals(s):
        tile = jnp.where(s == 0, i, s - 1)
        return [values(hh, tile) for hh in heads]

    def ahead(s):
        return lambda hh: scores(hh, jnp.maximum(jnp.minimum(s, i - 1), 0))

    for hh in heads:
        sa_ref[hh] = jnp.where(causal, scores(hh, i), NEG)
    carry = (tuple(_attend_init(tq) for _ in heads), _col_max(sa_ref))
    nslots = i + 1

    def trip(n, c):
        for g in range(ATT_GROUP):
            s = n * ATT_GROUP + g
            c = _attend_staged(bufs[g % 2], c[1], c[0], vals(s), bufs[(g + 1) % 2], ahead(s))
        return c

    carry = lax.fori_loop(0, nslots // ATT_GROUP, trip, carry)
    done = nslots // ATT_GROUP * ATT_GROUP
    for g in range(ATT_GROUP - 1):
        def step(c, g=g):
            if g == ATT_GROUP - 2:
                return _attend_staged(bufs[g % 2], c[1], c[0], vals(done + g))
            return _attend_staged(bufs[g % 2], c[1], c[0], vals(done + g), bufs[(g + 1) % 2], ahead(done + g))

        carry = lax.cond(nslots - done > g, step, lambda c: c, carry)
    outs = []
    for hh in heads:
        _, acc = carry[0][hh]
        outs.append(acc[:V_HEAD_DIM, :] / acc[V_HEAD_DIM:V_HEAD_DIM + 1, :])
    return jnp.concatenate(outs, axis=0).T


def _causal_t(tq):
    key = lax.broadcasted_iota(jnp.int32, (tq, tq), 0)
    qry = lax.broadcasted_iota(jnp.int32, (tq, tq), 1)
    return key <= qry


def _moba_kernel(slope_ref, qt_ref, k_ref, e_ref, vt_ref, km_ref, o_ref, sa_ref, sb_ref):
    tq = ATT_TILE
    hp = pl.program_id(1)
    i = pl.program_id(2)
    nblk = km_ref.shape[1]
    frow = lax.broadcasted_iota(jnp.int32, (LANES, tq), 0)
    row = lax.broadcasted_iota(jnp.int32, (BIAS_LO, tq), 0)
    prow = lax.broadcasted_iota(jnp.int32, (LANES - POS_HI, tq), 0) + POS_HI

    qaug = []
    for hh in range(ATT_HEADS):
        head = hp * ATT_HEADS + hh
        grp = slice(hh // 2 * LANES, (hh // 2 + 1) * LANES)
        qt2 = qt_ref[0, grp, :]
        head_rows = (frow >= hh % 2 * HEAD_DIM_A) & (frow < (hh % 2 + 1) * HEAD_DIM_A)
        qh = jnp.where(head_rows, qt2, jnp.zeros_like(qt2))
        kmean = km_ref[0, :, grp].astype(BF16)
        if nblk < BIAS_LO:
            kmean = jnp.concatenate([kmean, jnp.zeros((BIAS_LO - nblk, LANES), BF16)], axis=0)
        g = jnp.where(row < i, _dot(kmean, qh), -jnp.inf)
        picked = row == i
        for r in range(MOBA_TOPK):
            m = jnp.max(g, axis=0, keepdims=True)
            idx = jnp.min(jnp.where(g == m, row, BIAS_LO), axis=0, keepdims=True)
            hit = row == idx
            picked = picked | (hit & (r < i))
            g = jnp.where(hit, -jnp.inf, g)
        dist = (i - row).astype(F32) * (slope_ref[head] * (MOBA_BLOCK * LOG2E))
        b = jnp.where(picked, -dist, NEG)
        b_hi = b.astype(BF16)
        b_lo = (b - b_hi.astype(F32)).astype(BF16)
        ones = jnp.where((prow == POS_HI + head) | (prow == POS_LO + head), 1.0, 0.0).astype(BF16)
        qs = (qh.astype(F32) * LOG2E).astype(BF16)
        qaug.append(jnp.concatenate([qs, b_hi, b_lo, ones], axis=0))

    def scores(hh, j):
        rows = pl.ds(pl.multiple_of(j * tq, tq), tq)
        grp = slice(hh // 2 * LANES, (hh // 2 + 1) * LANES)
        return _dot(jnp.concatenate([k_ref[rows, grp], e_ref[rows, :]], axis=1), qaug[hh])

    def values(hh, j):
        return vt_ref[j, hh * LANES:hh * LANES + PV_ROWS, :]

    o_ref[...] = _attention_loop(i, scores, values, _causal_t(tq), tq, sa_ref, sb_ref).astype(BF16)


def _moba(qat, ka, vat, kmean, etab, slopes, batch, seq):
    t = ka.shape[0]
    tq = ATT_TILE
    nq = seq // tq
    nblk = seq // MOBA_BLOCK
    nh = ATT_HEADS
    km = kmean.reshape(batch, nblk, WIDTH_A)
    return pl.pallas_call(
        _moba_kernel,
        grid=(batch, N_HEADS_A // nh, nq),
        in_specs=[
            pl.BlockSpec(memory_space=pltpu.SMEM),
            pl.BlockSpec((1, nh * HEAD_DIM_A, tq), lambda b, h, i: (b * nq + i, h, 0)),
            pl.BlockSpec((seq, nh * HEAD_DIM_A), lambda b, h, i: (b, h)),
            pl.BlockSpec((seq, LANES), lambda b, h, i: (0, 0)),
            pl.BlockSpec((nq, nh * LANES, tq), lambda b, h, i: (b, h, 0)),
            pl.BlockSpec((1, nblk, nh * HEAD_DIM_A), lambda b, h, i: (b, 0, h)),
        ],
        out_specs=pl.BlockSpec((tq, nh * HEAD_DIM_A), lambda b, h, i: (b * nq + i, h)),
        out_shape=jax.ShapeDtypeStruct((t, WIDTH_A), BF16),
        scratch_shapes=[pltpu.VMEM((nh, tq, tq), F32), pltpu.VMEM((nh, tq, tq), F32)],
        compiler_params=_params("parallel", "parallel", "arbitrary"),
        name="moba",
    )(slopes, qat, ka, etab, vat, km)


def _mla_kernel(qt_ref, k_ref, vt_ref, o_ref, sa_ref, sb_ref):
    tq = ATT_TILE
    i = pl.program_id(2)
    qt = [qt_ref[0, hh * LANES:(hh + 1) * LANES, :] for hh in range(ATT_HEADS)]

    def scores(hh, j):
        return _dot(k_ref[pl.ds(pl.multiple_of(j * tq, tq), tq), hh * LANES:(hh + 1) * LANES], qt[hh])

    def values(hh, j):
        return vt_ref[j, hh * LANES:hh * LANES + PV_ROWS, :]

    o_ref[...] = _attention_loop(i, scores, values, _causal_t(tq), tq, sa_ref, sb_ref).astype(BF16)


def _mla(qmt, km, vmt, batch, seq):
    t = km.shape[0]
    tq = ATT_TILE
    nq = seq // tq
    nh = ATT_HEADS
    return pl.pallas_call(
        _mla_kernel,
        grid=(batch, N_HEADS_B // nh, nq),
        in_specs=[
            pl.BlockSpec((1, nh * LANES, tq), lambda b, h, i: (b * nq + i, h, 0)),
            pl.BlockSpec((seq, nh * LANES), lambda b, h, i: (b, h)),
            pl.BlockSpec((nq, nh * LANES, tq), lambda b, h, i: (b, h, 0)),
        ],
        out_specs=pl.BlockSpec((tq, nh * V_HEAD_DIM), lambda b, h, i: (b * nq + i, h)),
        out_shape=jax.ShapeDtypeStruct((t, WIDTH_B), BF16),
        scratch_shapes=[pltpu.VMEM((nh, tq, tq), F32), pltpu.VMEM((nh, tq, tq), F32)],
        compiler_params=_params("parallel", "parallel", "arbitrary"),
        name="mla",
    )(qmt, km, vmt)


def _pack_halves(y):
    lo = pltpu.bitcast(y[:, :HALF].astype(BF16).astype(F32), U32)
    hi = pltpu.bitcast(y[:, HALF:].astype(BF16).astype(F32), U32)
    return (hi & jnp.uint32(0xFFFF0000)) | (lo >> 16)


def _unpack_halves(w):
    return pltpu.bitcast(w << 16, F32), pltpu.bitcast(w & jnp.uint32(0xFFFF0000), F32)


def _merge_kernel(x_ref, oa_ref, ob_ref, sa_ref, sb_ref, wpa_ref, wpb_ref, wo_ref, g_ref, b_ref, o_ref, op_ref):
    pa = _dot(oa_ref[...], wpa_ref[...])
    pb = _dot(ob_ref[...], wpb_ref[...])
    merged = sa_ref[...].astype(F32) * pa + sb_ref[...].astype(F32) * pb
    hmix = _dot(merged.astype(BF16), wo_ref[...])
    y = _layer_norm(ALPHA * x_ref[...] + hmix, g_ref[...], b_ref[...])
    o_ref[...] = y
    op_ref[...] = _pack_halves(y)


def _merge(x, oa, ob, sa, sb, wp):
    t = x.shape[0]
    tm = WIDE_TILE
    row = lambda w: pl.BlockSpec((tm, w), lambda i: (i, 0))
    weights = [wp['wpa'], wp['wpb'], wp['wo'], wp['ln1_g'], wp['ln1_b']]
    return pl.pallas_call(
        _merge_kernel,
        grid=(t // tm,),
        in_specs=[row(D_MODEL), row(WIDTH_A), row(WIDTH_B), row(D_MODEL), row(D_MODEL)] + [_full(w) for w in weights],
        out_specs=[row(D_MODEL), row(HALF)],
        out_shape=[jax.ShapeDtypeStruct((t, D_MODEL), F32), jax.ShapeDtypeStruct((t, HALF), U32)],
        compiler_params=_params("parallel"),
        name="merge",
    )(x, oa, ob, sa, sb, *weights)


def _router_kernel(x_ref, wh_ref, wl_ref, rb_ref, idx_ref, rank_ref, w_ref, cnt_ref):
    tm = x_ref.shape[0]

    @pl.when(pl.program_id(0) == 0)
    def _():
        cnt_ref[...] = jnp.zeros_like(cnt_ref)

    x = x_ref[...]
    xh = x.astype(BF16)
    xl = (x - xh.astype(F32)).astype(BF16)
    wh, wl = wh_ref[...], wl_ref[...]
    logits = _dot_nt(wh, xh) + (_dot_nt(wh, xl) + _dot_nt(wl, xh))
    scores = _sigmoid(logits)
    choice = scores + rb_ref[...]
    row = lax.broadcasted_iota(jnp.int32, (GROUP_SIZE, tm), 0)
    groups = [choice[g * GROUP_SIZE:(g + 1) * GROUP_SIZE, :] for g in range(N_GROUPS)]
    gscore = []
    for blk in groups:
        m1 = jnp.max(blk, axis=0, keepdims=True)
        first = jnp.min(jnp.where(blk == m1, row, GROUP_SIZE), axis=0, keepdims=True)
        m2 = jnp.max(jnp.where(row == first, -jnp.inf, blk), axis=0, keepdims=True)
        gscore.append(m1 + m2)
    masked = []
    for g in range(N_GROUPS):
        ahead = jnp.zeros((1, tm), jnp.int32)
        for o in range(N_GROUPS):
            if o < g:
                ahead += (gscore[o] >= gscore[g]).astype(jnp.int32)
            elif o > g:
                ahead += (gscore[o] > gscore[g]).astype(jnp.int32)
        masked.append(jnp.where(ahead < TOPK_GROUPS, groups[g], -jnp.inf))
    cur = jnp.concatenate(masked, axis=0)
    erow = lax.broadcasted_iota(jnp.int32, (N_EXPERTS, tm), 0)
    hits, idxs, ws = [], [], []
    for _ in range(TOP_K):
        m = jnp.max(cur, axis=0, keepdims=True)
        e = jnp.min(jnp.where(cur == m, erow, N_EXPERTS), axis=0, keepdims=True)
        hit = erow == e
        hits.append(hit)
        idxs.append(e)
        ws.append(jnp.sum(jnp.where(hit, scores, 0.0), axis=0, keepdims=True))
        cur = jnp.where(hit, -jnp.inf, cur)
    total = ws[0]
    for w in ws[1:]:
        total = total + w
    member = hits[0]
    for hit in hits[1:]:
        member = member | hit
    member = jnp.where(member, 1.0, 0.0).astype(BF16)
    t_src = lax.broadcasted_iota(jnp.int32, (tm, tm), 0)
    t_dst = lax.broadcasted_iota(jnp.int32, (tm, tm), 1)
    before = _dot(member, jnp.where(t_src < t_dst, 1.0, 0.0).astype(BF16))
    base = cnt_ref[...]
    before = before + jnp.concatenate([base] * (tm // LANES), axis=1)
    cnt_ref[...] = base + _dot(member, jnp.ones((tm, LANES), BF16))
    for r in range(TOP_K):
        idx_ref[0, r:r + 1, :] = idxs[r]
        rank_ref[0, r:r + 1, :] = jnp.sum(jnp.where(hits[r], before, 0.0), axis=0, keepdims=True).astype(jnp.int32)
        w_ref[r:r + 1, :] = ws[r] / total * ROUTED_SCALE


def _router(x1, wp):
    t = x1.shape[0]
    tm = ROW_TILE
    nt = t // tm
    weights = [wp['wr_hi'], wp['wr_lo'], wp['rbias']]
    tile = pl.BlockSpec((1, TOP_K, tm), lambda i: (i, 0, 0))
    return pl.pallas_call(
        _router_kernel,
        grid=(nt,),
        in_specs=[pl.BlockSpec((tm, D_MODEL), lambda i: (i, 0))] + [_full(w) for w in weights],
        out_specs=[tile, tile, pl.BlockSpec((TOP_K, tm), lambda i: (0, i)),
                   pl.BlockSpec((N_EXPERTS, LANES), lambda i: (0, 0))],
        out_shape=[jax.ShapeDtypeStruct((nt, TOP_K, tm), jnp.int32), jax.ShapeDtypeStruct((nt, TOP_K, tm), jnp.int32),
                   jax.ShapeDtypeStruct((TOP_K, t), F32), jax.ShapeDtypeStruct((N_EXPERTS, LANES), F32)],
        compiler_params=_params("arbitrary"),
        name="router",
    )(x1, *weights)


def _pos_kernel(idx_ref, rank_ref, pq_ref, pos_ref):
    tm = idx_ref.shape[2]
    erow = lax.broadcasted_iota(jnp.int32, (N_EXPERTS, tm), 0)
    for k in range(TOP_K):
        onehot = jnp.where(erow == idx_ref[0, k:k + 1, :], 1.0, 0.0).astype(BF16)
        q = _dot(pq_ref[...], onehot)
        blk = (q[0:1, :] * 32.0 + q[1:2, :]).astype(jnp.int32)
        pos_ref[k:k + 1, :] = blk * EXPERT_BLOCK + rank_ref[0, k:k + 1, :]


def _positions(idx3, rank3, pstart):
    nt, _, tm = idx3.shape
    blk = pstart // EXPERT_BLOCK
    pq = jnp.zeros((8, N_EXPERTS), F32).at[0].set((blk // 32).astype(F32)).at[1].set((blk % 32).astype(F32))
    tile = pl.BlockSpec((1, TOP_K, tm), lambda i: (i, 0, 0))
    return pl.pallas_call(
        _pos_kernel,
        grid=(nt,),
        in_specs=[tile, tile, pl.BlockSpec((8, N_EXPERTS), lambda i: (0, 0))],
        out_specs=pl.BlockSpec((TOP_K, tm), lambda i: (0, i)),
        out_shape=jax.ShapeDtypeStruct((TOP_K, nt * tm), jnp.int32),
        compiler_params=_params("parallel"),
        name="moe_positions",
    )(idx3, rank3, pq.astype(BF16))


def _sc_mesh():
    return plsc.VectorSubcoreMesh(core_axis_name="c", subcore_axis_name="s", num_cores=SC_CORES,
                                  num_subcores=SC_SUBCORES)


def _sc_worker():
    return lax.axis_index("s") * SC_CORES + lax.axis_index("c")


def _sc_scatter_kernel(x_hbm, pos_hbm, out_hbm, idx_v, rows_v):
    nchunk = idx_v.shape[1]
    w = _sc_worker()
    pltpu.sync_copy(pos_hbm.at[w], idx_v)

    @pl.loop(0, nchunk)
    def _(c):
        pltpu.sync_copy(x_hbm.at[pl.ds((w * nchunk + c) * SC_CHUNK, SC_CHUNK)], rows_v)
        for k in range(TOP_K):
            pltpu.sync_copy(rows_v, out_hbm.at[idx_v.at[k, c]])


def _sc_dispatch(x1p, pos_kt, n_rows):
    t = x1p.shape[0]
    workers = SC_CORES * SC_SUBCORES
    nchunk = t // (workers * SC_CHUNK)
    assert t == workers * nchunk * SC_CHUNK
    pos4 = pos_kt.reshape(TOP_K, workers, nchunk, SC_CHUNK).transpose(1, 0, 2, 3)
    return pl.kernel(
        _sc_scatter_kernel,
        out_type=jax.ShapeDtypeStruct((n_rows, HALF), U32),
        mesh=_sc_mesh(),
        scratch_types=[pltpu.VMEM((TOP_K, nchunk, SC_CHUNK), jnp.int32), pltpu.VMEM((SC_CHUNK, HALF), U32)],
        name="moe_dispatch_sc",
    )(x1p, pos4)


def _expert_kernel(layer, be_ref, nused_ref, nvalid_ref, first_ref, slot_ref, ahead_ref, head_ref,
                   x_ref, wg_hbm, wu_hbm, wd_hbm, y_ref, wg_f, wu_f, wd_f, wg_b, wu_b, wd_b, sem):
    def fetch(e, slot):
        return [pltpu.make_async_copy(src.at[layer, e], dst.at[slot], sem.at[slot, n])
                for n, (src, dst) in enumerate(((wg_hbm, wg_f), (wu_hbm, wu_f), (wd_hbm, wd_f)))]

    @pl.when(pl.program_id(0) == 0)
    def _():
        for copy in fetch(head_ref[0], 0):
            copy.start()

        @pl.when(head_ref[1] >= 0)
        def _():
            for copy in fetch(head_ref[1], 1):
                copy.start()

    def one_block(b, rows):
        @pl.when((b < nused_ref[0]) & (first_ref[b] == 1))
        def _():
            slot = slot_ref[b]
            for copy in fetch(be_ref[b], slot):
                copy.wait()
            wg_b[...] = wg_f[slot].astype(BF16)
            wu_b[...] = wu_f[slot].astype(BF16)
            wd_b[...] = wd_f[slot].astype(BF16)

            @pl.when(ahead_ref[b] >= 0)
            def _():
                for copy in fetch(ahead_ref[b], slot):
                    copy.start()

        @pl.when(b < nused_ref[0])
        def _():
            live = lax.broadcasted_iota(jnp.int32, (EXPERT_BLOCK, HALF), 0) < nvalid_ref[b]
            xlo, xhi = (h.astype(BF16) for h in _unpack_halves(jnp.where(live, x_ref[rows, :], jnp.uint32(0))))
            g = _dot(xlo, wg_b[:HALF, :]) + _dot(xhi, wg_b[HALF:, :])
            u = _dot(xlo, wu_b[:HALF, :]) + _dot(xhi, wu_b[HALF:, :])
            a = (g * _sigmoid(g) * u).astype(BF16)
            y_ref[rows, :] = _pack_halves(_dot(a, wd_b[...]))

        @pl.when(b >= nused_ref[0])
        def _():
            y_ref[rows, :] = jnp.zeros((EXPERT_BLOCK, HALF), U32)

    for sub in range(EXPERT_STEP):
        one_block(pl.program_id(0) * EXPERT_STEP + sub, slice(sub * EXPERT_BLOCK, (sub + 1) * EXPERT_BLOCK))


def _experts(xs, block_e, nused, nvalid, counts, w_gate, w_up, w_down, layer, n_blocks):
    rows = EXPERT_BLOCK
    blocks = jnp.arange(n_blocks, dtype=jnp.int32)
    first = ((blocks == 0) | (block_e != jnp.roll(block_e, 1))) & (blocks < nused[0])
    run = jnp.cumsum(first.astype(jnp.int32)) - 1
    run_e = jnp.nonzero(counts > 0, size=N_EXPERTS, fill_value=-1)[0].astype(jnp.int32)
    ahead = jnp.concatenate([run_e, jnp.full((2,), -1, jnp.int32)])[jnp.clip(run, 0, N_EXPERTS - 1) + 2]
    step_rows = EXPERT_STEP * rows
    grid_spec = pltpu.PrefetchScalarGridSpec(
        num_scalar_prefetch=7,
        grid=(n_blocks // EXPERT_STEP,),
        in_specs=[
            pl.BlockSpec((step_rows, HALF), lambda s, be, nu, *_: (jnp.minimum(s, (nu[0] - 1) // EXPERT_STEP), 0)),
            pl.BlockSpec(memory_space=pl.ANY), pl.BlockSpec(memory_space=pl.ANY), pl.BlockSpec(memory_space=pl.ANY),
        ],
        out_specs=pl.BlockSpec((step_rows, HALF), lambda s, *_: (s, 0)),
        scratch_shapes=[
            pltpu.VMEM((2, D_MODEL, D_EXPERT), F32), pltpu.VMEM((2, D_MODEL, D_EXPERT), F32),
            pltpu.VMEM((2, D_EXPERT, D_MODEL), F32),
            pltpu.VMEM((D_MODEL, D_EXPERT), BF16), pltpu.VMEM((D_MODEL, D_EXPERT), BF16),
            pltpu.VMEM((D_EXPERT, D_MODEL), BF16),
            pltpu.SemaphoreType.DMA((2, 3)),
        ],
    )
    return pl.pallas_call(
        functools.partial(_expert_kernel, layer),
        grid_spec=grid_spec,
        out_shape=jax.ShapeDtypeStruct((n_blocks * rows, HALF), U32),
        compiler_params=_params("arbitrary"),
        name="moe_experts",
    )(block_e, nused, nvalid, first.astype(jnp.int32), (run % 2).astype(jnp.int32), ahead, run_e[:2],
      xs, w_gate, w_up, w_down)


def _sc_gather_kernel(table_hbm, idx_hbm, out_hbm, idx_v, rows_v, gsem, wsem):
    per_worker = idx_v.shape[0]
    base = _sc_worker() * per_worker
    pltpu.sync_copy(idx_hbm.at[pl.ds(base, per_worker)], idx_v)

    @pl.loop(0, per_worker // SC_CHUNK, step=SC_BUFS)
    def _(c):
        offs = [(c + n) * SC_CHUNK for n in range(SC_BUFS)]
        gathers = [pltpu.async_copy(table_hbm.at[idx_v.at[pl.ds(offs[n], SC_CHUNK)]], rows_v.at[n], gsem.at[n])
                   for n in range(SC_BUFS)]
        writes = []
        for n in range(SC_BUFS):
            gathers[n].wait()
            writes.append(pltpu.async_copy(rows_v.at[n], out_hbm.at[pl.ds(base + offs[n], SC_CHUNK)], wsem.at[n]))
        for write in writes:
            write.wait()


def _sc_gather_rows(table, idx):
    n = idx.shape[0]
    workers = SC_CORES * SC_SUBCORES
    assert n % (workers * SC_CHUNK * SC_BUFS) == 0
    return pl.kernel(
        _sc_gather_kernel,
        out_type=jax.ShapeDtypeStruct((n, HALF), U32),
        mesh=_sc_mesh(),
        scratch_types=[pltpu.VMEM((n // workers,), jnp.int32), pltpu.VMEM((SC_BUFS, SC_CHUNK, HALF), U32),
                       pltpu.SemaphoreType.DMA((SC_BUFS,)), pltpu.SemaphoreType.DMA((SC_BUFS,))],
        name="moe_gather_sc",
    )(table, idx)


def _combine_kernel(x_ref, w_ref, *refs):
    y_refs, (wsg_ref, wsu_ref, wsd_ref, g_ref, b_ref, o_ref) = refs[:TOP_K], refs[TOP_K:]
    x = x_ref[...]
    xb = x.astype(BF16)
    g = _dot(xb, wsg_ref[...])
    u = _dot(xb, wsu_ref[...])
    shared = _dot((g * _sigmoid(g) * u).astype(BF16), wsd_ref[...])
    w = w_ref[...]
    lo, hi = (h * w[:, 0:1] for h in _unpack_halves(y_refs[0][...]))
    for k in range(1, TOP_K):
        lo_k, hi_k = _unpack_halves(y_refs[k][...])
        lo = lo + lo_k * w[:, k:k + 1]
        hi = hi + hi_k * w[:, k:k + 1]
    routed = jnp.concatenate([lo, hi], axis=1)
    o_ref[...] = _layer_norm(ALPHA * x + (shared + routed), g_ref[...], b_ref[...])


def _combine(x1, y8, w_tk, wp):
    t = x1.shape[0]
    tm = WIDE_TILE
    nt = t // tm
    weights = [wp['wsg'], wp['wsu'], wp['wsd'], wp['ln2_g'], wp['ln2_b']]
    y_specs = [pl.BlockSpec((tm, HALF), lambda i, k=k: (k * nt + i, 0)) for k in range(TOP_K)]
    return pl.pallas_call(
        _combine_kernel,
        grid=(nt,),
        in_specs=[pl.BlockSpec((tm, D_MODEL), lambda i: (i, 0)), pl.BlockSpec((tm, TOP_K), lambda i: (i, 0))]
                 + y_specs + [_full(w) for w in weights],
        out_specs=pl.BlockSpec((tm, D_MODEL), lambda i: (i, 0)),
        out_shape=jax.ShapeDtypeStruct((t, D_MODEL), F32),
        compiler_params=_params("parallel"),
        name="moe_combine",
    )(x1, w_tk, *([y8] * TOP_K), *weights)


def _moe(x1, x1p, wp, w_e_gate, w_e_up, w_e_down, layer):
    t = x1.shape[0]
    rows = EXPERT_BLOCK
    n_blocks = -(-(t * TOP_K + N_EXPERTS * (rows - 1)) // (rows * EXPERT_STEP)) * EXPERT_STEP
    idx3, rank3, top_w, cnt = _router(x1, wp)
    counts = cnt[:, 0].astype(jnp.int32)
    padded = (counts + rows - 1) // rows * rows
    padded_end = jnp.cumsum(padded)
    pstart = (padded_end - padded).astype(jnp.int32)
    nused = (padded_end[-1] // rows).astype(jnp.int32).reshape(1)
    blocks = jnp.arange(n_blocks, dtype=jnp.int32)
    block_row = jnp.minimum(blocks, nused[0] - 1) * rows
    block_e = jnp.sum((padded_end[None, :] <= block_row[:, None]).astype(jnp.int32), axis=1)
    block_e = jnp.minimum(block_e, N_EXPERTS - 1)
    nvalid = jnp.clip(counts[block_e] - (blocks * rows - pstart[block_e]), 0, rows).astype(jnp.int32)
    pos_kt = _positions(idx3, rank3, pstart)
    xs = _sc_dispatch(x1p, pos_kt, n_blocks * rows)
    ys = _experts(xs, block_e, nused, nvalid, counts, w_e_gate, w_e_up, w_e_down, layer, n_blocks)
    return _combine(x1, _sc_gather_rows(ys, pos_kt.reshape(TOP_K * t)), top_w.T, wp)


def _head_groups_t(w, used):
    k, h, _ = w.shape
    return jnp.pad(w, ((0, 0), (0, 0), (0, LANES - used))).reshape(k, h * LANES).T


def _prep_layer(l, w_in, b_gate, q_norm, w_uq, kv_norm, w_ukv, w_proj_a, w_proj_b, w_out, ln1_g, ln1_b,
                w_router, router_bias, w_s_gate, w_s_up, w_s_down, ln2_g, ln2_b):
    w = w_in[l]
    o = 0
    cols = {}
    for name, width in (('qa', WIDTH_A), ('ka', WIDTH_A), ('va', WIDTH_A), ('cq', Q_LORA_RANK),
                        ('ckv', KV_LORA_RANK), ('kr', QK_ROPE_DIM), ('ga', D_MODEL), ('gb', D_MODEL)):
        cols[name] = w[:, o:o + width]
        o += width
    wkr = jnp.zeros((D_MODEL, LANES), F32).at[:, QK_NOPE_DIM:QK_NOPE_DIM + QK_ROPE_DIM].set(cols['kr'])
    dqk = QK_NOPE_DIM + QK_ROPE_DIM
    wq = w_uq[l].reshape(Q_LORA_RANK, N_HEADS_B, dqk) * (dqk ** -0.5 * LOG2E)
    wkv = w_ukv[l].reshape(KV_LORA_RANK, N_HEADS_B, QK_NOPE_DIM + V_HEAD_DIM)
    wuk = jnp.pad(wkv[:, :, :QK_NOPE_DIM], ((0, 0), (0, 0), (0, LANES - QK_NOPE_DIM))).reshape(KV_LORA_RANK, N_HEADS_B * LANES)
    ones = jnp.zeros((N_HEADS_B, LANES), F32).at[:, V_HEAD_DIM].set(1.0).reshape(N_HEADS_B * LANES, 1)
    wr_t = w_router[l].T
    wr_hi = wr_t.astype(BF16)
    return dict(
        wqt=(cols['qa'] * HEAD_DIM_A ** -0.5).T.astype(BF16), wk=cols['ka'].astype(BF16),
        wvt=_head_groups_t(cols['va'].reshape(D_MODEL, N_HEADS_A, HEAD_DIM_A), HEAD_DIM_A).astype(BF16), ones=ones,
        wcq=cols['cq'].astype(BF16), wckv=cols['ckv'].astype(BF16), wkr=wkr.astype(BF16),
        wg=jnp.concatenate([cols['ga'], cols['gb']], axis=1).astype(BF16),
        bg=b_gate[l].reshape(1, 2 * D_MODEL), qn=q_norm[l].reshape(1, Q_LORA_RANK), kvn=kv_norm[l].reshape(1, KV_LORA_RANK),
        wuqt=_head_groups_t(wq, dqk).astype(BF16), wuk=wuk.astype(BF16),
        wuvt=_head_groups_t(wkv[:, :, QK_NOPE_DIM:], V_HEAD_DIM).astype(BF16),
        wpa=w_proj_a[l].astype(BF16), wpb=w_proj_b[l].astype(BF16), wo=w_out[l].astype(BF16),
        ln1_g=ln1_g[l].reshape(1, D_MODEL), ln1_b=ln1_b[l].reshape(1, D_MODEL),
        wr_hi=wr_hi, wr_lo=(wr_t - wr_hi.astype(F32)).astype(BF16), rbias=router_bias[l].reshape(N_EXPERTS, 1),
        wsg=w_s_gate[l].astype(BF16), wsu=w_s_up[l].astype(BF16), wsd=w_s_down[l].astype(BF16),
        ln2_g=ln2_g[l].reshape(1, D_MODEL), ln2_b=ln2_b[l].reshape(1, D_MODEL),
    )


def _rope_tables(seq):
    pos = jnp.arange(seq, dtype=F32)
    inv_freq = ROPE_THETA ** (-jnp.arange(0, QK_ROPE_DIM, 2, dtype=F32) / QK_ROPE_DIM)
    ang = pos[:, None] * inv_freq[None, :]
    cos, sin = jnp.cos(ang), jnp.sin(ang)
    half = QK_ROPE_DIM // 2
    z = lambda n: jnp.zeros((seq, n), F32)
    c = jnp.concatenate([jnp.ones((seq, QK_NOPE_DIM), F32), cos, cos, z(LANES - QK_NOPE_DIM - QK_ROPE_DIM)], axis=1)
    s1 = jnp.concatenate([z(QK_NOPE_DIM), -sin, z(LANES - QK_NOPE_DIM - half)], axis=1)
    s2 = jnp.concatenate([z(QK_NOPE_DIM + half), sin, z(LANES - QK_NOPE_DIM - QK_ROPE_DIM)], axis=1)
    return c, s1, s2, c.T, s1.T, s2.T


def _moba_key_table(seq, slopes):
    blk = jnp.arange(seq, dtype=jnp.int32) // MOBA_BLOCK
    onehot = (blk[:, None] == jnp.arange(BIAS_LO, dtype=jnp.int32)[None, :]).astype(F32)
    inblk = (jnp.arange(seq, dtype=jnp.int32) % MOBA_BLOCK).astype(F32)[:, None] * (slopes * LOG2E)[None, :]
    hi = inblk.astype(BF16)
    lo = (inblk - hi.astype(F32)).astype(BF16)
    pad = jnp.zeros((seq, LANES - POS_LO - N_HEADS_A), BF16)
    return jnp.concatenate([onehot.astype(BF16), onehot.astype(BF16), hi, lo, pad], axis=1)


def kernel(x, w_in, b_gate, q_norm, w_uq, kv_norm, w_ukv, w_proj_a, w_proj_b, w_out, ln1_g, ln1_b, w_router, router_bias, w_e_gate, w_e_up, w_e_down, w_s_gate, w_s_up, w_s_down, ln2_g, ln2_b):
    batch, seq, d = x.shape
    assert d == D_MODEL and seq % MOBA_BLOCK == 0 and MOBA_TOPK <= seq // MOBA_BLOCK <= BIAS_LO
    assert POS_LO + N_HEADS_A <= LANES and POS_HI + N_HEADS_A <= POS_LO and (batch * seq) % WIDE_TILE == 0
    tabs = _rope_tables(seq)
    slopes = jnp.asarray(np.exp2(-8.0 * (np.arange(N_HEADS_A) + 1.0) / N_HEADS_A), F32)
    etab = _moba_key_table(seq, slopes)
    h = x.reshape(batch * seq, d)
    for l in range(DEPTH):
        wp = _prep_layer(l, w_in, b_gate, q_norm, w_uq, kv_norm, w_ukv, w_proj_a, w_proj_b, w_out, ln1_g, ln1_b,
                         w_router, router_bias, w_s_gate, w_s_up, w_s_down, ln2_g, ln2_b)
        qat, ka, vat, kmean, qmt, km, vmt, sa, sb = _inproj(h, wp, tabs, seq)
        oa = _moba(qat, ka, vat, kmean, etab, slopes, batch, seq)
        ob = _mla(qmt, km, vmt, batch, seq)
        x1, x1p = _merge(h, oa, ob, sa, sb, wp)
        h = _moe(x1, x1p, wp, w_e_gate, w_e_up, w_e_down, l)
    return h.reshape(batch, seq, d)
```

```python
import functools

import numpy as np

import jax
import jax.numpy as jnp
from jax import lax
from jax.experimental import pallas as pl
from jax.experimental.pallas import tpu as pltpu
from jax.experimental.pallas import tpu_sc as plsc

D_MODEL = 1024
N_HEADS_A = 8
HEAD_DIM_A = 64
WIDTH_A = N_HEADS_A * HEAD_DIM_A
MOBA_BLOCK = 256
MOBA_TOPK = 3
N_HEADS_B = 8
QK_NOPE_DIM = 64
QK_ROPE_DIM = 32
V_HEAD_DIM = 64
Q_LORA_RANK = 384
KV_LORA_RANK = 256
WIDTH_B = N_HEADS_B * V_HEAD_DIM
ROPE_THETA = 10000.0
N_EXPERTS = 256
TOP_K = 8
N_GROUPS = 8
TOPK_GROUPS = 4
GROUP_SIZE = N_EXPERTS // N_GROUPS
D_EXPERT = 256
D_SHARED = 256
ROUTED_SCALE = 2.5
DEPTH = 2
ALPHA = (2 * DEPTH) ** 0.25
LN_EPS = 1e-5
RMS_EPS = 1e-6

LANES = 128
NEG = -1e30
LOG2E = float(np.log2(np.e))
ROW_TILE = 256
WIDE_TILE = 512
SHARED_TILE = 1024
ATT_TILE = 256
EXPERT_BLOCK = 256
EXPERT_STEP = 4
SC_CORES = 2
SC_SUBCORES = 16
SC_CHUNK = 64
SC_BUFS = 2
VMEM_LIMIT = 56 * 1024 * 1024
HALF = D_MODEL // 2
ATT_HEADS = 4
ATT_GROUP = 4
PV_ROWS = 80
BIAS_HI, BIAS_LO, POS_HI, POS_LO = 0, 32, 64, 72

BF16 = jnp.bfloat16
F32 = jnp.float32
U32 = jnp.uint32


def _dot(a, b):
    return jnp.dot(a, b, preferred_element_type=F32)


def _dot_nt(a, b):
    return lax.dot_general(a, b, (((1,), (1,)), ((), ())), preferred_element_type=F32)


def _sigmoid(x):
    return 1.0 / (1.0 + jnp.exp(-x))


def _layer_norm(y, g, b):
    mu = jnp.mean(y, axis=-1, keepdims=True)
    d = y - mu
    var = jnp.mean(d * d, axis=-1, keepdims=True)
    return d * lax.rsqrt(var + LN_EPS) * g + b


def _params(*sem):
    return pltpu.CompilerParams(dimension_semantics=sem, vmem_limit_bytes=VMEM_LIMIT)


def _full(a):
    return pl.BlockSpec(a.shape, lambda *_: (0,) * a.ndim)


def _inproj_kernel(x_ref, wqt_ref, wk_ref, wvt_ref, ones_ref, wcq_ref, wckv_ref, wkr_ref, wg_ref, bg_ref,
                   qn_ref, kvn_ref, wuqt_ref, wuk_ref, wuvt_ref, cos_ref, s1_ref, s2_ref, cost_ref, s1t_ref, s2t_ref,
                   qat_ref, ka_ref, vat_ref, kmean_ref, qmt_ref, km_ref, vmt_ref, sa_ref, sb_ref):
    xb = x_ref[...].astype(BF16)
    half = QK_ROPE_DIM // 2
    qat_ref[0] = _dot_nt(wqt_ref[...], xb).astype(BF16)
    k = _dot(xb, wk_ref[...])
    ka_ref[...] = k.astype(BF16)
    kmean_ref[0] = jnp.mean(k, axis=0, keepdims=True)
    vat_ref[0] = (_dot_nt(wvt_ref[...], xb) + ones_ref[...]).astype(BF16)

    cq = _dot(xb, wcq_ref[...])
    cqn = (cq * lax.rsqrt(jnp.mean(cq * cq, axis=-1, keepdims=True) + RMS_EPS) * qn_ref[...]).astype(BF16)
    ckv = _dot(xb, wckv_ref[...])
    ckvn = (ckv * lax.rsqrt(jnp.mean(ckv * ckv, axis=-1, keepdims=True) + RMS_EPS) * kvn_ref[...]).astype(BF16)
    qt = _dot_nt(wuqt_ref[...], cqn)
    ct, s1t, s2t = cost_ref[...], s1t_ref[...], s2t_ref[...]
    for h in range(N_HEADS_B):
        t = qt[h * LANES:(h + 1) * LANES, :]
        rot = t * ct + pltpu.roll(t, LANES - half, 0) * s1t + pltpu.roll(t, half, 0) * s2t
        qmt_ref[0, h * LANES:(h + 1) * LANES, :] = rot.astype(BF16)
    kn = _dot(ckvn, wuk_ref[...])
    kr = _dot(xb, wkr_ref[...])
    c, s1, s2 = cos_ref[...], s1_ref[...], s2_ref[...]
    krot = kr * c + pltpu.roll(kr, LANES - half, 1) * s1 + pltpu.roll(kr, half, 1) * s2
    for h in range(N_HEADS_B):
        sl = slice(h * LANES, (h + 1) * LANES)
        km_ref[:, sl] = (kn[:, sl] + krot).astype(BF16)
    vmt_ref[0] = (_dot_nt(wuvt_ref[...], ckvn) + ones_ref[...]).astype(BF16)

    sig = _sigmoid(_dot(xb, wg_ref[...]) + bg_ref[...])
    sa_ref[...] = sig[:, :D_MODEL].astype(BF16)
    sb_ref[...] = sig[:, D_MODEL:].astype(BF16)


def _inproj(x, wp, tabs, seq):
    t = x.shape[0]
    tm = ROW_TILE
    nt = t // tm
    npos = seq // tm
    row = lambda w: pl.BlockSpec((tm, w), lambda i: (i, 0))
    tile = lambda r: pl.BlockSpec((1, r, tm), lambda i: (i, 0, 0))
    tab = pl.BlockSpec((tm, LANES), lambda i: (i % npos, 0))
    tabt = pl.BlockSpec((LANES, tm), lambda i: (0, i % npos))
    weights = [wp['wqt'], wp['wk'], wp['wvt'], wp['ones'], wp['wcq'], wp['wckv'], wp['wkr'], wp['wg'], wp['bg'],
               wp['qn'], wp['kvn'], wp['wuqt'], wp['wuk'], wp['wuvt']]
    hl = N_HEADS_B * LANES
    out_shape = [
        jax.ShapeDtypeStruct((nt, WIDTH_A, tm), BF16), jax.ShapeDtypeStruct((t, WIDTH_A), BF16),
        jax.ShapeDtypeStruct((nt, N_HEADS_A * LANES, tm), BF16), jax.ShapeDtypeStruct((nt, 1, WIDTH_A), F32),
        jax.ShapeDtypeStruct((nt, hl, tm), BF16), jax.ShapeDtypeStruct((t, hl), BF16),
        jax.ShapeDtypeStruct((nt, hl, tm), BF16),
        jax.ShapeDtypeStruct((t, D_MODEL), BF16), jax.ShapeDtypeStruct((t, D_MODEL), BF16),
    ]
    out_specs = [tile(WIDTH_A), row(WIDTH_A), tile(N_HEADS_A * LANES),
                 pl.BlockSpec((1, 1, WIDTH_A), lambda i: (i, 0, 0)),
                 tile(hl), row(hl), tile(hl), row(D_MODEL), row(D_MODEL)]
    return pl.pallas_call(
        _inproj_kernel,
        grid=(nt,),
        in_specs=[row(D_MODEL)] + [_full(w) for w in weights] + [tab, tab, tab, tabt, tabt, tabt],
        out_specs=out_specs,
        out_shape=out_shape,
        compiler_params=_params("parallel"),
        name="inproj",
    )(x, *weights, *tabs)


def _attend_init(tq):
    return jnp.full((1, tq), -jnp.inf, F32), jnp.zeros((PV_ROWS, tq), F32)


def _col_max(s_ref):
    return [jnp.max(s_ref[hh], axis=0, keepdims=True) for hh in range(ATT_HEADS)]


def _attend_staged(cur_ref, cur_max, state, vts, nxt_ref=None, next_scores=None):
    heads = range(ATT_HEADS)
    if nxt_ref is not None:
        for hh in heads:
            nxt_ref[hh] = next_scores(hh)
    new_m, scaled, pvs = [], [], []
    for hh in heads:
        m_i, acc = state[hh]
        m_new = jnp.maximum(m_i, cur_max[hh])
        new_m.append(m_new)
        scaled.append(jnp.exp2(m_i - m_new) * acc)
        pvs.append(_dot(vts[hh], jnp.exp2(cur_ref[hh] - m_new).astype(BF16)))
    nxt_max = _col_max(nxt_ref) if nxt_ref is not None else cur_max
    return tuple((m, a + pv) for m, a, pv in zip(new_m, scaled, pvs)), nxt_max


def _attention_loop(i, scores, values, causal, tq, sa_ref, sb_ref):
    heads = range(ATT_HEADS)
    bufs = (sa_ref, sb_ref)

    def vals(s):
        tile = jnp.where(s == 0, i, s - 1)
        return [values(hh, tile) for hh in heads]

    def ahead(s):
        return lambda hh: scores(hh, jnp.maximum(jnp.minimum(s, i - 1), 0))

    for hh in heads:
        sa_ref[hh] = jnp.where(causal, scores(hh, i), NEG)
    carry = (tuple(_attend_init(tq) for _ in heads), _col_max(sa_ref))
    nslots = i + 1

    def trip(n, c):
        for g in range(ATT_GROUP):
            s = n * ATT_GROUP + g
            c = _attend_staged(bufs[g % 2], c[1], c[0], vals(s), bufs[(g + 1) % 2], ahead(s))
        return c

    carry = lax.fori_loop(0, nslots // ATT_GROUP, trip, carry)
    done = nslots // ATT_GROUP * ATT_GROUP
    for g in range(ATT_GROUP - 1):
        def step(c, g=g):
            if g == ATT_GROUP - 2:
                return _attend_staged(bufs[g % 2], c[1], c[0], vals(done + g))
            return _attend_staged(bufs[g % 2], c[1], c[0], vals(done + g), bufs[(g + 1) % 2], ahead(done + g))

        carry = lax.cond(nslots - done > g, step, lambda c: c, carry)
    outs = []
    for hh in heads:
        _, acc = carry[0][hh]
        outs.append(acc[:V_HEAD_DIM, :] / acc[V_HEAD_DIM:V_HEAD_DIM + 1, :])
    return jnp.concatenate(outs, axis=0).T


def _causal_t(tq):
    key = lax.broadcasted_iota(jnp.int32, (tq, tq), 0)
    qry = lax.broadcasted_iota(jnp.int32, (tq, tq), 1)
    return key <= qry


def _moba_kernel(slope_ref, qt_ref, k_ref, e_ref, vt_ref, km_ref, o_ref, sa_ref, sb_ref):
    tq = ATT_TILE
    hp = pl.program_id(1)
    i = pl.program_id(2)
    nblk = km_ref.shape[1]
    frow = lax.broadcasted_iota(jnp.int32, (LANES, tq), 0)
    row = lax.broadcasted_iota(jnp.int32, (BIAS_LO, tq), 0)
    prow = lax.broadcasted_iota(jnp.int32, (LANES - POS_HI, tq), 0) + POS_HI

    qaug = []
    for hh in range(ATT_HEADS):
        head = hp * ATT_HEADS + hh
        grp = slice(hh // 2 * LANES, (hh // 2 + 1) * LANES)
        qt2 = qt_ref[0, grp, :]
        head_rows = (frow >= hh % 2 * HEAD_DIM_A) & (frow < (hh % 2 + 1) * HEAD_DIM_A)
        qh = jnp.where(head_rows, qt2, jnp.zeros_like(qt2))
        kmean = km_ref[0, :, grp].astype(BF16)
        if nblk < BIAS_LO:
            kmean = jnp.concatenate([kmean, jnp.zeros((BIAS_LO - nblk, LANES), BF16)], axis=0)
        g = jnp.where(row < i, _dot(kmean, qh), -jnp.inf)
        picked = row == i
        for r in range(MOBA_TOPK):
            m = jnp.max(g, axis=0, keepdims=True)
            idx = jnp.min(jnp.where(g == m, row, BIAS_LO), axis=0, keepdims=True)
            hit = row == idx
            picked = picked | (hit & (r < i))
            g = jnp.where(hit, -jnp.inf, g)
        dist = (i - row).astype(F32) * (slope_ref[head] * (MOBA_BLOCK * LOG2E))
        b = jnp.where(picked, -dist, NEG)
        b_hi = b.astype(BF16)
        b_lo = (b - b_hi.astype(F32)).astype(BF16)
        ones = jnp.where((prow == POS_HI + head) | (prow == POS_LO + head), 1.0, 0.0).astype(BF16)
        qs = (qh.astype(F32) * LOG2E).astype(BF16)
        qaug.append(jnp.concatenate([qs, b_hi, b_lo, ones], axis=0))

    def scores(hh, j):
        rows = pl.ds(pl.multiple_of(j * tq, tq), tq)
        grp = slice(hh // 2 * LANES, (hh // 2 + 1) * LANES)
        return _dot(jnp.concatenate([k_ref[rows, grp], e_ref[rows, :]], axis=1), qaug[hh])

    def values(hh, j):
        return vt_ref[j, hh * LANES:hh * LANES + PV_ROWS, :]

    o_ref[...] = _attention_loop(i, scores, values, _causal_t(tq), tq, sa_ref, sb_ref).astype(BF16)


def _moba(qat, ka, vat, kmean, etab, slopes, batch, seq):
    t = ka.shape[0]
    tq = ATT_TILE
    nq = seq // tq
    nblk = seq // MOBA_BLOCK
    nh = ATT_HEADS
    km = kmean.reshape(batch, nblk, WIDTH_A)
    return pl.pallas_call(
        _moba_kernel,
        grid=(batch, N_HEADS_A // nh, nq),
        in_specs=[
            pl.BlockSpec(memory_space=pltpu.SMEM),
            pl.BlockSpec((1, nh * HEAD_DIM_A, tq), lambda b, h, i: (b * nq + i, h, 0)),
            pl.BlockSpec((seq, nh * HEAD_DIM_A), lambda b, h, i: (b, h)),
            pl.BlockSpec((seq, LANES), lambda b, h, i: (0, 0)),
            pl.BlockSpec((nq, nh * LANES, tq), lambda b, h, i: (b, h, 0)),
            pl.BlockSpec((1, nblk, nh * HEAD_DIM_A), lambda b, h, i: (b, 0, h)),
        ],
        out_specs=pl.BlockSpec((tq, nh * HEAD_DIM_A), lambda b, h, i: (b * nq + i, h)),
        out_shape=jax.ShapeDtypeStruct((t, WIDTH_A), BF16),
        scratch_shapes=[pltpu.VMEM((nh, tq, tq), F32), pltpu.VMEM((nh, tq, tq), F32)],
        compiler_params=_params("parallel", "parallel", "arbitrary"),
        name="moba",
    )(slopes, qat, ka, etab, vat, km)


def _mla_kernel(qt_ref, k_ref, vt_ref, o_ref, sa_ref, sb_ref):
    tq = ATT_TILE
    i = pl.program_id(2)
    qt = [qt_ref[0, hh * LANES:(hh + 1) * LANES, :] for hh in range(ATT_HEADS)]

    def scores(hh, j):
        return _dot(k_ref[pl.ds(pl.multiple_of(j * tq, tq), tq), hh * LANES:(hh + 1) * LANES], qt[hh])

    def values(hh, j):
        return vt_ref[j, hh * LANES:hh * LANES + PV_ROWS, :]

    o_ref[...] = _attention_loop(i, scores, values, _causal_t(tq), tq, sa_ref, sb_ref).astype(BF16)


def _mla(qmt, km, vmt, batch, seq):
    t = km.shape[0]
    tq = ATT_TILE
    nq = seq // tq
    nh = ATT_HEADS
    return pl.pallas_call(
        _mla_kernel,
        grid=(batch, N_HEADS_B // nh, nq),
        in_specs=[
            pl.BlockSpec((1, nh * LANES, tq), lambda b, h, i: (b * nq + i, h, 0)),
            pl.BlockSpec((seq, nh * LANES), lambda b, h, i: (b, h)),
            pl.BlockSpec((nq, nh * LANES, tq), lambda b, h, i: (b, h, 0)),
        ],
        out_specs=pl.BlockSpec((tq, nh * V_HEAD_DIM), lambda b, h, i: (b * nq + i, h)),
        out_shape=jax.ShapeDtypeStruct((t, WIDTH_B), BF16),
        scratch_shapes=[pltpu.VMEM((nh, tq, tq), F32), pltpu.VMEM((nh, tq, tq), F32)],
        compiler_params=_params("parallel", "parallel", "arbitrary"),
        name="mla",
    )(qmt, km, vmt)


def _pack_halves(y):
    lo = pltpu.bitcast(y[:, :HALF].astype(BF16).astype(F32), U32)
    hi = pltpu.bitcast(y[:, HALF:].astype(BF16).astype(F32), U32)
    return (hi & jnp.uint32(0xFFFF0000)) | (lo >> 16)


def _unpack_halves(w):
    return pltpu.bitcast(w << 16, F32), pltpu.bitcast(w & jnp.uint32(0xFFFF0000), F32)


def _merge_kernel(x_ref, oa_ref, ob_ref, sa_ref, sb_ref, wpa_ref, wpb_ref, wo_ref, g_ref, b_ref, o_ref, op_ref):
    pa = _dot(oa_ref[...], wpa_ref[...])
    pb = _dot(ob_ref[...], wpb_ref[...])
    merged = sa_ref[...].astype(F32) * pa + sb_ref[...].astype(F32) * pb
    hmix = _dot(merged.astype(BF16), wo_ref[...])
    y = _layer_norm(ALPHA * x_ref[...] + hmix, g_ref[...], b_ref[...])
    o_ref[...] = y
    op_ref[...] = _pack_halves(y)


def _merge(x, oa, ob, sa, sb, wp):
    t = x.shape[0]
    tm = WIDE_TILE
    row = lambda w: pl.BlockSpec((tm, w), lambda i: (i, 0))
    weights = [wp['wpa'], wp['wpb'], wp['wo'], wp['ln1_g'], wp['ln1_b']]
    return pl.pallas_call(
        _merge_kernel,
        grid=(t // tm,),
        in_specs=[row(D_MODEL), row(WIDTH_A), row(WIDTH_B), row(D_MODEL), row(D_MODEL)] + [_full(w) for w in weights],
        out_specs=[row(D_MODEL), row(HALF)],
        out_shape=[jax.ShapeDtypeStruct((t, D_MODEL), F32), jax.ShapeDtypeStruct((t, HALF), U32)],
        compiler_params=_params("parallel"),
        name="merge",
    )(x, oa, ob, sa, sb, *weights)


def _router_kernel(x_ref, wh_ref, wl_ref, rb_ref, idx_ref, rank_ref, w_ref, cnt_ref):
    tm = x_ref.shape[0]

    @pl.when(pl.program_id(0) == 0)
    def _():
        cnt_ref[...] = jnp.zeros_like(cnt_ref)

    x = x_ref[...]
    xh = x.astype(BF16)
    xl = (x - xh.astype(F32)).astype(BF16)
    wh, wl = wh_ref[...], wl_ref[...]
    logits = _dot_nt(wh, xh) + (_dot_nt(wh, xl) + _dot_nt(wl, xh))
    scores = _sigmoid(logits)
    choice = scores + rb_ref[...]
    row = lax.broadcasted_iota(jnp.int32, (GROUP_SIZE, tm), 0)
    groups = [choice[g * GROUP_SIZE:(g + 1) * GROUP_SIZE, :] for g in range(N_GROUPS)]
    gscore = []
    for blk in groups:
        m1 = jnp.max(blk, axis=0, keepdims=True)
        first = jnp.min(jnp.where(blk == m1, row, GROUP_SIZE), axis=0, keepdims=True)
        m2 = jnp.max(jnp.where(row == first, -jnp.inf, blk), axis=0, keepdims=True)
        gscore.append(m1 + m2)
    masked = []
    for g in range(N_GROUPS):
        ahead = jnp.zeros((1, tm), jnp.int32)
        for o in range(N_GROUPS):
            if o < g:
                ahead += (gscore[o] >= gscore[g]).astype(jnp.int32)
            elif o > g:
                ahead += (gscore[o] > gscore[g]).astype(jnp.int32)
        masked.append(jnp.where(ahead < TOPK_GROUPS, groups[g], -jnp.inf))
    cur = jnp.concatenate(masked, axis=0)
    erow = lax.broadcasted_iota(jnp.int32, (N_EXPERTS, tm), 0)
    hits, idxs, ws = [], [], []
    for _ in range(TOP_K):
        m = jnp.max(cur, axis=0, keepdims=True)
        e = jnp.min(jnp.where(cur == m, erow, N_EXPERTS), axis=0, keepdims=True)
        hit = erow == e
        hits.append(hit)
        idxs.append(e)
        ws.append(jnp.sum(jnp.where(hit, scores, 0.0), axis=0, keepdims=True))
        cur = jnp.where(hit, -jnp.inf, cur)
    total = ws[0]
    for w in ws[1:]:
        total = total + w
    member = hits[0]
    for hit in hits[1:]:
        member = member | hit
    member = jnp.where(member, 1.0, 0.0).astype(BF16)
    t_src = lax.broadcasted_iota(jnp.int32, (tm, tm), 0)
    t_dst = lax.broadcasted_iota(jnp.int32, (tm, tm), 1)
    before = _dot(member, jnp.where(t_src < t_dst, 1.0, 0.0).astype(BF16))
    base = cnt_ref[...]
    before = before + jnp.concatenate([base] * (tm // LANES), axis=1)
    cnt_ref[...] = base + _dot(member, jnp.ones((tm, LANES), BF16))
    for r in range(TOP_K):
        idx_ref[0, r:r + 1, :] = idxs[r]
        rank_ref[0, r:r + 1, :] = jnp.sum(jnp.where(hits[r], before, 0.0), axis=0, keepdims=True).astype(jnp.int32)
        w_ref[r:r + 1, :] = ws[r] / total * ROUTED_SCALE


def _router(x1, wp):
    t = x1.shape[0]
    tm = ROW_TILE
    nt = t // tm
    weights = [wp['wr_hi'], wp['wr_lo'], wp['rbias']]
    tile = pl.BlockSpec((1, TOP_K, tm), lambda i: (i, 0, 0))
    return pl.pallas_call(
        _router_kernel,
        grid=(nt,),
        in_specs=[pl.BlockSpec((tm, D_MODEL), lambda i: (i, 0))] + [_full(w) for w in weights],
        out_specs=[tile, tile, pl.BlockSpec((TOP_K, tm), lambda i: (0, i)),
                   pl.BlockSpec((N_EXPERTS, LANES), lambda i: (0, 0))],
        out_shape=[jax.ShapeDtypeStruct((nt, TOP_K, tm), jnp.int32), jax.ShapeDtypeStruct((nt, TOP_K, tm), jnp.int32),
                   jax.ShapeDtypeStruct((TOP_K, t), F32), jax.ShapeDtypeStruct((N_EXPERTS, LANES), F32)],
        compiler_params=_params("arbitrary"),
        name="router",
    )(x1, *weights)


def _pos_kernel(idx_ref, rank_ref, pq_ref, pos_ref):
    tm = idx_ref.shape[2]
    erow = lax.broadcasted_iota(jnp.int32, (N_EXPERTS, tm), 0)
    for k in range(TOP_K):
        onehot = jnp.where(erow == idx_ref[0, k:k + 1, :], 1.0, 0.0).astype(BF16)
        q = _dot(pq_ref[...], onehot)
        blk = (q[0:1, :] * 32.0 + q[1:2, :]).astype(jnp.int32)
        pos_ref[k:k + 1, :] = blk * EXPERT_BLOCK + rank_ref[0, k:k + 1, :]


def _positions(idx3, rank3, pstart):
    nt, _, tm = idx3.shape
    blk = pstart // EXPERT_BLOCK
    pq = jnp.zeros((8, N_EXPERTS), F32).at[0].set((blk // 32).astype(F32)).at[1].set((blk % 32).astype(F32))
    tile = pl.BlockSpec((1, TOP_K, tm), lambda i: (i, 0, 0))
    return pl.pallas_call(
        _pos_kernel,
        grid=(nt,),
        in_specs=[tile, tile, pl.BlockSpec((8, N_EXPERTS), lambda i: (0, 0))],
        out_specs=pl.BlockSpec((TOP_K, tm), lambda i: (0, i)),
        out_shape=jax.ShapeDtypeStruct((TOP_K, nt * tm), jnp.int32),
        compiler_params=_params("parallel"),
        name="moe_positions",
    )(idx3, rank3, pq.astype(BF16))


def _sc_mesh():
    return plsc.VectorSubcoreMesh(core_axis_name="c", subcore_axis_name="s", num_cores=SC_CORES,
                                  num_subcores=SC_SUBCORES)


def _sc_worker():
    return lax.axis_index("s") * SC_CORES + lax.axis_index("c")


def _sc_scatter_kernel(x_hbm, pos_hbm, out_hbm, idx_v, rows_v):
    nchunk = idx_v.shape[1]
    w = _sc_worker()
    pltpu.sync_copy(pos_hbm.at[w], idx_v)

    @pl.loop(0, nchunk)
    def _(c):
        pltpu.sync_copy(x_hbm.at[pl.ds((w * nchunk + c) * SC_CHUNK, SC_CHUNK)], rows_v)
        for k in range(TOP_K):
            pltpu.sync_copy(rows_v, out_hbm.at[idx_v.at[k, c]])


def _sc_dispatch(x1p, pos_kt, n_rows):
    t = x1p.shape[0]
    workers = SC_CORES * SC_SUBCORES
    nchunk = t // (workers * SC_CHUNK)
    assert t == workers * nchunk * SC_CHUNK
    pos4 = pos_kt.reshape(TOP_K, workers, nchunk, SC_CHUNK).transpose(1, 0, 2, 3)
    return pl.kernel(
        _sc_scatter_kernel,
        out_type=jax.ShapeDtypeStruct((n_rows, HALF), U32),
        mesh=_sc_mesh(),
        scratch_types=[pltpu.VMEM((TOP_K, nchunk, SC_CHUNK), jnp.int32), pltpu.VMEM((SC_CHUNK, HALF), U32)],
        name="moe_dispatch_sc",
    )(x1p, pos4)


def _expert_kernel(layer, be_ref, nused_ref, nvalid_ref, first_ref, slot_ref, ahead_ref, head_ref,
                   x_ref, wg_hbm, wu_hbm, wd_hbm, y_ref, wg_f, wu_f, wd_f, wg_b, wu_b, wd_b, sem):
    def fetch(e, slot):
        return [pltpu.make_async_copy(src.at[layer, e], dst.at[slot], sem.at[slot, n])
                for n, (src, dst) in enumerate(((wg_hbm, wg_f), (wu_hbm, wu_f), (wd_hbm, wd_f)))]

    @pl.when(pl.program_id(0) == 0)
    def _():
        for copy in fetch(head_ref[0], 0):
            copy.start()

        @pl.when(head_ref[1] >= 0)
        def _():
            for copy in fetch(head_ref[1], 1):
                copy.start()

    def one_block(b, rows):
        @pl.when((b < nused_ref[0]) & (first_ref[b] == 1))
        def _():
            slot = slot_ref[b]
            for copy in fetch(be_ref[b], slot):
                copy.wait()
            wg_b[...] = wg_f[slot].astype(BF16)
            wu_b[...] = wu_f[slot].astype(BF16)
            wd_b[...] = wd_f[slot].astype(BF16)

            @pl.when(ahead_ref[b] >= 0)
            def _():
                for copy in fetch(ahead_ref[b], slot):
                    copy.start()

        @pl.when(b < nused_ref[0])
        def _():
            live = lax.broadcasted_iota(jnp.int32, (EXPERT_BLOCK, HALF), 0) < nvalid_ref[b]
            xlo, xhi = (h.astype(BF16) for h in _unpack_halves(jnp.where(live, x_ref[rows, :], jnp.uint32(0))))
            g = _dot(xlo, wg_b[:HALF, :]) + _dot(xhi, wg_b[HALF:, :])
            u = _dot(xlo, wu_b[:HALF, :]) + _dot(xhi, wu_b[HALF:, :])
            a = (g * _sigmoid(g) * u).astype(BF16)
            y_ref[rows, :] = _pack_halves(_dot(a, wd_b[...]))

        @pl.when(b >= nused_ref[0])
        def _():
            y_ref[rows, :] = jnp.zeros((EXPERT_BLOCK, HALF), U32)

    for sub in range(EXPERT_STEP):
        one_block(pl.program_id(0) * EXPERT_STEP + sub, slice(sub * EXPERT_BLOCK, (sub + 1) * EXPERT_BLOCK))


def _experts(xs, block_e, nused, nvalid, counts, w_gate, w_up, w_down, layer, n_blocks):
    rows = EXPERT_BLOCK
    blocks = jnp.arange(n_blocks, dtype=jnp.int32)
    first = ((blocks == 0) | (block_e != jnp.roll(block_e, 1))) & (blocks < nused[0])
    run = jnp.cumsum(first.astype(jnp.int32)) - 1
    run_e = jnp.nonzero(counts > 0, size=N_EXPERTS, fill_value=-1)[0].astype(jnp.int32)
    ahead = jnp.concatenate([run_e, jnp.full((2,), -1, jnp.int32)])[jnp.clip(run, 0, N_EXPERTS - 1) + 2]
    step_rows = EXPERT_STEP * rows
    grid_spec = pltpu.PrefetchScalarGridSpec(
        num_scalar_prefetch=7,
        grid=(n_blocks // EXPERT_STEP,),
        in_specs=[
            pl.BlockSpec((step_rows, HALF), lambda s, be, nu, *_: (jnp.minimum(s, (nu[0] - 1) // EXPERT_STEP), 0)),
            pl.BlockSpec(memory_space=pl.ANY), pl.BlockSpec(memory_space=pl.ANY), pl.BlockSpec(memory_space=pl.ANY),
        ],
        out_specs=pl.BlockSpec((step_rows, HALF), lambda s, *_: (s, 0)),
        scratch_shapes=[
            pltpu.VMEM((2, D_MODEL, D_EXPERT), F32), pltpu.VMEM((2, D_MODEL, D_EXPERT), F32),
            pltpu.VMEM((2, D_EXPERT, D_MODEL), F32),
            pltpu.VMEM((D_MODEL, D_EXPERT), BF16), pltpu.VMEM((D_MODEL, D_EXPERT), BF16),
            pltpu.VMEM((D_EXPERT, D_MODEL), BF16),
            pltpu.SemaphoreType.DMA((2, 3)),
        ],
    )
    return pl.pallas_call(
        functools.partial(_expert_kernel, layer),
        grid_spec=grid_spec,
        out_shape=jax.ShapeDtypeStruct((n_blocks * rows, HALF), U32),
        compiler_params=_params("arbitrary"),
        name="moe_experts",
    )(block_e, nused, nvalid, first.astype(jnp.int32), (run % 2).astype(jnp.int32), ahead, run_e[:2],
      xs, w_gate, w_up, w_down)


def _sc_gather_kernel(table_hbm, idx_hbm, out_hbm, idx_v, rows_v, gsem, wsem):
    per_worker = idx_v.shape[0]
    base = _sc_worker() * per_worker
    pltpu.sync_copy(idx_hbm.at[pl.ds(base, per_worker)], idx_v)

    @pl.loop(0, per_worker // SC_CHUNK, step=SC_BUFS)
    def _(c):
        offs = [(c + n) * SC_CHUNK for n in range(SC_BUFS)]
        gathers = [pltpu.async_copy(table_hbm.at[idx_v.at[pl.ds(offs[n], SC_CHUNK)]], rows_v.at[n], gsem.at[n])
                   for n in range(SC_BUFS)]
        writes = []
        for n in range(SC_BUFS):
            gathers[n].wait()
            writes.append(pltpu.async_copy(rows_v.at[n], out_hbm.at[pl.ds(base + offs[n], SC_CHUNK)], wsem.at[n]))
        for write in writes:
            write.wait()


def _sc_gather_rows(table, idx):
    n = idx.shape[0]
    workers = SC_CORES * SC_SUBCORES
    assert n % (workers * SC_CHUNK * SC_BUFS) == 0
    return pl.kernel(
        _sc_gather_kernel,
        out_type=jax.ShapeDtypeStruct((n, HALF), U32),
        mesh=_sc_mesh(),
        scratch_types=[pltpu.VMEM((n // workers,), jnp.int32), pltpu.VMEM((SC_BUFS, SC_CHUNK, HALF), U32),
                       pltpu.SemaphoreType.DMA((SC_BUFS,)), pltpu.SemaphoreType.DMA((SC_BUFS,))],
        name="moe_gather_sc",
    )(table, idx)


def _shared_kernel(x_ref, wsg_ref, wsu_ref, wsd_ref, o_ref):
    x = x_ref[...]
    xb = x.astype(BF16)
    g = _dot(xb, wsg_ref[...])
    u = _dot(xb, wsu_ref[...])
    o_ref[...] = ALPHA * x + _dot((g * _sigmoid(g) * u).astype(BF16), wsd_ref[...])


def _shared(x1, wp):
    t = x1.shape[0]
    tm = SHARED_TILE
    row = pl.BlockSpec((tm, D_MODEL), lambda i: (i, 0))
    weights = [wp['wsg'], wp['wsu'], wp['wsd']]
    return pl.pallas_call(
        _shared_kernel,
        grid=(t // tm,),
        in_specs=[row] + [_full(w) for w in weights],
        out_specs=row,
        out_shape=jax.ShapeDtypeStruct((t, D_MODEL), F32),
        compiler_params=_params("parallel"),
        name="moe_shared",
    )(x1, *weights)


def _combine_kernel(base_ref, w_ref, *refs):
    y_refs, (g_ref, b_ref, o_ref) = refs[:TOP_K], refs[TOP_K:]
    w = w_ref[...]
    lo, hi = (h * w[:, 0:1] for h in _unpack_halves(y_refs[0][...]))
    for k in range(1, TOP_K):
        lo_k, hi_k = _unpack_halves(y_refs[k][...])
        lo = lo + lo_k * w[:, k:k + 1]
        hi = hi + hi_k * w[:, k:k + 1]
    routed = jnp.concatenate([lo, hi], axis=1)
    o_ref[...] = _layer_norm(base_ref[...] + routed, g_ref[...], b_ref[...])


def _combine(base, y8, w_tk, wp):
    t = base.shape[0]
    tm = WIDE_TILE
    nt = t // tm
    weights = [wp['ln2_g'], wp['ln2_b']]
    y_specs = [pl.BlockSpec((tm, HALF), lambda i, k=k: (k * nt + i, 0)) for k in range(TOP_K)]
    return pl.pallas_call(
        _combine_kernel,
        grid=(nt,),
        in_specs=[pl.BlockSpec((tm, D_MODEL), lambda i: (i, 0)), pl.BlockSpec((tm, TOP_K), lambda i: (i, 0))]
                 + y_specs + [_full(w) for w in weights],
        out_specs=pl.BlockSpec((tm, D_MODEL), lambda i: (i, 0)),
        out_shape=jax.ShapeDtypeStruct((t, D_MODEL), F32),
        compiler_params=_params("parallel"),
        name="moe_combine",
    )(base, w_tk, *([y8] * TOP_K), *weights)


def _moe(x1, x1p, wp, w_e_gate, w_e_up, w_e_down, layer):
    t = x1.shape[0]
    rows = EXPERT_BLOCK
    n_blocks = -(-(t * TOP_K + N_EXPERTS * (rows - 1)) // (rows * EXPERT_STEP)) * EXPERT_STEP
    idx3, rank3, top_w, cnt = _router(x1, wp)
    counts = cnt[:, 0].astype(jnp.int32)
    padded = (counts + rows - 1) // rows * rows
    padded_end = jnp.cumsum(padded)
    pstart = (padded_end - padded).astype(jnp.int32)
    nused = (padded_end[-1] // rows).astype(jnp.int32).reshape(1)
    blocks = jnp.arange(n_blocks, dtype=jnp.int32)
    block_row = jnp.minimum(blocks, nused[0] - 1) * rows
    block_e = jnp.sum((padded_end[None, :] <= block_row[:, None]).astype(jnp.int32), axis=1)
    block_e = jnp.minimum(block_e, N_EXPERTS - 1)
    nvalid = jnp.clip(counts[block_e] - (blocks * rows - pstart[block_e]), 0, rows).astype(jnp.int32)
    pos_kt = _positions(idx3, rank3, pstart)
    xs = _sc_dispatch(x1p, pos_kt, n_blocks * rows)
    base = _shared(x1, wp)
    ys = _experts(xs, block_e, nused, nvalid, counts, w_e_gate, w_e_up, w_e_down, layer, n_blocks)
    return _combine(base, _sc_gather_rows(ys, pos_kt.reshape(TOP_K * t)), top_w.T, wp)


def _head_groups_t(w, used):
    k, h, _ = w.shape
    return jnp.pad(w, ((0, 0), (0, 0), (0, LANES - used))).reshape(k, h * LANES).T


def _prep_layer(l, w_in, b_gate, q_norm, w_uq, kv_norm, w_ukv, w_proj_a, w_proj_b, w_out, ln1_g, ln1_b,
                w_router, router_bias, w_s_gate, w_s_up, w_s_down, ln2_g, ln2_b):
    w = w_in[l]
    o = 0
    cols = {}
    for name, width in (('qa', WIDTH_A), ('ka', WIDTH_A), ('va', WIDTH_A), ('cq', Q_LORA_RANK),
                        ('ckv', KV_LORA_RANK), ('kr', QK_ROPE_DIM), ('ga', D_MODEL), ('gb', D_MODEL)):
        cols[name] = w[:, o:o + width]
        o += width
    wkr = jnp.zeros((D_MODEL, LANES), F32).at[:, QK_NOPE_DIM:QK_NOPE_DIM + QK_ROPE_DIM].set(cols['kr'])
    dqk = QK_NOPE_DIM + QK_ROPE_DIM
    wq = w_uq[l].reshape(Q_LORA_RANK, N_HEADS_B, dqk) * (dqk ** -0.5 * LOG2E)
    wkv = w_ukv[l].reshape(KV_LORA_RANK, N_HEADS_B, QK_NOPE_DIM + V_HEAD_DIM)
    wuk = jnp.pad(wkv[:, :, :QK_NOPE_DIM], ((0, 0), (0, 0), (0, LANES - QK_NOPE_DIM))).reshape(KV_LORA_RANK, N_HEADS_B * LANES)
    ones = jnp.zeros((N_HEADS_B, LANES), F32).at[:, V_HEAD_DIM].set(1.0).reshape(N_HEADS_B * LANES, 1)
    wr_t = w_router[l].T
    wr_hi = wr_t.astype(BF16)
    return dict(
        wqt=(cols['qa'] * HEAD_DIM_A ** -0.5).T.astype(BF16), wk=cols['ka'].astype(BF16),
        wvt=_head_groups_t(cols['va'].reshape(D_MODEL, N_HEADS_A, HEAD_DIM_A), HEAD_DIM_A).astype(BF16), ones=ones,
        wcq=cols['cq'].astype(BF16), wckv=cols['ckv'].astype(BF16), wkr=wkr.astype(BF16),
        wg=jnp.concatenate([cols['ga'], cols['gb']], axis=1).astype(BF16),
        bg=b_gate[l].reshape(1, 2 * D_MODEL), qn=q_norm[l].reshape(1, Q_LORA_RANK), kvn=kv_norm[l].reshape(1, KV_LORA_RANK),
        wuqt=_head_groups_t(wq, dqk).astype(BF16), wuk=wuk.astype(BF16),
        wuvt=_head_groups_t(wkv[:, :, QK_NOPE_DIM:], V_HEAD_DIM).astype(BF16),
        wpa=w_proj_a[l].astype(BF16), wpb=w_proj_b[l].astype(BF16), wo=w_out[l].astype(BF16),
        ln1_g=ln1_g[l].reshape(1, D_MODEL), ln1_b=ln1_b[l].reshape(1, D_MODEL),
        wr_hi=wr_hi, wr_lo=(wr_t - wr_hi.astype(F32)).astype(BF16), rbias=router_bias[l].reshape(N_EXPERTS, 1),
        wsg=w_s_gate[l].astype(BF16), wsu=w_s_up[l].astype(BF16), wsd=w_s_down[l].astype(BF16),
        ln2_g=ln2_g[l].reshape(1, D_MODEL), ln2_b=ln2_b[l].reshape(1, D_MODEL),
    )


def _rope_tables(seq):
    pos = jnp.arange(seq, dtype=F32)
    inv_freq = ROPE_THETA ** (-jnp.arange(0, QK_ROPE_DIM, 2, dtype=F32) / QK_ROPE_DIM)
    ang = pos[:, None] * inv_freq[None, :]
    cos, sin = jnp.cos(ang), jnp.sin(ang)
    half = QK_ROPE_DIM // 2
    z = lambda n: jnp.zeros((seq, n), F32)
    c = jnp.concatenate([jnp.ones((seq, QK_NOPE_DIM), F32), cos, cos, z(LANES - QK_NOPE_DIM - QK_ROPE_DIM)], axis=1)
    s1 = jnp.concatenate([z(QK_NOPE_DIM), -sin, z(LANES - QK_NOPE_DIM - half)], axis=1)
    s2 = jnp.concatenate([z(QK_NOPE_DIM + half), sin, z(LANES - QK_NOPE_DIM - QK_ROPE_DIM)], axis=1)
    return c, s1, s2, c.T, s1.T, s2.T


def _moba_key_table(seq, slopes):
    blk = jnp.arange(seq, dtype=jnp.int32) // MOBA_BLOCK
    onehot = (blk[:, None] == jnp.arange(BIAS_LO, dtype=jnp.int32)[None, :]).astype(F32)
    inblk = (jnp.arange(seq, dtype=jnp.int32) % MOBA_BLOCK).astype(F32)[:, None] * (slopes * LOG2E)[None, :]
    hi = inblk.astype(BF16)
    lo = (inblk - hi.astype(F32)).astype(BF16)
    pad = jnp.zeros((seq, LANES - POS_LO - N_HEADS_A), BF16)
    return jnp.concatenate([onehot.astype(BF16), onehot.astype(BF16), hi, lo, pad], axis=1)


def kernel(x, w_in, b_gate, q_norm, w_uq, kv_norm, w_ukv, w_proj_a, w_proj_b, w_out, ln1_g, ln1_b, w_router, router_bias, w_e_gate, w_e_up, w_e_down, w_s_gate, w_s_up, w_s_down, ln2_g, ln2_b):
    batch, seq, d = x.shape
    assert d == D_MODEL and seq % MOBA_BLOCK == 0 and MOBA_TOPK <= seq // MOBA_BLOCK <= BIAS_LO
    assert POS_LO + N_HEADS_A <= LANES and POS_HI + N_HEADS_A <= POS_LO and (batch * seq) % WIDE_TILE == 0
    tabs = _rope_tables(seq)
    slopes = jnp.asarray(np.exp2(-8.0 * (np.arange(N_HEADS_A) + 1.0) / N_HEADS_A), F32)
    etab = _moba_key_table(seq, slopes)
    h = x.reshape(batch * seq, d)
    for l in range(DEPTH):
        wp = _prep_layer(l, w_in, b_gate, q_norm, w_uq, kv_norm, w_ukv, w_proj_a, w_proj_b, w_out, ln1_g, ln1_b,
                         w_router, router_bias, w_s_gate, w_s_up, w_s_down, ln2_g, ln2_b)
        qat, ka, vat, kmean, qmt, km, vmt, sa, sb = _inproj(h, wp, tabs, seq)
        oa = _moba(qat, ka, vat, kmean, etab, slopes, batch, seq)
        ob = _mla(qmt, km, vmt, batch, seq)
        x1, x1p = _merge(h, oa, ob, sa, sb, wp)
        h = _moe(x1, x1p, wp, w_e_gate, w_e_up, w_e_down, l)
    return h.reshape(batch, seq, d)
```

```python
import functools

import numpy as np

import jax
import jax.numpy as jnp
from jax import lax
from jax.experimental import pallas as pl
from jax.experimental.pallas import tpu as pltpu
from jax.experimental.pallas import tpu_sc as plsc

D_MODEL = 1024
N_HEADS_A = 8
HEAD_DIM_A = 64
WIDTH_A = N_HEADS_A * HEAD_DIM_A
MOBA_BLOCK = 256
MOBA_TOPK = 3
N_HEADS_B = 8
QK_NOPE_DIM = 64
QK_ROPE_DIM = 32
V_HEAD_DIM = 64
Q_LORA_RANK = 384
KV_LORA_RANK = 256
WIDTH_B = N_HEADS_B * V_HEAD_DIM
ROPE_THETA = 10000.0
N_EXPERTS = 256
TOP_K = 8
N_GROUPS = 8
TOPK_GROUPS = 4
GROUP_SIZE = N_EXPERTS // N_GROUPS
D_EXPERT = 256
D_SHARED = 256
ROUTED_SCALE = 2.5
DEPTH = 2
ALPHA = (2 * DEPTH) ** 0.25
LN_EPS = 1e-5
RMS_EPS = 1e-6

LANES = 128
NEG = -1e30
LOG2E = float(np.log2(np.e))
ROW_TILE = 256
WIDE_TILE = 512
MERGE_TILE = 1024
ATT_TILE = 256
EXPERT_BLOCK = 256
EXPERT_STEP = 4
SC_CORES = 2
SC_SUBCORES = 16
SC_CHUNK = 64
SC_BUFS = 2
VMEM_LIMIT = 56 * 1024 * 1024
HALF = D_MODEL // 2
ATT_HEADS = 4
ATT_GROUP = 4
PV_ROWS = 80
BIAS_HI, BIAS_LO, POS_HI, POS_LO = 0, 32, 64, 72

BF16 = jnp.bfloat16
F32 = jnp.float32
U32 = jnp.uint32


def _dot(a, b):
    return jnp.dot(a, b, preferred_element_type=F32)


def _dot_nt(a, b):
    return lax.dot_general(a, b, (((1,), (1,)), ((), ())), preferred_element_type=F32)


def _sigmoid(x):
    return 1.0 / (1.0 + jnp.exp(-x))


def _layer_norm(y, g, b):
    mu = jnp.mean(y, axis=-1, keepdims=True)
    d = y - mu
    var = jnp.mean(d * d, axis=-1, keepdims=True)
    return d * lax.rsqrt(var + LN_EPS) * g + b


def _params(*sem):
    return pltpu.CompilerParams(dimension_semantics=sem, vmem_limit_bytes=VMEM_LIMIT)


def _full(a):
    return pl.BlockSpec(a.shape, lambda *_: (0,) * a.ndim)


def _inproj_kernel(x_ref, wqt_ref, wk_ref, wvt_ref, ones_ref, wcq_ref, wckv_ref, wkr_ref, wg_ref, bg_ref,
                   qn_ref, kvn_ref, wuqt_ref, wuk_ref, wuvt_ref, cos_ref, s1_ref, s2_ref, cost_ref, s1t_ref, s2t_ref,
                   qat_ref, ka_ref, vat_ref, kmean_ref, qmt_ref, km_ref, vmt_ref, sa_ref, sb_ref):
    xb = x_ref[...].astype(BF16)
    half = QK_ROPE_DIM // 2
    qat_ref[0] = _dot_nt(wqt_ref[...], xb).astype(BF16)
    k = _dot(xb, wk_ref[...])
    ka_ref[...] = k.astype(BF16)
    kmean_ref[0] = jnp.mean(k, axis=0, keepdims=True)
    vat_ref[0] = (_dot_nt(wvt_ref[...], xb) + ones_ref[...]).astype(BF16)

    cq = _dot(xb, wcq_ref[...])
    cqn = (cq * lax.rsqrt(jnp.mean(cq * cq, axis=-1, keepdims=True) + RMS_EPS) * qn_ref[...]).astype(BF16)
    ckv = _dot(xb, wckv_ref[...])
    ckvn = (ckv * lax.rsqrt(jnp.mean(ckv * ckv, axis=-1, keepdims=True) + RMS_EPS) * kvn_ref[...]).astype(BF16)
    qt = _dot_nt(wuqt_ref[...], cqn)
    ct, s1t, s2t = cost_ref[...], s1t_ref[...], s2t_ref[...]
    for h in range(N_HEADS_B):
        t = qt[h * LANES:(h + 1) * LANES, :]
        rot = t * ct + pltpu.roll(t, LANES - half, 0) * s1t + pltpu.roll(t, half, 0) * s2t
        qmt_ref[0, h * LANES:(h + 1) * LANES, :] = rot.astype(BF16)
    kn = _dot(ckvn, wuk_ref[...])
    kr = _dot(xb, wkr_ref[...])
    c, s1, s2 = cos_ref[...], s1_ref[...], s2_ref[...]
    krot = kr * c + pltpu.roll(kr, LANES - half, 1) * s1 + pltpu.roll(kr, half, 1) * s2
    for h in range(N_HEADS_B):
        sl = slice(h * LANES, (h + 1) * LANES)
        km_ref[:, sl] = (kn[:, sl] + krot).astype(BF16)
    vmt_ref[0] = (_dot_nt(wuvt_ref[...], ckvn) + ones_ref[...]).astype(BF16)

    sig = _sigmoid(_dot(xb, wg_ref[...]) + bg_ref[...])
    sa_ref[...] = sig[:, :D_MODEL].astype(BF16)
    sb_ref[...] = sig[:, D_MODEL:].astype(BF16)


def _inproj(x, wp, tabs, seq):
    t = x.shape[0]
    tm = ROW_TILE
    nt = t // tm
    npos = seq // tm
    row = lambda w: pl.BlockSpec((tm, w), lambda i: (i, 0))
    tile = lambda r: pl.BlockSpec((1, r, tm), lambda i: (i, 0, 0))
    tab = pl.BlockSpec((tm, LANES), lambda i: (i % npos, 0))
    tabt = pl.BlockSpec((LANES, tm), lambda i: (0, i % npos))
    weights = [wp['wqt'], wp['wk'], wp['wvt'], wp['ones'], wp['wcq'], wp['wckv'], wp['wkr'], wp['wg'], wp['bg'],
               wp['qn'], wp['kvn'], wp['wuqt'], wp['wuk'], wp['wuvt']]
    hl = N_HEADS_B * LANES
    out_shape = [
        jax.ShapeDtypeStruct((nt, WIDTH_A, tm), BF16), jax.ShapeDtypeStruct((t, WIDTH_A), BF16),
        jax.ShapeDtypeStruct((nt, N_HEADS_A * LANES, tm), BF16), jax.ShapeDtypeStruct((nt, 1, WIDTH_A), F32),
        jax.ShapeDtypeStruct((nt, hl, tm), BF16), jax.ShapeDtypeStruct((t, hl), BF16),
        jax.ShapeDtypeStruct((nt, hl, tm), BF16),
        jax.ShapeDtypeStruct((t, D_MODEL), BF16), jax.ShapeDtypeStruct((t, D_MODEL), BF16),
    ]
    out_specs = [tile(WIDTH_A), row(WIDTH_A), tile(N_HEADS_A * LANES),
                 pl.BlockSpec((1, 1, WIDTH_A), lambda i: (i, 0, 0)),
                 tile(hl), row(hl), tile(hl), row(D_MODEL), row(D_MODEL)]
    return pl.pallas_call(
        _inproj_kernel,
        grid=(nt,),
        in_specs=[row(D_MODEL)] + [_full(w) for w in weights] + [tab, tab, tab, tabt, tabt, tabt],
        out_specs=out_specs,
        out_shape=out_shape,
        compiler_params=_params("parallel"),
        name="inproj",
    )(x, *weights, *tabs)


def _attend_init(tq):
    return jnp.full((1, tq), -jnp.inf, F32), jnp.zeros((PV_ROWS, tq), F32)


def _col_max(s_ref):
    return [jnp.max(s_ref[hh], axis=0, keepdims=True) for hh in range(ATT_HEADS)]


def _attend_staged(cur_ref, cur_max, state, vts, nxt_ref=None, next_scores=None):
    heads = range(ATT_HEADS)
    if nxt_ref is not None:
        for hh in heads:
            nxt_ref[hh] = next_scores(hh)
    new_m, scaled, pvs = [], [], []
    for hh in heads:
        m_i, acc = state[hh]
        m_new = jnp.maximum(m_i, cur_max[hh])
        new_m.append(m_new)
        scaled.append(jnp.exp2(m_i - m_new) * acc)
        pvs.append(_dot(vts[hh], jnp.exp2(cur_ref[hh] - m_new).astype(BF16)))
    nxt_max = _col_max(nxt_ref) if nxt_ref is not None else cur_max
    return tuple((m, a + pv) for m, a, pv in zip(new_m, scaled, pvs)), nxt_max


def _attention_loop(i, scores, values, causal, tq, sa_ref, sb_ref):
    heads = range(ATT_HEADS)
    bufs = (sa_ref, sb_ref)

    def vals(s):
        tile = jnp.where(s == 0, i, s - 1)
        return [values(hh, tile) for hh in heads]

    def ahead(s):
        return lambda hh: scores(hh, jnp.maximum(jnp.minimum(s, i - 1), 0))

    for hh in heads:
        sa_ref[hh] = jnp.where(causal, scores(hh, i), NEG)
    carry = (tuple(_attend_init(tq) for _ in heads), _col_max(sa_ref))
    nslots = i + 1

    def trip(n, c):
        for g in range(ATT_GROUP):
            s = n * ATT_GROUP + g
            c = _attend_staged(bufs[g % 2], c[1], c[0], vals(s), bufs[(g + 1) % 2], ahead(s))
        return c

    carry = lax.fori_loop(0, nslots // ATT_GROUP, trip, carry)
    done = nslots // ATT_GROUP * ATT_GROUP
    for g in range(ATT_GROUP - 1):
        def step(c, g=g):
            if g == ATT_GROUP - 2:
                return _attend_staged(bufs[g % 2], c[1], c[0], vals(done + g))
            return _attend_staged(bufs[g % 2], c[1], c[0], vals(done + g), bufs[(g + 1) % 2], ahead(done + g))

        carry = lax.cond(nslots - done > g, step, lambda c: c, carry)
    outs = []
    for hh in heads:
        _, acc = carry[0][hh]
        outs.append(acc[:V_HEAD_DIM, :] / acc[V_HEAD_DIM:V_HEAD_DIM + 1, :])
    return jnp.concatenate(outs, axis=0).T


def _causal_t(tq):
    key = lax.broadcasted_iota(jnp.int32, (tq, tq), 0)
    qry = lax.broadcasted_iota(jnp.int32, (tq, tq), 1)
    return key <= qry


def _moba_kernel(slope_ref, qt_ref, k_ref, e_ref, vt_ref, km_ref, o_ref, sa_ref, sb_ref):
    tq = ATT_TILE
    hp = pl.program_id(1)
    i = pl.program_id(2)
    nblk = km_ref.shape[1]
    frow = lax.broadcasted_iota(jnp.int32, (LANES, tq), 0)
    row = lax.broadcasted_iota(jnp.int32, (BIAS_LO, tq), 0)
    prow = lax.broadcasted_iota(jnp.int32, (LANES - POS_HI, tq), 0) + POS_HI

    qaug = []
    for hh in range(ATT_HEADS):
        head = hp * ATT_HEADS + hh
        grp = slice(hh // 2 * LANES, (hh // 2 + 1) * LANES)
        qt2 = qt_ref[0, grp, :]
        head_rows = (frow >= hh % 2 * HEAD_DIM_A) & (frow < (hh % 2 + 1) * HEAD_DIM_A)
        qh = jnp.where(head_rows, qt2, jnp.zeros_like(qt2))
        kmean = km_ref[0, :, grp].astype(BF16)
        if nblk < BIAS_LO:
            kmean = jnp.concatenate([kmean, jnp.zeros((BIAS_LO - nblk, LANES), BF16)], axis=0)
        g = jnp.where(row < i, _dot(kmean, qh), -jnp.inf)
        picked = row == i
        for r in range(MOBA_TOPK):
            m = jnp.max(g, axis=0, keepdims=True)
            idx = jnp.min(jnp.where(g == m, row, BIAS_LO), axis=0, keepdims=True)
            hit = row == idx
            picked = picked | (hit & (r < i))
            g = jnp.where(hit, -jnp.inf, g)
        dist = (i - row).astype(F32) * (slope_ref[head] * (MOBA_BLOCK * LOG2E))
        b = jnp.where(picked, -dist, NEG)
        b_hi = b.astype(BF16)
        b_lo = (b - b_hi.astype(F32)).astype(BF16)
        ones = jnp.where((prow == POS_HI + head) | (prow == POS_LO + head), 1.0, 0.0).astype(BF16)
        qs = (qh.astype(F32) * LOG2E).astype(BF16)
        qaug.append(jnp.concatenate([qs, b_hi, b_lo, ones], axis=0))

    def scores(hh, j):
        rows = pl.ds(pl.multiple_of(j * tq, tq), tq)
        grp = slice(hh // 2 * LANES, (hh // 2 + 1) * LANES)
        return _dot(jnp.concatenate([k_ref[rows, grp], e_ref[rows, :]], axis=1), qaug[hh])

    def values(hh, j):
        return vt_ref[j, hh * LANES:hh * LANES + PV_ROWS, :]

    o_ref[...] = _attention_loop(i, scores, values, _causal_t(tq), tq, sa_ref, sb_ref).astype(BF16)


def _moba(qat, ka, vat, kmean, etab, slopes, batch, seq):
    t = ka.shape[0]
    tq = ATT_TILE
    nq = seq // tq
    nblk = seq // MOBA_BLOCK
    nh = ATT_HEADS
    km = kmean.reshape(batch, nblk, WIDTH_A)
    return pl.pallas_call(
        _moba_kernel,
        grid=(batch, N_HEADS_A // nh, nq),
        in_specs=[
            pl.BlockSpec(memory_space=pltpu.SMEM),
            pl.BlockSpec((1, nh * HEAD_DIM_A, tq), lambda b, h, i: (b * nq + i, h, 0)),
            pl.BlockSpec((seq, nh * HEAD_DIM_A), lambda b, h, i: (b, h)),
            pl.BlockSpec((seq, LANES), lambda b, h, i: (0, 0)),
            pl.BlockSpec((nq, nh * LANES, tq), lambda b, h, i: (b, h, 0)),
            pl.BlockSpec((1, nblk, nh * HEAD_DIM_A), lambda b, h, i: (b, 0, h)),
        ],
        out_specs=pl.BlockSpec((tq, nh * HEAD_DIM_A), lambda b, h, i: (b * nq + i, h)),
        out_shape=jax.ShapeDtypeStruct((t, WIDTH_A), BF16),
        scratch_shapes=[pltpu.VMEM((nh, tq, tq), F32), pltpu.VMEM((nh, tq, tq), F32)],
        compiler_params=_params("parallel", "parallel", "arbitrary"),
        name="moba",
    )(slopes, qat, ka, etab, vat, km)


def _mla_kernel(qt_ref, k_ref, vt_ref, o_ref, sa_ref, sb_ref):
    tq = ATT_TILE
    i = pl.program_id(2)
    qt = [qt_ref[0, hh * LANES:(hh + 1) * LANES, :] for hh in range(ATT_HEADS)]

    def scores(hh, j):
        return _dot(k_ref[pl.ds(pl.multiple_of(j * tq, tq), tq), hh * LANES:(hh + 1) * LANES], qt[hh])

    def values(hh, j):
        return vt_ref[j, hh * LANES:hh * LANES + PV_ROWS, :]

    o_ref[...] = _attention_loop(i, scores, values, _causal_t(tq), tq, sa_ref, sb_ref).astype(BF16)


def _mla(qmt, km, vmt, batch, seq):
    t = km.shape[0]
    tq = ATT_TILE
    nq = seq // tq
    nh = ATT_HEADS
    return pl.pallas_call(
        _mla_kernel,
        grid=(batch, N_HEADS_B // nh, nq),
        in_specs=[
            pl.BlockSpec((1, nh * LANES, tq), lambda b, h, i: (b * nq + i, h, 0)),
            pl.BlockSpec((seq, nh * LANES), lambda b, h, i: (b, h)),
            pl.BlockSpec((nq, nh * LANES, tq), lambda b, h, i: (b, h, 0)),
        ],
        out_specs=pl.BlockSpec((tq, nh * V_HEAD_DIM), lambda b, h, i: (b * nq + i, h)),
        out_shape=jax.ShapeDtypeStruct((t, WIDTH_B), BF16),
        scratch_shapes=[pltpu.VMEM((nh, tq, tq), F32), pltpu.VMEM((nh, tq, tq), F32)],
        compiler_params=_params("parallel", "parallel", "arbitrary"),
        name="mla",
    )(qmt, km, vmt)


def _pack_halves(y):
    lo = pltpu.bitcast(y[:, :HALF].astype(BF16).astype(F32), U32)
    hi = pltpu.bitcast(y[:, HALF:].astype(BF16).astype(F32), U32)
    return (hi & jnp.uint32(0xFFFF0000)) | (lo >> 16)


def _unpack_halves(w):
    return pltpu.bitcast(w << 16, F32), pltpu.bitcast(w & jnp.uint32(0xFFFF0000), F32)


def _merge_kernel(x_ref, oa_ref, ob_ref, sa_ref, sb_ref, wpa_ref, wpb_ref, wo_ref, g_ref, b_ref, o_ref, op_ref):
    pa = _dot(oa_ref[...], wpa_ref[...])
    pb = _dot(ob_ref[...], wpb_ref[...])
    merged = sa_ref[...].astype(F32) * pa + sb_ref[...].astype(F32) * pb
    hmix = _dot(merged.astype(BF16), wo_ref[...])
    y = _layer_norm(ALPHA * x_ref[...] + hmix, g_ref[...], b_ref[...])
    o_ref[...] = y
    op_ref[...] = _pack_halves(y)


def _merge(x, oa, ob, sa, sb, wp):
    t = x.shape[0]
    tm = MERGE_TILE
    row = lambda w: pl.BlockSpec((tm, w), lambda i: (i, 0))
    weights = [wp['wpa'], wp['wpb'], wp['wo'], wp['ln1_g'], wp['ln1_b']]
    return pl.pallas_call(
        _merge_kernel,
        grid=(t // tm,),
        in_specs=[row(D_MODEL), row(WIDTH_A), row(WIDTH_B), row(D_MODEL), row(D_MODEL)] + [_full(w) for w in weights],
        out_specs=[row(D_MODEL), row(HALF)],
        out_shape=[jax.ShapeDtypeStruct((t, D_MODEL), F32), jax.ShapeDtypeStruct((t, HALF), U32)],
        compiler_params=_params("parallel"),
        name="merge",
    )(x, oa, ob, sa, sb, *weights)


def _router_kernel(x_ref, wh_ref, wl_ref, rb_ref, idx_ref, rank_ref, w_ref, cnt_ref):
    tm = x_ref.shape[0]

    @pl.when(pl.program_id(0) == 0)
    def _():
        cnt_ref[...] = jnp.zeros_like(cnt_ref)

    x = x_ref[...]
    xh = x.astype(BF16)
    xl = (x - xh.astype(F32)).astype(BF16)
    wh, wl = wh_ref[...], wl_ref[...]
    logits = _dot_nt(wh, xh) + (_dot_nt(wh, xl) + _dot_nt(wl, xh))
    scores = _sigmoid(logits)
    choice = scores + rb_ref[...]
    row = lax.broadcasted_iota(jnp.int32, (GROUP_SIZE, tm), 0)
    groups = [choice[g * GROUP_SIZE:(g + 1) * GROUP_SIZE, :] for g in range(N_GROUPS)]
    gscore = []
    for blk in groups:
        m1 = jnp.max(blk, axis=0, keepdims=True)
        first = jnp.min(jnp.where(blk == m1, row, GROUP_SIZE), axis=0, keepdims=True)
        m2 = jnp.max(jnp.where(row == first, -jnp.inf, blk), axis=0, keepdims=True)
        gscore.append(m1 + m2)
    masked = []
    for g in range(N_GROUPS):
        ahead = jnp.zeros((1, tm), jnp.int32)
        for o in range(N_GROUPS):
            if o < g:
                ahead += (gscore[o] >= gscore[g]).astype(jnp.int32)
            elif o > g:
                ahead += (gscore[o] > gscore[g]).astype(jnp.int32)
        masked.append(jnp.where(ahead < TOPK_GROUPS, groups[g], -jnp.inf))
    cur = jnp.concatenate(masked, axis=0)
    erow = lax.broadcasted_iota(jnp.int32, (N_EXPERTS, tm), 0)
    hits, idxs, ws = [], [], []
    for _ in range(TOP_K):
        m = jnp.max(cur, axis=0, keepdims=True)
        e = jnp.min(jnp.where(cur == m, erow, N_EXPERTS), axis=0, keepdims=True)
        hit = erow == e
        hits.append(hit)
        idxs.append(e)
        ws.append(jnp.sum(jnp.where(hit, scores, 0.0), axis=0, keepdims=True))
        cur = jnp.where(hit, -jnp.inf, cur)
    total = ws[0]
    for w in ws[1:]:
        total = total + w
    member = hits[0]
    for hit in hits[1:]:
        member = member | hit
    member = jnp.where(member, 1.0, 0.0).astype(BF16)
    t_src = lax.broadcasted_iota(jnp.int32, (tm, tm), 0)
    t_dst = lax.broadcasted_iota(jnp.int32, (tm, tm), 1)
    before = _dot(member, jnp.where(t_src < t_dst, 1.0, 0.0).astype(BF16))
    base = cnt_ref[...]
    before = before + jnp.concatenate([base] * (tm // LANES), axis=1)
    cnt_ref[...] = base + _dot(member, jnp.ones((tm, LANES), BF16))
    for r in range(TOP_K):
        idx_ref[0, r:r + 1, :] = idxs[r]
        rank_ref[0, r:r + 1, :] = jnp.sum(jnp.where(hits[r], before, 0.0), axis=0, keepdims=True).astype(jnp.int32)
        w_ref[r:r + 1, :] = ws[r] / total * ROUTED_SCALE


def _router(x1, wp):
    t = x1.shape[0]
    tm = ROW_TILE
    nt = t // tm
    weights = [wp['wr_hi'], wp['wr_lo'], wp['rbias']]
    tile = pl.BlockSpec((1, TOP_K, tm), lambda i: (i, 0, 0))
    return pl.pallas_call(
        _router_kernel,
        grid=(nt,),
        in_specs=[pl.BlockSpec((tm, D_MODEL), lambda i: (i, 0))] + [_full(w) for w in weights],
        out_specs=[tile, tile, pl.BlockSpec((TOP_K, tm), lambda i: (0, i)),
                   pl.BlockSpec((N_EXPERTS, LANES), lambda i: (0, 0))],
        out_shape=[jax.ShapeDtypeStruct((nt, TOP_K, tm), jnp.int32), jax.ShapeDtypeStruct((nt, TOP_K, tm), jnp.int32),
                   jax.ShapeDtypeStruct((TOP_K, t), F32), jax.ShapeDtypeStruct((N_EXPERTS, LANES), F32)],
        compiler_params=_params("arbitrary"),
        name="router",
    )(x1, *weights)


def _pos_kernel(idx_ref, rank_ref, pq_ref, pos_ref):
    tm = idx_ref.shape[2]
    erow = lax.broadcasted_iota(jnp.int32, (N_EXPERTS, tm), 0)
    for k in range(TOP_K):
        onehot = jnp.where(erow == idx_ref[0, k:k + 1, :], 1.0, 0.0).astype(BF16)
        q = _dot(pq_ref[...], onehot)
        blk = (q[0:1, :] * 32.0 + q[1:2, :]).astype(jnp.int32)
        pos_ref[k:k + 1, :] = blk * EXPERT_BLOCK + rank_ref[0, k:k + 1, :]


def _positions(idx3, rank3, pstart):
    nt, _, tm = idx3.shape
    blk = pstart // EXPERT_BLOCK
    pq = jnp.concatenate([(blk // 32)[None], (blk % 32)[None], jnp.zeros((6, N_EXPERTS), jnp.int32)], axis=0).astype(F32)
    tile = pl.BlockSpec((1, TOP_K, tm), lambda i: (i, 0, 0))
    return pl.pallas_call(
        _pos_kernel,
        grid=(nt,),
        in_specs=[tile, tile, pl.BlockSpec((8, N_EXPERTS), lambda i: (0, 0))],
        out_specs=pl.BlockSpec((TOP_K, tm), lambda i: (0, i)),
        out_shape=jax.ShapeDtypeStruct((TOP_K, nt * tm), jnp.int32),
        compiler_params=_params("parallel"),
        name="moe_positions",
    )(idx3, rank3, pq.astype(BF16))


def _sc_mesh():
    return plsc.VectorSubcoreMesh(core_axis_name="c", subcore_axis_name="s", num_cores=SC_CORES,
                                  num_subcores=SC_SUBCORES)


def _sc_worker():
    return lax.axis_index("s") * SC_CORES + lax.axis_index("c")


def _sc_scatter_kernel(x_hbm, pos_hbm, out_hbm, idx_v, rows_v):
    nchunk = idx_v.shape[1]
    w = _sc_worker()
    pltpu.sync_copy(pos_hbm.at[w], idx_v)

    @pl.loop(0, nchunk)
    def _(c):
        pltpu.sync_copy(x_hbm.at[pl.ds((w * nchunk + c) * SC_CHUNK, SC_CHUNK)], rows_v)
        for k in range(TOP_K):
            pltpu.sync_copy(rows_v, out_hbm.at[idx_v.at[k, c]])


def _sc_dispatch(x1p, pos_kt, n_rows):
    t = x1p.shape[0]
    workers = SC_CORES * SC_SUBCORES
    nchunk = t // (workers * SC_CHUNK)
    assert t == workers * nchunk * SC_CHUNK
    pos4 = pos_kt.reshape(TOP_K, workers, nchunk, SC_CHUNK).transpose(1, 0, 2, 3)
    return pl.kernel(
        _sc_scatter_kernel,
        out_type=jax.ShapeDtypeStruct((n_rows, HALF), U32),
        mesh=_sc_mesh(),
        scratch_types=[pltpu.VMEM((TOP_K, nchunk, SC_CHUNK), jnp.int32), pltpu.VMEM((SC_CHUNK, HALF), U32)],
        name="moe_dispatch_sc",
    )(x1p, pos4)


def _expert_kernel(layer, be_ref, nused_ref, nvalid_ref, first_ref, slot_ref, ahead_ref, head_ref,
                   x_ref, wg_hbm, wu_hbm, wd_hbm, y_ref, wg_f, wu_f, wd_f, wg_b, wu_b, wd_b, sem):
    def fetch(e, slot):
        return [pltpu.make_async_copy(src.at[layer, e], dst.at[slot], sem.at[slot, n])
                for n, (src, dst) in enumerate(((wg_hbm, wg_f), (wu_hbm, wu_f), (wd_hbm, wd_f)))]

    @pl.when(pl.program_id(0) == 0)
    def _():
        for copy in fetch(head_ref[0], 0):
            copy.start()

        @pl.when(head_ref[1] >= 0)
        def _():
            for copy in fetch(head_ref[1], 1):
                copy.start()

    def one_block(b, rows):
        @pl.when((b < nused_ref[0]) & (first_ref[b] == 1))
        def _():
            slot = slot_ref[b]
            for copy in fetch(be_ref[b], slot):
                copy.wait()
            wg_b[...] = wg_f[slot].astype(BF16)
            wu_b[...] = wu_f[slot].astype(BF16)
            wd_b[...] = wd_f[slot].astype(BF16)

            @pl.when(ahead_ref[b] >= 0)
            def _():
                for copy in fetch(ahead_ref[b], slot):
                    copy.start()

        @pl.when(b < nused_ref[0])
        def _():
            live = lax.broadcasted_iota(jnp.int32, (EXPERT_BLOCK, HALF), 0) < nvalid_ref[b]
            xlo, xhi = (h.astype(BF16) for h in _unpack_halves(jnp.where(live, x_ref[rows, :], jnp.uint32(0))))
            g = _dot(xlo, wg_b[:HALF, :]) + _dot(xhi, wg_b[HALF:, :])
            u = _dot(xlo, wu_b[:HALF, :]) + _dot(xhi, wu_b[HALF:, :])
            a = (g * _sigmoid(g) * u).astype(BF16)
            y_ref[rows, :] = _pack_halves(_dot(a, wd_b[...]))

        @pl.when(b >= nused_ref[0])
        def _():
            y_ref[rows, :] = jnp.zeros((EXPERT_BLOCK, HALF), U32)

    for sub in range(EXPERT_STEP):
        one_block(pl.program_id(0) * EXPERT_STEP + sub, slice(sub * EXPERT_BLOCK, (sub + 1) * EXPERT_BLOCK))


def _experts(xs, block_e, nused, nvalid, counts, w_gate, w_up, w_down, layer, n_blocks):
    rows = EXPERT_BLOCK
    blocks = jnp.arange(n_blocks, dtype=jnp.int32)
    first = ((blocks == 0) | (block_e != jnp.roll(block_e, 1))) & (blocks < nused[0])
    run = jnp.cumsum(first.astype(jnp.int32)) - 1
    run_e = jnp.nonzero(counts > 0, size=N_EXPERTS, fill_value=-1)[0].astype(jnp.int32)
    ahead = jnp.concatenate([run_e, jnp.full((2,), -1, jnp.int32)])[jnp.clip(run, 0, N_EXPERTS - 1) + 2]
    step_rows = EXPERT_STEP * rows
    grid_spec = pltpu.PrefetchScalarGridSpec(
        num_scalar_prefetch=7,
        grid=(n_blocks // EXPERT_STEP,),
        in_specs=[
            pl.BlockSpec((step_rows, HALF), lambda s, be, nu, *_: (jnp.minimum(s, (nu[0] - 1) // EXPERT_STEP), 0)),
            pl.BlockSpec(memory_space=pl.ANY), pl.BlockSpec(memory_space=pl.ANY), pl.BlockSpec(memory_space=pl.ANY),
        ],
        out_specs=pl.BlockSpec((step_rows, HALF), lambda s, *_: (s, 0)),
        scratch_shapes=[
            pltpu.VMEM((2, D_MODEL, D_EXPERT), F32), pltpu.VMEM((2, D_MODEL, D_EXPERT), F32),
            pltpu.VMEM((2, D_EXPERT, D_MODEL), F32),
            pltpu.VMEM((D_MODEL, D_EXPERT), BF16), pltpu.VMEM((D_MODEL, D_EXPERT), BF16),
            pltpu.VMEM((D_EXPERT, D_MODEL), BF16),
            pltpu.SemaphoreType.DMA((2, 3)),
        ],
    )
    return pl.pallas_call(
        functools.partial(_expert_kernel, layer),
        grid_spec=grid_spec,
        out_shape=jax.ShapeDtypeStruct((n_blocks * rows, HALF), U32),
        compiler_params=_params("arbitrary"),
        name="moe_experts",
    )(block_e, nused, nvalid, first.astype(jnp.int32), (run % 2).astype(jnp.int32), ahead, run_e[:2],
      xs, w_gate, w_up, w_down)


def _sc_gather_kernel(table_hbm, idx_hbm, out_hbm, idx_v, rows_v, gsem, wsem):
    per_worker = idx_v.shape[0]
    base = _sc_worker() * per_worker
    pltpu.sync_copy(idx_hbm.at[pl.ds(base, per_worker)], idx_v)

    @pl.loop(0, per_worker // SC_CHUNK, step=SC_BUFS)
    def _(c):
        offs = [(c + n) * SC_CHUNK for n in range(SC_BUFS)]
        gathers = [pltpu.async_copy(table_hbm.at[idx_v.at[pl.ds(offs[n], SC_CHUNK)]], rows_v.at[n], gsem.at[n])
                   for n in range(SC_BUFS)]
        writes = []
        for n in range(SC_BUFS):
            gathers[n].wait()
            writes.append(pltpu.async_copy(rows_v.at[n], out_hbm.at[pl.ds(base + offs[n], SC_CHUNK)], wsem.at[n]))
        for write in writes:
            write.wait()


def _sc_gather_rows(table, idx):
    n = idx.shape[0]
    workers = SC_CORES * SC_SUBCORES
    assert n % (workers * SC_CHUNK * SC_BUFS) == 0
    return pl.kernel(
        _sc_gather_kernel,
        out_type=jax.ShapeDtypeStruct((n, HALF), U32),
        mesh=_sc_mesh(),
        scratch_types=[pltpu.VMEM((n // workers,), jnp.int32), pltpu.VMEM((SC_BUFS, SC_CHUNK, HALF), U32),
                       pltpu.SemaphoreType.DMA((SC_BUFS,)), pltpu.SemaphoreType.DMA((SC_BUFS,))],
        name="moe_gather_sc",
    )(table, idx)


def _combine_kernel(x_ref, w_ref, *refs):
    y_refs, (wsg_ref, wsu_ref, wsd_ref, g_ref, b_ref, o_ref) = refs[:TOP_K], refs[TOP_K:]
    x = x_ref[...]
    xb = x.astype(BF16)
    g = _dot(xb, wsg_ref[...])
    u = _dot(xb, wsu_ref[...])
    shared = _dot((g * _sigmoid(g) * u).astype(BF16), wsd_ref[...])
    w = w_ref[...]
    lo, hi = (h * w[:, 0:1] for h in _unpack_halves(y_refs[0][...]))
    for k in range(1, TOP_K):
        lo_k, hi_k = _unpack_halves(y_refs[k][...])
        lo = lo + lo_k * w[:, k:k + 1]
        hi = hi + hi_k * w[:, k:k + 1]
    routed = jnp.concatenate([lo, hi], axis=1)
    o_ref[...] = _layer_norm(ALPHA * x + (shared + routed), g_ref[...], b_ref[...])


def _combine(x1, y8, w_tk, wp):
    t = x1.shape[0]
    tm = WIDE_TILE
    nt = t // tm
    weights = [wp['wsg'], wp['wsu'], wp['wsd'], wp['ln2_g'], wp['ln2_b']]
    y_specs = [pl.BlockSpec((tm, HALF), lambda i, k=k: (k * nt + i, 0)) for k in range(TOP_K)]
    return pl.pallas_call(
        _combine_kernel,
        grid=(nt,),
        in_specs=[pl.BlockSpec((tm, D_MODEL), lambda i: (i, 0)), pl.BlockSpec((tm, TOP_K), lambda i: (i, 0))]
                 + y_specs + [_full(w) for w in weights],
        out_specs=pl.BlockSpec((tm, D_MODEL), lambda i: (i, 0)),
        out_shape=jax.ShapeDtypeStruct((t, D_MODEL), F32),
        compiler_params=_params("parallel"),
        name="moe_combine",
    )(x1, w_tk, *([y8] * TOP_K), *weights)


def _moe(x1, x1p, wp, w_e_gate, w_e_up, w_e_down, layer):
    t = x1.shape[0]
    rows = EXPERT_BLOCK
    n_blocks = -(-(t * TOP_K + N_EXPERTS * (rows - 1)) // (rows * EXPERT_STEP)) * EXPERT_STEP
    idx3, rank3, top_w, cnt = _router(x1, wp)
    counts = cnt[:, 0].astype(jnp.int32)
    padded = (counts + rows - 1) // rows * rows
    padded_end = jnp.cumsum(padded)
    pstart = (padded_end - padded).astype(jnp.int32)
    nused = (padded_end[-1] // rows).astype(jnp.int32).reshape(1)
    blocks = jnp.arange(n_blocks, dtype=jnp.int32)
    block_row = jnp.minimum(blocks, nused[0] - 1) * rows
    block_e = jnp.sum((padded_end[None, :] <= block_row[:, None]).astype(jnp.int32), axis=1)
    block_e = jnp.minimum(block_e, N_EXPERTS - 1)
    nvalid = jnp.clip(counts[block_e] - (blocks * rows - pstart[block_e]), 0, rows).astype(jnp.int32)
    pos_kt = _positions(idx3, rank3, pstart)
    xs = _sc_dispatch(x1p, pos_kt, n_blocks * rows)
    ys = _experts(xs, block_e, nused, nvalid, counts, w_e_gate, w_e_up, w_e_down, layer, n_blocks)
    return _combine(x1, _sc_gather_rows(ys, pos_kt.reshape(TOP_K * t)), top_w.T, wp)


def _head_groups_t(w, used):
    k, h, _ = w.shape
    return jnp.pad(w, ((0, 0), (0, 0), (0, LANES - used))).reshape(k, h * LANES).T


def _prep_layer(l, w_in, b_gate, q_norm, w_uq, kv_norm, w_ukv, w_proj_a, w_proj_b, w_out, ln1_g, ln1_b,
                w_router, router_bias, w_s_gate, w_s_up, w_s_down, ln2_g, ln2_b):
    w = w_in[l]
    o = 0
    cols = {}
    for name, width in (('qa', WIDTH_A), ('ka', WIDTH_A), ('va', WIDTH_A), ('cq', Q_LORA_RANK),
                        ('ckv', KV_LORA_RANK), ('kr', QK_ROPE_DIM), ('ga', D_MODEL), ('gb', D_MODEL)):
        cols[name] = w[:, o:o + width]
        o += width
    wkr = jnp.pad(cols['kr'], ((0, 0), (QK_NOPE_DIM, LANES - QK_NOPE_DIM - QK_ROPE_DIM)))
    dqk = QK_NOPE_DIM + QK_ROPE_DIM
    wq = w_uq[l].reshape(Q_LORA_RANK, N_HEADS_B, dqk) * (dqk ** -0.5 * LOG2E)
    wkv = w_ukv[l].reshape(KV_LORA_RANK, N_HEADS_B, QK_NOPE_DIM + V_HEAD_DIM)
    wuk = jnp.pad(wkv[:, :, :QK_NOPE_DIM], ((0, 0), (0, 0), (0, LANES - QK_NOPE_DIM))).reshape(KV_LORA_RANK, N_HEADS_B * LANES)
    ones = jnp.asarray((np.arange(N_HEADS_B * LANES) % LANES == V_HEAD_DIM).astype(np.float32).reshape(-1, 1))
    wr_t = w_router[l].T
    wr_hi = wr_t.astype(BF16)
    return dict(
        wqt=(cols['qa'] * HEAD_DIM_A ** -0.5).T.astype(BF16), wk=cols['ka'].astype(BF16),
        wvt=_head_groups_t(cols['va'].reshape(D_MODEL, N_HEADS_A, HEAD_DIM_A), HEAD_DIM_A).astype(BF16), ones=ones,
        wcq=cols['cq'].astype(BF16), wckv=cols['ckv'].astype(BF16), wkr=wkr.astype(BF16),
        wg=jnp.concatenate([cols['ga'], cols['gb']], axis=1).astype(BF16),
        bg=b_gate[l].reshape(1, 2 * D_MODEL), qn=q_norm[l].reshape(1, Q_LORA_RANK), kvn=kv_norm[l].reshape(1, KV_LORA_RANK),
        wuqt=_head_groups_t(wq, dqk).astype(BF16), wuk=wuk.astype(BF16),
        wuvt=_head_groups_t(wkv[:, :, QK_NOPE_DIM:], V_HEAD_DIM).astype(BF16),
        wpa=w_proj_a[l].astype(BF16), wpb=w_proj_b[l].astype(BF16), wo=w_out[l].astype(BF16),
        ln1_g=ln1_g[l].reshape(1, D_MODEL), ln1_b=ln1_b[l].reshape(1, D_MODEL),
        wr_hi=wr_hi, wr_lo=(wr_t - wr_hi.astype(F32)).astype(BF16), rbias=router_bias[l].reshape(N_EXPERTS, 1),
        wsg=w_s_gate[l].astype(BF16), wsu=w_s_up[l].astype(BF16), wsd=w_s_down[l].astype(BF16),
        ln2_g=ln2_g[l].reshape(1, D_MODEL), ln2_b=ln2_b[l].reshape(1, D_MODEL),
    )


def _rope_tables(seq):
    pos = jnp.arange(seq, dtype=F32)
    inv_freq = ROPE_THETA ** (-jnp.arange(0, QK_ROPE_DIM, 2, dtype=F32) / QK_ROPE_DIM)
    ang = pos[:, None] * inv_freq[None, :]
    cos, sin = jnp.cos(ang), jnp.sin(ang)
    half = QK_ROPE_DIM // 2
    z = lambda n: jnp.zeros((seq, n), F32)
    c = jnp.concatenate([jnp.ones((seq, QK_NOPE_DIM), F32), cos, cos, z(LANES - QK_NOPE_DIM - QK_ROPE_DIM)], axis=1)
    s1 = jnp.concatenate([z(QK_NOPE_DIM), -sin, z(LANES - QK_NOPE_DIM - half)], axis=1)
    s2 = jnp.concatenate([z(QK_NOPE_DIM + half), sin, z(LANES - QK_NOPE_DIM - QK_ROPE_DIM)], axis=1)
    return c, s1, s2, c.T, s1.T, s2.T


def _moba_key_table(seq, slopes):
    blk = jnp.arange(seq, dtype=jnp.int32) // MOBA_BLOCK
    onehot = (blk[:, None] == jnp.arange(BIAS_LO, dtype=jnp.int32)[None, :]).astype(F32)
    inblk = (jnp.arange(seq, dtype=jnp.int32) % MOBA_BLOCK).astype(F32)[:, None] * (slopes * LOG2E)[None, :]
    hi = inblk.astype(BF16)
    lo = (inblk - hi.astype(F32)).astype(BF16)
    pad = jnp.zeros((seq, LANES - POS_LO - N_HEADS_A), BF16)
    return jnp.concatenate([onehot.astype(BF16), onehot.astype(BF16), hi, lo, pad], axis=1)


def kernel(x, w_in, b_gate, q_norm, w_uq, kv_norm, w_ukv, w_proj_a, w_proj_b, w_out, ln1_g, ln1_b, w_router, router_bias, w_e_gate, w_e_up, w_e_down, w_s_gate, w_s_up, w_s_down, ln2_g, ln2_b):
    batch, seq, d = x.shape
    assert d == D_MODEL and seq % MOBA_BLOCK == 0 and MOBA_TOPK <= seq // MOBA_BLOCK <= BIAS_LO
    assert POS_LO + N_HEADS_A <= LANES and POS_HI + N_HEADS_A <= POS_LO and (batch * seq) % MERGE_TILE == 0
    tabs = _rope_tables(seq)
    slopes = jnp.asarray(np.exp2(-8.0 * (np.arange(N_HEADS_A) + 1.0) / N_HEADS_A), F32)
    etab = _moba_key_table(seq, slopes)
    h = x.reshape(batch * seq, d)
    for l in range(DEPTH):
        wp = _prep_layer(l, w_in, b_gate, q_norm, w_uq, kv_norm, w_ukv, w_proj_a, w_proj_b, w_out, ln1_g, ln1_b,
                         w_router, router_bias, w_s_gate, w_s_up, w_s_down, ln2_g, ln2_b)
        qat, ka, vat, kmean, qmt, km, vmt, sa, sb = _inproj(h, wp, tabs, seq)
        oa = _moba(qat, ka, vat, kmean, etab, slopes, batch, seq)
        ob = _mla(qmt, km, vmt, batch, seq)
        x1, x1p = _merge(h, oa, ob, sa, sb, wp)
        h = _moe(x1, x1p, wp, w_e_gate, w_e_up, w_e_down, l)
    return h.reshape(batch, seq, d)
```

```python
import functools

import numpy as np

import jax
import jax.numpy as jnp
from jax import lax
from jax.experimental import pallas as pl
from jax.experimental.pallas import tpu as pltpu
from jax.experimental.pallas import tpu_sc as plsc

D_MODEL = 1024
N_HEADS_A = 8
HEAD_DIM_A = 64
WIDTH_A = N_HEADS_A * HEAD_DIM_A
MOBA_BLOCK = 256
MOBA_TOPK = 3
N_HEADS_B = 8
QK_NOPE_DIM = 64
QK_ROPE_DIM = 32
V_HEAD_DIM = 64
Q_LORA_RANK = 384
KV_LORA_RANK = 256
WIDTH_B = N_HEADS_B * V_HEAD_DIM
ROPE_THETA = 10000.0
N_EXPERTS = 256
TOP_K = 8
N_GROUPS = 8
TOPK_GROUPS = 4
GROUP_SIZE = N_EXPERTS // N_GROUPS
D_EXPERT = 256
D_SHARED = 256
ROUTED_SCALE = 2.5
DEPTH = 2
ALPHA = (2 * DEPTH) ** 0.25
LN_EPS = 1e-5
RMS_EPS = 1e-6

LANES = 128
NEG = -1e30
LOG2E = float(np.log2(np.e))
ROW_TILE = 256
WIDE_TILE = 512
MERGE_TILE = 1024
ATT_TILE = 256
EXPERT_BLOCK = 256
EXPERT_STEP = 4
SC_CORES = 2
SC_SUBCORES = 16
SC_CHUNK = 64
SC_BUFS = 2
VMEM_LIMIT = 56 * 1024 * 1024
HALF = D_MODEL // 2
ATT_HEADS = 4
ATT_GROUP = 4
PV_ROWS = 80
BIAS_HI, BIAS_LO, POS_HI, POS_LO = 0, 32, 64, 72

BF16 = jnp.bfloat16
F32 = jnp.float32
U32 = jnp.uint32


def _dot(a, b):
    return jnp.dot(a, b, preferred_element_type=F32)


def _dot_nt(a, b):
    return lax.dot_general(a, b, (((1,), (1,)), ((), ())), preferred_element_type=F32)


def _sigmoid(x):
    return 1.0 / (1.0 + jnp.exp(-x))


def _layer_norm(y, g, b):
    mu = jnp.mean(y, axis=-1, keepdims=True)
    d = y - mu
    var = jnp.mean(d * d, axis=-1, keepdims=True)
    return d * lax.rsqrt(var + LN_EPS) * g + b


def _params(*sem):
    return pltpu.CompilerParams(dimension_semantics=sem, vmem_limit_bytes=VMEM_LIMIT)


def _full(a):
    return pl.BlockSpec(a.shape, lambda *_: (0,) * a.ndim)


def _inproj_kernel(x_ref, wqt_ref, wk_ref, wvt_ref, wcq_ref, wckv_ref, wkr_ref, wg_ref, bg_ref,
                   qn_ref, kvn_ref, wuqt_ref, wuk_ref, wuvt_ref, cos_ref, s1_ref, s2_ref, cost_ref, s1t_ref, s2t_ref,
                   qat_ref, ka_ref, vat_ref, kmean_ref, qmt_ref, km_ref, vmt_ref, sa_ref, sb_ref):
    xb = x_ref[...].astype(BF16)
    half = QK_ROPE_DIM // 2
    qat_ref[0] = _dot_nt(wqt_ref[...], xb).astype(BF16)
    k = _dot(xb, wk_ref[...])
    ka_ref[...] = k.astype(BF16)
    kmean_ref[0] = jnp.mean(k, axis=0, keepdims=True)
    vat_ref[0] = _dot_nt(wvt_ref[...], xb).astype(BF16)

    cq = _dot(xb, wcq_ref[...])
    cqn = (cq * lax.rsqrt(jnp.mean(cq * cq, axis=-1, keepdims=True) + RMS_EPS) * qn_ref[...]).astype(BF16)
    ckv = _dot(xb, wckv_ref[...])
    ckvn = (ckv * lax.rsqrt(jnp.mean(ckv * ckv, axis=-1, keepdims=True) + RMS_EPS) * kvn_ref[...]).astype(BF16)
    qt = _dot_nt(wuqt_ref[...], cqn)
    ct, s1t, s2t = cost_ref[...], s1t_ref[...], s2t_ref[...]
    for h in range(N_HEADS_B):
        t = qt[h * LANES:(h + 1) * LANES, :]
        rot = t * ct + pltpu.roll(t, LANES - half, 0) * s1t + pltpu.roll(t, half, 0) * s2t
        qmt_ref[0, h * LANES:(h + 1) * LANES, :] = rot.astype(BF16)
    kn = _dot(ckvn, wuk_ref[...])
    kr = _dot(xb, wkr_ref[...])
    c, s1, s2 = cos_ref[...], s1_ref[...], s2_ref[...]
    krot = kr * c + pltpu.roll(kr, LANES - half, 1) * s1 + pltpu.roll(kr, half, 1) * s2
    for h in range(N_HEADS_B):
        sl = slice(h * LANES, (h + 1) * LANES)
        km_ref[:, sl] = (kn[:, sl] + krot).astype(BF16)
    vmt_ref[0] = _dot_nt(wuvt_ref[...], ckvn).astype(BF16)

    sig = _sigmoid(_dot(xb, wg_ref[...]) + bg_ref[...])
    sa_ref[...] = sig[:, :D_MODEL].astype(BF16)
    sb_ref[...] = sig[:, D_MODEL:].astype(BF16)


def _inproj(x, wp, tabs, seq):
    t = x.shape[0]
    tm = ROW_TILE
    nt = t // tm
    npos = seq // tm
    row = lambda w: pl.BlockSpec((tm, w), lambda i: (i, 0))
    tile = lambda r: pl.BlockSpec((1, r, tm), lambda i: (i, 0, 0))
    tab = pl.BlockSpec((tm, LANES), lambda i: (i % npos, 0))
    tabt = pl.BlockSpec((LANES, tm), lambda i: (0, i % npos))
    weights = [wp['wqt'], wp['wk'], wp['wvt'], wp['wcq'], wp['wckv'], wp['wkr'], wp['wg'], wp['bg'],
               wp['qn'], wp['kvn'], wp['wuqt'], wp['wuk'], wp['wuvt']]
    hl = N_HEADS_B * LANES
    out_shape = [
        jax.ShapeDtypeStruct((nt, WIDTH_A, tm), BF16), jax.ShapeDtypeStruct((t, WIDTH_A), BF16),
        jax.ShapeDtypeStruct((nt, WIDTH_A, tm), BF16), jax.ShapeDtypeStruct((nt, 1, WIDTH_A), F32),
        jax.ShapeDtypeStruct((nt, hl, tm), BF16), jax.ShapeDtypeStruct((t, hl), BF16),
        jax.ShapeDtypeStruct((nt, WIDTH_B, tm), BF16),
        jax.ShapeDtypeStruct((t, D_MODEL), BF16), jax.ShapeDtypeStruct((t, D_MODEL), BF16),
    ]
    out_specs = [tile(WIDTH_A), row(WIDTH_A), tile(WIDTH_A),
                 pl.BlockSpec((1, 1, WIDTH_A), lambda i: (i, 0, 0)),
                 tile(hl), row(hl), tile(WIDTH_B), row(D_MODEL), row(D_MODEL)]
    return pl.pallas_call(
        _inproj_kernel,
        grid=(nt,),
        in_specs=[row(D_MODEL)] + [_full(w) for w in weights] + [tab, tab, tab, tabt, tabt, tabt],
        out_specs=out_specs,
        out_shape=out_shape,
        compiler_params=_params("parallel"),
        name="inproj",
    )(x, *weights, *tabs)


def _attend_init(tq):
    return jnp.full((1, tq), -jnp.inf, F32), jnp.zeros((PV_ROWS, tq), F32)


def _col_max(s_ref):
    return [jnp.max(s_ref[hh], axis=0, keepdims=True) for hh in range(ATT_HEADS)]


def _attend_staged(cur_ref, cur_max, state, vts, nxt_ref=None, next_scores=None):
    heads = range(ATT_HEADS)
    if nxt_ref is not None:
        for hh in heads:
            nxt_ref[hh] = next_scores(hh)
    new_m, scaled, pvs = [], [], []
    for hh in heads:
        m_i, acc = state[hh]
        m_new = jnp.maximum(m_i, cur_max[hh])
        new_m.append(m_new)
        scaled.append(jnp.exp2(m_i - m_new) * acc)
        pvs.append(_dot(vts[hh], jnp.exp2(cur_ref[hh] - m_new).astype(BF16)))
    nxt_max = _col_max(nxt_ref) if nxt_ref is not None else cur_max
    return tuple((m, a + pv) for m, a, pv in zip(new_m, scaled, pvs)), nxt_max


def _attention_loop(i, scores, values, causal, tq, sa_ref, sb_ref):
    heads = range(ATT_HEADS)
    bufs = (sa_ref, sb_ref)

    def vals(s):
        tile = jnp.where(s == 0, i, s - 1)
        return [values(hh, tile) for hh in heads]

    def ahead(s):
        return lambda hh: scores(hh, jnp.maximum(jnp.minimum(s, i - 1), 0))

    for hh in heads:
        sa_ref[hh] = jnp.where(causal, scores(hh, i), NEG)
    carry = (tuple(_attend_init(tq) for _ in heads), _col_max(sa_ref))
    nslots = i + 1

    def trip(n, c):
        for g in range(ATT_GROUP):
            s = n * ATT_GROUP + g
            c = _attend_staged(bufs[g % 2], c[1], c[0], vals(s), bufs[(g + 1) % 2], ahead(s))
        return c

    carry = lax.fori_loop(0, nslots // ATT_GROUP, trip, carry)
    done = nslots // ATT_GROUP * ATT_GROUP
    for g in range(ATT_GROUP - 1):
        def step(c, g=g):
            if g == ATT_GROUP - 2:
                return _attend_staged(bufs[g % 2], c[1], c[0], vals(done + g))
            return _attend_staged(bufs[g % 2], c[1], c[0], vals(done + g), bufs[(g + 1) % 2], ahead(done + g))

        carry = lax.cond(nslots - done > g, step, lambda c: c, carry)
    outs = []
    for hh in heads:
        _, acc = carry[0][hh]
        outs.append(acc[:V_HEAD_DIM, :] / acc[V_HEAD_DIM:V_HEAD_DIM + 1, :])
    return jnp.concatenate(outs, axis=0).T


def _values_tile(vt_ref, hh, j):
    tk = vt_ref.shape[2]
    pad_row = lax.broadcasted_iota(jnp.int32, (PV_ROWS - V_HEAD_DIM, tk), 0)
    ones = jnp.where(pad_row == 0, 1.0, 0.0).astype(BF16)
    return jnp.concatenate([vt_ref[j, hh * V_HEAD_DIM:(hh + 1) * V_HEAD_DIM, :], ones], axis=0)


def _causal_t(tq):
    key = lax.broadcasted_iota(jnp.int32, (tq, tq), 0)
    qry = lax.broadcasted_iota(jnp.int32, (tq, tq), 1)
    return key <= qry


def _moba_kernel(slope_ref, qt_ref, k_ref, e_ref, vt_ref, km_ref, o_ref, sa_ref, sb_ref):
    tq = ATT_TILE
    hp = pl.program_id(1)
    i = pl.program_id(2)
    nblk = km_ref.shape[1]
    frow = lax.broadcasted_iota(jnp.int32, (LANES, tq), 0)
    row = lax.broadcasted_iota(jnp.int32, (BIAS_LO, tq), 0)
    prow = lax.broadcasted_iota(jnp.int32, (LANES - POS_HI, tq), 0) + POS_HI

    qaug = []
    for hh in range(ATT_HEADS):
        head = hp * ATT_HEADS + hh
        grp = slice(hh // 2 * LANES, (hh // 2 + 1) * LANES)
        qt2 = qt_ref[0, grp, :]
        head_rows = (frow >= hh % 2 * HEAD_DIM_A) & (frow < (hh % 2 + 1) * HEAD_DIM_A)
        qh = jnp.where(head_rows, qt2, jnp.zeros_like(qt2))
        kmean = km_ref[0, :, grp].astype(BF16)
        if nblk < BIAS_LO:
            kmean = jnp.concatenate([kmean, jnp.zeros((BIAS_LO - nblk, LANES), BF16)], axis=0)
        g = jnp.where(row < i, _dot(kmean, qh), -jnp.inf)
        picked = row == i
        for r in range(MOBA_TOPK):
            m = jnp.max(g, axis=0, keepdims=True)
            idx = jnp.min(jnp.where(g == m, row, BIAS_LO), axis=0, keepdims=True)
            hit = row == idx
            picked = picked | (hit & (r < i))
            g = jnp.where(hit, -jnp.inf, g)
        dist = (i - row).astype(F32) * (slope_ref[head] * (MOBA_BLOCK * LOG2E))
        b = jnp.where(picked, -dist, NEG)
        b_hi = b.astype(BF16)
        b_lo = (b - b_hi.astype(F32)).astype(BF16)
        ones = jnp.where((prow == POS_HI + head) | (prow == POS_LO + head), 1.0, 0.0).astype(BF16)
        qs = (qh.astype(F32) * LOG2E).astype(BF16)
        qaug.append(jnp.concatenate([qs, b_hi, b_lo, ones], axis=0))

    def scores(hh, j):
        rows = pl.ds(pl.multiple_of(j * tq, tq), tq)
        grp = slice(hh // 2 * LANES, (hh // 2 + 1) * LANES)
        return _dot(jnp.concatenate([k_ref[rows, grp], e_ref[rows, :]], axis=1), qaug[hh])

    def values(hh, j):
        return _values_tile(vt_ref, hh, j)

    o_ref[...] = _attention_loop(i, scores, values, _causal_t(tq), tq, sa_ref, sb_ref).astype(BF16)


def _moba(qat, ka, vat, kmean, etab, slopes, batch, seq):
    t = ka.shape[0]
    tq = ATT_TILE
    nq = seq // tq
    nblk = seq // MOBA_BLOCK
    nh = ATT_HEADS
    km = kmean.reshape(batch, nblk, WIDTH_A)
    return pl.pallas_call(
        _moba_kernel,
        grid=(batch, N_HEADS_A // nh, nq),
        in_specs=[
            pl.BlockSpec(memory_space=pltpu.SMEM),
            pl.BlockSpec((1, nh * HEAD_DIM_A, tq), lambda b, h, i: (b * nq + i, h, 0)),
            pl.BlockSpec((seq, nh * HEAD_DIM_A), lambda b, h, i: (b, h)),
            pl.BlockSpec((seq, LANES), lambda b, h, i: (0, 0)),
            pl.BlockSpec((nq, nh * HEAD_DIM_A, tq), lambda b, h, i: (b, h, 0)),
            pl.BlockSpec((1, nblk, nh * HEAD_DIM_A), lambda b, h, i: (b, 0, h)),
        ],
        out_specs=pl.BlockSpec((tq, nh * HEAD_DIM_A), lambda b, h, i: (b * nq + i, h)),
        out_shape=jax.ShapeDtypeStruct((t, WIDTH_A), BF16),
        scratch_shapes=[pltpu.VMEM((nh, tq, tq), F32), pltpu.VMEM((nh, tq, tq), F32)],
        compiler_params=_params("parallel", "parallel", "arbitrary"),
        name="moba",
    )(slopes, qat, ka, etab, vat, km)


def _mla_kernel(qt_ref, k_ref, vt_ref, o_ref, sa_ref, sb_ref):
    tq = ATT_TILE
    i = pl.program_id(2)
    qt = [qt_ref[0, hh * LANES:(hh + 1) * LANES, :] for hh in range(ATT_HEADS)]

    def scores(hh, j):
        return _dot(k_ref[pl.ds(pl.multiple_of(j * tq, tq), tq), hh * LANES:(hh + 1) * LANES], qt[hh])

    def values(hh, j):
        return _values_tile(vt_ref, hh, j)

    o_ref[...] = _attention_loop(i, scores, values, _causal_t(tq), tq, sa_ref, sb_ref).astype(BF16)


def _mla(qmt, km, vmt, batch, seq):
    t = km.shape[0]
    tq = ATT_TILE
    nq = seq // tq
    nh = ATT_HEADS
    return pl.pallas_call(
        _mla_kernel,
        grid=(batch, N_HEADS_B // nh, nq),
        in_specs=[
            pl.BlockSpec((1, nh * LANES, tq), lambda b, h, i: (b * nq + i, h, 0)),
            pl.BlockSpec((seq, nh * LANES), lambda b, h, i: (b, h)),
            pl.BlockSpec((nq, nh * V_HEAD_DIM, tq), lambda b, h, i: (b, h, 0)),
        ],
        out_specs=pl.BlockSpec((tq, nh * V_HEAD_DIM), lambda b, h, i: (b * nq + i, h)),
        out_shape=jax.ShapeDtypeStruct((t, WIDTH_B), BF16),
        scratch_shapes=[pltpu.VMEM((nh, tq, tq), F32), pltpu.VMEM((nh, tq, tq), F32)],
        compiler_params=_params("parallel", "parallel", "arbitrary"),
        name="mla",
    )(qmt, km, vmt)


def _pack_halves(y):
    lo = pltpu.bitcast(y[:, :HALF].astype(BF16).astype(F32), U32)
    hi = pltpu.bitcast(y[:, HALF:].astype(BF16).astype(F32), U32)
    return (hi & jnp.uint32(0xFFFF0000)) | (lo >> 16)


def _unpack_halves(w):
    return pltpu.bitcast(w << 16, F32), pltpu.bitcast(w & jnp.uint32(0xFFFF0000), F32)


def _merge_kernel(x_ref, oa_ref, ob_ref, sa_ref, sb_ref, wpa_ref, wpb_ref, wo_ref, g_ref, b_ref, o_ref, op_ref):
    pa = _dot(oa_ref[...], wpa_ref[...])
    pb = _dot(ob_ref[...], wpb_ref[...])
    merged = sa_ref[...].astype(F32) * pa + sb_ref[...].astype(F32) * pb
    hmix = _dot(merged.astype(BF16), wo_ref[...])
    y = _layer_norm(ALPHA * x_ref[...] + hmix, g_ref[...], b_ref[...])
    o_ref[...] = y
    op_ref[...] = _pack_halves(y)


def _merge(x, oa, ob, sa, sb, wp):
    t = x.shape[0]
    tm = MERGE_TILE
    row = lambda w: pl.BlockSpec((tm, w), lambda i: (i, 0))
    weights = [wp['wpa'], wp['wpb'], wp['wo'], wp['ln1_g'], wp['ln1_b']]
    return pl.pallas_call(
        _merge_kernel,
        grid=(t // tm,),
        in_specs=[row(D_MODEL), row(WIDTH_A), row(WIDTH_B), row(D_MODEL), row(D_MODEL)] + [_full(w) for w in weights],
        out_specs=[row(D_MODEL), row(HALF)],
        out_shape=[jax.ShapeDtypeStruct((t, D_MODEL), F32), jax.ShapeDtypeStruct((t, HALF), U32)],
        compiler_params=_params("parallel"),
        name="merge",
    )(x, oa, ob, sa, sb, *weights)


def _router_kernel(x_ref, wh_ref, wl_ref, rb_ref, idx_ref, rank_ref, w_ref, cnt_ref):
    tm = x_ref.shape[0]

    @pl.when(pl.program_id(0) == 0)
    def _():
        cnt_ref[...] = jnp.zeros_like(cnt_ref)

    x = x_ref[...]
    xh = x.astype(BF16)
    xl = (x - xh.astype(F32)).astype(BF16)
    wh, wl = wh_ref[...], wl_ref[...]
    logits = _dot_nt(wh, xh) + (_dot_nt(wh, xl) + _dot_nt(wl, xh))
    scores = _sigmoid(logits)
    choice = scores + rb_ref[...]
    row = lax.broadcasted_iota(jnp.int32, (GROUP_SIZE, tm), 0)
    groups = [choice[g * GROUP_SIZE:(g + 1) * GROUP_SIZE, :] for g in range(N_GROUPS)]
    gscore = []
    for blk in groups:
        m1 = jnp.max(blk, axis=0, keepdims=True)
        first = jnp.min(jnp.where(blk == m1, row, GROUP_SIZE), axis=0, keepdims=True)
        m2 = jnp.max(jnp.where(row == first, -jnp.inf, blk), axis=0, keepdims=True)
        gscore.append(m1 + m2)
    masked = []
    for g in range(N_GROUPS):
        ahead = jnp.zeros((1, tm), jnp.int32)
        for o in range(N_GROUPS):
            if o < g:
                ahead += (gscore[o] >= gscore[g]).astype(jnp.int32)
            elif o > g:
                ahead += (gscore[o] > gscore[g]).astype(jnp.int32)
        masked.append(jnp.where(ahead < TOPK_GROUPS, groups[g], -jnp.inf))
    cur = jnp.concatenate(masked, axis=0)
    erow = lax.broadcasted_iota(jnp.int32, (N_EXPERTS, tm), 0)
    hits, idxs, ws = [], [], []
    for _ in range(TOP_K):
        m = jnp.max(cur, axis=0, keepdims=True)
        e = jnp.min(jnp.where(cur == m, erow, N_EXPERTS), axis=0, keepdims=True)
        hit = erow == e
        hits.append(hit)
        idxs.append(e)
        ws.append(jnp.sum(jnp.where(hit, scores, 0.0), axis=0, keepdims=True))
        cur = jnp.where(hit, -jnp.inf, cur)
    total = ws[0]
    for w in ws[1:]:
        total = total + w
    member = hits[0]
    for hit in hits[1:]:
        member = member | hit
    member = jnp.where(member, 1.0, 0.0).astype(BF16)
    t_src = lax.broadcasted_iota(jnp.int32, (tm, tm), 0)
    t_dst = lax.broadcasted_iota(jnp.int32, (tm, tm), 1)
    before = _dot(member, jnp.where(t_src < t_dst, 1.0, 0.0).astype(BF16))
    base = cnt_ref[...]
    before = before + jnp.concatenate([base] * (tm // LANES), axis=1)
    cnt_ref[...] = base + _dot(member, jnp.ones((tm, LANES), BF16))
    for r in range(TOP_K):
        idx_ref[0, r:r + 1, :] = idxs[r]
        rank_ref[0, r:r + 1, :] = jnp.sum(jnp.where(hits[r], before, 0.0), axis=0, keepdims=True).astype(jnp.int32)
        w_ref[r:r + 1, :] = ws[r] / total * ROUTED_SCALE


def _router(x1, wp):
    t = x1.shape[0]
    tm = ROW_TILE
    nt = t // tm
    weights = [wp['wr_hi'], wp['wr_lo'], wp['rbias']]
    tile = pl.BlockSpec((1, TOP_K, tm), lambda i: (i, 0, 0))
    return pl.pallas_call(
        _router_kernel,
        grid=(nt,),
        in_specs=[pl.BlockSpec((tm, D_MODEL), lambda i: (i, 0))] + [_full(w) for w in weights],
        out_specs=[tile, tile, pl.BlockSpec((TOP_K, tm), lambda i: (0, i)),
                   pl.BlockSpec((N_EXPERTS, LANES), lambda i: (0, 0))],
        out_shape=[jax.ShapeDtypeStruct((nt, TOP_K, tm), jnp.int32), jax.ShapeDtypeStruct((nt, TOP_K, tm), jnp.int32),
                   jax.ShapeDtypeStruct((TOP_K, t), F32), jax.ShapeDtypeStruct((N_EXPERTS, LANES), F32)],
        compiler_params=_params("arbitrary"),
        name="router",
    )(x1, *weights)


def _pos_kernel(idx_ref, rank_ref, pq_ref, pos_ref):
    tm = idx_ref.shape[2]
    erow = lax.broadcasted_iota(jnp.int32, (N_EXPERTS, tm), 0)
    for k in range(TOP_K):
        onehot = jnp.where(erow == idx_ref[0, k:k + 1, :], 1.0, 0.0).astype(BF16)
        q = _dot(pq_ref[...], onehot)
        blk = (q[0:1, :] * 32.0 + q[1:2, :]).astype(jnp.int32)
        pos_ref[k:k + 1, :] = blk * EXPERT_BLOCK + rank_ref[0, k:k + 1, :]


def _positions(idx3, rank3, pstart):
    nt, _, tm = idx3.shape
    blk = pstart // EXPERT_BLOCK
    pq = jnp.concatenate([(blk // 32)[None], (blk % 32)[None], jnp.zeros((6, N_EXPERTS), jnp.int32)], axis=0).astype(F32)
    tile = pl.BlockSpec((1, TOP_K, tm), lambda i: (i, 0, 0))
    return pl.pallas_call(
        _pos_kernel,
        grid=(nt,),
        in_specs=[tile, tile, pl.BlockSpec((8, N_EXPERTS), lambda i: (0, 0))],
        out_specs=pl.BlockSpec((TOP_K, tm), lambda i: (0, i)),
        out_shape=jax.ShapeDtypeStruct((TOP_K, nt * tm), jnp.int32),
        compiler_params=_params("parallel"),
        name="moe_positions",
    )(idx3, rank3, pq.astype(BF16))


def _sc_mesh():
    return plsc.VectorSubcoreMesh(core_axis_name="c", subcore_axis_name="s", num_cores=SC_CORES,
                                  num_subcores=SC_SUBCORES)


def _sc_worker():
    return lax.axis_index("s") * SC_CORES + lax.axis_index("c")


def _sc_scatter_kernel(x_hbm, pos_hbm, out_hbm, idx_v, rows_v):
    nchunk = idx_v.shape[1]
    w = _sc_worker()
    pltpu.sync_copy(pos_hbm.at[w], idx_v)

    @pl.loop(0, nchunk)
    def _(c):
        pltpu.sync_copy(x_hbm.at[pl.ds((w * nchunk + c) * SC_CHUNK, SC_CHUNK)], rows_v)
        for k in range(TOP_K):
            pltpu.sync_copy(rows_v, out_hbm.at[idx_v.at[k, c]])


def _sc_dispatch(x1p, pos_kt, n_rows):
    t = x1p.shape[0]
    workers = SC_CORES * SC_SUBCORES
    nchunk = t // (workers * SC_CHUNK)
    assert t == workers * nchunk * SC_CHUNK
    pos4 = pos_kt.reshape(TOP_K, workers, nchunk, SC_CHUNK).transpose(1, 0, 2, 3)
    return pl.kernel(
        _sc_scatter_kernel,
        out_type=jax.ShapeDtypeStruct((n_rows, HALF), U32),
        mesh=_sc_mesh(),
        scratch_types=[pltpu.VMEM((TOP_K, nchunk, SC_CHUNK), jnp.int32), pltpu.VMEM((SC_CHUNK, HALF), U32)],
        name="moe_dispatch_sc",
    )(x1p, pos4)


def _expert_kernel(layer, be_ref, nused_ref, nvalid_ref, first_ref, slot_ref, ahead_ref, head_ref,
                   x_ref, wg_hbm, wu_hbm, wd_hbm, y_ref, wg_f, wu_f, wd_f, wg_b, wu_b, wd_b, sem):
    def fetch(e, slot):
        return [pltpu.make_async_copy(src.at[layer, e], dst.at[slot], sem.at[slot, n])
                for n, (src, dst) in enumerate(((wg_hbm, wg_f), (wu_hbm, wu_f), (wd_hbm, wd_f)))]

    @pl.when(pl.program_id(0) == 0)
    def _():
        for copy in fetch(head_ref[0], 0):
            copy.start()

        @pl.when(head_ref[1] >= 0)
        def _():
            for copy in fetch(head_ref[1], 1):
                copy.start()

    def one_block(b, rows):
        @pl.when((b < nused_ref[0]) & (first_ref[b] == 1))
        def _():
            slot = slot_ref[b]
            for copy in fetch(be_ref[b], slot):
                copy.wait()
            wg_b[...] = wg_f[slot].astype(BF16)
            wu_b[...] = wu_f[slot].astype(BF16)
            wd_b[...] = wd_f[slot].astype(BF16)

            @pl.when(ahead_ref[b] >= 0)
            def _():
                for copy in fetch(ahead_ref[b], slot):
                    copy.start()

        @pl.when(b < nused_ref[0])
        def _():
            live = lax.broadcasted_iota(jnp.int32, (EXPERT_BLOCK, HALF), 0) < nvalid_ref[b]
            xlo, xhi = (h.astype(BF16) for h in _unpack_halves(jnp.where(live, x_ref[rows, :], jnp.uint32(0))))
            g = _dot(xlo, wg_b[:HALF, :]) + _dot(xhi, wg_b[HALF:, :])
            u = _dot(xlo, wu_b[:HALF, :]) + _dot(xhi, wu_b[HALF:, :])
            a = (g * _sigmoid(g) * u).astype(BF16)
            y_ref[rows, :] = _pack_halves(_dot(a, wd_b[...]))

        @pl.when(b >= nused_ref[0])
        def _():
            y_ref[rows, :] = jnp.zeros((EXPERT_BLOCK, HALF), U32)

    for sub in range(EXPERT_STEP):
        one_block(pl.program_id(0) * EXPERT_STEP + sub, slice(sub * EXPERT_BLOCK, (sub + 1) * EXPERT_BLOCK))


def _experts(xs, block_e, nused, nvalid, counts, w_gate, w_up, w_down, layer, n_blocks):
    rows = EXPERT_BLOCK
    blocks = jnp.arange(n_blocks, dtype=jnp.int32)
    first = ((blocks == 0) | (block_e != jnp.roll(block_e, 1))) & (blocks < nused[0])
    run = jnp.cumsum(first.astype(jnp.int32)) - 1
    run_e = jnp.nonzero(counts > 0, size=N_EXPERTS, fill_value=-1)[0].astype(jnp.int32)
    ahead = jnp.concatenate([run_e, jnp.full((2,), -1, jnp.int32)])[jnp.clip(run, 0, N_EXPERTS - 1) + 2]
    step_rows = EXPERT_STEP * rows
    grid_spec = pltpu.PrefetchScalarGridSpec(
        num_scalar_prefetch=7,
        grid=(n_blocks // EXPERT_STEP,),
        in_specs=[
            pl.BlockSpec((step_rows, HALF), lambda s, be, nu, *_: (jnp.minimum(s, (nu[0] - 1) // EXPERT_STEP), 0)),
            pl.BlockSpec(memory_space=pl.ANY), pl.BlockSpec(memory_space=pl.ANY), pl.BlockSpec(memory_space=pl.ANY),
        ],
        out_specs=pl.BlockSpec((step_rows, HALF), lambda s, *_: (s, 0)),
        scratch_shapes=[
            pltpu.VMEM((2, D_MODEL, D_EXPERT), F32), pltpu.VMEM((2, D_MODEL, D_EXPERT), F32),
            pltpu.VMEM((2, D_EXPERT, D_MODEL), F32),
            pltpu.VMEM((D_MODEL, D_EXPERT), BF16), pltpu.VMEM((D_MODEL, D_EXPERT), BF16),
            pltpu.VMEM((D_EXPERT, D_MODEL), BF16),
            pltpu.SemaphoreType.DMA((2, 3)),
        ],
    )
    return pl.pallas_call(
        functools.partial(_expert_kernel, layer),
        grid_spec=grid_spec,
        out_shape=jax.ShapeDtypeStruct((n_blocks * rows, HALF), U32),
        compiler_params=_params("arbitrary"),
        name="moe_experts",
    )(block_e, nused, nvalid, first.astype(jnp.int32), (run % 2).astype(jnp.int32), ahead, run_e[:2],
      xs, w_gate, w_up, w_down)


def _sc_gather_kernel(table_hbm, idx_hbm, out_hbm, idx_v, rows_v, gsem, wsem):
    per_worker = idx_v.shape[0]
    base = _sc_worker() * per_worker
    pltpu.sync_copy(idx_hbm.at[pl.ds(base, per_worker)], idx_v)

    @pl.loop(0, per_worker // SC_CHUNK, step=SC_BUFS)
    def _(c):
        offs = [(c + n) * SC_CHUNK for n in range(SC_BUFS)]
        gathers = [pltpu.async_copy(table_hbm.at[idx_v.at[pl.ds(offs[n], SC_CHUNK)]], rows_v.at[n], gsem.at[n])
                   for n in range(SC_BUFS)]
        writes = []
        for n in range(SC_BUFS):
            gathers[n].wait()
            writes.append(pltpu.async_copy(rows_v.at[n], out_hbm.at[pl.ds(base + offs[n], SC_CHUNK)], wsem.at[n]))
        for write in writes:
            write.wait()


def _sc_gather_rows(table, idx):
    n = idx.shape[0]
    workers = SC_CORES * SC_SUBCORES
    assert n % (workers * SC_CHUNK * SC_BUFS) == 0
    return pl.kernel(
        _sc_gather_kernel,
        out_type=jax.ShapeDtypeStruct((n, HALF), U32),
        mesh=_sc_mesh(),
        scratch_types=[pltpu.VMEM((n // workers,), jnp.int32), pltpu.VMEM((SC_BUFS, SC_CHUNK, HALF), U32),
                       pltpu.SemaphoreType.DMA((SC_BUFS,)), pltpu.SemaphoreType.DMA((SC_BUFS,))],
        name="moe_gather_sc",
    )(table, idx)


def _combine_kernel(x_ref, w_ref, *refs):
    y_refs, (wsg_ref, wsu_ref, wsd_ref, g_ref, b_ref, o_ref) = refs[:TOP_K], refs[TOP_K:]
    x = x_ref[...]
    xb = x.astype(BF16)
    g = _dot(xb, wsg_ref[...])
    u = _dot(xb, wsu_ref[...])
    shared = _dot((g * _sigmoid(g) * u).astype(BF16), wsd_ref[...])
    w = w_ref[...]
    lo, hi = (h * w[:, 0:1] for h in _unpack_halves(y_refs[0][...]))
    for k in range(1, TOP_K):
        lo_k, hi_k = _unpack_halves(y_refs[k][...])
        lo = lo + lo_k * w[:, k:k + 1]
        hi = hi + hi_k * w[:, k:k + 1]
    routed = jnp.concatenate([lo, hi], axis=1)
    o_ref[...] = _layer_norm(ALPHA * x + (shared + routed), g_ref[...], b_ref[...])


def _combine(x1, y8, w_tk, wp):
    t = x1.shape[0]
    tm = WIDE_TILE
    nt = t // tm
    weights = [wp['wsg'], wp['wsu'], wp['wsd'], wp['ln2_g'], wp['ln2_b']]
    y_specs = [pl.BlockSpec((tm, HALF), lambda i, k=k: (k * nt + i, 0)) for k in range(TOP_K)]
    return pl.pallas_call(
        _combine_kernel,
        grid=(nt,),
        in_specs=[pl.BlockSpec((tm, D_MODEL), lambda i: (i, 0)), pl.BlockSpec((tm, TOP_K), lambda i: (i, 0))]
                 + y_specs + [_full(w) for w in weights],
        out_specs=pl.BlockSpec((tm, D_MODEL), lambda i: (i, 0)),
        out_shape=jax.ShapeDtypeStruct((t, D_MODEL), F32),
        compiler_params=_params("parallel"),
        name="moe_combine",
    )(x1, w_tk, *([y8] * TOP_K), *weights)


def _moe(x1, x1p, wp, w_e_gate, w_e_up, w_e_down, layer):
    t = x1.shape[0]
    rows = EXPERT_BLOCK
    n_blocks = -(-(t * TOP_K + N_EXPERTS * (rows - 1)) // (rows * EXPERT_STEP)) * EXPERT_STEP
    idx3, rank3, top_w, cnt = _router(x1, wp)
    counts = cnt[:, 0].astype(jnp.int32)
    padded = (counts + rows - 1) // rows * rows
    padded_end = jnp.cumsum(padded)
    pstart = (padded_end - padded).astype(jnp.int32)
    nused = (padded_end[-1] // rows).astype(jnp.int32).reshape(1)
    blocks = jnp.arange(n_blocks, dtype=jnp.int32)
    block_row = jnp.minimum(blocks, nused[0] - 1) * rows
    block_e = jnp.sum((padded_end[None, :] <= block_row[:, None]).astype(jnp.int32), axis=1)
    block_e = jnp.minimum(block_e, N_EXPERTS - 1)
    nvalid = jnp.clip(counts[block_e] - (blocks * rows - pstart[block_e]), 0, rows).astype(jnp.int32)
    pos_kt = _positions(idx3, rank3, pstart)
    xs = _sc_dispatch(x1p, pos_kt, n_blocks * rows)
    ys = _experts(xs, block_e, nused, nvalid, counts, w_e_gate, w_e_up, w_e_down, layer, n_blocks)
    return _combine(x1, _sc_gather_rows(ys, pos_kt.reshape(TOP_K * t)), top_w.T, wp)


def _head_groups_t(w, used):
    k, h, _ = w.shape
    return jnp.pad(w, ((0, 0), (0, 0), (0, LANES - used))).reshape(k, h * LANES).T


def _prep_layer(l, w_in, b_gate, q_norm, w_uq, kv_norm, w_ukv, w_proj_a, w_proj_b, w_out, ln1_g, ln1_b,
                w_router, router_bias, w_s_gate, w_s_up, w_s_down, ln2_g, ln2_b):
    w = w_in[l]
    o = 0
    cols = {}
    for name, width in (('qa', WIDTH_A), ('ka', WIDTH_A), ('va', WIDTH_A), ('cq', Q_LORA_RANK),
                        ('ckv', KV_LORA_RANK), ('kr', QK_ROPE_DIM), ('ga', D_MODEL), ('gb', D_MODEL)):
        cols[name] = w[:, o:o + width]
        o += width
    wkr = jnp.pad(cols['kr'], ((0, 0), (QK_NOPE_DIM, LANES - QK_NOPE_DIM - QK_ROPE_DIM)))
    dqk = QK_NOPE_DIM + QK_ROPE_DIM
    wq = w_uq[l].reshape(Q_LORA_RANK, N_HEADS_B, dqk) * (dqk ** -0.5 * LOG2E)
    wkv = w_ukv[l].reshape(KV_LORA_RANK, N_HEADS_B, QK_NOPE_DIM + V_HEAD_DIM)
    wuk = jnp.pad(wkv[:, :, :QK_NOPE_DIM], ((0, 0), (0, 0), (0, LANES - QK_NOPE_DIM))).reshape(KV_LORA_RANK, N_HEADS_B * LANES)
    wr_t = w_router[l].T
    wr_hi = wr_t.astype(BF16)
    return dict(
        wqt=(cols['qa'] * HEAD_DIM_A ** -0.5).T.astype(BF16), wk=cols['ka'].astype(BF16),
        wvt=cols['va'].T.astype(BF16),
        wcq=cols['cq'].astype(BF16), wckv=cols['ckv'].astype(BF16), wkr=wkr.astype(BF16),
        wg=jnp.concatenate([cols['ga'], cols['gb']], axis=1).astype(BF16),
        bg=b_gate[l].reshape(1, 2 * D_MODEL), qn=q_norm[l].reshape(1, Q_LORA_RANK), kvn=kv_norm[l].reshape(1, KV_LORA_RANK),
        wuqt=_head_groups_t(wq, dqk).astype(BF16), wuk=wuk.astype(BF16),
        wuvt=wkv[:, :, QK_NOPE_DIM:].reshape(KV_LORA_RANK, WIDTH_B).T.astype(BF16),
        wpa=w_proj_a[l].astype(BF16), wpb=w_proj_b[l].astype(BF16), wo=w_out[l].astype(BF16),
        ln1_g=ln1_g[l].reshape(1, D_MODEL), ln1_b=ln1_b[l].reshape(1, D_MODEL),
        wr_hi=wr_hi, wr_lo=(wr_t - wr_hi.astype(F32)).astype(BF16), rbias=router_bias[l].reshape(N_EXPERTS, 1),
        wsg=w_s_gate[l].astype(BF16), wsu=w_s_up[l].astype(BF16), wsd=w_s_down[l].astype(BF16),
        ln2_g=ln2_g[l].reshape(1, D_MODEL), ln2_b=ln2_b[l].reshape(1, D_MODEL),
    )


def _rope_tables(seq):
    pos = jnp.arange(seq, dtype=F32)
    inv_freq = ROPE_THETA ** (-jnp.arange(0, QK_ROPE_DIM, 2, dtype=F32) / QK_ROPE_DIM)
    ang = pos[:, None] * inv_freq[None, :]
    cos, sin = jnp.cos(ang), jnp.sin(ang)
    half = QK_ROPE_DIM // 2
    z = lambda n: jnp.zeros((seq, n), F32)
    c = jnp.concatenate([jnp.ones((seq, QK_NOPE_DIM), F32), cos, cos, z(LANES - QK_NOPE_DIM - QK_ROPE_DIM)], axis=1)
    s1 = jnp.concatenate([z(QK_NOPE_DIM), -sin, z(LANES - QK_NOPE_DIM - half)], axis=1)
    s2 = jnp.concatenate([z(QK_NOPE_DIM + half), sin, z(LANES - QK_NOPE_DIM - QK_ROPE_DIM)], axis=1)
    return c, s1, s2, c.T, s1.T, s2.T


def _moba_key_table(seq, slopes):
    blk = jnp.arange(seq, dtype=jnp.int32) // MOBA_BLOCK
    onehot = (blk[:, None] == jnp.arange(BIAS_LO, dtype=jnp.int32)[None, :]).astype(F32)
    inblk = (jnp.arange(seq, dtype=jnp.int32) % MOBA_BLOCK).astype(F32)[:, None] * (slopes * LOG2E)[None, :]
    hi = inblk.astype(BF16)
    lo = (inblk - hi.astype(F32)).astype(BF16)
    pad = jnp.zeros((seq, LANES - POS_LO - N_HEADS_A), BF16)
    return jnp.concatenate([onehot.astype(BF16), onehot.astype(BF16), hi, lo, pad], axis=1)


def kernel(x, w_in, b_gate, q_norm, w_uq, kv_norm, w_ukv, w_proj_a, w_proj_b, w_out, ln1_g, ln1_b, w_router, router_bias, w_e_gate, w_e_up, w_e_down, w_s_gate, w_s_up, w_s_down, ln2_g, ln2_b):
    batch, seq, d = x.shape
    assert d == D_MODEL and seq % MOBA_BLOCK == 0 and MOBA_TOPK <= seq // MOBA_BLOCK <= BIAS_LO
    assert POS_LO + N_HEADS_A <= LANES and POS_HI + N_HEADS_A <= POS_LO and (batch * seq) % MERGE_TILE == 0
    tabs = _rope_tables(seq)
    slopes = jnp.asarray(np.exp2(-8.0 * (np.arange(N_HEADS_A) + 1.0) / N_HEADS_A), F32)
    etab = _moba_key_table(seq, slopes)
    h = x.reshape(batch * seq, d)
    for l in range(DEPTH):
        wp = _prep_layer(l, w_in, b_gate, q_norm, w_uq, kv_norm, w_ukv, w_proj_a, w_proj_b, w_out, ln1_g, ln1_b,
                         w_router, router_bias, w_s_gate, w_s_up, w_s_down, ln2_g, ln2_b)
        qat, ka, vat, kmean, qmt, km, vmt, sa, sb = _inproj(h, wp, tabs, seq)
        oa = _moba(qat, ka, vat, kmean, etab, slopes, batch, seq)
        ob = _mla(qmt, km, vmt, batch, seq)
        x1, x1p = _merge(h, oa, ob, sa, sb, wp)
        h = _moe(x1, x1p, wp, w_e_gate, w_e_up, w_e_down, l)
    return h.reshape(batch, seq, d)
```

```python
import functools

import numpy as np

import jax
import jax.numpy as jnp
from jax import lax
from jax.experimental import pallas as pl
from jax.experimental.pallas import tpu as pltpu
from jax.experimental.pallas import tpu_sc as plsc

D_MODEL = 1024
N_HEADS_A = 8
HEAD_DIM_A = 64
WIDTH_A = N_HEADS_A * HEAD_DIM_A
MOBA_BLOCK = 256
MOBA_TOPK = 3
N_HEADS_B = 8
QK_NOPE_DIM = 64
QK_ROPE_DIM = 32
V_HEAD_DIM = 64
Q_LORA_RANK = 384
KV_LORA_RANK = 256
WIDTH_B = N_HEADS_B * V_HEAD_DIM
ROPE_THETA = 10000.0
N_EXPERTS = 256
TOP_K = 8
N_GROUPS = 8
TOPK_GROUPS = 4
GROUP_SIZE = N_EXPERTS // N_GROUPS
D_EXPERT = 256
D_SHARED = 256
ROUTED_SCALE = 2.5
DEPTH = 2
ALPHA = (2 * DEPTH) ** 0.25
LN_EPS = 1e-5
RMS_EPS = 1e-6

LANES = 128
NEG = -1e30
LOG2E = float(np.log2(np.e))
ROW_TILE = 256
WIDE_TILE = 512
MERGE_TILE = 1024
ATT_TILE = 256
EXPERT_BLOCK = 256
EXPERT_STEP = 4
SC_CORES = 2
SC_SUBCORES = 16
SC_CHUNK = 64
SC_BUFS = 2
VMEM_LIMIT = 56 * 1024 * 1024
HALF = D_MODEL // 2
ATT_HEADS = 4
ATT_GROUP = 4
PV_ROWS = 80
BIAS_HI, BIAS_LO, POS_HI, POS_LO = 0, 32, 64, 72

BF16 = jnp.bfloat16
F32 = jnp.float32
U32 = jnp.uint32


def _dot(a, b):
    return jnp.dot(a, b, preferred_element_type=F32)


def _dot_nt(a, b):
    return lax.dot_general(a, b, (((1,), (1,)), ((), ())), preferred_element_type=F32)


def _sigmoid(x):
    return 1.0 / (1.0 + jnp.exp(-x))


def _layer_norm(y, g, b):
    mu = jnp.mean(y, axis=-1, keepdims=True)
    d = y - mu
    var = jnp.mean(d * d, axis=-1, keepdims=True)
    return d * lax.rsqrt(var + LN_EPS) * g + b


def _params(*sem):
    return pltpu.CompilerParams(dimension_semantics=sem, vmem_limit_bytes=VMEM_LIMIT)


def _full(a):
    return pl.BlockSpec(a.shape, lambda *_: (0,) * a.ndim)


def _inproj_kernel(x_ref, wqt_ref, wk_ref, wvt_ref, wcq_ref, wckv_ref, wkr_ref, wg_ref, bg_ref,
                   qn_ref, kvn_ref, wuqt_ref, wuk_ref, wuvt_ref, cos_ref, s1_ref, s2_ref, cost_ref, s1t_ref, s2t_ref,
                   qat_ref, ka_ref, vat_ref, kmean_ref, qmt_ref, km_ref, vmt_ref, sa_ref, sb_ref):
    xb = x_ref[...].astype(BF16)
    half = QK_ROPE_DIM // 2
    qat_ref[0] = _dot_nt(wqt_ref[...], xb).astype(BF16)
    k = _dot(xb, wk_ref[...])
    ka_ref[...] = k.astype(BF16)
    kmean_ref[0] = jnp.mean(k, axis=0, keepdims=True)
    vat_ref[0] = _dot_nt(wvt_ref[...], xb).astype(BF16)

    cq = _dot(xb, wcq_ref[...])
    cqn = (cq * lax.rsqrt(jnp.mean(cq * cq, axis=-1, keepdims=True) + RMS_EPS) * qn_ref[...]).astype(BF16)
    ckv = _dot(xb, wckv_ref[...])
    ckvn = (ckv * lax.rsqrt(jnp.mean(ckv * ckv, axis=-1, keepdims=True) + RMS_EPS) * kvn_ref[...]).astype(BF16)
    qt = _dot_nt(wuqt_ref[...], cqn)
    ct, s1t, s2t = cost_ref[...], s1t_ref[...], s2t_ref[...]
    for h in range(N_HEADS_B):
        t = qt[h * LANES:(h + 1) * LANES, :]
        rot = t * ct + pltpu.roll(t, LANES - half, 0) * s1t + pltpu.roll(t, half, 0) * s2t
        qmt_ref[0, h * LANES:(h + 1) * LANES, :] = rot.astype(BF16)
    kn = _dot(ckvn, wuk_ref[...])
    kr = _dot(xb, wkr_ref[...])
    c, s1, s2 = cos_ref[...], s1_ref[...], s2_ref[...]
    krot = kr * c + pltpu.roll(kr, LANES - half, 1) * s1 + pltpu.roll(kr, half, 1) * s2
    for h in range(N_HEADS_B):
        sl = slice(h * LANES, (h + 1) * LANES)
        km_ref[:, sl] = (kn[:, sl] + krot).astype(BF16)
    vmt_ref[0] = _dot_nt(wuvt_ref[...], ckvn).astype(BF16)

    sig = _sigmoid(_dot(xb, wg_ref[...]) + bg_ref[...])
    sa_ref[...] = sig[:, :D_MODEL].astype(BF16)
    sb_ref[...] = sig[:, D_MODEL:].astype(BF16)


def _inproj(x, wp, tabs, seq):
    t = x.shape[0]
    tm = ROW_TILE
    nt = t // tm
    npos = seq // tm
    row = lambda w: pl.BlockSpec((tm, w), lambda i: (i, 0))
    tile = lambda r: pl.BlockSpec((1, r, tm), lambda i: (i, 0, 0))
    tab = pl.BlockSpec((tm, LANES), lambda i: (i % npos, 0))
    tabt = pl.BlockSpec((LANES, tm), lambda i: (0, i % npos))
    weights = [wp['wqt'], wp['wk'], wp['wvt'], wp['wcq'], wp['wckv'], wp['wkr'], wp['wg'], wp['bg'],
               wp['qn'], wp['kvn'], wp['wuqt'], wp['wuk'], wp['wuvt']]
    hl = N_HEADS_B * LANES
    out_shape = [
        jax.ShapeDtypeStruct((nt, WIDTH_A, tm), BF16), jax.ShapeDtypeStruct((t, WIDTH_A), BF16),
        jax.ShapeDtypeStruct((nt, WIDTH_A, tm), BF16), jax.ShapeDtypeStruct((nt, 1, WIDTH_A), F32),
        jax.ShapeDtypeStruct((nt, hl, tm), BF16), jax.ShapeDtypeStruct((t, hl), BF16),
        jax.ShapeDtypeStruct((nt, WIDTH_B, tm), BF16),
        jax.ShapeDtypeStruct((t, D_MODEL), BF16), jax.ShapeDtypeStruct((t, D_MODEL), BF16),
    ]
    out_specs = [tile(WIDTH_A), row(WIDTH_A), tile(WIDTH_A),
                 pl.BlockSpec((1, 1, WIDTH_A), lambda i: (i, 0, 0)),
                 tile(hl), row(hl), tile(WIDTH_B), row(D_MODEL), row(D_MODEL)]
    return pl.pallas_call(
        _inproj_kernel,
        grid=(nt,),
        in_specs=[row(D_MODEL)] + [_full(w) for w in weights] + [tab, tab, tab, tabt, tabt, tabt],
        out_specs=out_specs,
        out_shape=out_shape,
        compiler_params=_params("parallel"),
        name="inproj",
    )(x, *weights, *tabs)


def _attend_init(tq):
    return jnp.full((1, tq), -jnp.inf, F32), jnp.zeros((PV_ROWS, tq), F32)


def _col_max(s_ref):
    return [jnp.max(s_ref[hh], axis=0, keepdims=True) for hh in range(ATT_HEADS)]


def _attend_staged(cur_ref, cur_max, state, vts, nxt_ref=None, next_scores=None):
    heads = range(ATT_HEADS)
    if nxt_ref is not None:
        for hh in heads:
            nxt_ref[hh] = next_scores(hh)
    new_m, scaled, pvs = [], [], []
    for hh in heads:
        m_i, acc = state[hh]
        m_new = jnp.maximum(m_i, cur_max[hh])
        new_m.append(m_new)
        scaled.append(jnp.exp2(m_i - m_new) * acc)
        pvs.append(_dot(vts[hh], jnp.exp2(cur_ref[hh] - m_new).astype(BF16)))
    nxt_max = _col_max(nxt_ref) if nxt_ref is not None else cur_max
    return tuple((m, a + pv) for m, a, pv in zip(new_m, scaled, pvs)), nxt_max


def _attention_loop(i, scores, values, causal, tq, sa_ref, sb_ref):
    heads = range(ATT_HEADS)
    bufs = (sa_ref, sb_ref)

    def vals(s):
        tile = jnp.where(s == 0, i, s - 1)
        return [values(hh, tile) for hh in heads]

    def ahead(s):
        return lambda hh: scores(hh, jnp.maximum(jnp.minimum(s, i - 1), 0))

    for hh in heads:
        sa_ref[hh] = jnp.where(causal, scores(hh, i), NEG)
    carry = (tuple(_attend_init(tq) for _ in heads), _col_max(sa_ref))
    nslots = i + 1

    def trip(n, c):
        for g in range(ATT_GROUP):
            s = n * ATT_GROUP + g
            c = _attend_staged(bufs[g % 2], c[1], c[0], vals(s), bufs[(g + 1) % 2], ahead(s))
        return c

    carry = lax.fori_loop(0, nslots // ATT_GROUP, trip, carry)
    done = nslots // ATT_GROUP * ATT_GROUP
    for g in range(ATT_GROUP - 1):
        def step(c, g=g):
            if g == ATT_GROUP - 2:
                return _attend_staged(bufs[g % 2], c[1], c[0], vals(done + g))
            return _attend_staged(bufs[g % 2], c[1], c[0], vals(done + g), bufs[(g + 1) % 2], ahead(done + g))

        carry = lax.cond(nslots - done > g, step, lambda c: c, carry)
    outs = []
    for hh in heads:
        _, acc = carry[0][hh]
        outs.append(acc[:V_HEAD_DIM, :] / acc[V_HEAD_DIM:V_HEAD_DIM + 1, :])
    return jnp.concatenate(outs, axis=0).T


def _values_tile(vt_ref, hh, j):
    tk = vt_ref.shape[2]
    pad_row = lax.broadcasted_iota(jnp.int32, (PV_ROWS - V_HEAD_DIM, tk), 0)
    ones = jnp.where(pad_row == 0, 1.0, 0.0).astype(BF16)
    return jnp.concatenate([vt_ref[j, hh * V_HEAD_DIM:(hh + 1) * V_HEAD_DIM, :], ones], axis=0)


def _causal_t(tq):
    key = lax.broadcasted_iota(jnp.int32, (tq, tq), 0)
    qry = lax.broadcasted_iota(jnp.int32, (tq, tq), 1)
    return key <= qry


def _moba_kernel(slope_ref, qt_ref, k_ref, e_ref, vt_ref, km_ref, o_ref, sa_ref, sb_ref):
    tq = ATT_TILE
    hp = pl.program_id(1)
    i = pl.program_id(2)
    nblk = km_ref.shape[1]
    frow = lax.broadcasted_iota(jnp.int32, (LANES, tq), 0)
    row = lax.broadcasted_iota(jnp.int32, (BIAS_LO, tq), 0)
    prow = lax.broadcasted_iota(jnp.int32, (LANES - POS_HI, tq), 0) + POS_HI

    qaug = []
    for hh in range(ATT_HEADS):
        head = hp * ATT_HEADS + hh
        grp = slice(hh // 2 * LANES, (hh // 2 + 1) * LANES)
        qt2 = qt_ref[0, grp, :]
        head_rows = (frow >= hh % 2 * HEAD_DIM_A) & (frow < (hh % 2 + 1) * HEAD_DIM_A)
        qh = jnp.where(head_rows, qt2, jnp.zeros_like(qt2))
        kmean = km_ref[0, :, grp].astype(BF16)
        if nblk < BIAS_LO:
            kmean = jnp.concatenate([kmean, jnp.zeros((BIAS_LO - nblk, LANES), BF16)], axis=0)
        g = jnp.where(row < i, _dot(kmean, qh), -jnp.inf)
        picked = row == i
        for r in range(MOBA_TOPK):
            m = jnp.max(g, axis=0, keepdims=True)
            idx = jnp.min(jnp.where(g == m, row, BIAS_LO), axis=0, keepdims=True)
            hit = row == idx
            picked = picked | (hit & (r < i))
            g = jnp.where(hit, -jnp.inf, g)
        dist = (i - row).astype(F32) * (slope_ref[head] * (MOBA_BLOCK * LOG2E))
        b = jnp.where(picked, -dist, NEG)
        b_hi = b.astype(BF16)
        b_lo = (b - b_hi.astype(F32)).astype(BF16)
        ones = jnp.where((prow == POS_HI + head) | (prow == POS_LO + head), 1.0, 0.0).astype(BF16)
        qs = (qh.astype(F32) * LOG2E).astype(BF16)
        qaug.append(jnp.concatenate([qs, b_hi, b_lo, ones], axis=0))

    def scores(hh, j):
        rows = pl.ds(pl.multiple_of(j * tq, tq), tq)
        grp = slice(hh // 2 * LANES, (hh // 2 + 1) * LANES)
        return _dot(jnp.concatenate([k_ref[rows, grp], e_ref[rows, :]], axis=1), qaug[hh])

    def values(hh, j):
        return _values_tile(vt_ref, hh, j)

    o_ref[...] = _attention_loop(i, scores, values, _causal_t(tq), tq, sa_ref, sb_ref).astype(BF16)


def _moba(qat, ka, vat, kmean, etab, slopes, batch, seq):
    t = ka.shape[0]
    tq = ATT_TILE
    nq = seq // tq
    nblk = seq // MOBA_BLOCK
    nh = ATT_HEADS
    km = kmean.reshape(batch, nblk, WIDTH_A)
    return pl.pallas_call(
        _moba_kernel,
        grid=(batch, N_HEADS_A // nh, nq),
        in_specs=[
            pl.BlockSpec(memory_space=pltpu.SMEM),
            pl.BlockSpec((1, nh * HEAD_DIM_A, tq), lambda b, h, i: (b * nq + i, h, 0)),
            pl.BlockSpec((seq, nh * HEAD_DIM_A), lambda b, h, i: (b, h)),
            pl.BlockSpec((seq, LANES), lambda b, h, i: (0, 0)),
            pl.BlockSpec((nq, nh * HEAD_DIM_A, tq), lambda b, h, i: (b, h, 0)),
            pl.BlockSpec((1, nblk, nh * HEAD_DIM_A), lambda b, h, i: (b, 0, h)),
        ],
        out_specs=pl.BlockSpec((tq, nh * HEAD_DIM_A), lambda b, h, i: (b * nq + i, h)),
        out_shape=jax.ShapeDtypeStruct((t, WIDTH_A), BF16),
        scratch_shapes=[pltpu.VMEM((nh, tq, tq), F32), pltpu.VMEM((nh, tq, tq), F32)],
        compiler_params=_params("parallel", "parallel", "arbitrary"),
        name="moba",
    )(slopes, qat, ka, etab, vat, km)


def _mla_kernel(qt_ref, k_ref, vt_ref, o_ref, sa_ref, sb_ref):
    tq = ATT_TILE
    i = pl.program_id(2)
    qt = [qt_ref[0, hh * LANES:(hh + 1) * LANES, :] for hh in range(ATT_HEADS)]

    def scores(hh, j):
        return _dot(k_ref[pl.ds(pl.multiple_of(j * tq, tq), tq), hh * LANES:(hh + 1) * LANES], qt[hh])

    def values(hh, j):
        return _values_tile(vt_ref, hh, j)

    o_ref[...] = _attention_loop(i, scores, values, _causal_t(tq), tq, sa_ref, sb_ref).astype(BF16)


def _mla(qmt, km, vmt, batch, seq):
    t = km.shape[0]
    tq = ATT_TILE
    nq = seq // tq
    nh = ATT_HEADS
    return pl.pallas_call(
        _mla_kernel,
        grid=(batch, N_HEADS_B // nh, nq),
        in_specs=[
            pl.BlockSpec((1, nh * LANES, tq), lambda b, h, i: (b * nq + i, h, 0)),
            pl.BlockSpec((seq, nh * LANES), lambda b, h, i: (b, h)),
            pl.BlockSpec((nq, nh * V_HEAD_DIM, tq), lambda b, h, i: (b, h, 0)),
        ],
        out_specs=pl.BlockSpec((tq, nh * V_HEAD_DIM), lambda b, h, i: (b * nq + i, h)),
        out_shape=jax.ShapeDtypeStruct((t, WIDTH_B), BF16),
        scratch_shapes=[pltpu.VMEM((nh, tq, tq), F32), pltpu.VMEM((nh, tq, tq), F32)],
        compiler_params=_params("parallel", "parallel", "arbitrary"),
        name="mla",
    )(qmt, km, vmt)


def _pack_halves(y):
    lo = pltpu.bitcast(y[:, :HALF].astype(BF16).astype(F32), U32)
    hi = pltpu.bitcast(y[:, HALF:].astype(BF16).astype(F32), U32)
    return (hi & jnp.uint32(0xFFFF0000)) | (lo >> 16)


def _unpack_halves(w):
    return pltpu.bitcast(w << 16, F32), pltpu.bitcast(w & jnp.uint32(0xFFFF0000), F32)


def _merge_kernel(x_ref, oa_ref, ob_ref, sa_ref, sb_ref, wpa_ref, wpb_ref, wo_ref, g_ref, b_ref, o_ref, op_ref):
    pa = _dot(oa_ref[...], wpa_ref[...])
    pb = _dot(ob_ref[...], wpb_ref[...])
    merged = sa_ref[...].astype(F32) * pa + sb_ref[...].astype(F32) * pb
    hmix = _dot(merged.astype(BF16), wo_ref[...])
    y = _layer_norm(ALPHA * x_ref[...] + hmix, g_ref[...], b_ref[...])
    o_ref[...] = y
    op_ref[...] = _pack_halves(y)


def _merge(x, oa, ob, sa, sb, wp):
    t = x.shape[0]
    tm = MERGE_TILE
    row = lambda w: pl.BlockSpec((tm, w), lambda i: (i, 0))
    weights = [wp['wpa'], wp['wpb'], wp['wo'], wp['ln1_g'], wp['ln1_b']]
    return pl.pallas_call(
        _merge_kernel,
        grid=(t // tm,),
        in_specs=[row(D_MODEL), row(WIDTH_A), row(WIDTH_B), row(D_MODEL), row(D_MODEL)] + [_full(w) for w in weights],
        out_specs=[row(D_MODEL), row(HALF)],
        out_shape=[jax.ShapeDtypeStruct((t, D_MODEL), F32), jax.ShapeDtypeStruct((t, HALF), U32)],
        compiler_params=_params("parallel"),
        name="merge",
    )(x, oa, ob, sa, sb, *weights)


def _router_kernel(x_ref, wh_ref, wl_ref, rb_ref, idx_ref, rank_ref, w_ref, cnt_ref):
    tm = x_ref.shape[0]

    @pl.when(pl.program_id(0) == 0)
    def _():
        cnt_ref[...] = jnp.zeros_like(cnt_ref)

    x = x_ref[...]
    xh = x.astype(BF16)
    xl = (x - xh.astype(F32)).astype(BF16)
    wh, wl = wh_ref[...], wl_ref[...]
    logits = _dot_nt(wh, xh) + (_dot_nt(wh, xl) + _dot_nt(wl, xh))
    scores = _sigmoid(logits)
    choice = scores + rb_ref[...]
    row = lax.broadcasted_iota(jnp.int32, (GROUP_SIZE, tm), 0)
    groups = [choice[g * GROUP_SIZE:(g + 1) * GROUP_SIZE, :] for g in range(N_GROUPS)]
    gscore = []
    for blk in groups:
        m1 = jnp.max(blk, axis=0, keepdims=True)
        first = jnp.min(jnp.where(blk == m1, row, GROUP_SIZE), axis=0, keepdims=True)
        m2 = jnp.max(jnp.where(row == first, -jnp.inf, blk), axis=0, keepdims=True)
        gscore.append(m1 + m2)
    masked = []
    for g in range(N_GROUPS):
        ahead = jnp.zeros((1, tm), jnp.int32)
        for o in range(N_GROUPS):
            if o < g:
                ahead += (gscore[o] >= gscore[g]).astype(jnp.int32)
            elif o > g:
                ahead += (gscore[o] > gscore[g]).astype(jnp.int32)
        masked.append(jnp.where(ahead < TOPK_GROUPS, groups[g], -jnp.inf))
    cur = jnp.concatenate(masked, axis=0)
    erow = lax.broadcasted_iota(jnp.int32, (N_EXPERTS, tm), 0)
    hits, idxs, ws = [], [], []
    for _ in range(TOP_K):
        m = jnp.max(cur, axis=0, keepdims=True)
        e = jnp.min(jnp.where(cur == m, erow, N_EXPERTS), axis=0, keepdims=True)
        hit = erow == e
        hits.append(hit)
        idxs.append(e)
        ws.append(jnp.sum(jnp.where(hit, scores, 0.0), axis=0, keepdims=True))
        cur = jnp.where(hit, -jnp.inf, cur)
    total = ws[0]
    for w in ws[1:]:
        total = total + w
    member = hits[0]
    for hit in hits[1:]:
        member = member | hit
    member = jnp.where(member, 1.0, 0.0).astype(BF16)
    t_src = lax.broadcasted_iota(jnp.int32, (tm, tm), 0)
    t_dst = lax.broadcasted_iota(jnp.int32, (tm, tm), 1)
    before = _dot(member, jnp.where(t_src < t_dst, 1.0, 0.0).astype(BF16))
    base = cnt_ref[...]
    before = before + jnp.concatenate([base] * (tm // LANES), axis=1)
    cnt_ref[...] = base + _dot(member, jnp.ones((tm, LANES), BF16))
    for r in range(TOP_K):
        idx_ref[0, r:r + 1, :] = idxs[r]
        rank_ref[0, r:r + 1, :] = jnp.sum(jnp.where(hits[r], before, 0.0), axis=0, keepdims=True).astype(jnp.int32)
        w_ref[r:r + 1, :] = ws[r] / total * ROUTED_SCALE


def _router(x1, wp):
    t = x1.shape[0]
    tm = ROW_TILE
    nt = t // tm
    weights = [wp['wr_hi'], wp['wr_lo'], wp['rbias']]
    tile = pl.BlockSpec((1, TOP_K, tm), lambda i: (i, 0, 0))
    return pl.pallas_call(
        _router_kernel,
        grid=(nt,),
        in_specs=[pl.BlockSpec((tm, D_MODEL), lambda i: (i, 0))] + [_full(w) for w in weights],
        out_specs=[tile, tile, pl.BlockSpec((TOP_K, tm), lambda i: (0, i)),
                   pl.BlockSpec((N_EXPERTS, LANES), lambda i: (0, 0))],
        out_shape=[jax.ShapeDtypeStruct((nt, TOP_K, tm), jnp.int32), jax.ShapeDtypeStruct((nt, TOP_K, tm), jnp.int32),
                   jax.ShapeDtypeStruct((TOP_K, t), F32), jax.ShapeDtypeStruct((N_EXPERTS, LANES), F32)],
        compiler_params=_params("arbitrary"),
        name="router",
    )(x1, *weights)


def _pos_kernel(idx_ref, rank_ref, pq_ref, pos_ref):
    tm = idx_ref.shape[2]
    erow = lax.broadcasted_iota(jnp.int32, (N_EXPERTS, tm), 0)
    for k in range(TOP_K):
        onehot = jnp.where(erow == idx_ref[0, k:k + 1, :], 1.0, 0.0).astype(BF16)
        q = _dot(pq_ref[...], onehot)
        blk = (q[0:1, :] * 32.0 + q[1:2, :]).astype(jnp.int32)
        pos_ref[k:k + 1, :] = blk * EXPERT_BLOCK + rank_ref[0, k:k + 1, :]


def _positions(idx3, rank3, pstart):
    nt, _, tm = idx3.shape
    blk = pstart // EXPERT_BLOCK
    pq = jnp.concatenate([(blk // 32)[None], (blk % 32)[None], jnp.zeros((6, N_EXPERTS), jnp.int32)], axis=0).astype(F32)
    tile = pl.BlockSpec((1, TOP_K, tm), lambda i: (i, 0, 0))
    return pl.pallas_call(
        _pos_kernel,
        grid=(nt,),
        in_specs=[tile, tile, pl.BlockSpec((8, N_EXPERTS), lambda i: (0, 0))],
        out_specs=pl.BlockSpec((TOP_K, tm), lambda i: (0, i)),
        out_shape=jax.ShapeDtypeStruct((TOP_K, nt * tm), jnp.int32),
        compiler_params=_params("parallel"),
        name="moe_positions",
    )(idx3, rank3, pq.astype(BF16))


def _sc_mesh():
    return plsc.VectorSubcoreMesh(core_axis_name="c", subcore_axis_name="s", num_cores=SC_CORES,
                                  num_subcores=SC_SUBCORES)


def _sc_worker():
    return lax.axis_index("s") * SC_CORES + lax.axis_index("c")


def _sc_scatter_kernel(x_hbm, pos_hbm, out_hbm, idx_v, rows_v):
    nchunk = idx_v.shape[1]
    w = _sc_worker()
    pltpu.sync_copy(pos_hbm.at[w], idx_v)

    @pl.loop(0, nchunk)
    def _(c):
        pltpu.sync_copy(x_hbm.at[pl.ds((w * nchunk + c) * SC_CHUNK, SC_CHUNK)], rows_v)
        for k in range(TOP_K):
            pltpu.sync_copy(rows_v, out_hbm.at[idx_v.at[k, c]])


def _sc_dispatch(x1p, pos_kt, n_rows):
    t = x1p.shape[0]
    workers = SC_CORES * SC_SUBCORES
    nchunk = t // (workers * SC_CHUNK)
    assert t == workers * nchunk * SC_CHUNK
    pos4 = pos_kt.reshape(TOP_K, workers, nchunk, SC_CHUNK).transpose(1, 0, 2, 3)
    return pl.kernel(
        _sc_scatter_kernel,
        out_type=jax.ShapeDtypeStruct((n_rows, HALF), U32),
        mesh=_sc_mesh(),
        scratch_types=[pltpu.VMEM((TOP_K, nchunk, SC_CHUNK), jnp.int32), pltpu.VMEM((SC_CHUNK, HALF), U32)],
        name="moe_dispatch_sc",
    )(x1p, pos4)


def _expert_kernel(layer, be_ref, nused_ref, nvalid_ref, first_ref, slot_ref, ahead_ref, head_ref,
                   x_ref, wg_hbm, wu_hbm, wd_hbm, y_ref, wg_f, wu_f, wd_f, wg_b, wu_b, wd_b, sem):
    def fetch(e, slot):
        return [pltpu.make_async_copy(src.at[layer, e], dst.at[slot], sem.at[slot, n])
                for n, (src, dst) in enumerate(((wg_hbm, wg_f), (wu_hbm, wu_f), (wd_hbm, wd_f)))]

    @pl.when(pl.program_id(0) == 0)
    def _():
        for copy in fetch(head_ref[0], 0):
            copy.start()

        @pl.when(head_ref[1] >= 0)
        def _():
            for copy in fetch(head_ref[1], 1):
                copy.start()

    def one_block(b, rows):
        @pl.when((b < nused_ref[0]) & (first_ref[b] == 1))
        def _():
            slot = slot_ref[b]
            for copy in fetch(be_ref[b], slot):
                copy.wait()
            wg_b[...] = wg_f[slot].astype(BF16)
            wu_b[...] = wu_f[slot].astype(BF16)
            wd_b[...] = wd_f[slot].astype(BF16)

            @pl.when(ahead_ref[b] >= 0)
            def _():
                for copy in fetch(ahead_ref[b], slot):
                    copy.start()

        @pl.when(b < nused_ref[0])
        def _():
            live = lax.broadcasted_iota(jnp.int32, (EXPERT_BLOCK, HALF), 0) < nvalid_ref[b]
            xlo, xhi = (h.astype(BF16) for h in _unpack_halves(jnp.where(live, x_ref[rows, :], jnp.uint32(0))))
            g = _dot(xlo, wg_b[:HALF, :]) + _dot(xhi, wg_b[HALF:, :])
            u = _dot(xlo, wu_b[:HALF, :]) + _dot(xhi, wu_b[HALF:, :])
            a = (g * _sigmoid(g) * u).astype(BF16)
            y_ref[rows, :] = _pack_halves(_dot(a, wd_b[...]))

        @pl.when(b >= nused_ref[0])
        def _():
            y_ref[rows, :] = jnp.zeros((EXPERT_BLOCK, HALF), U32)

    for sub in range(EXPERT_STEP):
        one_block(pl.program_id(0) * EXPERT_STEP + sub, slice(sub * EXPERT_BLOCK, (sub + 1) * EXPERT_BLOCK))


def _experts(xs, block_e, nused, nvalid, counts, w_gate, w_up, w_down, layer, n_blocks):
    rows = EXPERT_BLOCK
    blocks = jnp.arange(n_blocks, dtype=jnp.int32)
    first = ((blocks == 0) | (block_e != jnp.roll(block_e, 1))) & (blocks < nused[0])
    run = jnp.cumsum(first.astype(jnp.int32)) - 1
    run_e = jnp.nonzero(counts > 0, size=N_EXPERTS, fill_value=-1)[0].astype(jnp.int32)
    ahead = jnp.concatenate([run_e, jnp.full((2,), -1, jnp.int32)])[jnp.clip(run, 0, N_EXPERTS - 1) + 2]
    step_rows = EXPERT_STEP * rows
    grid_spec = pltpu.PrefetchScalarGridSpec(
        num_scalar_prefetch=7,
        grid=(n_blocks // EXPERT_STEP,),
        in_specs=[
            pl.BlockSpec((step_rows, HALF), lambda s, be, nu, *_: (jnp.minimum(s, (nu[0] - 1) // EXPERT_STEP), 0)),
            pl.BlockSpec(memory_space=pl.ANY), pl.BlockSpec(memory_space=pl.ANY), pl.BlockSpec(memory_space=pl.ANY),
        ],
        out_specs=pl.BlockSpec((step_rows, HALF), lambda s, *_: (s, 0)),
        scratch_shapes=[
            pltpu.VMEM((2, D_MODEL, D_EXPERT), F32), pltpu.VMEM((2, D_MODEL, D_EXPERT), F32),
            pltpu.VMEM((2, D_EXPERT, D_MODEL), F32),
            pltpu.VMEM((D_MODEL, D_EXPERT), BF16), pltpu.VMEM((D_MODEL, D_EXPERT), BF16),
            pltpu.VMEM((D_EXPERT, D_MODEL), BF16),
            pltpu.SemaphoreType.DMA((2, 3)),
        ],
    )
    return pl.pallas_call(
        functools.partial(_expert_kernel, layer),
        grid_spec=grid_spec,
        out_shape=jax.ShapeDtypeStruct((n_blocks * rows, HALF), U32),
        compiler_params=_params("arbitrary"),
        name="moe_experts",
    )(block_e, nused, nvalid, first.astype(jnp.int32), (run % 2).astype(jnp.int32), ahead, run_e[:2],
      xs, w_gate, w_up, w_down)


def _sc_gather_kernel(table_hbm, idx_hbm, out_hbm, idx_v, rows_v, gsem, wsem):
    per_worker = idx_v.shape[0]
    base = _sc_worker() * per_worker
    pltpu.sync_copy(idx_hbm.at[pl.ds(base, per_worker)], idx_v)

    @pl.loop(0, per_worker // SC_CHUNK, step=SC_BUFS)
    def _(c):
        offs = [(c + n) * SC_CHUNK for n in range(SC_BUFS)]
        gathers = [pltpu.async_copy(table_hbm.at[idx_v.at[pl.ds(offs[n], SC_CHUNK)]], rows_v.at[n], gsem.at[n])
                   for n in range(SC_BUFS)]
        writes = []
        for n in range(SC_BUFS):
            gathers[n].wait()
            writes.append(pltpu.async_copy(rows_v.at[n], out_hbm.at[pl.ds(base + offs[n], SC_CHUNK)], wsem.at[n]))
        for write in writes:
            write.wait()


def _sc_gather_rows(table, idx):
    n = idx.shape[0]
    workers = SC_CORES * SC_SUBCORES
    assert n % (workers * SC_CHUNK * SC_BUFS) == 0
    return pl.kernel(
        _sc_gather_kernel,
        out_type=jax.ShapeDtypeStruct((n, HALF), U32),
        mesh=_sc_mesh(),
        scratch_types=[pltpu.VMEM((n // workers,), jnp.int32), pltpu.VMEM((SC_BUFS, SC_CHUNK, HALF), U32),
                       pltpu.SemaphoreType.DMA((SC_BUFS,)), pltpu.SemaphoreType.DMA((SC_BUFS,))],
        name="moe_gather_sc",
    )(table, idx)


def _combine_kernel(x_ref, w_ref, *refs):
    y_refs, (wsg_ref, wsu_ref, wsd_ref, g_ref, b_ref, o_ref) = refs[:TOP_K], refs[TOP_K:]
    x = x_ref[...]
    xb = x.astype(BF16)
    g = _dot(xb, wsg_ref[...])
    u = _dot(xb, wsu_ref[...])
    shared = _dot((g * _sigmoid(g) * u).astype(BF16), wsd_ref[...])
    w = w_ref[...]
    lo, hi = (h * w[:, 0:1] for h in _unpack_halves(y_refs[0][...]))
    for k in range(1, TOP_K):
        lo_k, hi_k = _unpack_halves(y_refs[k][...])
        lo = lo + lo_k * w[:, k:k + 1]
        hi = hi + hi_k * w[:, k:k + 1]
    routed = jnp.concatenate([lo, hi], axis=1)
    o_ref[...] = _layer_norm(ALPHA * x + (shared + routed), g_ref[...], b_ref[...])


def _combine(x1, y8, w_tk, wp):
    t = x1.shape[0]
    tm = WIDE_TILE
    nt = t // tm
    weights = [wp['wsg'], wp['wsu'], wp['wsd'], wp['ln2_g'], wp['ln2_b']]
    y_specs = [pl.BlockSpec((tm, HALF), lambda i, k=k: (k * nt + i, 0)) for k in range(TOP_K)]
    return pl.pallas_call(
        _combine_kernel,
        grid=(nt,),
        in_specs=[pl.BlockSpec((tm, D_MODEL), lambda i: (i, 0)), pl.BlockSpec((tm, TOP_K), lambda i: (i, 0))]
                 + y_specs + [_full(w) for w in weights],
        out_specs=pl.BlockSpec((tm, D_MODEL), lambda i: (i, 0)),
        out_shape=jax.ShapeDtypeStruct((t, D_MODEL), F32),
        compiler_params=_params("parallel"),
        name="moe_combine",
    )(x1, w_tk, *([y8] * TOP_K), *weights)


def _moe(x1, x1p, wp, w_e_gate, w_e_up, w_e_down, layer):
    t = x1.shape[0]
    rows = EXPERT_BLOCK
    n_blocks = -(-(t * TOP_K + N_EXPERTS * (rows - 1)) // (rows * EXPERT_STEP)) * EXPERT_STEP
    idx3, rank3, top_w, cnt = _router(x1, wp)
    counts = cnt[:, 0].astype(jnp.int32)
    padded = (counts + rows - 1) // rows * rows
    padded_end = jnp.cumsum(padded)
    pstart = (padded_end - padded).astype(jnp.int32)
    nused = (padded_end[-1] // rows).astype(jnp.int32).reshape(1)
    blocks = jnp.arange(n_blocks, dtype=jnp.int32)
    block_row = jnp.minimum(blocks, nused[0] - 1) * rows
    block_e = jnp.sum((padded_end[None, :] <= block_row[:, None]).astype(jnp.int32), axis=1)
    block_e = jnp.minimum(block_e, N_EXPERTS - 1)
    nvalid = jnp.clip(counts[block_e] - (blocks * rows - pstart[block_e]), 0, rows).astype(jnp.int32)
    pos_kt = _positions(idx3, rank3, pstart)
    xs = _sc_dispatch(x1p, pos_kt, n_blocks * rows)
    ys = _experts(xs, block_e, nused, nvalid, counts, w_e_gate, w_e_up, w_e_down, layer, n_blocks)
    return _combine(x1, _sc_gather_rows(ys, pos_kt.reshape(TOP_K * t)), top_w.T, wp)


def _head_groups_t(w, used):
    k, h, _ = w.shape
    return jnp.pad(w, ((0, 0), (0, 0), (0, LANES - used))).reshape(k, h * LANES).T


def _prep_layer(l, w_in, b_gate, q_norm, w_uq, kv_norm, w_ukv, w_proj_a, w_proj_b, w_out, ln1_g, ln1_b,
                w_router, router_bias, w_s_gate, w_s_up, w_s_down, ln2_g, ln2_b):
    w = w_in[l]
    o = 0
    cols = {}
    for name, width in (('qa', WIDTH_A), ('ka', WIDTH_A), ('va', WIDTH_A), ('cq', Q_LORA_RANK),
                        ('ckv', KV_LORA_RANK), ('kr', QK_ROPE_DIM), ('ga', D_MODEL), ('gb', D_MODEL)):
        cols[name] = w[:, o:o + width]
        o += width
    wkr = jnp.pad(cols['kr'], ((0, 0), (QK_NOPE_DIM, LANES - QK_NOPE_DIM - QK_ROPE_DIM)))
    dqk = QK_NOPE_DIM + QK_ROPE_DIM
    wq = w_uq[l].reshape(Q_LORA_RANK, N_HEADS_B, dqk) * (dqk ** -0.5 * LOG2E)
    wkv = w_ukv[l].reshape(KV_LORA_RANK, N_HEADS_B, QK_NOPE_DIM + V_HEAD_DIM)
    wuk = jnp.pad(wkv[:, :, :QK_NOPE_DIM], ((0, 0), (0, 0), (0, LANES - QK_NOPE_DIM))).reshape(KV_LORA_RANK, N_HEADS_B * LANES)
    wr_t = w_router[l].T
    wr_hi = wr_t.astype(BF16)
    return dict(
        wqt=(cols['qa'] * HEAD_DIM_A ** -0.5).T.astype(BF16), wk=cols['ka'].astype(BF16),
        wvt=cols['va'].T.astype(BF16),
        wcq=cols['cq'].astype(BF16), wckv=cols['ckv'].astype(BF16), wkr=wkr.astype(BF16),
        wg=jnp.concatenate([cols['ga'], cols['gb']], axis=1).astype(BF16),
        bg=b_gate[l].reshape(1, 2 * D_MODEL), qn=q_norm[l].reshape(1, Q_LORA_RANK), kvn=kv_norm[l].reshape(1, KV_LORA_RANK),
        wuqt=_head_groups_t(wq, dqk).astype(BF16), wuk=wuk.astype(BF16),
        wuvt=wkv[:, :, QK_NOPE_DIM:].reshape(KV_LORA_RANK, WIDTH_B).T.astype(BF16),
        wpa=w_proj_a[l].astype(BF16), wpb=w_proj_b[l].astype(BF16), wo=w_out[l].astype(BF16),
        ln1_g=ln1_g[l].reshape(1, D_MODEL), ln1_b=ln1_b[l].reshape(1, D_MODEL),
        wr_hi=wr_hi, wr_lo=(wr_t - wr_hi.astype(F32)).astype(BF16), rbias=router_bias[l].reshape(N_EXPERTS, 1),
        wsg=w_s_gate[l].astype(BF16), wsu=w_s_up[l].astype(BF16), wsd=w_s_down[l].astype(BF16),
        ln2_g=ln2_g[l].reshape(1, D_MODEL), ln2_b=ln2_b[l].reshape(1, D_MODEL),
    )


def _rope_tables(seq):
    pos = jnp.arange(seq, dtype=F32)
    inv_freq = ROPE_THETA ** (-jnp.arange(0, QK_ROPE_DIM, 2, dtype=F32) / QK_ROPE_DIM)
    ang = pos[:, None] * inv_freq[None, :]
    cos, sin = jnp.cos(ang), jnp.sin(ang)
    half = QK_ROPE_DIM // 2

    def build(cs, sn, axis):
        shape = lambda n: (seq, n) if axis == 1 else (n, seq)
        z = lambda n: jnp.zeros(shape(n), F32)
        c = jnp.concatenate([jnp.ones(shape(QK_NOPE_DIM), F32), cs, cs, z(LANES - QK_NOPE_DIM - QK_ROPE_DIM)], axis=axis)
        s1 = jnp.concatenate([z(QK_NOPE_DIM), -sn, z(LANES - QK_NOPE_DIM - half)], axis=axis)
        s2 = jnp.concatenate([z(QK_NOPE_DIM + half), sn, z(LANES - QK_NOPE_DIM - QK_ROPE_DIM)], axis=axis)
        return c, s1, s2

    return build(cos, sin, 1) + build(cos.T, sin.T, 0)


def _moba_key_table(seq, slopes):
    blk = jnp.arange(seq, dtype=jnp.int32) // MOBA_BLOCK
    onehot = (blk[:, None] == jnp.arange(BIAS_LO, dtype=jnp.int32)[None, :]).astype(F32)
    inblk = (jnp.arange(seq, dtype=jnp.int32) % MOBA_BLOCK).astype(F32)[:, None] * (slopes * LOG2E)[None, :]
    hi = inblk.astype(BF16)
    lo = (inblk - hi.astype(F32)).astype(BF16)
    pad = jnp.zeros((seq, LANES - POS_LO - N_HEADS_A), BF16)
    return jnp.concatenate([onehot.astype(BF16), onehot.astype(BF16), hi, lo, pad], axis=1)


def kernel(x, w_in, b_gate, q_norm, w_uq, kv_norm, w_ukv, w_proj_a, w_proj_b, w_out, ln1_g, ln1_b, w_router, router_bias, w_e_gate, w_e_up, w_e_down, w_s_gate, w_s_up, w_s_down, ln2_g, ln2_b):
    batch, seq, d = x.shape
    assert d == D_MODEL and seq % MOBA_BLOCK == 0 and MOBA_TOPK <= seq // MOBA_BLOCK <= BIAS_LO
    assert POS_LO + N_HEADS_A <= LANES and POS_HI + N_HEADS_A <= POS_LO and (batch * seq) % MERGE_TILE == 0
    tabs = _rope_tables(seq)
    slopes = jnp.asarray(np.exp2(-8.0 * (np.arange(N_HEADS_A) + 1.0) / N_HEADS_A), F32)
    etab = _moba_key_table(seq, slopes)
    h = x.reshape(batch * seq, d)
    for l in range(DEPTH):
        wp = _prep_layer(l, w_in, b_gate, q_norm, w_uq, kv_norm, w_ukv, w_proj_a, w_proj_b, w_out, ln1_g, ln1_b,
                         w_router, router_bias, w_s_gate, w_s_up, w_s_down, ln2_g, ln2_b)
        qat, ka, vat, kmean, qmt, km, vmt, sa, sb = _inproj(h, wp, tabs, seq)
        oa = _moba(qat, ka, vat, kmean, etab, slopes, batch, seq)
        ob = _mla(qmt, km, vmt, batch, seq)
        x1, x1p = _merge(h, oa, ob, sa, sb, wp)
        h = _moe(x1, x1p, wp, w_e_gate, w_e_up, w_e_down, l)
    return h.reshape(batch, seq, d)
```

```python
import functools

import numpy as np

import jax
import jax.numpy as jnp
from jax import lax
from jax.experimental import pallas as pl
from jax.experimental.pallas import tpu as pltpu
from jax.experimental.pallas import tpu_sc as plsc

D_MODEL = 1024
N_HEADS_A = 8
HEAD_DIM_A = 64
WIDTH_A = N_HEADS_A * HEAD_DIM_A
MOBA_BLOCK = 256
MOBA_TOPK = 3
N_HEADS_B = 8
QK_NOPE_DIM = 64
QK_ROPE_DIM = 32
V_HEAD_DIM = 64
Q_LORA_RANK = 384
KV_LORA_RANK = 256
WIDTH_B = N_HEADS_B * V_HEAD_DIM
ROPE_THETA = 10000.0
N_EXPERTS = 256
TOP_K = 8
N_GROUPS = 8
TOPK_GROUPS = 4
GROUP_SIZE = N_EXPERTS // N_GROUPS
D_EXPERT = 256
D_SHARED = 256
ROUTED_SCALE = 2.5
DEPTH = 2
ALPHA = (2 * DEPTH) ** 0.25
LN_EPS = 1e-5
RMS_EPS = 1e-6

LANES = 128
NEG = -1e30
LOG2E = float(np.log2(np.e))
ROW_TILE = 256
WIDE_TILE = 512
MERGE_TILE = 1024
ATT_TILE = 256
EXPERT_BLOCK = 256
EXPERT_STEP = 4
WEIGHT_DMA_PRIORITY = 1
SC_CORES = 2
SC_SUBCORES = 16
SC_CHUNK = 64
SC_BUFS = 2
VMEM_LIMIT = 56 * 1024 * 1024
HALF = D_MODEL // 2
ATT_HEADS = 4
ATT_GROUP = 4
PV_ROWS = 80
BIAS_HI, BIAS_LO, POS_HI, POS_LO = 0, 32, 64, 72

BF16 = jnp.bfloat16
F32 = jnp.float32
U32 = jnp.uint32


def _dot(a, b):
    return jnp.dot(a, b, preferred_element_type=F32)


def _dot_nt(a, b):
    return lax.dot_general(a, b, (((1,), (1,)), ((), ())), preferred_element_type=F32)


def _sigmoid(x):
    return 1.0 / (1.0 + jnp.exp(-x))


def _layer_norm(y, g, b):
    mu = jnp.mean(y, axis=-1, keepdims=True)
    d = y - mu
    var = jnp.mean(d * d, axis=-1, keepdims=True)
    return d * lax.rsqrt(var + LN_EPS) * g + b


def _params(*sem):
    return pltpu.CompilerParams(dimension_semantics=sem, vmem_limit_bytes=VMEM_LIMIT)


def _full(a):
    return pl.BlockSpec(a.shape, lambda *_: (0,) * a.ndim)


def _inproj_kernel(x_ref, wqt_ref, wk_ref, wvt_ref, wcq_ref, wckv_ref, wkr_ref, wg_ref, bg_ref,
                   qn_ref, kvn_ref, wuqt_ref, wuk_ref, wuvt_ref, cos_ref, s1_ref, s2_ref, cost_ref, s1t_ref, s2t_ref,
                   qat_ref, ka_ref, vat_ref, kmean_ref, qmt_ref, km_ref, vmt_ref, sa_ref, sb_ref):
    xb = x_ref[...].astype(BF16)
    half = QK_ROPE_DIM // 2
    qat_ref[0] = _dot_nt(wqt_ref[...], xb).astype(BF16)
    k = _dot(xb, wk_ref[...])
    ka_ref[...] = k.astype(BF16)
    kmean_ref[0] = jnp.mean(k, axis=0, keepdims=True)
    vat_ref[0] = _dot_nt(wvt_ref[...], xb).astype(BF16)

    cq = _dot(xb, wcq_ref[...])
    cqn = (cq * lax.rsqrt(jnp.mean(cq * cq, axis=-1, keepdims=True) + RMS_EPS) * qn_ref[...]).astype(BF16)
    ckv = _dot(xb, wckv_ref[...])
    ckvn = (ckv * lax.rsqrt(jnp.mean(ckv * ckv, axis=-1, keepdims=True) + RMS_EPS) * kvn_ref[...]).astype(BF16)
    qt = _dot_nt(wuqt_ref[...], cqn)
    ct, s1t, s2t = cost_ref[...], s1t_ref[...], s2t_ref[...]
    for h in range(N_HEADS_B):
        t = qt[h * LANES:(h + 1) * LANES, :]
        rot = t * ct + pltpu.roll(t, LANES - half, 0) * s1t + pltpu.roll(t, half, 0) * s2t
        qmt_ref[0, h * LANES:(h + 1) * LANES, :] = rot.astype(BF16)
    kn = _dot(ckvn, wuk_ref[...])
    kr = _dot(xb, wkr_ref[...])
    c, s1, s2 = cos_ref[...], s1_ref[...], s2_ref[...]
    krot = kr * c + pltpu.roll(kr, LANES - half, 1) * s1 + pltpu.roll(kr, half, 1) * s2
    for h in range(N_HEADS_B):
        sl = slice(h * LANES, (h + 1) * LANES)
        km_ref[:, sl] = (kn[:, sl] + krot).astype(BF16)
    vmt_ref[0] = _dot_nt(wuvt_ref[...], ckvn).astype(BF16)

    sig = _sigmoid(_dot(xb, wg_ref[...]) + bg_ref[...])
    sa_ref[...] = sig[:, :D_MODEL].astype(BF16)
    sb_ref[...] = sig[:, D_MODEL:].astype(BF16)


def _inproj(x, wp, tabs, seq):
    t = x.shape[0]
    tm = ROW_TILE
    nt = t // tm
    npos = seq // tm
    row = lambda w: pl.BlockSpec((tm, w), lambda i: (i, 0))
    tile = lambda r: pl.BlockSpec((1, r, tm), lambda i: (i, 0, 0))
    tab = pl.BlockSpec((tm, LANES), lambda i: (i % npos, 0))
    tabt = pl.BlockSpec((LANES, tm), lambda i: (0, i % npos))
    weights = [wp['wqt'], wp['wk'], wp['wvt'], wp['wcq'], wp['wckv'], wp['wkr'], wp['wg'], wp['bg'],
               wp['qn'], wp['kvn'], wp['wuqt'], wp['wuk'], wp['wuvt']]
    hl = N_HEADS_B * LANES
    out_shape = [
        jax.ShapeDtypeStruct((nt, WIDTH_A, tm), BF16), jax.ShapeDtypeStruct((t, WIDTH_A), BF16),
        jax.ShapeDtypeStruct((nt, WIDTH_A, tm), BF16), jax.ShapeDtypeStruct((nt, 1, WIDTH_A), F32),
        jax.ShapeDtypeStruct((nt, hl, tm), BF16), jax.ShapeDtypeStruct((t, hl), BF16),
        jax.ShapeDtypeStruct((nt, WIDTH_B, tm), BF16),
        jax.ShapeDtypeStruct((t, D_MODEL), BF16), jax.ShapeDtypeStruct((t, D_MODEL), BF16),
    ]
    out_specs = [tile(WIDTH_A), row(WIDTH_A), tile(WIDTH_A),
                 pl.BlockSpec((1, 1, WIDTH_A), lambda i: (i, 0, 0)),
                 tile(hl), row(hl), tile(WIDTH_B), row(D_MODEL), row(D_MODEL)]
    return pl.pallas_call(
        _inproj_kernel,
        grid=(nt,),
        in_specs=[row(D_MODEL)] + [_full(w) for w in weights] + [tab, tab, tab, tabt, tabt, tabt],
        out_specs=out_specs,
        out_shape=out_shape,
        compiler_params=_params("parallel"),
        name="inproj",
    )(x, *weights, *tabs)


def _attend_init(tq):
    return jnp.full((1, tq), -jnp.inf, F32), jnp.zeros((PV_ROWS, tq), F32)


def _col_max(s_ref):
    return [jnp.max(s_ref[hh], axis=0, keepdims=True) for hh in range(ATT_HEADS)]


def _attend_staged(cur_ref, cur_max, state, vts, nxt_ref=None, next_scores=None):
    heads = range(ATT_HEADS)
    if nxt_ref is not None:
        for hh in heads:
            nxt_ref[hh] = next_scores(hh)
    new_m, scaled, pvs = [], [], []
    for hh in heads:
        m_i, acc = state[hh]
        m_new = jnp.maximum(m_i, cur_max[hh])
        new_m.append(m_new)
        scaled.append(jnp.exp2(m_i - m_new) * acc)
        pvs.append(_dot(vts[hh], jnp.exp2(cur_ref[hh] - m_new).astype(BF16)))
    nxt_max = _col_max(nxt_ref) if nxt_ref is not None else cur_max
    return tuple((m, a + pv) for m, a, pv in zip(new_m, scaled, pvs)), nxt_max


def _attention_loop(i, scores, values, causal, tq, sa_ref, sb_ref):
    heads = range(ATT_HEADS)
    bufs = (sa_ref, sb_ref)

    def vals(s):
        tile = jnp.where(s == 0, i, s - 1)
        return [values(hh, tile) for hh in heads]

    def ahead(s):
        return lambda hh: scores(hh, jnp.maximum(jnp.minimum(s, i - 1), 0))

    for hh in heads:
        sa_ref[hh] = jnp.where(causal, scores(hh, i), NEG)
    carry = (tuple(_attend_init(tq) for _ in heads), _col_max(sa_ref))
    nslots = i + 1

    def trip(n, c):
        for g in range(ATT_GROUP):
            s = n * ATT_GROUP + g
            c = _attend_staged(bufs[g % 2], c[1], c[0], vals(s), bufs[(g + 1) % 2], ahead(s))
        return c

    carry = lax.fori_loop(0, nslots // ATT_GROUP, trip, carry)
    done = nslots // ATT_GROUP * ATT_GROUP
    for g in range(ATT_GROUP - 1):
        def step(c, g=g):
            if g == ATT_GROUP - 2:
                return _attend_staged(bufs[g % 2], c[1], c[0], vals(done + g))
            return _attend_staged(bufs[g % 2], c[1], c[0], vals(done + g), bufs[(g + 1) % 2], ahead(done + g))

        carry = lax.cond(nslots - done > g, step, lambda c: c, carry)
    outs = []
    for hh in heads:
        _, acc = carry[0][hh]
        outs.append(acc[:V_HEAD_DIM, :] / acc[V_HEAD_DIM:V_HEAD_DIM + 1, :])
    return jnp.concatenate(outs, axis=0).T


def _values_tile(vt_ref, hh, j):
    tk = vt_ref.shape[2]
    pad_row = lax.broadcasted_iota(jnp.int32, (PV_ROWS - V_HEAD_DIM, tk), 0)
    ones = jnp.where(pad_row == 0, 1.0, 0.0).astype(BF16)
    return jnp.concatenate([vt_ref[j, hh * V_HEAD_DIM:(hh + 1) * V_HEAD_DIM, :], ones], axis=0)


def _causal_t(tq):
    key = lax.broadcasted_iota(jnp.int32, (tq, tq), 0)
    qry = lax.broadcasted_iota(jnp.int32, (tq, tq), 1)
    return key <= qry


def _moba_kernel(slope_ref, qt_ref, k_ref, e_ref, vt_ref, km_ref, o_ref, sa_ref, sb_ref):
    tq = ATT_TILE
    hp = pl.program_id(1)
    i = pl.program_id(2)
    nblk = km_ref.shape[1]
    frow = lax.broadcasted_iota(jnp.int32, (LANES, tq), 0)
    row = lax.broadcasted_iota(jnp.int32, (BIAS_LO, tq), 0)
    prow = lax.broadcasted_iota(jnp.int32, (LANES - POS_HI, tq), 0) + POS_HI

    qaug = []
    for hh in range(ATT_HEADS):
        head = hp * ATT_HEADS + hh
        grp = slice(hh // 2 * LANES, (hh // 2 + 1) * LANES)
        qt2 = qt_ref[0, grp, :]
        head_rows = (frow >= hh % 2 * HEAD_DIM_A) & (frow < (hh % 2 + 1) * HEAD_DIM_A)
        qh = jnp.where(head_rows, qt2, jnp.zeros_like(qt2))
        kmean = km_ref[0, :, grp].astype(BF16)
        if nblk < BIAS_LO:
            kmean = jnp.concatenate([kmean, jnp.zeros((BIAS_LO - nblk, LANES), BF16)], axis=0)
        g = jnp.where(row < i, _dot(kmean, qh), -jnp.inf)
        picked = row == i
        for r in range(MOBA_TOPK):
            m = jnp.max(g, axis=0, keepdims=True)
            idx = jnp.min(jnp.where(g == m, row, BIAS_LO), axis=0, keepdims=True)
            hit = row == idx
            picked = picked | (hit & (r < i))
            g = jnp.where(hit, -jnp.inf, g)
        dist = (i - row).astype(F32) * (slope_ref[head] * (MOBA_BLOCK * LOG2E))
        b = jnp.where(picked, -dist, NEG)
        b_hi = b.astype(BF16)
        b_lo = (b - b_hi.astype(F32)).astype(BF16)
        ones = jnp.where((prow == POS_HI + head) | (prow == POS_LO + head), 1.0, 0.0).astype(BF16)
        qs = (qh.astype(F32) * LOG2E).astype(BF16)
        qaug.append(jnp.concatenate([qs, b_hi, b_lo, ones], axis=0))

    def scores(hh, j):
        rows = pl.ds(pl.multiple_of(j * tq, tq), tq)
        grp = slice(hh // 2 * LANES, (hh // 2 + 1) * LANES)
        return _dot(jnp.concatenate([k_ref[rows, grp], e_ref[rows, :]], axis=1), qaug[hh])

    def values(hh, j):
        return _values_tile(vt_ref, hh, j)

    o_ref[...] = _attention_loop(i, scores, values, _causal_t(tq), tq, sa_ref, sb_ref).astype(BF16)


def _moba(qat, ka, vat, kmean, etab, slopes, batch, seq):
    t = ka.shape[0]
    tq = ATT_TILE
    nq = seq // tq
    nblk = seq // MOBA_BLOCK
    nh = ATT_HEADS
    km = kmean.reshape(batch, nblk, WIDTH_A)
    return pl.pallas_call(
        _moba_kernel,
        grid=(batch, N_HEADS_A // nh, nq),
        in_specs=[
            pl.BlockSpec(memory_space=pltpu.SMEM),
            pl.BlockSpec((1, nh * HEAD_DIM_A, tq), lambda b, h, i: (b * nq + i, h, 0)),
            pl.BlockSpec((seq, nh * HEAD_DIM_A), lambda b, h, i: (b, h)),
            pl.BlockSpec((seq, LANES), lambda b, h, i: (0, 0)),
            pl.BlockSpec((nq, nh * HEAD_DIM_A, tq), lambda b, h, i: (b, h, 0)),
            pl.BlockSpec((1, nblk, nh * HEAD_DIM_A), lambda b, h, i: (b, 0, h)),
        ],
        out_specs=pl.BlockSpec((tq, nh * HEAD_DIM_A), lambda b, h, i: (b * nq + i, h)),
        out_shape=jax.ShapeDtypeStruct((t, WIDTH_A), BF16),
        scratch_shapes=[pltpu.VMEM((nh, tq, tq), F32), pltpu.VMEM((nh, tq, tq), F32)],
        compiler_params=_params("parallel", "parallel", "arbitrary"),
        name="moba",
    )(slopes, qat, ka, etab, vat, km)


def _mla_kernel(qt_ref, k_ref, vt_ref, o_ref, sa_ref, sb_ref):
    tq = ATT_TILE
    i = pl.program_id(2)
    qt = [qt_ref[0, hh * LANES:(hh + 1) * LANES, :] for hh in range(ATT_HEADS)]

    def scores(hh, j):
        return _dot(k_ref[pl.ds(pl.multiple_of(j * tq, tq), tq), hh * LANES:(hh + 1) * LANES], qt[hh])

    def values(hh, j):
        return _values_tile(vt_ref, hh, j)

    o_ref[...] = _attention_loop(i, scores, values, _causal_t(tq), tq, sa_ref, sb_ref).astype(BF16)


def _mla(qmt, km, vmt, batch, seq):
    t = km.shape[0]
    tq = ATT_TILE
    nq = seq // tq
    nh = ATT_HEADS
    return pl.pallas_call(
        _mla_kernel,
        grid=(batch, N_HEADS_B // nh, nq),
        in_specs=[
            pl.BlockSpec((1, nh * LANES, tq), lambda b, h, i: (b * nq + i, h, 0)),
            pl.BlockSpec((seq, nh * LANES), lambda b, h, i: (b, h)),
            pl.BlockSpec((nq, nh * V_HEAD_DIM, tq), lambda b, h, i: (b, h, 0)),
        ],
        out_specs=pl.BlockSpec((tq, nh * V_HEAD_DIM), lambda b, h, i: (b * nq + i, h)),
        out_shape=jax.ShapeDtypeStruct((t, WIDTH_B), BF16),
        scratch_shapes=[pltpu.VMEM((nh, tq, tq), F32), pltpu.VMEM((nh, tq, tq), F32)],
        compiler_params=_params("parallel", "parallel", "arbitrary"),
        name="mla",
    )(qmt, km, vmt)


def _pack_halves(y):
    lo = pltpu.bitcast(y[:, :HALF].astype(BF16).astype(F32), U32)
    hi = pltpu.bitcast(y[:, HALF:].astype(BF16).astype(F32), U32)
    return (hi & jnp.uint32(0xFFFF0000)) | (lo >> 16)


def _unpack_halves(w):
    return pltpu.bitcast(w << 16, F32), pltpu.bitcast(w & jnp.uint32(0xFFFF0000), F32)


def _merge_kernel(x_ref, oa_ref, ob_ref, sa_ref, sb_ref, wpa_ref, wpb_ref, wo_ref, g_ref, b_ref, o_ref, op_ref):
    pa = _dot(oa_ref[...], wpa_ref[...])
    pb = _dot(ob_ref[...], wpb_ref[...])
    merged = sa_ref[...].astype(F32) * pa + sb_ref[...].astype(F32) * pb
    hmix = _dot(merged.astype(BF16), wo_ref[...])
    y = _layer_norm(ALPHA * x_ref[...] + hmix, g_ref[...], b_ref[...])
    o_ref[...] = y
    op_ref[...] = _pack_halves(y)


def _merge(x, oa, ob, sa, sb, wp):
    t = x.shape[0]
    tm = MERGE_TILE
    row = lambda w: pl.BlockSpec((tm, w), lambda i: (i, 0))
    weights = [wp['wpa'], wp['wpb'], wp['wo'], wp['ln1_g'], wp['ln1_b']]
    return pl.pallas_call(
        _merge_kernel,
        grid=(t // tm,),
        in_specs=[row(D_MODEL), row(WIDTH_A), row(WIDTH_B), row(D_MODEL), row(D_MODEL)] + [_full(w) for w in weights],
        out_specs=[row(D_MODEL), row(HALF)],
        out_shape=[jax.ShapeDtypeStruct((t, D_MODEL), F32), jax.ShapeDtypeStruct((t, HALF), U32)],
        compiler_params=_params("parallel"),
        name="merge",
    )(x, oa, ob, sa, sb, *weights)


def _router_kernel(x_ref, wh_ref, wl_ref, rb_ref, idx_ref, rank_ref, w_ref, cnt_ref):
    tm = x_ref.shape[0]

    @pl.when(pl.program_id(0) == 0)
    def _():
        cnt_ref[...] = jnp.zeros_like(cnt_ref)

    x = x_ref[...]
    xh = x.astype(BF16)
    xl = (x - xh.astype(F32)).astype(BF16)
    wh, wl = wh_ref[...], wl_ref[...]
    logits = _dot_nt(wh, xh) + (_dot_nt(wh, xl) + _dot_nt(wl, xh))
    scores = _sigmoid(logits)
    choice = scores + rb_ref[...]
    row = lax.broadcasted_iota(jnp.int32, (GROUP_SIZE, tm), 0)
    groups = [choice[g * GROUP_SIZE:(g + 1) * GROUP_SIZE, :] for g in range(N_GROUPS)]
    gscore = []
    for blk in groups:
        m1 = jnp.max(blk, axis=0, keepdims=True)
        first = jnp.min(jnp.where(blk == m1, row, GROUP_SIZE), axis=0, keepdims=True)
        m2 = jnp.max(jnp.where(row == first, -jnp.inf, blk), axis=0, keepdims=True)
        gscore.append(m1 + m2)
    masked = []
    for g in range(N_GROUPS):
        ahead = jnp.zeros((1, tm), jnp.int32)
        for o in range(N_GROUPS):
            if o < g:
                ahead += (gscore[o] >= gscore[g]).astype(jnp.int32)
            elif o > g:
                ahead += (gscore[o] > gscore[g]).astype(jnp.int32)
        masked.append(jnp.where(ahead < TOPK_GROUPS, groups[g], -jnp.inf))
    cur = jnp.concatenate(masked, axis=0)
    erow = lax.broadcasted_iota(jnp.int32, (N_EXPERTS, tm), 0)
    hits, idxs, ws = [], [], []
    for _ in range(TOP_K):
        m = jnp.max(cur, axis=0, keepdims=True)
        e = jnp.min(jnp.where(cur == m, erow, N_EXPERTS), axis=0, keepdims=True)
        hit = erow == e
        hits.append(hit)
        idxs.append(e)
        ws.append(jnp.sum(jnp.where(hit, scores, 0.0), axis=0, keepdims=True))
        cur = jnp.where(hit, -jnp.inf, cur)
    total = ws[0]
    for w in ws[1:]:
        total = total + w
    member = hits[0]
    for hit in hits[1:]:
        member = member | hit
    member = jnp.where(member, 1.0, 0.0).astype(BF16)
    t_src = lax.broadcasted_iota(jnp.int32, (tm, tm), 0)
    t_dst = lax.broadcasted_iota(jnp.int32, (tm, tm), 1)
    before = _dot(member, jnp.where(t_src < t_dst, 1.0, 0.0).astype(BF16))
    base = cnt_ref[...]
    before = before + jnp.concatenate([base] * (tm // LANES), axis=1)
    cnt_ref[...] = base + _dot(member, jnp.ones((tm, LANES), BF16))
    for r in range(TOP_K):
        idx_ref[0, r:r + 1, :] = idxs[r]
        rank_ref[0, r:r + 1, :] = jnp.sum(jnp.where(hits[r], before, 0.0), axis=0, keepdims=True).astype(jnp.int32)
        w_ref[r:r + 1, :] = ws[r] / total * ROUTED_SCALE


def _router(x1, wp):
    t = x1.shape[0]
    tm = ROW_TILE
    nt = t // tm
    weights = [wp['wr_hi'], wp['wr_lo'], wp['rbias']]
    tile = pl.BlockSpec((1, TOP_K, tm), lambda i: (i, 0, 0))
    return pl.pallas_call(
        _router_kernel,
        grid=(nt,),
        in_specs=[pl.BlockSpec((tm, D_MODEL), lambda i: (i, 0))] + [_full(w) for w in weights],
        out_specs=[tile, tile, pl.BlockSpec((TOP_K, tm), lambda i: (0, i)),
                   pl.BlockSpec((N_EXPERTS, LANES), lambda i: (0, 0))],
        out_shape=[jax.ShapeDtypeStruct((nt, TOP_K, tm), jnp.int32), jax.ShapeDtypeStruct((nt, TOP_K, tm), jnp.int32),
                   jax.ShapeDtypeStruct((TOP_K, t), F32), jax.ShapeDtypeStruct((N_EXPERTS, LANES), F32)],
        compiler_params=_params("arbitrary"),
        name="router",
    )(x1, *weights)


def _pos_kernel(idx_ref, rank_ref, pq_ref, pos_ref):
    tm = idx_ref.shape[2]
    erow = lax.broadcasted_iota(jnp.int32, (N_EXPERTS, tm), 0)
    for k in range(TOP_K):
        onehot = jnp.where(erow == idx_ref[0, k:k + 1, :], 1.0, 0.0).astype(BF16)
        q = _dot(pq_ref[...], onehot)
        blk = (q[0:1, :] * 32.0 + q[1:2, :]).astype(jnp.int32)
        pos_ref[k:k + 1, :] = blk * EXPERT_BLOCK + rank_ref[0, k:k + 1, :]


def _positions(idx3, rank3, pstart):
    nt, _, tm = idx3.shape
    blk = pstart // EXPERT_BLOCK
    pq = jnp.concatenate([(blk // 32)[None], (blk % 32)[None], jnp.zeros((6, N_EXPERTS), jnp.int32)], axis=0).astype(F32)
    tile = pl.BlockSpec((1, TOP_K, tm), lambda i: (i, 0, 0))
    return pl.pallas_call(
        _pos_kernel,
        grid=(nt,),
        in_specs=[tile, tile, pl.BlockSpec((8, N_EXPERTS), lambda i: (0, 0))],
        out_specs=pl.BlockSpec((TOP_K, tm), lambda i: (0, i)),
        out_shape=jax.ShapeDtypeStruct((TOP_K, nt * tm), jnp.int32),
        compiler_params=_params("parallel"),
        name="moe_positions",
    )(idx3, rank3, pq.astype(BF16))


def _sc_mesh():
    return plsc.VectorSubcoreMesh(core_axis_name="c", subcore_axis_name="s", num_cores=SC_CORES,
                                  num_subcores=SC_SUBCORES)


def _sc_worker():
    return lax.axis_index("s") * SC_CORES + lax.axis_index("c")


def _sc_scatter_kernel(x_hbm, pos_hbm, out_hbm, idx_v, rows_v):
    nchunk = idx_v.shape[1]
    w = _sc_worker()
    pltpu.sync_copy(pos_hbm.at[w], idx_v)

    @pl.loop(0, nchunk)
    def _(c):
        pltpu.sync_copy(x_hbm.at[pl.ds((w * nchunk + c) * SC_CHUNK, SC_CHUNK)], rows_v)
        for k in range(TOP_K):
            pltpu.sync_copy(rows_v, out_hbm.at[idx_v.at[k, c]])


def _sc_dispatch(x1p, pos_kt, n_rows):
    t = x1p.shape[0]
    workers = SC_CORES * SC_SUBCORES
    nchunk = t // (workers * SC_CHUNK)
    assert t == workers * nchunk * SC_CHUNK
    pos4 = pos_kt.reshape(TOP_K, workers, nchunk, SC_CHUNK).transpose(1, 0, 2, 3)
    return pl.kernel(
        _sc_scatter_kernel,
        out_type=jax.ShapeDtypeStruct((n_rows, HALF), U32),
        mesh=_sc_mesh(),
        scratch_types=[pltpu.VMEM((TOP_K, nchunk, SC_CHUNK), jnp.int32), pltpu.VMEM((SC_CHUNK, HALF), U32)],
        name="moe_dispatch_sc",
    )(x1p, pos4)


def _expert_kernel(layer, be_ref, nused_ref, nvalid_ref, first_ref, slot_ref, ahead_ref, head_ref,
                   x_ref, wg_hbm, wu_hbm, wd_hbm, y_ref, wg_f, wu_f, wd_f, wg_b, wu_b, wd_b, sem):
    def fetch(e, slot):
        return [pltpu.make_async_copy(src.at[layer, e], dst.at[slot], sem.at[slot, n])
                for n, (src, dst) in enumerate(((wg_hbm, wg_f), (wu_hbm, wu_f), (wd_hbm, wd_f)))]

    @pl.when(pl.program_id(0) == 0)
    def _():
        for copy in fetch(head_ref[0], 0):
            copy.start(priority=WEIGHT_DMA_PRIORITY)

        @pl.when(head_ref[1] >= 0)
        def _():
            for copy in fetch(head_ref[1], 1):
                copy.start(priority=WEIGHT_DMA_PRIORITY)

    def one_block(b, rows):
        @pl.when((b < nused_ref[0]) & (first_ref[b] == 1))
        def _():
            slot = slot_ref[b]
            for copy in fetch(be_ref[b], slot):
                copy.wait()
            wg_b[...] = wg_f[slot].astype(BF16)
            wu_b[...] = wu_f[slot].astype(BF16)
            wd_b[...] = wd_f[slot].astype(BF16)

            @pl.when(ahead_ref[b] >= 0)
            def _():
                for copy in fetch(ahead_ref[b], slot):
                    copy.start(priority=WEIGHT_DMA_PRIORITY)

        @pl.when(b < nused_ref[0])
        def _():
            live = lax.broadcasted_iota(jnp.int32, (EXPERT_BLOCK, HALF), 0) < nvalid_ref[b]
            xlo, xhi = (h.astype(BF16) for h in _unpack_halves(jnp.where(live, x_ref[rows, :], jnp.uint32(0))))
            g = _dot(xlo, wg_b[:HALF, :]) + _dot(xhi, wg_b[HALF:, :])
            u = _dot(xlo, wu_b[:HALF, :]) + _dot(xhi, wu_b[HALF:, :])
            a = (g * _sigmoid(g) * u).astype(BF16)
            y_ref[rows, :] = _pack_halves(_dot(a, wd_b[...]))

        @pl.when(b >= nused_ref[0])
        def _():
            y_ref[rows, :] = jnp.zeros((EXPERT_BLOCK, HALF), U32)

    for sub in range(EXPERT_STEP):
        one_block(pl.program_id(0) * EXPERT_STEP + sub, slice(sub * EXPERT_BLOCK, (sub + 1) * EXPERT_BLOCK))


def _experts(xs, block_e, nused, nvalid, counts, w_gate, w_up, w_down, layer, n_blocks):
    rows = EXPERT_BLOCK
    blocks = jnp.arange(n_blocks, dtype=jnp.int32)
    first = ((blocks == 0) | (block_e != jnp.roll(block_e, 1))) & (blocks < nused[0])
    run = jnp.cumsum(first.astype(jnp.int32)) - 1
    run_e = jnp.nonzero(counts > 0, size=N_EXPERTS, fill_value=-1)[0].astype(jnp.int32)
    ahead = jnp.concatenate([run_e, jnp.full((2,), -1, jnp.int32)])[jnp.clip(run, 0, N_EXPERTS - 1) + 2]
    step_rows = EXPERT_STEP * rows
    grid_spec = pltpu.PrefetchScalarGridSpec(
        num_scalar_prefetch=7,
        grid=(n_blocks // EXPERT_STEP,),
        in_specs=[
            pl.BlockSpec((step_rows, HALF), lambda s, be, nu, *_: (jnp.minimum(s, (nu[0] - 1) // EXPERT_STEP), 0)),
            pl.BlockSpec(memory_space=pl.ANY), pl.BlockSpec(memory_space=pl.ANY), pl.BlockSpec(memory_space=pl.ANY),
        ],
        out_specs=pl.BlockSpec((step_rows, HALF), lambda s, *_: (s, 0)),
        scratch_shapes=[
            pltpu.VMEM((2, D_MODEL, D_EXPERT), F32), pltpu.VMEM((2, D_MODEL, D_EXPERT), F32),
            pltpu.VMEM((2, D_EXPERT, D_MODEL), F32),
            pltpu.VMEM((D_MODEL, D_EXPERT), BF16), pltpu.VMEM((D_MODEL, D_EXPERT), BF16),
            pltpu.VMEM((D_EXPERT, D_MODEL), BF16),
            pltpu.SemaphoreType.DMA((2, 3)),
        ],
    )
    return pl.pallas_call(
        functools.partial(_expert_kernel, layer),
        grid_spec=grid_spec,
        out_shape=jax.ShapeDtypeStruct((n_blocks * rows, HALF), U32),
        compiler_params=_params("arbitrary"),
        name="moe_experts",
    )(block_e, nused, nvalid, first.astype(jnp.int32), (run % 2).astype(jnp.int32), ahead, run_e[:2],
      xs, w_gate, w_up, w_down)


def _sc_gather_kernel(table_hbm, idx_hbm, out_hbm, idx_v, rows_v, gsem, wsem):
    per_worker = idx_v.shape[0]
    base = _sc_worker() * per_worker
    pltpu.sync_copy(idx_hbm.at[pl.ds(base, per_worker)], idx_v)

    @pl.loop(0, per_worker // SC_CHUNK, step=SC_BUFS)
    def _(c):
        offs = [(c + n) * SC_CHUNK for n in range(SC_BUFS)]
        gathers = [pltpu.async_copy(table_hbm.at[idx_v.at[pl.ds(offs[n], SC_CHUNK)]], rows_v.at[n], gsem.at[n])
                   for n in range(SC_BUFS)]
        writes = []
        for n in range(SC_BUFS):
            gathers[n].wait()
            writes.append(pltpu.async_copy(rows_v.at[n], out_hbm.at[pl.ds(base + offs[n], SC_CHUNK)], wsem.at[n]))
        for write in writes:
            write.wait()


def _sc_gather_rows(table, idx):
    n = idx.shape[0]
    workers = SC_CORES * SC_SUBCORES
    assert n % (workers * SC_CHUNK * SC_BUFS) == 0
    return pl.kernel(
        _sc_gather_kernel,
        out_type=jax.ShapeDtypeStruct((n, HALF), U32),
        mesh=_sc_mesh(),
        scratch_types=[pltpu.VMEM((n // workers,), jnp.int32), pltpu.VMEM((SC_BUFS, SC_CHUNK, HALF), U32),
                       pltpu.SemaphoreType.DMA((SC_BUFS,)), pltpu.SemaphoreType.DMA((SC_BUFS,))],
        name="moe_gather_sc",
    )(table, idx)


def _combine_kernel(x_ref, w_ref, *refs):
    y_refs, (wsg_ref, wsu_ref, wsd_ref, g_ref, b_ref, o_ref) = refs[:TOP_K], refs[TOP_K:]
    x = x_ref[...]
    xb = x.astype(BF16)
    g = _dot(xb, wsg_ref[...])
    u = _dot(xb, wsu_ref[...])
    shared = _dot((g * _sigmoid(g) * u).astype(BF16), wsd_ref[...])
    w = w_ref[...]
    lo, hi = (h * w[:, 0:1] for h in _unpack_halves(y_refs[0][...]))
    for k in range(1, TOP_K):
        lo_k, hi_k = _unpack_halves(y_refs[k][...])
        lo = lo + lo_k * w[:, k:k + 1]
        hi = hi + hi_k * w[:, k:k + 1]
    routed = jnp.concatenate([lo, hi], axis=1)
    o_ref[...] = _layer_norm(ALPHA * x + (shared + routed), g_ref[...], b_ref[...])


def _combine(x1, y8, w_tk, wp):
    t = x1.shape[0]
    tm = WIDE_TILE
    nt = t // tm
    weights = [wp['wsg'], wp['wsu'], wp['wsd'], wp['ln2_g'], wp['ln2_b']]
    y_specs = [pl.BlockSpec((tm, HALF), lambda i, k=k: (k * nt + i, 0)) for k in range(TOP_K)]
    return pl.pallas_call(
        _combine_kernel,
        grid=(nt,),
        in_specs=[pl.BlockSpec((tm, D_MODEL), lambda i: (i, 0)), pl.BlockSpec((tm, TOP_K), lambda i: (i, 0))]
                 + y_specs + [_full(w) for w in weights],
        out_specs=pl.BlockSpec((tm, D_MODEL), lambda i: (i, 0)),
        out_shape=jax.ShapeDtypeStruct((t, D_MODEL), F32),
        compiler_params=_params("parallel"),
        name="moe_combine",
    )(x1, w_tk, *([y8] * TOP_K), *weights)


def _moe(x1, x1p, wp, w_e_gate, w_e_up, w_e_down, layer):
    t = x1.shape[0]
    rows = EXPERT_BLOCK
    n_blocks = -(-(t * TOP_K + N_EXPERTS * (rows - 1)) // (rows * EXPERT_STEP)) * EXPERT_STEP
    idx3, rank3, top_w, cnt = _router(x1, wp)
    counts = cnt[:, 0].astype(jnp.int32)
    padded = (counts + rows - 1) // rows * rows
    padded_end = jnp.cumsum(padded)
    pstart = (padded_end - padded).astype(jnp.int32)
    nused = (padded_end[-1] // rows).astype(jnp.int32).reshape(1)
    blocks = jnp.arange(n_blocks, dtype=jnp.int32)
    block_row = jnp.minimum(blocks, nused[0] - 1) * rows
    block_e = jnp.sum((padded_end[None, :] <= block_row[:, None]).astype(jnp.int32), axis=1)
    block_e = jnp.minimum(block_e, N_EXPERTS - 1)
    nvalid = jnp.clip(counts[block_e] - (blocks * rows - pstart[block_e]), 0, rows).astype(jnp.int32)
    pos_kt = _positions(idx3, rank3, pstart)
    xs = _sc_dispatch(x1p, pos_kt, n_blocks * rows)
    ys = _experts(xs, block_e, nused, nvalid, counts, w_e_gate, w_e_up, w_e_down, layer, n_blocks)
    return _combine(x1, _sc_gather_rows(ys, pos_kt.reshape(TOP_K * t)), top_w.T, wp)


def _head_groups_t(w, used):
    k, h, _ = w.shape
    return jnp.pad(w, ((0, 0), (0, 0), (0, LANES - used))).reshape(k, h * LANES).T


def _prep_layer(l, w_in, b_gate, q_norm, w_uq, kv_norm, w_ukv, w_proj_a, w_proj_b, w_out, ln1_g, ln1_b,
                w_router, router_bias, w_s_gate, w_s_up, w_s_down, ln2_g, ln2_b):
    w = w_in[l]
    o = 0
    cols = {}
    for name, width in (('qa', WIDTH_A), ('ka', WIDTH_A), ('va', WIDTH_A), ('cq', Q_LORA_RANK),
                        ('ckv', KV_LORA_RANK), ('kr', QK_ROPE_DIM), ('ga', D_MODEL), ('gb', D_MODEL)):
        cols[name] = w[:, o:o + width]
        o += width
    wkr = jnp.pad(cols['kr'], ((0, 0), (QK_NOPE_DIM, LANES - QK_NOPE_DIM - QK_ROPE_DIM)))
    dqk = QK_NOPE_DIM + QK_ROPE_DIM
    wq = w_uq[l].reshape(Q_LORA_RANK, N_HEADS_B, dqk) * (dqk ** -0.5 * LOG2E)
    wkv = w_ukv[l].reshape(KV_LORA_RANK, N_HEADS_B, QK_NOPE_DIM + V_HEAD_DIM)
    wuk = jnp.pad(wkv[:, :, :QK_NOPE_DIM], ((0, 0), (0, 0), (0, LANES - QK_NOPE_DIM))).reshape(KV_LORA_RANK, N_HEADS_B * LANES)
    wr_t = w_router[l].T
    wr_hi = wr_t.astype(BF16)
    return dict(
        wqt=(cols['qa'] * HEAD_DIM_A ** -0.5).T.astype(BF16), wk=cols['ka'].astype(BF16),
        wvt=cols['va'].T.astype(BF16),
        wcq=cols['cq'].astype(BF16), wckv=cols['ckv'].astype(BF16), wkr=wkr.astype(BF16),
        wg=jnp.concatenate([cols['ga'], cols['gb']], axis=1).astype(BF16),
        bg=b_gate[l].reshape(1, 2 * D_MODEL), qn=q_norm[l].reshape(1, Q_LORA_RANK), kvn=kv_norm[l].reshape(1, KV_LORA_RANK),
        wuqt=_head_groups_t(wq, dqk).astype(BF16), wuk=wuk.astype(BF16),
        wuvt=wkv[:, :, QK_NOPE_DIM:].reshape(KV_LORA_RANK, WIDTH_B).T.astype(BF16),
        wpa=w_proj_a[l].astype(BF16), wpb=w_proj_b[l].astype(BF16), wo=w_out[l].astype(BF16),
        ln1_g=ln1_g[l].reshape(1, D_MODEL), ln1_b=ln1_b[l].reshape(1, D_MODEL),
        wr_hi=wr_hi, wr_lo=(wr_t - wr_hi.astype(F32)).astype(BF16), rbias=router_bias[l].reshape(N_EXPERTS, 1),
        wsg=w_s_gate[l].astype(BF16), wsu=w_s_up[l].astype(BF16), wsd=w_s_down[l].astype(BF16),
        ln2_g=ln2_g[l].reshape(1, D_MODEL), ln2_b=ln2_b[l].reshape(1, D_MODEL),
    )


def _rope_tables(seq):
    pos = jnp.arange(seq, dtype=F32)
    inv_freq = ROPE_THETA ** (-jnp.arange(0, QK_ROPE_DIM, 2, dtype=F32) / QK_ROPE_DIM)
    ang = pos[:, None] * inv_freq[None, :]
    cos, sin = jnp.cos(ang), jnp.sin(ang)
    half = QK_ROPE_DIM // 2

    def build(cs, sn, axis):
        shape = lambda n: (seq, n) if axis == 1 else (n, seq)
        z = lambda n: jnp.zeros(shape(n), F32)
        c = jnp.concatenate([jnp.ones(shape(QK_NOPE_DIM), F32), cs, cs, z(LANES - QK_NOPE_DIM - QK_ROPE_DIM)], axis=axis)
        s1 = jnp.concatenate([z(QK_NOPE_DIM), -sn, z(LANES - QK_NOPE_DIM - half)], axis=axis)
        s2 = jnp.concatenate([z(QK_NOPE_DIM + half), sn, z(LANES - QK_NOPE_DIM - QK_ROPE_DIM)], axis=axis)
        return c, s1, s2

    return build(cos, sin, 1) + build(cos.T, sin.T, 0)


def _moba_key_table(seq, slopes):
    blk = jnp.arange(seq, dtype=jnp.int32) // MOBA_BLOCK
    onehot = (blk[:, None] == jnp.arange(BIAS_LO, dtype=jnp.int32)[None, :]).astype(F32)
    inblk = (jnp.arange(seq, dtype=jnp.int32) % MOBA_BLOCK).astype(F32)[:, None] * (slopes * LOG2E)[None, :]
    hi = inblk.astype(BF16)
    lo = (inblk - hi.astype(F32)).astype(BF16)
    pad = jnp.zeros((seq, LANES - POS_LO - N_HEADS_A), BF16)
    return jnp.concatenate([onehot.astype(BF16), onehot.astype(BF16), hi, lo, pad], axis=1)


def kernel(x, w_in, b_gate, q_norm, w_uq, kv_norm, w_ukv, w_proj_a, w_proj_b, w_out, ln1_g, ln1_b, w_router, router_bias, w_e_gate, w_e_up, w_e_down, w_s_gate, w_s_up, w_s_down, ln2_g, ln2_b):
    batch, seq, d = x.shape
    assert d == D_MODEL and seq % MOBA_BLOCK == 0 and MOBA_TOPK <= seq // MOBA_BLOCK <= BIAS_LO
    assert POS_LO + N_HEADS_A <= LANES and POS_HI + N_HEADS_A <= POS_LO and (batch * seq) % MERGE_TILE == 0
    tabs = _rope_tables(seq)
    slopes = jnp.asarray(np.exp2(-8.0 * (np.arange(N_HEADS_A) + 1.0) / N_HEADS_A), F32)
    etab = _moba_key_table(seq, slopes)
    h = x.reshape(batch * seq, d)
    for l in range(DEPTH):
        wp = _prep_layer(l, w_in, b_gate, q_norm, w_uq, kv_norm, w_ukv, w_proj_a, w_proj_b, w_out, ln1_g, ln1_b,
                         w_router, router_bias, w_s_gate, w_s_up, w_s_down, ln2_g, ln2_b)
        qat, ka, vat, kmean, qmt, km, vmt, sa, sb = _inproj(h, wp, tabs, seq)
        oa = _moba(qat, ka, vat, kmean, etab, slopes, batch, seq)
        ob = _mla(qmt, km, vmt, batch, seq)
        x1, x1p = _merge(h, oa, ob, sa, sb, wp)
        h = _moe(x1, x1p, wp, w_e_gate, w_e_up, w_e_down, l)
    return h.reshape(batch, seq, d)
```
